```python
import math
import jax
import jax.numpy as jnp
from jax import lax
import numpy as np

D_MODEL = 2048
BATCH = 4
SEQ = 2048
DEPTH = 4
DEC_BATCH = 8
DEC_SEQ = 4
PAST_LEN = 16384
PAGE_SIZE = 128

HEAD_DIM = 128
ROPE_THETA = 500000.0
ROPE_FRACTION = 4
NORM_EPS = 1e-6

NSA_HEADS = 8
NSA_GROUPS = 2
NSA_REP = NSA_HEADS // NSA_GROUPS
NSA_CMP_LEN = 32
NSA_CMP_STRIDE = 16
NSA_SLC_LEN = 64
NSA_TOPN = 16
NSA_WINDOW = 512
NSA_CMP_HIDDEN = 128
NSA_FORCE_BONUS = 1000.0

DIFF_HEADS = 4
DIFF_QK_DIM = HEAD_DIM // 2

MOBA_HEADS = 4
MOBA_BLOCK = 256
MOBA_TOPK = 3

MEM_TOKENS = 256
MEM_HEADS = 4
MEM_DIM = 128

D_MIX = (NSA_HEADS + DIFF_HEADS + MOBA_HEADS) * HEAD_DIM
IN_SIZES = (
    NSA_HEADS * HEAD_DIM,
    6 * NSA_GROUPS * HEAD_DIM,
    3 * NSA_HEADS,
    NSA_HEADS * HEAD_DIM,
    DIFF_HEADS * HEAD_DIM, DIFF_HEADS * HEAD_DIM, DIFF_HEADS * HEAD_DIM, DIFF_HEADS * HEAD_DIM,
    MOBA_HEADS * HEAD_DIM, MOBA_HEADS * HEAD_DIM, MOBA_HEADS * HEAD_DIM, MOBA_HEADS * HEAD_DIM,
)
D_IN = sum(IN_SIZES)
IN_SPLITS = tuple(int(s) for s in np.cumsum(IN_SIZES)[:-1])
Q_BLOCK_DENSE = 128
Q_BLOCK_SPARSE = 16

kernel_name = 'hybrid_nsa_diff_moba_decode_step'


def rmsnorm(x, g):
    x32 = x.astype(jnp.float32)
    y = x32 * lax.rsqrt(jnp.mean(x32 * x32, axis=-1, keepdims=True) + NORM_EPS)
    return (y * g.astype(jnp.float32)).astype(x.dtype)


def rope(x, pos):
    rd = x.shape[-1] // ROPE_FRACTION
    half = rd // 2
    inv = ROPE_THETA ** (-2.0 * jnp.arange(half, dtype=jnp.float32) / rd)
    ang = pos.astype(jnp.float32)[:, None] * inv[None, :]
    shape = (1, pos.shape[0]) + (1,) * (x.ndim - 3) + (half,)
    c = jnp.cos(ang).reshape(shape).astype(x.dtype)
    s = jnp.sin(ang).reshape(shape).astype(x.dtype)
    x1 = x[..., :half]
    x2 = x[..., half:rd]
    return jnp.concatenate([x1 * c - x2 * s, x1 * s + x2 * c, x[..., rd:]], axis=-1)


def masked_softmax(s, mask):
    s = jnp.where(mask, s, -jnp.inf)
    m = jnp.max(s, axis=-1, keepdims=True)
    m = jnp.where(jnp.isfinite(m), m, 0.0)
    e = jnp.where(mask, jnp.exp(s - m), 0.0)
    return e / jnp.maximum(jnp.sum(e, axis=-1, keepdims=True), 1e-30)


def map_qblocks(fn, qb, pos, *xs):
    T = pos.shape[0]
    if T <= qb or T % qb:
        return fn(pos, *xs)
    nb = T // qb
    split = lambda a: jnp.swapaxes(a.reshape((a.shape[0], nb, qb) + a.shape[2:]), 0, 1)
    out = lax.map(lambda args: fn(*args), (pos.reshape(nb, qb),) + tuple(split(a) for a in xs))
    out = jnp.swapaxes(out, 0, 1)
    return out.reshape((out.shape[0], T) + out.shape[3:])


def pad_time(a, mult):
    L = a.shape[1]
    Lp = -(-L // mult) * mult
    return jnp.pad(a, ((0, 0), (0, Lp - L)) + ((0, 0),) * (a.ndim - 2))


def gather_pages(pool, page_table, l):
    pages = pool[page_table, l]
    bd, n_pages = page_table.shape
    return pages.reshape((bd, n_pages * PAGE_SIZE) + pages.shape[3:])


def compress(k, pe, w1, w2):
    B, Lp, G, dh = k.shape
    nc = Lp // NSA_CMP_STRIDE
    ch = k.reshape(B, nc, NSA_CMP_STRIDE, G, dh).transpose(0, 1, 3, 2, 4).reshape(B, nc, G, NSA_CMP_STRIDE * dh)
    half = NSA_CMP_STRIDE * dh
    ha = ch @ w1[:half]
    hb = ch @ w1[half:]
    hb = jnp.concatenate([hb[:, 1:], jnp.zeros_like(hb[:, :1])], axis=1)
    return jax.nn.silu(ha + hb + pe.reshape(-1) @ w1) @ w2


def nsa_attend(q, gates, pos, kc, vc, ks, vs, kw, vw, win_pos0, pe_k, pe_v, w1k, w2k, w1v, w2v):
    B, T, _, dh = q.shape
    G = NSA_GROUPS
    Kc = compress(pad_time(kc, NSA_SLC_LEN), pe_k, w1k, w2k)
    Vc = compress(pad_time(vc, NSA_SLC_LEN), pe_v, w1v, w2v)
    nc = Kc.shape[1]
    nsb = nc * NSA_CMP_STRIDE // NSA_SLC_LEN
    cmp_end = jnp.arange(nc) * NSA_CMP_STRIDE + NSA_CMP_LEN - 1
    Ks = pad_time(ks, NSA_SLC_LEN).reshape(B, nsb, NSA_SLC_LEN, G, dh).transpose(0, 3, 1, 2, 4)
    Vs = pad_time(vs, NSA_SLC_LEN).reshape(B, nsb, NSA_SLC_LEN, G, dh).transpose(0, 3, 1, 2, 4)
    Kw = jnp.pad(kw, ((0, 0), (NSA_WINDOW, 0), (0, 0), (0, 0)))
    Vw = jnp.pad(vw, ((0, 0), (NSA_WINDOW, 0), (0, 0), (0, 0)))
    pw0 = win_pos0 - NSA_WINDOW
    nsel = min(NSA_TOPN, nsb)
    b_ix = jnp.arange(B)[:, None, None, None]
    g_ix = jnp.arange(G)[None, None, :, None]
    scale = dh ** -0.5
    per_blk = NSA_SLC_LEN // NSA_CMP_STRIDE

    def block(qpos, qb, gb):
        Q = qpos.shape[0]
        sc = jnp.einsum('bqgrd,bngd->bqgrn', qb, Kc).astype(jnp.float32) * scale
        pc = masked_softmax(sc, (cmp_end[None, :] <= qpos[:, None])[None, :, None, None, :])
        o_cmp = jnp.einsum('bqgrn,bngd->bqgrd', pc.astype(qb.dtype), Vc)
        pg = pc.sum(axis=3)
        pg2 = pg + jnp.pad(pg[..., :-1], ((0, 0), (0, 0), (0, 0), (1, 0)))
        slc = pg2.reshape(B, Q, G, nsb, per_blk).sum(-1)
        blk = jnp.arange(nsb)[None, :]
        tb = (qpos // NSA_SLC_LEN)[:, None]
        valid = blk <= tb
        forced = (blk == 0) | (blk == tb) | (blk == tb - 1)
        score = jnp.where(valid[None, :, None, :], slc + jnp.where(forced, NSA_FORCE_BONUS, 0.0)[None, :, None, :], -jnp.inf)
        top_v, top_i = lax.top_k(score, nsel)
        kg = Ks[b_ix, g_ix, top_i]
        vg = Vs[b_ix, g_ix, top_i]
        kpos = top_i[..., None] * NSA_SLC_LEN + jnp.arange(NSA_SLC_LEN)
        smask = jnp.isfinite(top_v)[..., None] & (kpos <= qpos[None, :, None, None, None])
        ss = jnp.einsum('bqgrd,bqgskd->bqgrsk', qb, kg).astype(jnp.float32) * scale
        ps = masked_softmax(ss.reshape(B, Q, G, NSA_REP, nsel * NSA_SLC_LEN), smask.reshape(B, Q, G, 1, nsel * NSA_SLC_LEN))
        o_slc = jnp.einsum('bqgrsk,bqgskd->bqgrd', ps.reshape(ss.shape).astype(qb.dtype), vg)
        L = Q + NSA_WINDOW - 1
        start = qpos[0] - (NSA_WINDOW - 1) - pw0
        kwb = lax.dynamic_slice_in_dim(Kw, start, L, axis=1)
        vwb = lax.dynamic_slice_in_dim(Vw, start, L, axis=1)
        wpos = qpos[0] - (NSA_WINDOW - 1) + jnp.arange(L)
        wmask = (wpos[None, :] <= qpos[:, None]) & (wpos[None, :] > qpos[:, None] - NSA_WINDOW) & (wpos[None, :] >= 0)
        sw = jnp.einsum('bqgrd,bkgd->bqgrk', qb, kwb).astype(jnp.float32) * scale
        pw = masked_softmax(sw, wmask[None, :, None, None, :])
        o_win = jnp.einsum('bqgrk,bkgd->bqgrd', pw.astype(qb.dtype), vwb)
        return gb[:, :, 0, ..., None] * o_cmp + gb[:, :, 1, ..., None] * o_slc + gb[:, :, 2, ..., None] * o_win

    qg = q.reshape(B, T, G, NSA_REP, dh)
    gg = gates.reshape(B, T, 3, G, NSA_REP)
    return map_qblocks(block, Q_BLOCK_SPARSE, pos, qg, gg)


def diff_attend(q, k, v, pos, lam, subln, lam_init):
    kpos = jnp.arange(k.shape[1])
    scale = DIFF_QK_DIM ** -0.5

    def block(qpos, qb):
        s = jnp.einsum('bqhcd,bkhcd->bhcqk', qb, k).astype(jnp.float32) * scale
        p = masked_softmax(s, kpos[None, :] <= qpos[:, None])
        a = p[:, :, 0] - lam * p[:, :, 1]
        return jnp.einsum('bhqk,bkhd->bqhd', a.astype(v.dtype), v)

    o = map_qblocks(block, Q_BLOCK_DENSE, pos, q)
    return rmsnorm(o, subln) * (1.0 - lam_init)


def moba_attend(q, k, v, pos):
    B, _, H, dh = q.shape
    Kb = pad_time(k, MOBA_BLOCK)
    nb = Kb.shape[1] // MOBA_BLOCK
    Kb = Kb.reshape(B, nb, MOBA_BLOCK, H, dh).transpose(0, 3, 1, 2, 4)
    Vb = pad_time(v, MOBA_BLOCK).reshape(B, nb, MOBA_BLOCK, H, dh).transpose(0, 3, 1, 2, 4)
    kmean = jnp.mean(Kb.astype(jnp.float32), axis=3)
    ksel = min(MOBA_TOPK, nb)
    b_ix = jnp.arange(B)[:, None, None, None]
    h_ix = jnp.arange(H)[None, None, :, None]
    scale = dh ** -0.5

    def block(qpos, qb):
        Q = qpos.shape[0]
        sc = jnp.einsum('bqhd,bhnd->bqhn', qb.astype(jnp.float32), kmean)
        bt = qpos // MOBA_BLOCK
        past_ok = jnp.arange(nb)[None, :] < bt[:, None]
        top_v, top_i = lax.top_k(jnp.where(past_ok[None, :, None, :], sc, -jnp.inf), ksel)
        own = jnp.broadcast_to(bt[None, :, None, None], (B, Q, H, 1)).astype(top_i.dtype)
        idx = jnp.concatenate([top_i, own], axis=-1)
        ok = jnp.concatenate([jnp.isfinite(top_v), jnp.ones((B, Q, H, 1), bool)], axis=-1)
        kg = Kb[b_ix, h_ix, idx]
        vg = Vb[b_ix, h_ix, idx]
        kpos = idx[..., None] * MOBA_BLOCK + jnp.arange(MOBA_BLOCK)
        mask = ok[..., None] & (kpos <= qpos[None, :, None, None, None])
        s = jnp.einsum('bqhd,bqhskd->bqhsk', qb, kg).astype(jnp.float32) * scale
        p = masked_softmax(s.reshape(B, Q, H, -1), mask.reshape(B, Q, H, -1))
        return jnp.einsum('bqhsk,bqhskd->bqhd', p.reshape(s.shape).astype(vg.dtype), vg)

    return map_qblocks(block, Q_BLOCK_SPARSE, pos, q)


def mem_cross(h, mem_kv, wq, wo):
    B, T, _ = h.shape
    q = (h @ wq).reshape(B, T, MEM_HEADS, MEM_DIM)
    s = jnp.einsum('bthd,bmhd->bhtm', q, mem_kv[:, :, 0]).astype(jnp.float32) * (MEM_DIM ** -0.5)
    p = jax.nn.softmax(s, axis=-1)
    o = jnp.einsum('bhtm,bmhd->bthd', p.astype(h.dtype), mem_kv[:, :, 1])
    return o.reshape(B, T, MEM_HEADS * MEM_DIM) @ wo


def mixer_sublayer(h, start, past, l, P):
    B, T, _ = h.shape
    pos = start + jnp.arange(T, dtype=jnp.int32)
    parts = jnp.split(h @ P['w_in'][l], IN_SPLITS, axis=-1)
    nq, nkv, ngate, nz, dq, dk, dv, dz, mq, mk, mv, mz = parts
    G = NSA_GROUPS
    nq = rope(nq.reshape(B, T, NSA_HEADS, HEAD_DIM), pos)
    nkv = nkv.reshape(B, T, 6, G, HEAD_DIM)
    kc, vc = rope(nkv[:, :, 0], pos), nkv[:, :, 1]
    ks, vs = rope(nkv[:, :, 2], pos), nkv[:, :, 3]
    kw, vw = rope(nkv[:, :, 4], pos), nkv[:, :, 5]
    gates = jax.nn.sigmoid(ngate).reshape(B, T, 3, NSA_HEADS)
    dq = rope(dq.reshape(B, T, DIFF_HEADS, 2, DIFF_QK_DIM), pos)
    dk = rope(dk.reshape(B, T, DIFF_HEADS, 2, DIFF_QK_DIM), pos)
    dv = dv.reshape(B, T, DIFF_HEADS, HEAD_DIM)
    mq = rope(mq.reshape(B, T, MOBA_HEADS, HEAD_DIM), pos)
    mk = rope(mk.reshape(B, T, MOBA_HEADS, HEAD_DIM), pos)
    mv = mv.reshape(B, T, MOBA_HEADS, HEAD_DIM)
    rows_cmp = jnp.stack([kc, vc], axis=2)
    rows_slc = jnp.stack([ks, vs], axis=2)
    rows_win = jnp.stack([kw, vw], axis=2)
    rows_diff = jnp.stack([dk.reshape(B, T, DIFF_HEADS, HEAD_DIM), dv], axis=2)
    rows_moba = jnp.stack([mk, mv], axis=2)
    if past is None:
        all_cmp, all_slc, all_diff, all_moba = rows_cmp, rows_slc, rows_diff, rows_moba
        win_all = rows_win
        win_pos0 = start
        win_keep = min(NSA_WINDOW, T)
    else:
        p_cmp, p_slc, p_win, p_diff, p_moba = past
        all_cmp = jnp.concatenate([p_cmp, rows_cmp], axis=1)
        all_slc = jnp.concatenate([p_slc, rows_slc], axis=1)
        all_diff = jnp.concatenate([p_diff, rows_diff], axis=1)
        all_moba = jnp.concatenate([p_moba, rows_moba], axis=1)
        win_all = jnp.concatenate([p_win, rows_win], axis=1)
        win_pos0 = start - p_win.shape[1]
        win_keep = p_win.shape[1]
    new_win = win_all[:, win_all.shape[1] - win_keep:]
    Lk = all_diff.shape[1]

    o_a = nsa_attend(nq, gates, pos, all_cmp[:, :, 0], all_cmp[:, :, 1], all_slc[:, :, 0], all_slc[:, :, 1],
                     win_all[:, :, 0], win_all[:, :, 1], win_pos0, P['nsa_pe_k'][l], P['nsa_pe_v'][l],
                     P['nsa_w1_k'][l], P['nsa_w2_k'][l], P['nsa_w1_v'][l], P['nsa_w2_v'][l])
    lam_init = 0.8 - 0.6 * math.exp(-0.3 * l)
    lp = P['diff_lambda'][l].astype(jnp.float32)
    lam = jnp.exp(jnp.sum(lp[0] * lp[1])) - jnp.exp(jnp.sum(lp[2] * lp[3])) + lam_init
    o_b = diff_attend(dq, all_diff[:, :, 0].reshape(B, Lk, DIFF_HEADS, 2, DIFF_QK_DIM), all_diff[:, :, 1],
                      pos, lam, P['diff_subln'][l], lam_init)
    o_c = moba_attend(mq, all_moba[:, :, 0], all_moba[:, :, 1], pos)
    mixed = jnp.concatenate([o_a.reshape(B, T, -1) * jax.nn.silu(nz),
                             o_b.reshape(B, T, -1) * jax.nn.silu(dz),
                             o_c.reshape(B, T, -1) * jax.nn.silu(mz)], axis=-1)
    return mixed @ P['w_out'][l], (rows_cmp, rows_slc, new_win, rows_diff, rows_moba)


def trunk(x, start, mem_kvs, past_fn, P):
    rows = []
    for l in range(DEPTH):
        past = None if past_fn is None else past_fn(l)
        o, r = mixer_sublayer(rmsnorm(x, P['norm_mix'][l]), start, past, l, P)
        x = x + o
        x = x + mem_cross(rmsnorm(x, P['norm_mem_x'][l]), mem_kvs[l], P['w_mem_q'][l], P['w_mem_o'][l])
        rows.append(r)
    y = rmsnorm(x, P['norm_final'])
    return y, [jnp.stack([r[i] for r in rows], axis=1) for i in range(5)]


def setup_inputs(seed: int = 0) -> dict:
    key = jax.random.key(seed)
    ks = jax.random.split(key, 32)
    nrm = lambda k, shape, s: s * jax.random.normal(k, shape, jnp.float32)
    n_pages = PAST_LEN // PAGE_SIZE
    n_used = DEC_BATCH * n_pages
    n_pool = n_used + max(1, n_used // 4)
    wb = min(NSA_WINDOW, PAST_LEN)
    page_table = jax.random.permutation(ks[0], n_pool)[:n_used].reshape(DEC_BATCH, n_pages).astype(jnp.int32)
    return {
        'x_prompt': nrm(ks[1], (BATCH, SEQ, D_MODEL), 1.0),
        'x_sample': nrm(ks[2], (DEC_BATCH, DEC_SEQ, D_MODEL), 1.0),
        'mem_prompt': nrm(ks[3], (BATCH, MEM_TOKENS, D_MODEL), 1.0),
        'cache_nsa_cmp': nrm(ks[4], (n_pool, DEPTH, PAGE_SIZE, 2, NSA_GROUPS, HEAD_DIM), 1.0),
        'cache_nsa_slc': nrm(ks[5], (n_pool, DEPTH, PAGE_SIZE, 2, NSA_GROUPS, HEAD_DIM), 1.0),
        'cache_nsa_win': nrm(ks[6], (DEC_BATCH, DEPTH, wb, 2, NSA_GROUPS, HEAD_DIM), 1.0),
        'cache_diff': nrm(ks[7], (n_pool, DEPTH, PAGE_SIZE, 2, DIFF_HEADS, HEAD_DIM), 1.0),
        'cache_moba': nrm(ks[8], (n_pool, DEPTH, PAGE_SIZE, 2, MOBA_HEADS, HEAD_DIM), 1.0),
        'cache_mem': nrm(ks[9], (DEC_BATCH, DEPTH, MEM_TOKENS, 2, MEM_HEADS, MEM_DIM), 1.0),
        'page_table': page_table,
        'norm_mix': 1.0 + nrm(ks[10], (DEPTH, D_MODEL), 0.02),
        'w_in': nrm(ks[11], (DEPTH, D_MODEL, D_IN), D_MODEL ** -0.5),
        'w_out': nrm(ks[12], (DEPTH, D_MIX, D_MODEL), D_MIX ** -0.5),
        'nsa_pe_k': nrm(ks[13], (DEPTH, NSA_CMP_LEN, HEAD_DIM), 0.1),
        'nsa_pe_v': nrm(ks[14], (DEPTH, NSA_CMP_LEN, HEAD_DIM), 0.1),
        'nsa_w1_k': nrm(ks[15], (DEPTH, NSA_CMP_LEN * HEAD_DIM, NSA_CMP_HIDDEN), (NSA_CMP_LEN * HEAD_DIM) ** -0.5),
        'nsa_w2_k': nrm(ks[16], (DEPTH, NSA_CMP_HIDDEN, HEAD_DIM), NSA_CMP_HIDDEN ** -0.5),
        'nsa_w1_v': nrm(ks[17], (DEPTH, NSA_CMP_LEN * HEAD_DIM, NSA_CMP_HIDDEN), (NSA_CMP_LEN * HEAD_DIM) ** -0.5),
        'nsa_w2_v': nrm(ks[18], (DEPTH, NSA_CMP_HIDDEN, HEAD_DIM), NSA_CMP_HIDDEN ** -0.5),
        'diff_lambda': nrm(ks[19], (DEPTH, 4, DIFF_QK_DIM), 0.1),
        'diff_subln': 1.0 + nrm(ks[20], (DEPTH, HEAD_DIM), 0.02),
        'norm_mem_x': 1.0 + nrm(ks[21], (DEPTH, D_MODEL), 0.02),
        'norm_mem_m': 1.0 + nrm(ks[22], (DEPTH, D_MODEL), 0.02),
        'w_mem_q': nrm(ks[23], (DEPTH, D_MODEL, MEM_HEADS * MEM_DIM), D_MODEL ** -0.5),
        'w_mem_kv': nrm(ks[24], (DEPTH, D_MODEL, 2 * MEM_HEADS * MEM_DIM), D_MODEL ** -0.5),
        'w_mem_o': nrm(ks[25], (DEPTH, MEM_HEADS * MEM_DIM, D_MODEL), (MEM_HEADS * MEM_DIM) ** -0.5),
        'norm_final': 1.0 + nrm(ks[26], (D_MODEL,), 0.02),
    }


def reference(x_prompt, x_sample, mem_prompt, cache_nsa_cmp, cache_nsa_slc, cache_nsa_win, cache_diff,
              cache_moba, cache_mem, page_table, norm_mix, w_in, w_out, nsa_pe_k, nsa_pe_v, nsa_w1_k,
              nsa_w2_k, nsa_w1_v, nsa_w2_v, diff_lambda, diff_subln, norm_mem_x, norm_mem_m, w_mem_q,
              w_mem_kv, w_mem_o, norm_final):
    P = {'norm_mix': norm_mix, 'w_in': w_in, 'w_out': w_out, 'nsa_pe_k': nsa_pe_k, 'nsa_pe_v': nsa_pe_v,
         'nsa_w1_k': nsa_w1_k, 'nsa_w2_k': nsa_w2_k, 'nsa_w1_v': nsa_w1_v, 'nsa_w2_v': nsa_w2_v,
         'diff_lambda': diff_lambda, 'diff_subln': diff_subln, 'norm_mem_x': norm_mem_x,
         'w_mem_q': w_mem_q, 'w_mem_o': w_mem_o, 'norm_final': norm_final}
    bp, n_mem = mem_prompt.shape[0], mem_prompt.shape[1]
    mem_kv_p = [(rmsnorm(mem_prompt, norm_mem_m[l]) @ w_mem_kv[l]).reshape(bp, n_mem, 2, MEM_HEADS, MEM_DIM)
                for l in range(DEPTH)]
    y_prompt, (p_cmp, p_slc, p_win, p_diff, p_moba) = trunk(x_prompt, 0, mem_kv_p, None, P)
    p_mem = jnp.stack(mem_kv_p, axis=1)
    past_len = page_table.shape[1] * PAGE_SIZE

    def sample_past(l):
        return (gather_pages(cache_nsa_cmp, page_table, l), gather_pages(cache_nsa_slc, page_table, l),
                cache_nsa_win[:, l], gather_pages(cache_diff, page_table, l),
                gather_pages(cache_moba, page_table, l))

    mem_kv_s = [cache_mem[:, l] for l in range(DEPTH)]
    y_sample, (s_cmp, s_slc, s_win, s_diff, s_moba) = trunk(x_sample, past_len, mem_kv_s, sample_past, P)
    return (y_prompt, y_sample, p_cmp, p_slc, p_win, p_diff, p_moba, p_mem, s_cmp, s_slc, s_win, s_diff, s_moba)
```

```python
import functools
import math

import numpy as np
import jax
import jax.numpy as jnp
from jax import lax
from jax.experimental import pallas as pl
from jax.experimental.pallas import tpu as pltpu

F32 = jnp.float32
BF16 = jnp.bfloat16

D_MODEL = 2048
HEAD_DIM = 128
ROPE_THETA = 500000.0
NORM_EPS = 1e-6
PAGE_SIZE = 128

NSA_HEADS = 8
NSA_GROUPS = 2
NSA_REP = NSA_HEADS // NSA_GROUPS
NSA_CMP_STRIDE = 16
NSA_SLC_LEN = 64
NSA_TOPN = 16
NSA_WINDOW = 512
NSA_FORCE_BONUS = 1000.0
DIFF_HEADS = 4
DIFF_QK_DIM = HEAD_DIM // 2
MOBA_HEADS = 4
MOBA_BLOCK = 256
MOBA_TOPK = 3
MEM_TOKENS = 256
MEM_HEADS = 4
MEM_DIM = 128

NEG = -1e30
LANES = 128
VMEM_LIMIT_BYTES = 56 * 1024 * 1024

BQ, BKC, BKS, BKW, BMQ, BMK = 0, 8, 10, 12, 14, 18
BDQ, BDK = 22, 26
BVC, BVS, BVW, BNZ, BDV, BDZ, BMV, BMZ = 30, 32, 34, 36, 44, 48, 52, 56
BGATE = 60
NBLK = 62
NPROJ = NBLK * LANES
_W_IN_SEGMENTS = (
    (0, 1024), (1024, 1280), (1536, 1792), (2048, 2304), (5656, 6168), (6168, 6680),
    (3608, 4120), (4120, 4632),
    (1280, 1536), (1792, 2048), (2304, 2560), (2584, 3608), (4632, 5144), (5144, 5656),
    (6680, 7192), (7192, 7704),
    (2560, 2584),
)


def _nt(a, b):
    return lax.dot_general(a, b, (((1,), (1,)), ((), ())), preferred_element_type=F32)


def _mm(a, b):
    return jnp.dot(a, b, preferred_element_type=F32)


def _silu(z):
    return z * jax.nn.sigmoid(z)


def _cparams(sem):
    return pltpu.CompilerParams(dimension_semantics=sem, vmem_limit_bytes=VMEM_LIMIT_BYTES)


def _norm_matmul_body(*refs, tn, splits, rope):
    if rope:
        x_ref, g_ref, w_ref, c1, a1, b1, c2, a2, b2, o_ref, h_ref = refs
    else:
        x_ref, g_ref, w_ref, o_ref, h_ref = refs
    j = pl.program_id(1)

    @pl.when(j == 0)
    def _():
        x = x_ref[...]
        ms = jnp.mean(x * x, axis=-1, keepdims=True)
        h_ref[...] = (x * lax.rsqrt(ms + NORM_EPS) * g_ref[...]).astype(BF16)

    acc = _mm(h_ref[...], w_ref[...])
    if not rope:
        o_ref[...] = acc
        return
    t1, t2, t3 = splits

    def rotate(c, a, b, half):
        for k in range(tn // LANES):
            blk = acc[:, k * LANES:(k + 1) * LANES]
            o_ref[:, k * LANES:(k + 1) * LANES] = (
                blk * c[...] + pltpu.roll(blk, LANES - half, 1) * a[...] + pltpu.roll(blk, half, 1) * b[...])

    @pl.when(j < t1)
    def _():
        rotate(c1, a1, b1, HEAD_DIM // 8)

    @pl.when((j >= t1) & (j < t2))
    def _():
        rotate(c2, a2, b2, DIFF_QK_DIM // 8)

    @pl.when((j >= t2) & (j < t3))
    def _():
        o_ref[...] = acc

    @pl.when(j >= t3)
    def _():
        o_ref[...] = jax.nn.sigmoid(acc)


def _norm_matmul(x, gain, w, tabs=None, *, tm, tn):
    M, K = x.shape
    Np = w.shape[1]
    rope = tabs is not None
    splits = (BDQ * LANES // tn, BVC * LANES // tn, BGATE * LANES // tn)
    in_specs = [
        pl.BlockSpec((tm, K), lambda i, j: (i, 0)),
        pl.BlockSpec((1, K), lambda i, j: (0, 0)),
        pl.BlockSpec((K, tn), lambda i, j: (0, j)),
    ]
    args = [x, gain.reshape(1, K), w]
    if rope:
        period = tabs[0].shape[0] // tm
        for t in tabs:
            in_specs.append(pl.BlockSpec((tm, LANES), lambda i, j: (i % period, 0)))
            args.append(t)
    return pl.pallas_call(
        functools.partial(_norm_matmul_body, tn=tn, splits=splits, rope=rope),
        grid=(M // tm, Np // tn),
        in_specs=in_specs,
        out_specs=pl.BlockSpec((tm, tn), lambda i, j: (i, j)),
        out_shape=jax.ShapeDtypeStruct((M, Np), F32),
        scratch_shapes=[pltpu.VMEM((tm, K), BF16)],
        compiler_params=_cparams(("parallel", "arbitrary")),
        name="norm_matmul",
    )(*args)


def _rope_tables(pos):
    posf = pos.astype(F32)[:, None]
    n = pos.shape[0]

    def one(width, reps):
        rd = width // 4
        half = rd // 2
        inv = ROPE_THETA ** (-2.0 * jnp.arange(half, dtype=F32) / rd)
        ang = posf * inv[None, :]
        c, s = jnp.cos(ang), jnp.sin(ang)
        z = jnp.zeros((n, width - rd), F32)
        zh = jnp.zeros((n, half), F32)
        C = jnp.concatenate([c, c, jnp.ones((n, width - rd), F32)], axis=1)
        A = jnp.concatenate([-s, zh, z], axis=1)
        B = jnp.concatenate([zh, s, z], axis=1)
        return [jnp.tile(t, (1, reps)) for t in (C, A, B)]

    return one(HEAD_DIM, 1) + one(DIFF_QK_DIM, 2)


def _online_init(m_ref, l_ref, acc_ref):
    m_ref[...] = jnp.full(m_ref.shape, NEG, F32)
    l_ref[...] = jnp.zeros(l_ref.shape, F32)
    acc_ref[...] = jnp.zeros(acc_ref.shape, F32)


def _online_update(s, v, m_ref, l_ref, acc_ref):
    m_prev = m_ref[...]
    m_new = jnp.maximum(m_prev, jnp.max(s, axis=-1, keepdims=True))
    alpha = jnp.exp(m_prev - m_new)
    p = jnp.exp(s - m_new)
    l_ref[...] = alpha * l_ref[...] + jnp.sum(p, axis=-1, keepdims=True)
    acc_ref[...] = alpha * acc_ref[...] + _mm(p.astype(BF16), v)
    m_ref[...] = m_new


def _masked_softmax(s, mask):
    m = jnp.max(jnp.where(mask, s, NEG), axis=-1, keepdims=True)
    m = jnp.where(m > 0.5 * NEG, m, 0.0)
    e = jnp.where(mask, jnp.exp(s - m), 0.0)
    return e / jnp.maximum(jnp.sum(e, axis=-1, keepdims=True), 1e-30)


def _rank_desc(score, ncand):
    lane = lax.broadcasted_iota(jnp.int32, score.shape, 1)
    rank = jnp.zeros(score.shape, F32)
    for c in range(ncand):
        col = score[:, c:c + 1]
        before = (col > score) | ((col == score) & (c < lane))
        rank = rank + before.astype(F32)
    return rank


def _lane_pick(x, idx):
    lane = lax.broadcasted_iota(jnp.int32, x.shape, 1)
    return jnp.sum(jnp.where(lane == idx, x, 0.0), axis=-1, keepdims=True)


def _compress(src_ref, pe_ref, w1_ref, w2_ref, nc):
    half = NSA_CMP_STRIDE * HEAD_DIM
    acc_a = jnp.zeros((nc, LANES), F32)
    acc_b = jnp.zeros((nc, LANES), F32)
    for j in range(NSA_CMP_STRIDE):
        rows = src_ref[pl.ds(j, nc, stride=NSA_CMP_STRIDE), :].astype(BF16)
        acc_a = acc_a + _mm(rows, w1_ref[j * LANES:(j + 1) * LANES, :].astype(BF16))
        acc_b = acc_b + _mm(rows, w1_ref[half + j * LANES:half + (j + 1) * LANES, :].astype(BF16))
    return _compress_finish(acc_a, acc_b, pe_ref, w1_ref, w2_ref, nc)


def _pe_term(pe_ref, w1_ref):
    acc = jnp.zeros((8, LANES), F32)
    for j in range(2 * NSA_CMP_STRIDE):
        pj = jnp.broadcast_to(pe_ref[j:j + 1, :], (8, LANES)).astype(BF16)
        acc = acc + _mm(pj, w1_ref[j * LANES:(j + 1) * LANES, :].astype(BF16))
    return acc[0:1, :]


def _compress_finish(acc_a, acc_b, pe_ref, w1_ref, w2_ref, nc):
    row = lax.broadcasted_iota(jnp.int32, (nc, LANES), 0)
    hb = jnp.where(row < nc - 1, pltpu.roll(acc_b, nc - 1, 0), 0.0)
    hid = _silu(acc_a + hb + _pe_term(pe_ref, w1_ref))
    return _mm(hid.astype(BF16), w2_ref[...].astype(BF16))


def _slc_weights(nc):
    r = lax.broadcasted_iota(jnp.int32, (nc, LANES), 0)
    c = lax.broadcasted_iota(jnp.int32, (nc, LANES), 1)
    d = r - 4 * c
    return jnp.where((d == -1) | (d == 3), 1.0, jnp.where((d >= 0) & (d <= 2), 2.0, 0.0)).astype(F32)


def _nsa_prompt_body(q_ref, kc_ref, ks_ref, kw_ref, vc_ref, vs_ref, vw_ref, gate_ref,
                     pek_ref, w1k_ref, w2k_ref, pev_ref, w1v_ref, w2v_ref, o_ref,
                     kcs, vcs, kaug, vsb, kwb, vwb, m_ref, l_ref, acc_ref, *, T, tq):
    g = pl.program_id(1)
    i = pl.program_id(2)
    nc = T // NSA_CMP_STRIDE
    nsb = T // NSA_SLC_LEN
    nsel = min(NSA_TOPN, nsb)
    R = NSA_REP * tq
    scale = HEAD_DIM ** -0.5

    @pl.when(i == 0)
    def _():
        kcs[...] = _compress(kc_ref, pek_ref, w1k_ref, w2k_ref, nc).astype(BF16)
        vcs[...] = _compress(vc_ref, pev_ref, w1v_ref, w2v_ref, nc).astype(BF16)
        kaug[:, :LANES] = ks_ref[...].astype(BF16)
        rblk = lax.broadcasted_iota(jnp.int32, (T, LANES), 0) // NSA_SLC_LEN
        lane = lax.broadcasted_iota(jnp.int32, (T, LANES), 1)
        kaug[:, LANES:] = (rblk == lane).astype(BF16)
        vsb[...] = vs_ref[...].astype(BF16)
        kwb[...] = kw_ref[...].astype(BF16)
        vwb[...] = vw_ref[...].astype(BF16)

    q = q_ref[...]
    Q = jnp.concatenate([q[:, h * LANES:(h + 1) * LANES] for h in range(NSA_REP)], axis=0).astype(BF16)

    s = _nt(Q, kcs[...]) * scale
    col = lax.broadcasted_iota(jnp.int32, (R, nc), 1)
    t_row = i * tq + lax.broadcasted_iota(jnp.int32, (R, nc), 0) % tq
    pc = _masked_softmax(s, (NSA_CMP_STRIDE * col + 2 * NSA_CMP_STRIDE - 1) <= t_row)
    o_cmp = _mm(pc.astype(BF16), vcs[...])
    pg = pc[0:tq]
    for h in range(1, NSA_REP):
        pg = pg + pc[h * tq:(h + 1) * tq]

    slc = jnp.dot(pg, _slc_weights(nc), precision=lax.Precision.HIGHEST, preferred_element_type=F32)
    blk = lax.broadcasted_iota(jnp.int32, (tq, LANES), 1)
    tb = (i * tq + lax.broadcasted_iota(jnp.int32, (tq, LANES), 0)) // NSA_SLC_LEN
    valid = blk <= tb
    forced = (blk == 0) | (blk == tb) | (blk == tb - 1)
    score = jnp.where(valid, slc + jnp.where(forced, NSA_FORCE_BONUS, 0.0), -jnp.inf)
    sel = (_rank_desc(score, nsb) < nsel) & valid
    bias = jnp.where(sel, 0.0, NEG).astype(BF16)
    Qaug = jnp.concatenate([Q, jnp.concatenate([bias] * NSA_REP, axis=0)], axis=1)

    lane_k = lax.broadcasted_iota(jnp.int32, (R, tq), 1)
    rloc = lax.broadcasted_iota(jnp.int32, (R, tq), 0) % tq

    _online_init(m_ref, l_ref, acc_ref)

    def slc_step(kt, carry):
        off = pl.multiple_of(kt * tq, tq)
        s2 = _nt(Qaug, kaug[pl.ds(off, tq), :]) * scale
        _online_update(s2, vsb[pl.ds(off, tq), :], m_ref, l_ref, acc_ref)
        return carry

    lax.fori_loop(0, i, slc_step, 0)
    off_d = pl.multiple_of(i * tq, tq)
    s2 = _nt(Qaug, kaug[pl.ds(off_d, tq), :]) * scale
    _online_update(jnp.where(lane_k <= rloc, s2, NEG), vsb[pl.ds(off_d, tq), :], m_ref, l_ref, acc_ref)
    o_slc = acc_ref[...] / l_ref[...]

    _online_init(m_ref, l_ref, acc_ref)
    nw = NSA_WINDOW // tq
    for d in range(nw, -1, -1):
        @pl.when(i >= d)
        def _(d=d):
            off = pl.multiple_of((i - d) * tq, tq)
            s3 = _nt(Q, kwb[pl.ds(off, tq), :]) * scale
            if d == nw:
                s3 = jnp.where(lane_k > rloc, s3, NEG)
            if d == 0:
                s3 = jnp.where(lane_k <= rloc, s3, NEG)
            _online_update(s3, vwb[pl.ds(off, tq), :], m_ref, l_ref, acc_ref)
    o_win = acc_ref[...] / l_ref[...]

    gt = gate_ref[...]

    def gate_col(branch):
        return jnp.concatenate(
            [_lane_pick(gt, branch * NSA_HEADS + NSA_REP * g + h) for h in range(NSA_REP)], axis=0)

    out = gate_col(0) * o_cmp + gate_col(1) * o_slc + gate_col(2) * o_win
    for h in range(NSA_REP):
        o_ref[:, h * LANES:(h + 1) * LANES] = out[h * tq:(h + 1) * tq]


def _nsa_prompt(proj, cw, *, B, T, tq):
    nqt = T // tq
    nc = T // NSA_CMP_STRIDE
    R = NSA_REP * tq
    assert T % NSA_SLC_LEN == 0 and NSA_WINDOW % tq == 0 and T // NSA_SLC_LEN <= LANES

    def kv_spec(blk):
        return pl.BlockSpec((T, LANES), lambda b, g, i: (b, blk + g))

    def full(a):
        return pl.BlockSpec(a.shape, lambda b, g, i: (0,) * a.ndim)

    in_specs = [
        pl.BlockSpec((tq, NSA_REP * LANES), lambda b, g, i: (b * nqt + i, g)),
        kv_spec(BKC), kv_spec(BKS), kv_spec(BKW), kv_spec(BVC), kv_spec(BVS), kv_spec(BVW),
        pl.BlockSpec((tq, LANES), lambda b, g, i: (b * nqt + i, BGATE)),
    ] + [full(a) for a in cw]
    return pl.pallas_call(
        functools.partial(_nsa_prompt_body, T=T, tq=tq),
        grid=(B, NSA_GROUPS, nqt),
        in_specs=in_specs,
        out_specs=pl.BlockSpec((tq, NSA_REP * LANES), lambda b, g, i: (b * nqt + i, g)),
        out_shape=jax.ShapeDtypeStruct((B * T, NSA_HEADS * LANES), F32),
        scratch_shapes=[
            pltpu.VMEM((nc, LANES), BF16), pltpu.VMEM((nc, LANES), BF16),
            pltpu.VMEM((T, 2 * LANES), BF16), pltpu.VMEM((T, LANES), BF16),
            pltpu.VMEM((T, LANES), BF16), pltpu.VMEM((T, LANES), BF16),
            pltpu.VMEM((R, 1), F32), pltpu.VMEM((R, 1), F32), pltpu.VMEM((R, LANES), F32),
        ],
        compiler_params=_cparams(("parallel", "parallel", "arbitrary")),
        name="nsa_prompt",
    )(proj, proj, proj, proj, proj, proj, proj, proj, *cw)


def _diff_lambda(lam_ref, lam_init):
    lp = lam_ref[...]
    return (jnp.exp(jnp.sum(lp[0:1] * lp[1:2], axis=-1, keepdims=True))
            - jnp.exp(jnp.sum(lp[2:3] * lp[3:4], axis=-1, keepdims=True)) + lam_init)


def _diff_finish(o1, o2, lam_ref, sub_ref, lam_init):
    a = o1 - _diff_lambda(lam_ref, lam_init) * o2
    ms = jnp.mean(a * a, axis=-1, keepdims=True)
    return a * lax.rsqrt(ms + NORM_EPS) * sub_ref[...] * (1.0 - lam_init)


def _split_halves(q):
    lane = lax.broadcasted_iota(jnp.int32, q.shape, 1)
    return jnp.concatenate([jnp.where(lane < DIFF_QK_DIM, q, 0.0), jnp.where(lane >= DIFF_QK_DIM, q, 0.0)], axis=0)


def _diff_prompt_body(q_ref, k_ref, v_ref, lam_ref, sub_ref, o_ref, kb, vb, m_ref, l_ref, acc_ref,
                      *, tq, lam_init):
    i = pl.program_id(2)
    scale = DIFF_QK_DIM ** -0.5

    @pl.when(i == 0)
    def _():
        kb[...] = k_ref[...].astype(BF16)
        vb[...] = v_ref[...].astype(BF16)

    Q = _split_halves(q_ref[...]).astype(BF16)
    R = 2 * tq
    lane_k = lax.broadcasted_iota(jnp.int32, (R, tq), 1)
    rloc = lax.broadcasted_iota(jnp.int32, (R, tq), 0) % tq
    _online_init(m_ref, l_ref, acc_ref)

    def step(kt, carry):
        off = pl.multiple_of(kt * tq, tq)
        _online_update(_nt(Q, kb[pl.ds(off, tq), :]) * scale, vb[pl.ds(off, tq), :], m_ref, l_ref, acc_ref)
        return carry

    lax.fori_loop(0, i, step, 0)
    off = pl.multiple_of(i * tq, tq)
    s = _nt(Q, kb[pl.ds(off, tq), :]) * scale
    _online_update(jnp.where(lane_k <= rloc, s, NEG), vb[pl.ds(off, tq), :], m_ref, l_ref, acc_ref)
    o = acc_ref[...] / l_ref[...]
    o_ref[...] = _diff_finish(o[:tq], o[tq:], lam_ref, sub_ref, lam_init)


def _diff_prompt(proj, lam, subln, *, B, T, tq, lam_init):
    nqt = T // tq
    return pl.pallas_call(
        functools.partial(_diff_prompt_body, tq=tq, lam_init=lam_init),
        grid=(B, DIFF_HEADS, nqt),
        in_specs=[
            pl.BlockSpec((tq, LANES), lambda b, h, i: (b * nqt + i, BDQ + h)),
            pl.BlockSpec((T, LANES), lambda b, h, i: (b, BDK + h)),
            pl.BlockSpec((T, LANES), lambda b, h, i: (b, BDV + h)),
            pl.BlockSpec(lam.shape, lambda b, h, i: (0, 0)),
            pl.BlockSpec((1, LANES), lambda b, h, i: (0, 0)),
        ],
        out_specs=pl.BlockSpec((tq, LANES), lambda b, h, i: (b * nqt + i, h)),
        out_shape=jax.ShapeDtypeStruct((B * T, DIFF_HEADS * LANES), F32),
        scratch_shapes=[
            pltpu.VMEM((T, LANES), BF16), pltpu.VMEM((T, LANES), BF16),
            pltpu.VMEM((2 * tq, 1), F32), pltpu.VMEM((2 * tq, 1), F32), pltpu.VMEM((2 * tq, LANES), F32),
        ],
        compiler_params=_cparams(("parallel", "parallel", "arbitrary")),
        name="diff_prompt",
    )(proj, proj, proj, lam, subln.reshape(1, LANES))


def _moba_prompt_body(q_ref, k_ref, v_ref, o_ref, kb, vb, km, m_ref, l_ref, acc_ref, *, T):
    i = pl.program_id(2)
    tq = MOBA_BLOCK
    nb = T // MOBA_BLOCK
    scale = HEAD_DIM ** -0.5

    @pl.when(i == 0)
    def _():
        kb[...] = k_ref[...].astype(BF16)
        vb[...] = v_ref[...].astype(BF16)
        km[...] = jnp.zeros(km.shape, BF16)
        for j in range(nb):
            km[j:j + 1, :] = jnp.mean(k_ref[j * tq:(j + 1) * tq, :], axis=0, keepdims=True).astype(BF16)

    Q = q_ref[...].astype(BF16)
    blk = lax.broadcasted_iota(jnp.int32, (tq, LANES), 1)
    past = blk < i
    score = jnp.where(past, _nt(Q, km[...]), -jnp.inf)
    sel = (_rank_desc(score, nb) < min(MOBA_TOPK, nb)) & past
    bias = jnp.where(sel, 0.0, NEG)
    lane_k = lax.broadcasted_iota(jnp.int32, (tq, tq), 1)
    rloc = lax.broadcasted_iota(jnp.int32, (tq, tq), 0)
    _online_init(m_ref, l_ref, acc_ref)

    def step(kt, carry):
        off = pl.multiple_of(kt * tq, tq)
        s = _nt(Q, kb[pl.ds(off, tq), :]) * scale + _lane_pick(bias, kt)
        _online_update(s, vb[pl.ds(off, tq), :], m_ref, l_ref, acc_ref)
        return carry

    lax.fori_loop(0, i, step, 0)
    off = pl.multiple_of(i * tq, tq)
    s = _nt(Q, kb[pl.ds(off, tq), :]) * scale
    _online_update(jnp.where(lane_k <= rloc, s, NEG), vb[pl.ds(off, tq), :], m_ref, l_ref, acc_ref)
    o_ref[...] = acc_ref[...] / l_ref[...]


def _moba_prompt(proj, *, B, T):
    tq = MOBA_BLOCK
    assert T % tq == 0 and T // tq <= LANES
    nqt = T // tq
    return pl.pallas_call(
        functools.partial(_moba_prompt_body, T=T),
        grid=(B, MOBA_HEADS, nqt),
        in_specs=[
            pl.BlockSpec((tq, LANES), lambda b, h, i: (b * nqt + i, BMQ + h)),
            pl.BlockSpec((T, LANES), lambda b, h, i: (b, BMK + h)),
            pl.BlockSpec((T, LANES), lambda b, h, i: (b, BMV + h)),
        ],
        out_specs=pl.BlockSpec((tq, LANES), lambda b, h, i: (b * nqt + i, h)),
        out_shape=jax.ShapeDtypeStruct((B * T, MOBA_HEADS * LANES), F32),
        scratch_shapes=[
            pltpu.VMEM((T, LANES), BF16), pltpu.VMEM((T, LANES), BF16), pltpu.VMEM((LANES, LANES), BF16),
            pltpu.VMEM((tq, 1), F32), pltpu.VMEM((tq, 1), F32), pltpu.VMEM((tq, LANES), F32),
        ],
        compiler_params=_cparams(("parallel", "parallel", "arbitrary")),
        name="moba_prompt",
    )(proj, proj, proj)


def _out_mem_body(an_ref, ad_ref, am_ref, nz0_ref, nz1_ref, dz_ref, mz_ref, x_ref, wout_ref, gmem_ref,
                  wq_ref, mkv_ref, wo_ref, o_ref):
    half = NSA_HEADS * LANES // 2
    an = an_ref[...]
    mixed = jnp.concatenate([
        an[:, :half] * _silu(nz0_ref[...]), an[:, half:] * _silu(nz1_ref[...]),
        ad_ref[...] * _silu(dz_ref[...]), am_ref[...] * _silu(mz_ref[...])], axis=1).astype(BF16)
    x1 = x_ref[...] + _mm(mixed, wout_ref[...])
    ms = jnp.mean(x1 * x1, axis=-1, keepdims=True)
    h2 = (x1 * lax.rsqrt(ms + NORM_EPS) * gmem_ref[...]).astype(BF16)
    q = _mm(h2, wq_ref[...])
    mkv = mkv_ref[...]
    scale = MEM_DIM ** -0.5
    outs = []
    for hh in range(MEM_HEADS):
        qh = q[:, hh * LANES:(hh + 1) * LANES].astype(BF16)
        kh = mkv[:, hh * LANES:(hh + 1) * LANES].astype(BF16)
        vh = mkv[:, (MEM_HEADS + hh) * LANES:(MEM_HEADS + hh + 1) * LANES].astype(BF16)
        s = _nt(qh, kh) * scale
        e = jnp.exp(s - jnp.max(s, axis=-1, keepdims=True))
        p = e / jnp.sum(e, axis=-1, keepdims=True)
        outs.append(_mm(p.astype(BF16), vh))
    oc = jnp.concatenate(outs, axis=1).astype(BF16)
    o_ref[...] = x1 + _mm(oc, wo_ref[...])


def _out_mem(a_nsa, a_diff, a_moba, proj3, x3, wout, gmem, wq, mkv, wo, *, tm):
    nb, rows, D = x3.shape
    grid = (nb, rows // tm)

    def row(cols, cb):
        return pl.BlockSpec((None, tm, cols), lambda b, i: (b, i, cb))

    def const(a):
        return pl.BlockSpec(a.shape, lambda b, i: (0,) * a.ndim, pipeline_mode=pl.Buffered(1))

    zc = 4 * LANES
    return pl.pallas_call(
        _out_mem_body,
        grid=grid,
        in_specs=[
            row(NSA_HEADS * LANES, 0), row(DIFF_HEADS * LANES, 0), row(MOBA_HEADS * LANES, 0),
            row(zc, BNZ * LANES // zc), row(zc, BNZ * LANES // zc + 1), row(zc, BDZ * LANES // zc),
            row(zc, BMZ * LANES // zc),
            row(D, 0), const(wout), const(gmem), const(wq),
            pl.BlockSpec((None, MEM_TOKENS, 2 * MEM_HEADS * MEM_DIM), lambda b, i: (b, 0, 0)),
            const(wo),
        ],
        out_specs=row(D, 0),
        out_shape=jax.ShapeDtypeStruct(x3.shape, F32),
        compiler_params=_cparams(("parallel", "parallel")),
        name="out_mem",
    )(a_nsa, a_diff, a_moba, proj3, proj3, proj3, proj3, x3, wout, gmem, wq, mkv, wo)


def _rmsnorm_body(x_ref, g_ref, o_ref):
    x = x_ref[...]
    ms = jnp.mean(x * x, axis=-1, keepdims=True)
    o_ref[...] = x * lax.rsqrt(ms + NORM_EPS) * g_ref[...]


def _rmsnorm(x, gain, *, tm):
    M, D = x.shape
    return pl.pallas_call(
        _rmsnorm_body,
        grid=(M // tm,),
        in_specs=[pl.BlockSpec((tm, D), lambda i: (i, 0)), pl.BlockSpec((1, D), lambda i: (0, 0))],
        out_specs=pl.BlockSpec((tm, D), lambda i: (i, 0)),
        out_shape=jax.ShapeDtypeStruct((M, D), F32),
        compiler_params=_cparams(("parallel",)),
        name="final_norm",
    )(x, gain.reshape(1, D))


def _regroup_w_in(w_in_l):
    K = w_in_l.shape[0]
    used = sum(b - a for a, b in _W_IN_SEGMENTS)
    cols = [w_in_l[:, a:b] for a, b in _W_IN_SEGMENTS] + [jnp.zeros((K, NPROJ - used), w_in_l.dtype)]
    return jnp.concatenate(cols, axis=1).astype(BF16)


def _cols(proj, blk, n):
    return proj[..., blk * LANES:(blk + n) * LANES]


def _cache_rows(proj3, kblk, vblk, heads):
    nb, rows, _ = proj3.shape
    kv = jnp.concatenate([_cols(proj3, kblk, heads), _cols(proj3, vblk, heads)], axis=-1)
    return kv.reshape(nb, rows, 2, heads, HEAD_DIM)


def _lam_init(l):
    return 0.8 - 0.6 * math.exp(-0.3 * l)


def _prompt_trunk(x_prompt, mem_prompt, P, *, tq=256):
    B, T, D = x_prompt.shape
    depth = P["w_in"].shape[0]
    N = B * T
    tabs = _rope_tables(jnp.arange(T, dtype=jnp.int32))
    tm = min(1024, T)
    memx = mem_prompt.reshape(B * MEM_TOKENS, D)
    x = x_prompt.reshape(N, D)
    rows = [[] for _ in range(5)]
    mem_rows = []
    for l in range(depth):
        mkv = _norm_matmul(memx, P["norm_mem_m"][l], P["w_mem_kv"][l].astype(BF16), tm=min(512, B * MEM_TOKENS),
                           tn=256)
        mkv3 = mkv.reshape(B, MEM_TOKENS, 2 * MEM_HEADS * MEM_DIM)
        mem_rows.append(mkv3.reshape(B, MEM_TOKENS, 2, MEM_HEADS, MEM_DIM))
        proj = _norm_matmul(x, P["norm_mix"][l], _regroup_w_in(P["w_in"][l]), tabs, tm=tm, tn=256)
        proj3 = proj.reshape(B, T, NPROJ)
        rows[0].append(_cache_rows(proj3, BKC, BVC, NSA_GROUPS))
        rows[1].append(_cache_rows(proj3, BKS, BVS, NSA_GROUPS))
        rows[2].append(_cache_rows(proj3, BKW, BVW, NSA_GROUPS)[:, T - min(NSA_WINDOW, T):])
        rows[3].append(_cache_rows(proj3, BDK, BDV, DIFF_HEADS))
        rows[4].append(_cache_rows(proj3, BMK, BMV, MOBA_HEADS))
        cw = (P["nsa_pe_k"][l], P["nsa_w1_k"][l], P["nsa_w2_k"][l],
              P["nsa_pe_v"][l], P["nsa_w1_v"][l], P["nsa_w2_v"][l])
        a_nsa = _nsa_prompt(proj, cw, B=B, T=T, tq=tq)
        a_diff = _diff_prompt(proj, P["diff_lambda"][l], P["diff_subln"][l], B=B, T=T, tq=tq, lam_init=_lam_init(l))
        a_moba = _moba_prompt(proj, B=B, T=T)
        x3 = _out_mem(a_nsa.reshape(B, T, -1), a_diff.reshape(B, T, -1), a_moba.reshape(B, T, -1), proj3,
                      x.reshape(B, T, D), P["w_out"][l].astype(BF16), P["norm_mem_x"][l].reshape(1, D),
                      P["w_mem_q"][l].astype(BF16), mkv3, P["w_mem_o"][l].astype(BF16), tm=min(256, T))
        x = x3.reshape(N, D)
    y = _rmsnorm(x, P["norm_final"], tm=min(512, N)).reshape(B, T, D)
    return y, [jnp.stack(r, axis=1) for r in rows], jnp.stack(mem_rows, axis=1)


DEC_ROWS = 16
DEC_POS = 8


def _cmp_scan_body(pt_ref, *refs, P):
    pages = refs[:P]
    w_ref, o_ref = refs[P], refs[P + 1]
    rows_per_page = PAGE_SIZE // NSA_CMP_STRIDE
    row_w = 2 * NSA_GROUPS * HEAD_DIM
    for kvg in range(2 * NSA_GROUPS):
        acc = jnp.zeros((rows_per_page * P, 2 * LANES), F32)
        for j in range(NSA_CMP_STRIDE):
            lo = j * row_w + kvg * LANES
            lhs = jnp.concatenate([pg[:, lo:lo + LANES] for pg in pages], axis=0).astype(BF16)
            acc = acc + _mm(lhs, w_ref[kvg // NSA_GROUPS, j])
        o_ref[:, kvg * 2 * LANES:(kvg + 1) * 2 * LANES] = acc


def _cmp_scan(cache, page_table, w1ab, l, *, P):
    n_pool, depth = cache.shape[:2]
    Bd, n_pages = page_table.shape
    rpp = PAGE_SIZE // NSA_CMP_STRIDE
    row_w = 2 * NSA_GROUPS * HEAD_DIM
    view = cache.reshape(n_pool, depth, rpp, NSA_CMP_STRIDE * row_w)
    assert n_pages % P == 0

    def page_map(b, s, pt, r):
        return (pt[b, s * P + r], l, 0, 0)

    in_specs = [pl.BlockSpec((None, None, rpp, NSA_CMP_STRIDE * row_w), functools.partial(page_map, r=r))
                for r in range(P)]
    in_specs.append(pl.BlockSpec(w1ab.shape, lambda b, s, pt: (0, 0, 0, 0)))
    ncol = 2 * NSA_GROUPS * 2 * LANES
    return pl.pallas_call(
        functools.partial(_cmp_scan_body, P=P),
        grid_spec=pltpu.PrefetchScalarGridSpec(
            num_scalar_prefetch=1, grid=(Bd, n_pages // P), in_specs=in_specs,
            out_specs=pl.BlockSpec((None, rpp * P, ncol), lambda b, s, pt: (b, s, 0))),
        out_shape=jax.ShapeDtypeStruct((Bd, rpp * n_pages, ncol), F32),
        compiler_params=_cparams(("parallel", "arbitrary")),
        name="cmp_scan",
    )(page_table, *([view] * P), w1ab)


def _slc_weights2(nc, ncols):
    r = lax.broadcasted_iota(jnp.int32, (nc, ncols), 0)
    c = lax.broadcasted_iota(jnp.int32, (nc, ncols), 1)
    d = r - 4 * c
    return jnp.where((d == -1) | (d == 3), 1.0, jnp.where((d >= 0) & (d <= 2), 2.0, 0.0)).astype(F32)


def _nsa_dec_select_body(ab_ref, q_ref, pek_ref, w1k_ref, w2k_ref, pev_ref, w1v_ref, w2v_ref,
                         ocmp_ref, bias_ref, *, past):
    nch = past // NSA_CMP_STRIDE
    nsbp = past // NSA_SLC_LEN
    R = NSA_REP * DEC_POS
    scale = HEAD_DIM ** -0.5
    for g in range(NSA_GROUPS):
        def comp(kv, pe_ref, w1_ref, w2_ref):
            lo = (kv * NSA_GROUPS + g) * 2 * LANES
            return _compress_finish(ab_ref[:, lo:lo + LANES], ab_ref[:, lo + LANES:lo + 2 * LANES],
                                    pe_ref, w1_ref, w2_ref, nch).astype(BF16)
        kc = comp(0, pek_ref, w1k_ref, w2k_ref)
        vc = comp(1, pev_ref, w1v_ref, w2v_ref)
        Q = q_ref[g].astype(BF16)
        s = _nt(Q, kc) * scale
        col = lax.broadcasted_iota(jnp.int32, (R, nch), 1)
        qpos = past + lax.broadcasted_iota(jnp.int32, (R, nch), 0) % DEC_POS
        pc = _masked_softmax(s, (NSA_CMP_STRIDE * col + 2 * NSA_CMP_STRIDE - 1) <= qpos)
        ocmp_ref[g] = _mm(pc.astype(BF16), vc)
        pg = pc[0:DEC_POS]
        for h in range(1, NSA_REP):
            pg = pg + pc[h * DEC_POS:(h + 1) * DEC_POS]
        slc = jnp.dot(pg, _slc_weights2(nch, nsbp), precision=lax.Precision.HIGHEST, preferred_element_type=F32)
        blk = lax.broadcasted_iota(jnp.int32, (DEC_POS, nsbp), 1)
        forced = (blk == 0) | (blk == nsbp - 1)
        score = slc + jnp.where(forced, NSA_FORCE_BONUS, 0.0)
        rank = _rank_desc(score, nsbp) + jnp.where(forced, 0.0, 1.0)
        b8 = jnp.where(rank < min(NSA_TOPN, nsbp + 1), 0.0, NEG)
        bias_ref[g] = jnp.concatenate([b8] * NSA_REP, axis=0)


def _nsa_dec_select(ab, qn, cw, *, past):
    Bd = ab.shape[0]
    R = NSA_REP * DEC_POS
    nsbp = past // NSA_SLC_LEN

    def full(a):
        return pl.BlockSpec(a.shape, lambda b: (0,) * a.ndim)

    return pl.pallas_call(
        functools.partial(_nsa_dec_select_body, past=past),
        grid=(Bd,),
        in_specs=[pl.BlockSpec((None,) + ab.shape[1:], lambda b: (b, 0, 0)),
                  pl.BlockSpec((None, NSA_GROUPS, R, LANES), lambda b: (b, 0, 0, 0))] + [full(a) for a in cw],
        out_specs=[pl.BlockSpec((None, NSA_GROUPS, R, LANES), lambda b: (b, 0, 0, 0)),
                   pl.BlockSpec((None, NSA_GROUPS, R, nsbp), lambda b: (b, 0, 0, 0))],
        out_shape=[jax.ShapeDtypeStruct((Bd, NSA_GROUPS, R, LANES), F32),
                   jax.ShapeDtypeStruct((Bd, NSA_GROUPS, R, nsbp), F32)],
        compiler_params=_cparams(("parallel",)),
        name="nsa_dec_select",
    )(ab, qn, *cw)


def _moba_kmean_body(pt_ref, *refs, P):
    pages, o_ref = refs[:P], refs[P]
    ppb = MOBA_BLOCK // PAGE_SIZE
    for r in range(P // ppb):
        tot = jnp.sum(pages[ppb * r][...], axis=0, keepdims=True)
        for e in range(1, ppb):
            tot = tot + jnp.sum(pages[ppb * r + e][...], axis=0, keepdims=True)
        o_ref[r:r + 1, :] = tot / MOBA_BLOCK


def _moba_kmean(cache, page_table, l, *, P):
    n_pool, depth = cache.shape[:2]
    Bd, n_pages = page_table.shape
    kw = MOBA_HEADS * HEAD_DIM
    view = cache.reshape(n_pool, depth, PAGE_SIZE, 2 * kw)
    ppb = MOBA_BLOCK // PAGE_SIZE
    assert n_pages % P == 0 and (P // ppb) % 8 == 0

    def page_map(b, s, pt, r):
        return (pt[b, s * P + r], l, 0, 0)

    return pl.pallas_call(
        functools.partial(_moba_kmean_body, P=P),
        grid_spec=pltpu.PrefetchScalarGridSpec(
            num_scalar_prefetch=1, grid=(Bd, n_pages // P),
            in_specs=[pl.BlockSpec((None, None, PAGE_SIZE, kw), functools.partial(page_map, r=r)) for r in range(P)],
            out_specs=pl.BlockSpec((None, P // ppb, kw), lambda b, s, pt: (b, s, 0))),
        out_shape=jax.ShapeDtypeStruct((Bd, n_pages // ppb, kw), F32),
        compiler_params=_cparams(("parallel", "arbitrary")),
        name="moba_kmean",
    )(page_table, *([view] * P))


def _moba_dec_select_body(km_ref, q_ref, bias_ref):
    nblk = km_ref.shape[0]
    for h in range(MOBA_HEADS):
        km = km_ref[:, h * LANES:(h + 1) * LANES].astype(BF16)
        score = _nt(q_ref[h].astype(BF16), km)
        bias_ref[h] = jnp.where(_rank_desc(score, nblk) < min(MOBA_TOPK, nblk), 0.0, NEG)


def _moba_dec_select(kmean, qm):
    Bd, nblk, kw = kmean.shape
    return pl.pallas_call(
        _moba_dec_select_body,
        grid=(Bd,),
        in_specs=[pl.BlockSpec((None, nblk, kw), lambda b: (b, 0, 0)),
                  pl.BlockSpec((None, MOBA_HEADS, DEC_POS, LANES), lambda b: (b, 0, 0, 0))],
        out_specs=pl.BlockSpec((None, MOBA_HEADS, DEC_POS, nblk), lambda b: (b, 0, 0, 0)),
        out_shape=jax.ShapeDtypeStruct((Bd, MOBA_HEADS, DEC_POS, nblk), F32),
        compiler_params=_cparams(("parallel",)),
        name="moba_dec_select",
    )(kmean, qm)


def _paged_attn_body(ptM_ref, ptm_ref, *refs, P, H, R, scale, blocksize, window, n_new):
    q_ref = refs[0]
    pages = refs[1:1 + P]
    rest = refs[1 + P:]
    if blocksize is not None:
        bias_ref, rest = rest[0], rest[1:]
    kn_ref, vn_ref, o_ref, m_ref, l_ref, acc_ref = rest
    s_id = pl.program_id(1)
    lane = lax.broadcasted_iota(jnp.int32, (R, PAGE_SIZE), 1)
    rpos = lax.broadcasted_iota(jnp.int32, (R, PAGE_SIZE), 0) % DEC_POS

    @pl.when(s_id == 0)
    def _():
        _online_init(m_ref, l_ref, acc_ref)

    for h in range(H):
        qh = q_ref[h].astype(BF16)
        for r in range(P):
            page = s_id * P + r
            k = pages[r][:, h * LANES:(h + 1) * LANES].astype(BF16)
            v = pages[r][:, (H + h) * LANES:(H + h + 1) * LANES].astype(BF16)
            s = _nt(qh, k) * scale
            if blocksize is not None:
                bm = bias_ref[h]
                if blocksize == PAGE_SIZE // 2:
                    s = s + jnp.where(lane < blocksize, _lane_pick(bm, 2 * page), _lane_pick(bm, 2 * page + 1))
                else:
                    s = s + _lane_pick(bm, page // (blocksize // PAGE_SIZE))
            if window:
                s = jnp.where(page * PAGE_SIZE + lane > rpos, s, NEG)
            _online_update(s, v, m_ref.at[h], l_ref.at[h], acc_ref.at[h])

    @pl.when(s_id == pl.num_programs(1) - 1)
    def _():
        for h in range(H):
            qh = q_ref[h].astype(BF16)
            kn = kn_ref[:, h * LANES:(h + 1) * LANES].astype(BF16)
            vn = vn_ref[:, h * LANES:(h + 1) * LANES].astype(BF16)
            s = _nt(qh, kn) * scale
            s = jnp.where((lane <= rpos) & (lane < n_new), s, NEG)
            _online_update(s, vn, m_ref.at[h], l_ref.at[h], acc_ref.at[h])
            o_ref[h] = acc_ref[h] / l_ref[h]


def _paged_attn(q, view5, ptM, ptm, l, knew, vnew, bias=None, *, scale, blocksize=None, window=False, P, n_new):
    Bd, H, R, _ = q.shape
    n_pages = ptM.shape[1]
    W = view5.shape[-1]
    assert n_pages % P == 0 and W == 2 * H * LANES

    def page_map(b, s, pM, pm, r):
        return (pM[b, s * P + r], l, pm[b, s * P + r], 0, 0)

    in_specs = [pl.BlockSpec((None, H, R, LANES), lambda b, s, pM, pm: (b, 0, 0, 0))]
    in_specs += [pl.BlockSpec((None, None, None, PAGE_SIZE, W), functools.partial(page_map, r=r)) for r in range(P)]
    args = [q] + [view5] * P
    if blocksize is not None:
        in_specs.append(pl.BlockSpec((None,) + bias.shape[1:], lambda b, s, pM, pm: (b, 0, 0, 0)))
        args.append(bias)
    for a in (knew, vnew):
        in_specs.append(pl.BlockSpec((None,) + a.shape[1:], lambda b, s, pM, pm: (b, 0, 0)))
        args.append(a)
    return pl.pallas_call(
        functools.partial(_paged_attn_body, P=P, H=H, R=R, scale=scale, blocksize=blocksize, window=window,
                          n_new=n_new),
        grid_spec=pltpu.PrefetchScalarGridSpec(
            num_scalar_prefetch=2, grid=(Bd, n_pages // P), in_specs=in_specs,
            out_specs=pl.BlockSpec((None, H, R, LANES), lambda b, s, pM, pm: (b, 0, 0, 0)),
            scratch_shapes=[pltpu.VMEM((H, R, 1), F32), pltpu.VMEM((H, R, 1), F32), pltpu.VMEM((H, R, LANES), F32)]),
        out_shape=jax.ShapeDtypeStruct((Bd, H, R, LANES), F32),
        compiler_params=_cparams(("parallel", "arbitrary")),
        name="paged_attn",
    )(ptM, ptm, *args)


def _dec_finalize_body(ocmp_ref, oslc_ref, owin_ref, gate_ref, odiff_ref, lam_ref, sub_ref, omoba_ref,
                       an_ref, ad_ref, am_ref, *, lam_init):
    gt = gate_ref[0:DEC_POS, :]
    an_ref[...] = jnp.zeros(an_ref.shape, F32)
    ad_ref[...] = jnp.zeros(ad_ref.shape, F32)
    am_ref[...] = jnp.zeros(am_ref.shape, F32)
    for g in range(NSA_GROUPS):
        for h in range(NSA_REP):
            head = NSA_REP * g + h
            rows = slice(h * DEC_POS, (h + 1) * DEC_POS)
            out = (_lane_pick(gt, head) * ocmp_ref[g, rows, :]
                   + _lane_pick(gt, NSA_HEADS + head) * oslc_ref[g, rows, :]
                   + _lane_pick(gt, 2 * NSA_HEADS + head) * owin_ref[g, rows, :])
            an_ref[0:DEC_POS, head * LANES:(head + 1) * LANES] = out
    for h in range(DIFF_HEADS):
        o = odiff_ref[h]
        ad_ref[0:DEC_POS, h * LANES:(h + 1) * LANES] = _diff_finish(o[:DEC_POS], o[DEC_POS:], lam_ref, sub_ref, lam_init)
    for h in range(MOBA_HEADS):
        am_ref[0:DEC_POS, h * LANES:(h + 1) * LANES] = omoba_ref[h]


def _dec_finalize(o_cmp, o_slc, o_win, proj3, o_diff, lam, subln, o_moba, *, lam_init):
    Bd = o_cmp.shape[0]

    def b4(a):
        return pl.BlockSpec((None,) + a.shape[1:], lambda b: (b, 0, 0, 0))

    outs = [(NSA_HEADS * LANES), (DIFF_HEADS * LANES), (MOBA_HEADS * LANES)]
    return pl.pallas_call(
        functools.partial(_dec_finalize_body, lam_init=lam_init),
        grid=(Bd,),
        in_specs=[b4(o_cmp), b4(o_slc), b4(o_win),
                  pl.BlockSpec((None, DEC_ROWS, LANES), lambda b: (b, 0, BGATE)),
                  b4(o_diff), pl.BlockSpec(lam.shape, lambda b: (0, 0)), pl.BlockSpec((1, LANES), lambda b: (0, 0)),
                  b4(o_moba)],
        out_specs=[pl.BlockSpec((None, DEC_ROWS, c), lambda b: (b, 0, 0)) for c in outs],
        out_shape=[jax.ShapeDtypeStruct((Bd, DEC_ROWS, c), F32) for c in outs],
        compiler_params=_cparams(("parallel",)),
        name="dec_finalize",
    )(o_cmp, o_slc, o_win, proj3, o_diff, lam, subln.reshape(1, LANES), o_moba)


def _head_major(cols, heads, n_pos):
    Bd = cols.shape[0]
    return cols[:, :DEC_POS].reshape(Bd, DEC_POS, heads, HEAD_DIM).transpose(0, 2, 1, 3)


def _new_rows(proj3, blk, heads):
    rows = _cols(proj3, blk, heads)
    return jnp.pad(rows, ((0, 0), (0, PAGE_SIZE - rows.shape[1]), (0, 0)))


def _sample_attn(proj3, caches, page_table, P, l, n_new, *, scan_pages=8):
    cache_cmp, cache_slc, cache_win, cache_diff, cache_moba = caches
    Bd = proj3.shape[0]
    n_pool, depth = cache_cmp.shape[:2]
    n_pages = page_table.shape[1]
    past = n_pages * PAGE_SIZE
    wkeep = cache_win.shape[2]
    assert n_new <= DEC_POS and wkeep == NSA_WINDOW and past >= NSA_WINDOW
    gw = 2 * NSA_GROUPS * HEAD_DIM
    hw = 2 * DIFF_HEADS * HEAD_DIM
    slc5 = cache_slc.reshape(n_pool, depth, 1, PAGE_SIZE, gw)
    diff5 = cache_diff.reshape(n_pool, depth, 1, PAGE_SIZE, hw)
    moba5 = cache_moba.reshape(n_pool, depth, 1, PAGE_SIZE, hw)
    win_pages = wkeep // PAGE_SIZE
    win5 = cache_win.reshape(Bd, depth, win_pages, PAGE_SIZE, gw)
    zeros_pt = jnp.zeros_like(page_table)
    win_major = jnp.broadcast_to(jnp.arange(Bd, dtype=jnp.int32)[:, None], (Bd, win_pages))
    win_minor = jnp.broadcast_to(jnp.arange(win_pages, dtype=jnp.int32)[None, :], (Bd, win_pages))
    half = NSA_CMP_STRIDE * HEAD_DIM

    def w1ab(w1):
        return jnp.concatenate([w1[:half].reshape(NSA_CMP_STRIDE, LANES, LANES),
                                w1[half:].reshape(NSA_CMP_STRIDE, LANES, LANES)], axis=-1)
    wab = jnp.stack([w1ab(P["nsa_w1_k"][l]), w1ab(P["nsa_w1_v"][l])]).astype(BF16)
    ab = _cmp_scan(cache_cmp, page_table, wab, l, P=scan_pages)
    qn = _head_major(_cols(proj3, BQ, NSA_HEADS), NSA_HEADS, n_new).reshape(Bd, NSA_GROUPS, NSA_REP * DEC_POS, LANES)
    cw = (P["nsa_pe_k"][l], P["nsa_w1_k"][l], P["nsa_w2_k"][l],
          P["nsa_pe_v"][l], P["nsa_w1_v"][l], P["nsa_w2_v"][l])
    o_cmp, bias_slc = _nsa_dec_select(ab, qn, cw, past=past)
    sc128 = HEAD_DIM ** -0.5
    o_slc = _paged_attn(qn, slc5, page_table, zeros_pt, l, _new_rows(proj3, BKS, NSA_GROUPS),
                        _new_rows(proj3, BVS, NSA_GROUPS), bias_slc, scale=sc128, blocksize=NSA_SLC_LEN,
                        P=scan_pages, n_new=n_new)
    o_win = _paged_attn(qn, win5, win_major, win_minor, l, _new_rows(proj3, BKW, NSA_GROUPS),
                        _new_rows(proj3, BVW, NSA_GROUPS), scale=sc128, window=True, P=win_pages, n_new=n_new)
    qd = _head_major(_cols(proj3, BDQ, DIFF_HEADS), DIFF_HEADS, n_new)
    lane = jnp.arange(LANES)
    qd = jnp.concatenate([jnp.where(lane < DIFF_QK_DIM, qd, 0.0), jnp.where(lane >= DIFF_QK_DIM, qd, 0.0)], axis=2)
    o_diff = _paged_attn(qd, diff5, page_table, zeros_pt, l, _new_rows(proj3, BDK, DIFF_HEADS),
                         _new_rows(proj3, BDV, DIFF_HEADS), scale=DIFF_QK_DIM ** -0.5, P=scan_pages, n_new=n_new)
    qm = _head_major(_cols(proj3, BMQ, MOBA_HEADS), MOBA_HEADS, n_new)
    kmean = _moba_kmean(cache_moba, page_table, l, P=16)
    bias_moba = _moba_dec_select(kmean, qm)
    o_moba = _paged_attn(qm, moba5, page_table, zeros_pt, l, _new_rows(proj3, BMK, MOBA_HEADS),
                         _new_rows(proj3, BMV, MOBA_HEADS), bias_moba, scale=sc128, blocksize=MOBA_BLOCK,
                         P=scan_pages, n_new=n_new)
    return _dec_finalize(o_cmp, o_slc, o_win, proj3, o_diff, P["diff_lambda"][l], P["diff_subln"][l], o_moba,
                         lam_init=_lam_init(l))


def _sample_trunk(x_sample, caches, page_table, P):
    cache_cmp, cache_slc, cache_win, cache_diff, cache_moba, cache_mem = caches
    Bd, n_new, D = x_sample.shape
    depth = P["w_in"].shape[0]
    past = page_table.shape[1] * PAGE_SIZE
    pos = past + jnp.arange(DEC_ROWS, dtype=jnp.int32)
    tabs = [jnp.tile(t, (Bd, 1)) for t in _rope_tables(pos)]
    x = jnp.pad(x_sample, ((0, 0), (0, DEC_ROWS - n_new), (0, 0))).reshape(Bd * DEC_ROWS, D)
    mem4 = cache_mem.reshape(Bd, depth, MEM_TOKENS, 2 * MEM_HEADS * MEM_DIM)
    rows = [[] for _ in range(5)]
    for l in range(depth):
        proj = _norm_matmul(x, P["norm_mix"][l], _regroup_w_in(P["w_in"][l]), tabs, tm=Bd * DEC_ROWS, tn=256)
        proj3 = proj.reshape(Bd, DEC_ROWS, NPROJ)
        new3 = proj3[:, :n_new]
        rows[0].append(_cache_rows(new3, BKC, BVC, NSA_GROUPS))
        rows[1].append(_cache_rows(new3, BKS, BVS, NSA_GROUPS))
        rows[2].append(jnp.concatenate(
            [cache_win[:, l], _cache_rows(new3, BKW, BVW, NSA_GROUPS)], axis=1)[:, n_new:])
        rows[3].append(_cache_rows(new3, BDK, BDV, DIFF_HEADS))
        rows[4].append(_cache_rows(new3, BMK, BMV, MOBA_HEADS))
        a_nsa, a_diff, a_moba = _sample_attn(proj3, caches[:5], page_table, P, l, n_new)
        x3 = _out_mem(a_nsa, a_diff, a_moba, proj3, x.reshape(Bd, DEC_ROWS, D), P["w_out"][l].astype(BF16),
                      P["norm_mem_x"][l].reshape(1, D), P["w_mem_q"][l].astype(BF16), mem4[:, l],
                      P["w_mem_o"][l].astype(BF16), tm=DEC_ROWS)
        x = x3.reshape(Bd * DEC_ROWS, D)
    y = _rmsnorm(x, P["norm_final"], tm=Bd * DEC_ROWS).reshape(Bd, DEC_ROWS, D)[:, :n_new]
    return y, [jnp.stack(r, axis=1) for r in rows]


def kernel(x_prompt, x_sample, mem_prompt, cache_nsa_cmp, cache_nsa_slc, cache_nsa_win, cache_diff, cache_moba, cache_mem, page_table, norm_mix, w_in, w_out, nsa_pe_k, nsa_pe_v, nsa_w1_k, nsa_w2_k, nsa_w1_v, nsa_w2_v, diff_lambda, diff_subln, norm_mem_x, norm_mem_m, w_mem_q, w_mem_kv, w_mem_o, norm_final):
    P = {"norm_mix": norm_mix, "w_in": w_in, "w_out": w_out, "nsa_pe_k": nsa_pe_k, "nsa_pe_v": nsa_pe_v,
         "nsa_w1_k": nsa_w1_k, "nsa_w2_k": nsa_w2_k, "nsa_w1_v": nsa_w1_v, "nsa_w2_v": nsa_w2_v,
         "diff_lambda": diff_lambda, "diff_subln": diff_subln, "norm_mem_x": norm_mem_x,
         "norm_mem_m": norm_mem_m, "w_mem_q": w_mem_q, "w_mem_kv": w_mem_kv, "w_mem_o": w_mem_o,
         "norm_final": norm_final}
    y_prompt, (p_cmp, p_slc, p_win, p_diff, p_moba), p_mem = _prompt_trunk(x_prompt, mem_prompt, P)
    caches = (cache_nsa_cmp, cache_nsa_slc, cache_nsa_win, cache_diff, cache_moba, cache_mem)
    y_sample, (s_cmp, s_slc, s_win, s_diff, s_moba) = _sample_trunk(x_sample, caches, page_table, P)
    return (y_prompt, y_sample, p_cmp, p_slc, p_win, p_diff, p_moba, p_mem, s_cmp, s_slc, s_win, s_diff, s_moba)
```

```python
import functools
import math

import jax
import jax.numpy as jnp
from jax import lax
from jax.experimental import pallas as pl
from jax.experimental.pallas import tpu as pltpu

F32 = jnp.float32
BF16 = jnp.bfloat16

D_MODEL = 2048
HEAD_DIM = 128
ROPE_THETA = 500000.0
NORM_EPS = 1e-6
PAGE_SIZE = 128

NSA_HEADS = 8
NSA_GROUPS = 2
NSA_REP = NSA_HEADS // NSA_GROUPS
NSA_CMP_STRIDE = 16
NSA_SLC_LEN = 64
NSA_TOPN = 16
NSA_WINDOW = 512
NSA_FORCE_BONUS = 1000.0
DIFF_HEADS = 4
DIFF_QK_DIM = HEAD_DIM // 2
MOBA_HEADS = 4
MOBA_BLOCK = 256
MOBA_TOPK = 3
MEM_TOKENS = 256
MEM_HEADS = 4
MEM_DIM = 128

NEG = -1e30
LANES = 128
SUBLANES = 8
VMEM_LIMIT_BYTES = 56 * 1024 * 1024

BQ, BKC, BKS, BKW, BMQ, BMK = 0, 8, 10, 12, 14, 18
BDQ, BDK = 22, 26
BVC, BVS, BVW, BNZ, BDV, BDZ, BMV, BMZ = 30, 32, 34, 36, 44, 48, 52, 56
BGATE = 60
NBLK = 62
NPROJ = NBLK * LANES
_W_IN_SEGMENTS = (
    (0, 1024), (1024, 1280), (1536, 1792), (2048, 2304), (5656, 6168), (6168, 6680),
    (3608, 4120), (4120, 4632),
    (1280, 1536), (1792, 2048), (2304, 2560), (2584, 3608), (4632, 5144), (5144, 5656),
    (6680, 7192), (7192, 7704),
    (2560, 2584),
)


def _round_up(n, m):
    return -(-n // m) * m


def _nt(a, b):
    return lax.dot_general(a, b, (((1,), (1,)), ((), ())), preferred_element_type=F32)


def _mm(a, b):
    return jnp.dot(a, b, preferred_element_type=F32)


def _silu(z):
    return z * jax.nn.sigmoid(z)


def _cparams(sem):
    return pltpu.CompilerParams(dimension_semantics=sem, vmem_limit_bytes=VMEM_LIMIT_BYTES)


def _norm_matmul_body(*refs, tn, splits, rope):
    if rope:
        x_ref, g_ref, w_ref, c1, a1, b1, c2, a2, b2, o_ref, h_ref = refs
    else:
        x_ref, g_ref, w_ref, o_ref, h_ref = refs
    j = pl.program_id(1)

    @pl.when(j == 0)
    def _():
        x = x_ref[...]
        ms = jnp.mean(x * x, axis=-1, keepdims=True)
        h_ref[...] = (x * lax.rsqrt(ms + NORM_EPS) * g_ref[...]).astype(BF16)

    acc = _mm(h_ref[...], w_ref[...])
    if not rope:
        o_ref[...] = acc
        return
    t1, t2, t3 = splits

    def rotate(c, a, b, half):
        for k in range(tn // LANES):
            blk = acc[:, k * LANES:(k + 1) * LANES]
            o_ref[:, k * LANES:(k + 1) * LANES] = (
                blk * c[...] + pltpu.roll(blk, LANES - half, 1) * a[...] + pltpu.roll(blk, half, 1) * b[...])

    @pl.when(j < t1)
    def _():
        rotate(c1, a1, b1, HEAD_DIM // 8)

    @pl.when((j >= t1) & (j < t2))
    def _():
        rotate(c2, a2, b2, DIFF_QK_DIM // 8)

    @pl.when((j >= t2) & (j < t3))
    def _():
        o_ref[...] = acc

    @pl.when(j >= t3)
    def _():
        o_ref[...] = jax.nn.sigmoid(acc)


def _norm_matmul(x, gain, w, tabs=None, *, tm, tn):
    M, K = x.shape
    Np = w.shape[1]
    rope = tabs is not None
    splits = (BDQ * LANES // tn, BVC * LANES // tn, BGATE * LANES // tn)
    in_specs = [
        pl.BlockSpec((tm, K), lambda i, j: (i, 0)),
        pl.BlockSpec((1, K), lambda i, j: (0, 0)),
        pl.BlockSpec((K, tn), lambda i, j: (0, j)),
    ]
    args = [x, gain.reshape(1, K), w]
    if rope:
        period = tabs[0].shape[0] // tm
        for t in tabs:
            in_specs.append(pl.BlockSpec((tm, LANES), lambda i, j: (i % period, 0)))
            args.append(t)
    return pl.pallas_call(
        functools.partial(_norm_matmul_body, tn=tn, splits=splits, rope=rope),
        grid=(M // tm, Np // tn),
        in_specs=in_specs,
        out_specs=pl.BlockSpec((tm, tn), lambda i, j: (i, j)),
        out_shape=jax.ShapeDtypeStruct((M, Np), F32),
        scratch_shapes=[pltpu.VMEM((tm, K), BF16)],
        compiler_params=_cparams(("parallel", "arbitrary")),
        name="norm_matmul",
    )(*args)


def _rope_tables(pos):
    posf = pos.astype(F32)[:, None]
    n = pos.shape[0]

    def one(width, reps):
        rd = width // 4
        half = rd // 2
        inv = ROPE_THETA ** (-2.0 * jnp.arange(half, dtype=F32) / rd)
        ang = posf * inv[None, :]
        c, s = jnp.cos(ang), jnp.sin(ang)
        z = jnp.zeros((n, width - rd), F32)
        zh = jnp.zeros((n, half), F32)
        C = jnp.concatenate([c, c, jnp.ones((n, width - rd), F32)], axis=1)
        A = jnp.concatenate([-s, zh, z], axis=1)
        B = jnp.concatenate([zh, s, z], axis=1)
        return [jnp.tile(t, (1, reps)) for t in (C, A, B)]

    return one(HEAD_DIM, 1) + one(DIFF_QK_DIM, 2)


def _online_init(m_ref, l_ref, acc_ref):
    m_ref[...] = jnp.full(m_ref.shape, NEG, F32)
    l_ref[...] = jnp.zeros(l_ref.shape, F32)
    acc_ref[...] = jnp.zeros(acc_ref.shape, F32)


def _online_update(s, v, m_ref, l_ref, acc_ref):
    m_prev = m_ref[...]
    m_new = jnp.maximum(m_prev, jnp.max(s, axis=-1, keepdims=True))
    alpha = jnp.exp(m_prev - m_new)
    p = jnp.exp(s - m_new)
    l_ref[...] = alpha * l_ref[...] + jnp.sum(p, axis=-1, keepdims=True)
    acc_ref[...] = alpha * acc_ref[...] + _mm(p.astype(BF16), v)
    m_ref[...] = m_new


def _online_update_t(s, vt, m_ref, l_ref, acc_ref):
    m_prev = m_ref[...]
    m_new = jnp.maximum(m_prev, jnp.max(s, axis=0, keepdims=True))
    alpha = jnp.exp(m_prev - m_new)
    p = jnp.exp(s - m_new)
    l_ref[...] = alpha * l_ref[...] + jnp.sum(p, axis=0, keepdims=True)
    acc_ref[...] = alpha * acc_ref[...] + _mm(vt, p.astype(BF16))
    m_ref[...] = m_new


def _masked_softmax(s, mask, axis):
    m = jnp.max(jnp.where(mask, s, NEG), axis=axis, keepdims=True)
    m = jnp.where(m > 0.5 * NEG, m, 0.0)
    e = jnp.where(mask, jnp.exp(s - m), 0.0)
    return e / jnp.maximum(jnp.sum(e, axis=axis, keepdims=True), 1e-30)


def _rank_desc(score, ncand, axis):
    idx = lax.broadcasted_iota(jnp.int32, score.shape, axis)
    rank = jnp.zeros(score.shape, F32)
    for c in range(ncand):
        cand = score[:, c:c + 1] if axis == 1 else score[c:c + 1, :]
        before = (cand > score) | ((cand == score) & (c < idx))
        rank = rank + before.astype(F32)
    return rank


def _lane_pick(x, idx):
    lane = lax.broadcasted_iota(jnp.int32, x.shape, 1)
    return jnp.sum(jnp.where(lane == idx, x, 0.0), axis=-1, keepdims=True)


def _compress(src_ref, pe_ref, w1_ref, w2_ref, nc):
    half = NSA_CMP_STRIDE * HEAD_DIM
    acc_a = jnp.zeros((nc, LANES), F32)
    acc_b = jnp.zeros((nc, LANES), F32)
    for j in range(NSA_CMP_STRIDE):
        rows = src_ref[pl.ds(j, nc, stride=NSA_CMP_STRIDE), :].astype(BF16)
        acc_a = acc_a + _mm(rows, w1_ref[j * LANES:(j + 1) * LANES, :].astype(BF16))
        acc_b = acc_b + _mm(rows, w1_ref[half + j * LANES:half + (j + 1) * LANES, :].astype(BF16))
    return _compress_finish(acc_a, acc_b, pe_ref, w1_ref, w2_ref, nc)


def _pe_term(pe_ref, w1_ref):
    acc = jnp.zeros((SUBLANES, LANES), F32)
    for j in range(2 * NSA_CMP_STRIDE):
        pj = jnp.broadcast_to(pe_ref[j:j + 1, :], (SUBLANES, LANES)).astype(BF16)
        acc = acc + _mm(pj, w1_ref[j * LANES:(j + 1) * LANES, :].astype(BF16))
    return acc[0:1, :]


def _compress_finish(acc_a, acc_b, pe_ref, w1_ref, w2_ref, nc):
    row = lax.broadcasted_iota(jnp.int32, (nc, LANES), 0)
    hb = jnp.where(row < nc - 1, pltpu.roll(acc_b, nc - 1, 0), 0.0)
    hid = _silu(acc_a + hb + _pe_term(pe_ref, w1_ref))
    return _mm(hid.astype(BF16), w2_ref[...].astype(BF16))


def _slc_weights(shape, cmp_axis):
    r = lax.broadcasted_iota(jnp.int32, shape, cmp_axis)
    c = lax.broadcasted_iota(jnp.int32, shape, 1 - cmp_axis)
    d = r - 4 * c
    return jnp.where((d == -1) | (d == 3), 1.0, jnp.where((d >= 0) & (d <= 2), 2.0, 0.0)).astype(F32)


def _transpose_into(dst_ref, src_ref, rows, chunk):
    for c in range(rows // chunk):
        dst_ref[:, c * chunk:(c + 1) * chunk] = src_ref[c * chunk:(c + 1) * chunk, :].T.astype(BF16)


def _nsa_prompt_body(q_ref, kc_ref, ks_ref, kw_ref, vc_ref, vs_ref, vw_ref, gate_ref,
                     pek_ref, w1k_ref, w2k_ref, pev_ref, w1v_ref, w2v_ref, o_ref,
                     kcs, vct, kaug, vst, kwb, vwt, gt_ref, m_ref, l_ref, acc_ref, *, T, tq):
    g = pl.program_id(1)
    i = pl.program_id(2)
    nc = T // NSA_CMP_STRIDE
    nsb = T // NSA_SLC_LEN
    nsbr = _round_up(nsb, SUBLANES)
    nsel = min(NSA_TOPN, nsb)
    R = NSA_REP * tq
    scale = HEAD_DIM ** -0.5

    @pl.when(i == 0)
    def _():
        kcs[...] = _compress(kc_ref, pek_ref, w1k_ref, w2k_ref, nc).astype(BF16)
        vct[...] = _compress(vc_ref, pev_ref, w1v_ref, w2v_ref, nc).T.astype(BF16)
        kaug[:, :LANES] = ks_ref[...].astype(BF16)
        rblk = lax.broadcasted_iota(jnp.int32, (T, LANES), 0) // NSA_SLC_LEN
        lane = lax.broadcasted_iota(jnp.int32, (T, LANES), 1)
        kaug[:, LANES:] = (rblk == lane).astype(BF16)
        kwb[...] = kw_ref[...].astype(BF16)
        _transpose_into(vst, vs_ref, T, tq)
        _transpose_into(vwt, vw_ref, T, tq)

    q = q_ref[...]
    qt = jnp.concatenate([q[:, h * LANES:(h + 1) * LANES].T for h in range(NSA_REP)], axis=1).astype(BF16)

    s = _mm(kcs[...], qt) * scale
    crow = lax.broadcasted_iota(jnp.int32, (nc, R), 0)
    t_col = i * tq + lax.broadcasted_iota(jnp.int32, (nc, R), 1) % tq
    pc = _masked_softmax(s, (NSA_CMP_STRIDE * crow + 2 * NSA_CMP_STRIDE - 1) <= t_col, 0)
    o_cmp = _mm(vct[...], pc.astype(BF16))
    pg = pc[:, 0:tq]
    for h in range(1, NSA_REP):
        pg = pg + pc[:, h * tq:(h + 1) * tq]

    slc = jnp.dot(_slc_weights((LANES, nc), 1), pg, precision=lax.Precision.HIGHEST,
                  preferred_element_type=F32)[:nsbr]
    blk = lax.broadcasted_iota(jnp.int32, (nsbr, tq), 0)
    tb = (i * tq + lax.broadcasted_iota(jnp.int32, (nsbr, tq), 1)) // NSA_SLC_LEN
    valid = blk <= tb
    forced = (blk == 0) | (blk == tb) | (blk == tb - 1)
    score = jnp.where(valid, slc + jnp.where(forced, NSA_FORCE_BONUS, 0.0), -jnp.inf)
    sel = (_rank_desc(score, nsb, 0) < nsel) & valid
    bias = jnp.where(sel, 0.0, NEG)
    if nsbr < LANES:
        bias = jnp.concatenate([bias, jnp.full((LANES - nsbr, tq), NEG, F32)], axis=0)
    bias = bias.astype(BF16)
    qaug = jnp.concatenate([qt, jnp.concatenate([bias] * NSA_REP, axis=1)], axis=0)

    krow = lax.broadcasted_iota(jnp.int32, (tq, R), 0)
    tloc = lax.broadcasted_iota(jnp.int32, (tq, R), 1) % tq

    _online_init(m_ref, l_ref, acc_ref)

    def slc_step(kt, carry):
        off = pl.multiple_of(kt * tq, tq)
        s2 = _mm(kaug[pl.ds(off, tq), :], qaug) * scale
        _online_update_t(s2, vst[:, pl.ds(off, tq)], m_ref, l_ref, acc_ref)
        return carry

    lax.fori_loop(0, i, slc_step, 0)
    off_d = pl.multiple_of(i * tq, tq)
    s2 = _mm(kaug[pl.ds(off_d, tq), :], qaug) * scale
    _online_update_t(jnp.where(krow <= tloc, s2, NEG), vst[:, pl.ds(off_d, tq)], m_ref, l_ref, acc_ref)
    o_slc = acc_ref[...] / l_ref[...]

    _online_init(m_ref, l_ref, acc_ref)
    nw = NSA_WINDOW // tq
    for d in range(nw, -1, -1):
        @pl.when(i >= d)
        def _(d=d):
            off = pl.multiple_of((i - d) * tq, tq)
            s3 = _mm(kwb[pl.ds(off, tq), :], qt) * scale
            if d == nw:
                s3 = jnp.where(krow > tloc, s3, NEG)
            if d == 0:
                s3 = jnp.where(krow <= tloc, s3, NEG)
            _online_update_t(s3, vwt[:, pl.ds(off, tq)], m_ref, l_ref, acc_ref)
    o_win = acc_ref[...] / l_ref[...]

    gt_ref[...] = gate_ref[...].T

    def gate_row(branch):
        return jnp.concatenate(
            [gt_ref[pl.ds(branch * NSA_HEADS + NSA_REP * g + h, 1), :] for h in range(NSA_REP)], axis=1)

    out = gate_row(0) * o_cmp + gate_row(1) * o_slc + gate_row(2) * o_win
    for h in range(NSA_REP):
        o_ref[:, h * LANES:(h + 1) * LANES] = out[:, h * tq:(h + 1) * tq].T


def _nsa_prompt(proj, cw, *, B, T, tq):
    nqt = T // tq
    nc = T // NSA_CMP_STRIDE
    R = NSA_REP * tq
    assert T % NSA_SLC_LEN == 0 and NSA_WINDOW % tq == 0 and T // NSA_SLC_LEN <= LANES and nc <= LANES

    def kv_spec(blk):
        return pl.BlockSpec((T, LANES), lambda b, g, i: (b, blk + g))

    def full(a):
        return pl.BlockSpec(a.shape, lambda b, g, i: (0,) * a.ndim)

    in_specs = [
        pl.BlockSpec((tq, NSA_REP * LANES), lambda b, g, i: (b * nqt + i, g)),
        kv_spec(BKC), kv_spec(BKS), kv_spec(BKW), kv_spec(BVC), kv_spec(BVS), kv_spec(BVW),
        pl.BlockSpec((tq, LANES), lambda b, g, i: (b * nqt + i, BGATE)),
    ] + [full(a) for a in cw]
    return pl.pallas_call(
        functools.partial(_nsa_prompt_body, T=T, tq=tq),
        grid=(B, NSA_GROUPS, nqt),
        in_specs=in_specs,
        out_specs=pl.BlockSpec((tq, NSA_REP * LANES), lambda b, g, i: (b * nqt + i, g)),
        out_shape=jax.ShapeDtypeStruct((B * T, NSA_HEADS * LANES), F32),
        scratch_shapes=[
            pltpu.VMEM((nc, LANES), BF16), pltpu.VMEM((LANES, nc), BF16),
            pltpu.VMEM((T, 2 * LANES), BF16), pltpu.VMEM((LANES, T), BF16),
            pltpu.VMEM((T, LANES), BF16), pltpu.VMEM((LANES, T), BF16),
            pltpu.VMEM((LANES, tq), F32),
            pltpu.VMEM((1, R), F32), pltpu.VMEM((1, R), F32), pltpu.VMEM((LANES, R), F32),
        ],
        compiler_params=_cparams(("parallel", "parallel", "arbitrary")),
        name="nsa_prompt",
    )(proj, proj, proj, proj, proj, proj, proj, proj, *cw)


def _diff_lambda(lam_ref, lam_init):
    lp = lam_ref[...]
    return (jnp.exp(jnp.sum(lp[0:1] * lp[1:2], axis=-1, keepdims=True))
            - jnp.exp(jnp.sum(lp[2:3] * lp[3:4], axis=-1, keepdims=True)) + lam_init)


def _diff_finish(o1, o2, lam_ref, sub_ref, lam_init):
    a = o1 - _diff_lambda(lam_ref, lam_init) * o2
    ms = jnp.mean(a * a, axis=-1, keepdims=True)
    return a * lax.rsqrt(ms + NORM_EPS) * sub_ref[...] * (1.0 - lam_init)


def _diff_prompt_body(q_ref, k_ref, v_ref, lam_ref, sub_ref, o_ref, kb, vt, m_ref, l_ref, acc_ref,
                      *, T, tq, lam_init):
    i = pl.program_id(2)
    scale = DIFF_QK_DIM ** -0.5

    @pl.when(i == 0)
    def _():
        kb[...] = k_ref[...].astype(BF16)
        _transpose_into(vt, v_ref, T, tq)

    q = q_ref[...]
    lane = lax.broadcasted_iota(jnp.int32, q.shape, 1)
    qt = jnp.concatenate([jnp.where(lane < DIFF_QK_DIM, q, 0.0).T, jnp.where(lane >= DIFF_QK_DIM, q, 0.0).T],
                         axis=1).astype(BF16)
    R = 2 * tq
    krow = lax.broadcasted_iota(jnp.int32, (tq, R), 0)
    tloc = lax.broadcasted_iota(jnp.int32, (tq, R), 1) % tq
    _online_init(m_ref, l_ref, acc_ref)

    def step(kt, carry):
        off = pl.multiple_of(kt * tq, tq)
        _online_update_t(_mm(kb[pl.ds(off, tq), :], qt) * scale, vt[:, pl.ds(off, tq)], m_ref, l_ref, acc_ref)
        return carry

    lax.fori_loop(0, i, step, 0)
    off = pl.multiple_of(i * tq, tq)
    s = _mm(kb[pl.ds(off, tq), :], qt) * scale
    _online_update_t(jnp.where(krow <= tloc, s, NEG), vt[:, pl.ds(off, tq)], m_ref, l_ref, acc_ref)
    o = acc_ref[...] / l_ref[...]
    o_ref[...] = _diff_finish(o[:, :tq].T, o[:, tq:].T, lam_ref, sub_ref, lam_init)


def _diff_prompt(proj, lam, subln, *, B, T, tq, lam_init):
    nqt = T // tq
    return pl.pallas_call(
        functools.partial(_diff_prompt_body, T=T, tq=tq, lam_init=lam_init),
        grid=(B, DIFF_HEADS, nqt),
        in_specs=[
            pl.BlockSpec((tq, LANES), lambda b, h, i: (b * nqt + i, BDQ + h)),
            pl.BlockSpec((T, LANES), lambda b, h, i: (b, BDK + h)),
            pl.BlockSpec((T, LANES), lambda b, h, i: (b, BDV + h)),
            pl.BlockSpec(lam.shape, lambda b, h, i: (0, 0)),
            pl.BlockSpec((1, LANES), lambda b, h, i: (0, 0)),
        ],
        out_specs=pl.BlockSpec((tq, LANES), lambda b, h, i: (b * nqt + i, h)),
        out_shape=jax.ShapeDtypeStruct((B * T, DIFF_HEADS * LANES), F32),
        scratch_shapes=[
            pltpu.VMEM((T, LANES), BF16), pltpu.VMEM((LANES, T), BF16),
            pltpu.VMEM((1, 2 * tq), F32), pltpu.VMEM((1, 2 * tq), F32), pltpu.VMEM((LANES, 2 * tq), F32),
        ],
        compiler_params=_cparams(("parallel", "parallel", "arbitrary")),
        name="diff_prompt",
    )(proj, proj, proj, lam, subln.reshape(1, LANES))


def _moba_prompt_body(q_ref, k_ref, v_ref, o_ref, kb, vt, km, bias_ref, m_ref, l_ref, acc_ref, *, T):
    i = pl.program_id(2)
    tq = MOBA_BLOCK
    nb = T // MOBA_BLOCK
    nbr = _round_up(nb, SUBLANES)
    scale = HEAD_DIM ** -0.5

    @pl.when(i == 0)
    def _():
        kb[...] = k_ref[...].astype(BF16)
        _transpose_into(vt, v_ref, T, tq)
        km[...] = jnp.zeros(km.shape, BF16)
        for j in range(nb):
            km[j:j + 1, :] = jnp.mean(k_ref[j * tq:(j + 1) * tq, :], axis=0, keepdims=True).astype(BF16)

    qt = q_ref[...].T.astype(BF16)
    blk = lax.broadcasted_iota(jnp.int32, (nbr, tq), 0)
    past = blk < i
    score = jnp.where(past, _mm(km[...], qt)[:nbr], -jnp.inf)
    sel = (_rank_desc(score, nb, 0) < min(MOBA_TOPK, nb)) & past
    bias_ref[...] = jnp.where(sel, 0.0, NEG)
    krow = lax.broadcasted_iota(jnp.int32, (tq, tq), 0)
    tloc = lax.broadcasted_iota(jnp.int32, (tq, tq), 1)
    _online_init(m_ref, l_ref, acc_ref)

    def step(kt, carry):
        off = pl.multiple_of(kt * tq, tq)
        s = _mm(kb[pl.ds(off, tq), :], qt) * scale + bias_ref[pl.ds(kt, 1), :]
        _online_update_t(s, vt[:, pl.ds(off, tq)], m_ref, l_ref, acc_ref)
        return carry

    lax.fori_loop(0, i, step, 0)
    off = pl.multiple_of(i * tq, tq)
    s = _mm(kb[pl.ds(off, tq), :], qt) * scale
    _online_update_t(jnp.where(krow <= tloc, s, NEG), vt[:, pl.ds(off, tq)], m_ref, l_ref, acc_ref)
    o_ref[...] = (acc_ref[...] / l_ref[...]).T


def _moba_prompt(proj, *, B, T):
    tq = MOBA_BLOCK
    assert T % tq == 0 and T // tq <= LANES
    nqt = T // tq
    nbr = _round_up(T // MOBA_BLOCK, SUBLANES)
    return pl.pallas_call(
        functools.partial(_moba_prompt_body, T=T),
        grid=(B, MOBA_HEADS, nqt),
        in_specs=[
            pl.BlockSpec((tq, LANES), lambda b, h, i: (b * nqt + i, BMQ + h)),
            pl.BlockSpec((T, LANES), lambda b, h, i: (b, BMK + h)),
            pl.BlockSpec((T, LANES), lambda b, h, i: (b, BMV + h)),
        ],
        out_specs=pl.BlockSpec((tq, LANES), lambda b, h, i: (b * nqt + i, h)),
        out_shape=jax.ShapeDtypeStruct((B * T, MOBA_HEADS * LANES), F32),
        scratch_shapes=[
            pltpu.VMEM((T, LANES), BF16), pltpu.VMEM((LANES, T), BF16), pltpu.VMEM((LANES, LANES), BF16),
            pltpu.VMEM((nbr, tq), F32),
            pltpu.VMEM((1, tq), F32), pltpu.VMEM((1, tq), F32), pltpu.VMEM((LANES, tq), F32),
        ],
        compiler_params=_cparams(("parallel", "parallel", "arbitrary")),
        name="moba_prompt",
    )(proj, proj, proj)


def _out_mem_body(an_ref, ad_ref, am_ref, nz0_ref, nz1_ref, dz_ref, mz_ref, x_ref, wout_ref, gmem_ref,
                  wq_ref, mkv_ref, wo_ref, o_ref):
    half = NSA_HEADS * LANES // 2
    an = an_ref[...]
    mixed = jnp.concatenate([
        an[:, :half] * _silu(nz0_ref[...]), an[:, half:] * _silu(nz1_ref[...]),
        ad_ref[...] * _silu(dz_ref[...]), am_ref[...] * _silu(mz_ref[...])], axis=1).astype(BF16)
    x1 = x_ref[...] + _mm(mixed, wout_ref[...])
    ms = jnp.mean(x1 * x1, axis=-1, keepdims=True)
    h2 = (x1 * lax.rsqrt(ms + NORM_EPS) * gmem_ref[...]).astype(BF16)
    q = _mm(h2, wq_ref[...])
    mkv = mkv_ref[...]
    scale = MEM_DIM ** -0.5
    outs = []
    for hh in range(MEM_HEADS):
        qh = q[:, hh * LANES:(hh + 1) * LANES].astype(BF16)
        kh = mkv[:, hh * LANES:(hh + 1) * LANES].astype(BF16)
        vh = mkv[:, (MEM_HEADS + hh) * LANES:(MEM_HEADS + hh + 1) * LANES].astype(BF16)
        s = _nt(qh, kh) * scale
        e = jnp.exp(s - jnp.max(s, axis=-1, keepdims=True))
        p = e / jnp.sum(e, axis=-1, keepdims=True)
        outs.append(_mm(p.astype(BF16), vh))
    oc = jnp.concatenate(outs, axis=1).astype(BF16)
    o_ref[...] = x1 + _mm(oc, wo_ref[...])


def _out_mem(a_nsa, a_diff, a_moba, proj3, x3, wout, gmem, wq, mkv, wo, *, tm):
    nb, rows, D = x3.shape
    grid = (nb, rows // tm)

    def row(cols, cb):
        return pl.BlockSpec((None, tm, cols), lambda b, i: (b, i, cb))

    def const(a):
        return pl.BlockSpec(a.shape, lambda b, i: (0,) * a.ndim, pipeline_mode=pl.Buffered(1))

    zc = 4 * LANES
    return pl.pallas_call(
        _out_mem_body,
        grid=grid,
        in_specs=[
            row(NSA_HEADS * LANES, 0), row(DIFF_HEADS * LANES, 0), row(MOBA_HEADS * LANES, 0),
            row(zc, BNZ * LANES // zc), row(zc, BNZ * LANES // zc + 1), row(zc, BDZ * LANES // zc),
            row(zc, BMZ * LANES // zc),
            row(D, 0), const(wout), const(gmem), const(wq),
            pl.BlockSpec((None, MEM_TOKENS, 2 * MEM_HEADS * MEM_DIM), lambda b, i: (b, 0, 0)),
            const(wo),
        ],
        out_specs=row(D, 0),
        out_shape=jax.ShapeDtypeStruct(x3.shape, F32),
        compiler_params=_cparams(("parallel", "parallel")),
        name="out_mem",
    )(a_nsa, a_diff, a_moba, proj3, proj3, proj3, proj3, x3, wout, gmem, wq, mkv, wo)


def _rmsnorm_body(x_ref, g_ref, o_ref):
    x = x_ref[...]
    ms = jnp.mean(x * x, axis=-1, keepdims=True)
    o_ref[...] = x * lax.rsqrt(ms + NORM_EPS) * g_ref[...]


def _rmsnorm(x, gain, *, tm):
    M, D = x.shape
    return pl.pallas_call(
        _rmsnorm_body,
        grid=(M // tm,),
        in_specs=[pl.BlockSpec((tm, D), lambda i: (i, 0)), pl.BlockSpec((1, D), lambda i: (0, 0))],
        out_specs=pl.BlockSpec((tm, D), lambda i: (i, 0)),
        out_shape=jax.ShapeDtypeStruct((M, D), F32),
        compiler_params=_cparams(("parallel",)),
        name="final_norm",
    )(x, gain.reshape(1, D))


def _regroup_w_in(w_in_l):
    K = w_in_l.shape[0]
    used = sum(b - a for a, b in _W_IN_SEGMENTS)
    cols = [w_in_l[:, a:b] for a, b in _W_IN_SEGMENTS] + [jnp.zeros((K, NPROJ - used), w_in_l.dtype)]
    return jnp.concatenate(cols, axis=1).astype(BF16)


def _cols(proj, blk, n):
    return proj[..., blk * LANES:(blk + n) * LANES]


def _cache_rows(proj3, kblk, vblk, heads):
    nb, rows, _ = proj3.shape
    kv = jnp.concatenate([_cols(proj3, kblk, heads), _cols(proj3, vblk, heads)], axis=-1)
    return kv.reshape(nb, rows, 2, heads, HEAD_DIM)


def _lam_init(l):
    return 0.8 - 0.6 * math.exp(-0.3 * l)


def _prompt_trunk(x_prompt, mem_prompt, P, *, tq=256):
    B, T, D = x_prompt.shape
    depth = P["w_in"].shape[0]
    N = B * T
    tabs = _rope_tables(jnp.arange(T, dtype=jnp.int32))
    tm = min(1024, T)
    memx = mem_prompt.reshape(B * MEM_TOKENS, D)
    x = x_prompt.reshape(N, D)
    rows = [[] for _ in range(5)]
    mem_rows = []
    for l in range(depth):
        mkv = _norm_matmul(memx, P["norm_mem_m"][l], P["w_mem_kv"][l].astype(BF16), tm=min(512, B * MEM_TOKENS),
                           tn=256)
        mkv3 = mkv.reshape(B, MEM_TOKENS, 2 * MEM_HEADS * MEM_DIM)
        mem_rows.append(mkv3.reshape(B, MEM_TOKENS, 2, MEM_HEADS, MEM_DIM))
        proj = _norm_matmul(x, P["norm_mix"][l], _regroup_w_in(P["w_in"][l]), tabs, tm=tm, tn=256)
        proj3 = proj.reshape(B, T, NPROJ)
        rows[0].append(_cache_rows(proj3, BKC, BVC, NSA_GROUPS))
        rows[1].append(_cache_rows(proj3, BKS, BVS, NSA_GROUPS))
        rows[2].append(_cache_rows(proj3, BKW, BVW, NSA_GROUPS)[:, T - min(NSA_WINDOW, T):])
        rows[3].append(_cache_rows(proj3, BDK, BDV, DIFF_HEADS))
        rows[4].append(_cache_rows(proj3, BMK, BMV, MOBA_HEADS))
        cw = (P["nsa_pe_k"][l], P["nsa_w1_k"][l], P["nsa_w2_k"][l],
              P["nsa_pe_v"][l], P["nsa_w1_v"][l], P["nsa_w2_v"][l])
        a_nsa = _nsa_prompt(proj, cw, B=B, T=T, tq=tq)
        a_diff = _diff_prompt(proj, P["diff_lambda"][l], P["diff_subln"][l], B=B, T=T, tq=tq, lam_init=_lam_init(l))
        a_moba = _moba_prompt(proj, B=B, T=T)
        x3 = _out_mem(a_nsa.reshape(B, T, -1), a_diff.reshape(B, T, -1), a_moba.reshape(B, T, -1), proj3,
                      x.reshape(B, T, D), P["w_out"][l].astype(BF16), P["norm_mem_x"][l].reshape(1, D),
                      P["w_mem_q"][l].astype(BF16), mkv3, P["w_mem_o"][l].astype(BF16), tm=min(256, T))
        x = x3.reshape(N, D)
    y = _rmsnorm(x, P["norm_final"], tm=min(512, N)).reshape(B, T, D)
    return y, [jnp.stack(r, axis=1) for r in rows], jnp.stack(mem_rows, axis=1)


DEC_ROWS = 16
DEC_POS = 8


def _plane(page_ref, plane, n_planes):
    return page_ref[pl.ds(plane, PAGE_SIZE, stride=n_planes), :]


def _flat_pages(cache):
    p0, depth, rows, two, heads, dh = cache.shape
    return cache.reshape(p0 * depth * rows * two * heads, dh)


def _cmp_scan_body(pt_ref, *refs, P):
    pages = refs[:P]
    w_ref, o_ref = refs[P], refs[P + 1]
    chunks = PAGE_SIZE // NSA_CMP_STRIDE
    n_planes = 2 * NSA_GROUPS
    for kvg in range(n_planes):
        acc = jnp.zeros((chunks * P, 2 * LANES), F32)
        for j in range(NSA_CMP_STRIDE):
            first = j * n_planes + kvg
            lhs = jnp.concatenate(
                [pg[pl.ds(first, chunks, stride=NSA_CMP_STRIDE * n_planes), :] for pg in pages], axis=0).astype(BF16)
            acc = acc + _mm(lhs, w_ref[kvg // NSA_GROUPS, j])
        o_ref[:, kvg * 2 * LANES:(kvg + 1) * 2 * LANES] = acc


def _cmp_scan(cache, page_table, w1ab, l, *, P):
    Bd, n_pages = page_table.shape
    chunks = PAGE_SIZE // NSA_CMP_STRIDE
    depth = cache.shape[1]
    view = _flat_pages(cache)
    prow = PAGE_SIZE * 2 * NSA_GROUPS
    assert n_pages % P == 0 and cache.shape[2] == PAGE_SIZE

    def page_map(b, s, pt, r):
        return (pt[b, s * P + r] * depth + l, 0)

    in_specs = [pl.BlockSpec((prow, LANES), functools.partial(page_map, r=r)) for r in range(P)]
    in_specs.append(pl.BlockSpec(w1ab.shape, lambda b, s, pt: (0, 0, 0, 0)))
    ncol = 2 * NSA_GROUPS * 2 * LANES
    return pl.pallas_call(
        functools.partial(_cmp_scan_body, P=P),
        grid_spec=pltpu.PrefetchScalarGridSpec(
            num_scalar_prefetch=1, grid=(Bd, n_pages // P), in_specs=in_specs,
            out_specs=pl.BlockSpec((None, chunks * P, ncol), lambda b, s, pt: (b, s, 0))),
        out_shape=jax.ShapeDtypeStruct((Bd, chunks * n_pages, ncol), F32),
        compiler_params=_cparams(("parallel", "arbitrary")),
        name="cmp_scan",
    )(page_table, *([view] * P), w1ab)


def _step_bias(bias, step, per_step):
    lo = step * per_step
    chunk = bias[:, (lo // LANES) * LANES:(lo // LANES + 1) * LANES]
    off = lo % LANES
    return chunk if off == 0 else pltpu.roll(chunk, LANES - off, 1)


def _nsa_dec_select_body(ab_ref, q_ref, pek_ref, w1k_ref, w2k_ref, pev_ref, w1v_ref, w2v_ref,
                         ocmp_ref, bias_ref, *, past, per_step):
    nch = past // NSA_CMP_STRIDE
    nsbp = past // NSA_SLC_LEN
    ncol = _round_up(nsbp, LANES)
    R = NSA_REP * DEC_POS
    scale = HEAD_DIM ** -0.5
    for g in range(NSA_GROUPS):
        def comp(kv, pe_ref, w1_ref, w2_ref):
            lo = (kv * NSA_GROUPS + g) * 2 * LANES
            return _compress_finish(ab_ref[:, lo:lo + LANES], ab_ref[:, lo + LANES:lo + 2 * LANES],
                                    pe_ref, w1_ref, w2_ref, nch).astype(BF16)
        kc = comp(0, pek_ref, w1k_ref, w2k_ref)
        vc = comp(1, pev_ref, w1v_ref, w2v_ref)
        s = _nt(q_ref[g].astype(BF16), kc) * scale
        col = lax.broadcasted_iota(jnp.int32, (R, nch), 1)
        qpos = past + lax.broadcasted_iota(jnp.int32, (R, nch), 0) % DEC_POS
        pc = _masked_softmax(s, (NSA_CMP_STRIDE * col + 2 * NSA_CMP_STRIDE - 1) <= qpos, 1)
        ocmp_ref[g] = _mm(pc.astype(BF16), vc)
        pg = pc[0:DEC_POS]
        for h in range(1, NSA_REP):
            pg = pg + pc[h * DEC_POS:(h + 1) * DEC_POS]
        slc = jnp.dot(pg, _slc_weights((nch, ncol), 0), precision=lax.Precision.HIGHEST,
                      preferred_element_type=F32)
        blk = lax.broadcasted_iota(jnp.int32, (DEC_POS, ncol), 1)
        forced = (blk == 0) | (blk == nsbp - 1)
        score = jnp.where(blk < nsbp, slc + jnp.where(forced, NSA_FORCE_BONUS, 0.0), -jnp.inf)
        rank = _rank_desc(score, nsbp, 1) + jnp.where(forced, 0.0, 1.0)
        b8 = jnp.where(rank < min(NSA_TOPN, nsbp + 1), 0.0, NEG)
        for st in range(nsbp // per_step):
            bias_ref[g, st] = jnp.concatenate([_step_bias(b8, st, per_step)] * NSA_REP, axis=0)


def _nsa_dec_select(ab, qn, cw, *, past, per_step):
    Bd = ab.shape[0]
    R = NSA_REP * DEC_POS
    n_steps = past // NSA_SLC_LEN // per_step

    def full(a):
        return pl.BlockSpec(a.shape, lambda b: (0,) * a.ndim)

    return pl.pallas_call(
        functools.partial(_nsa_dec_select_body, past=past, per_step=per_step),
        grid=(Bd,),
        in_specs=[pl.BlockSpec((None,) + ab.shape[1:], lambda b: (b, 0, 0)),
                  pl.BlockSpec((None, NSA_GROUPS, R, LANES), lambda b: (b, 0, 0, 0))] + [full(a) for a in cw],
        out_specs=[pl.BlockSpec((None, NSA_GROUPS, R, LANES), lambda b: (b, 0, 0, 0)),
                   pl.BlockSpec((None, NSA_GROUPS, n_steps, R, LANES), lambda b: (b, 0, 0, 0, 0))],
        out_shape=[jax.ShapeDtypeStruct((Bd, NSA_GROUPS, R, LANES), F32),
                   jax.ShapeDtypeStruct((Bd, NSA_GROUPS, n_steps, R, LANES), F32)],
        compiler_params=_cparams(("parallel",)),
        name="nsa_dec_select",
    )(ab, qn, *cw)


def _moba_kmean_body(pt_ref, *refs, P):
    pages, o_ref = refs[:P], refs[P]
    ppb = MOBA_BLOCK // PAGE_SIZE
    n_planes = 2 * MOBA_HEADS
    for r in range(P // ppb):
        tot = jnp.zeros((n_planes, LANES), F32)
        for e in range(ppb):
            tot = tot + jnp.sum(pages[ppb * r + e][...].reshape(PAGE_SIZE, n_planes, LANES), axis=0)
        o_ref[r] = tot / MOBA_BLOCK


def _moba_kmean(cache, page_table, l, *, P):
    Bd, n_pages = page_table.shape
    depth = cache.shape[1]
    view = _flat_pages(cache)
    n_planes = 2 * MOBA_HEADS
    prow = PAGE_SIZE * n_planes
    ppb = MOBA_BLOCK // PAGE_SIZE
    assert n_pages % P == 0 and P % ppb == 0 and n_planes == SUBLANES and cache.shape[2] == PAGE_SIZE

    def page_map(b, s, pt, r):
        return (pt[b, s * P + r] * depth + l, 0)

    return pl.pallas_call(
        functools.partial(_moba_kmean_body, P=P),
        grid_spec=pltpu.PrefetchScalarGridSpec(
            num_scalar_prefetch=1, grid=(Bd, n_pages // P),
            in_specs=[pl.BlockSpec((prow, LANES), functools.partial(page_map, r=r)) for r in range(P)],
            out_specs=pl.BlockSpec((None, P // ppb, n_planes, LANES), lambda b, s, pt: (b, s, 0, 0))),
        out_shape=jax.ShapeDtypeStruct((Bd, n_pages // ppb, n_planes, LANES), F32),
        compiler_params=_cparams(("parallel", "arbitrary")),
        name="moba_kmean",
    )(page_table, *([view] * P))


def _moba_dec_select_body(km_ref, q_ref, bias_ref, *, per_step):
    n_planes = 2 * MOBA_HEADS
    nblk = km_ref.shape[0] // n_planes
    for h in range(MOBA_HEADS):
        km = km_ref[pl.ds(h, nblk, stride=n_planes), :]
        if nblk < LANES:
            km = jnp.concatenate([km, jnp.zeros((LANES - nblk, LANES), F32)], axis=0)
        lane = lax.broadcasted_iota(jnp.int32, (DEC_POS, LANES), 1)
        score = jnp.where(lane < nblk, _nt(q_ref[h].astype(BF16), km.astype(BF16)), -jnp.inf)
        b8 = jnp.where(_rank_desc(score, nblk, 1) < min(MOBA_TOPK, nblk), 0.0, NEG)
        for st in range(nblk // per_step):
            bias_ref[h, st] = _step_bias(b8, st, per_step)


def _moba_dec_select(kmean, qm, *, per_step):
    Bd, nblk, n_planes, _ = kmean.shape
    assert nblk <= LANES
    n_steps = nblk // per_step
    return pl.pallas_call(
        functools.partial(_moba_dec_select_body, per_step=per_step),
        grid=(Bd,),
        in_specs=[pl.BlockSpec((None, nblk * n_planes, LANES), lambda b: (b, 0, 0)),
                  pl.BlockSpec((None, MOBA_HEADS, DEC_POS, LANES), lambda b: (b, 0, 0, 0))],
        out_specs=pl.BlockSpec((None, MOBA_HEADS, n_steps, DEC_POS, LANES), lambda b: (b, 0, 0, 0, 0)),
        out_shape=jax.ShapeDtypeStruct((Bd, MOBA_HEADS, n_steps, DEC_POS, LANES), F32),
        compiler_params=_cparams(("parallel",)),
        name="moba_dec_select",
    )(kmean.reshape(Bd, nblk * n_planes, LANES), qm)


def _paged_attn_body(ptM_ref, ptm_ref, *refs, P, H, R, scale, blocksize, window, n_new):
    q_ref = refs[0]
    pages = refs[1:1 + P]
    rest = refs[1 + P:]
    if blocksize is not None:
        bias_ref, rest = rest[0], rest[1:]
    kn_ref, vn_ref, o_ref, m_ref, l_ref, acc_ref = rest
    s_id = pl.program_id(1)
    nk = P * PAGE_SIZE
    n_planes = 2 * H

    @pl.when(s_id == 0)
    def _():
        _online_init(m_ref, l_ref, acc_ref)

    if blocksize is not None:
        kblk = lax.broadcasted_iota(jnp.int32, (nk, LANES), 0) // blocksize
        onehot = (kblk == lax.broadcasted_iota(jnp.int32, (nk, LANES), 1)).astype(BF16)
    if window:
        kidx = s_id * nk + lax.broadcasted_iota(jnp.int32, (R, nk), 1)
        in_window = kidx > lax.broadcasted_iota(jnp.int32, (R, nk), 0) % DEC_POS
    heads = range(H)
    scores = []
    for h in heads:
        q = q_ref[h].astype(BF16)
        k = jnp.concatenate([_plane(pg, h, n_planes) for pg in pages], axis=0).astype(BF16)
        if blocksize is not None:
            q = jnp.concatenate([q, bias_ref[h].astype(BF16)], axis=1)
            k = jnp.concatenate([k, onehot], axis=1)
        s = _nt(q, k) * scale
        scores.append(jnp.where(in_window, s, NEG) if window else s)
    m_prev = [m_ref[h] for h in heads]
    m_new = [jnp.maximum(m_prev[h], jnp.max(scores[h], axis=-1, keepdims=True)) for h in heads]
    alpha = [jnp.exp(m_prev[h] - m_new[h]) for h in heads]
    probs = [jnp.exp(scores[h] - m_new[h]) for h in heads]
    l_new = [alpha[h] * l_ref[h] + jnp.sum(probs[h], axis=-1, keepdims=True) for h in heads]
    pv = []
    for h in heads:
        v = jnp.concatenate([_plane(pg, H + h, n_planes) for pg in pages], axis=0).astype(BF16)
        pv.append(_mm(probs[h].astype(BF16), v))
    for h in heads:
        acc_ref[h] = alpha[h] * acc_ref[h] + pv[h]
        m_ref[h] = m_new[h]
        l_ref[h] = l_new[h]

    @pl.when(s_id == pl.num_programs(1) - 1)
    def _():
        lane = lax.broadcasted_iota(jnp.int32, (R, PAGE_SIZE), 1)
        rpos = lax.broadcasted_iota(jnp.int32, (R, PAGE_SIZE), 0) % DEC_POS
        for h in range(H):
            kn = kn_ref[:, h * LANES:(h + 1) * LANES].astype(BF16)
            vn = vn_ref[:, h * LANES:(h + 1) * LANES].astype(BF16)
            s = _nt(q_ref[h].astype(BF16), kn) * scale
            s = jnp.where((lane <= rpos) & (lane < n_new), s, NEG)
            _online_update(s, vn, m_ref.at[h], l_ref.at[h], acc_ref.at[h])
            o_ref[h] = acc_ref[h] / l_ref[h]


def _paged_attn(q, cache, ptM, ptm, l, knew, vnew, bias=None, *, scale, blocksize=None, window=False, P, n_new):
    Bd, H, R, _ = q.shape
    n_pages = ptM.shape[1]
    depth = cache.shape[1]
    ppe = cache.shape[2] // PAGE_SIZE
    prow = PAGE_SIZE * 2 * H
    view = _flat_pages(cache)
    assert n_pages % P == 0 and cache.shape[4] == H

    def page_map(b, s, pM, pm, r):
        return ((pM[b, s * P + r] * depth + l) * ppe + pm[b, s * P + r], 0)

    in_specs = [pl.BlockSpec((None, H, R, LANES), lambda b, s, pM, pm: (b, 0, 0, 0))]
    in_specs += [pl.BlockSpec((prow, LANES), functools.partial(page_map, r=r)) for r in range(P)]
    args = [q] + [view] * P
    if blocksize is not None:
        in_specs.append(pl.BlockSpec((None, H, None, R, LANES), lambda b, s, pM, pm: (b, 0, s, 0, 0)))
        args.append(bias)
    for a in (knew, vnew):
        in_specs.append(pl.BlockSpec((None,) + a.shape[1:], lambda b, s, pM, pm: (b, 0, 0)))
        args.append(a)
    return pl.pallas_call(
        functools.partial(_paged_attn_body, P=P, H=H, R=R, scale=scale, blocksize=blocksize, window=window,
                          n_new=n_new),
        grid_spec=pltpu.PrefetchScalarGridSpec(
            num_scalar_prefetch=2, grid=(Bd, n_pages // P), in_specs=in_specs,
            out_specs=pl.BlockSpec((None, H, R, LANES), lambda b, s, pM, pm: (b, 0, 0, 0)),
            scratch_shapes=[pltpu.VMEM((H, R, 1), F32), pltpu.VMEM((H, R, 1), F32), pltpu.VMEM((H, R, LANES), F32)]),
        out_shape=jax.ShapeDtypeStruct((Bd, H, R, LANES), F32),
        compiler_params=_cparams(("parallel", "arbitrary")),
        name="paged_attn",
    )(ptM, ptm, *args)


def _dec_finalize_body(ocmp_ref, oslc_ref, owin_ref, gate_ref, odiff_ref, lam_ref, sub_ref, omoba_ref,
                       an_ref, ad_ref, am_ref, *, lam_init):
    gt = gate_ref[0:DEC_POS, :]
    an_ref[...] = jnp.zeros(an_ref.shape, F32)
    ad_ref[...] = jnp.zeros(ad_ref.shape, F32)
    am_ref[...] = jnp.zeros(am_ref.shape, F32)
    for g in range(NSA_GROUPS):
        for h in range(NSA_REP):
            head = NSA_REP * g + h
            rows = slice(h * DEC_POS, (h + 1) * DEC_POS)
            out = (_lane_pick(gt, head) * ocmp_ref[g, rows, :]
                   + _lane_pick(gt, NSA_HEADS + head) * oslc_ref[g, rows, :]
                   + _lane_pick(gt, 2 * NSA_HEADS + head) * owin_ref[g, rows, :])
            an_ref[0:DEC_POS, head * LANES:(head + 1) * LANES] = out
    for h in range(DIFF_HEADS):
        o = odiff_ref[h]
        ad_ref[0:DEC_POS, h * LANES:(h + 1) * LANES] = _diff_finish(o[:DEC_POS], o[DEC_POS:], lam_ref, sub_ref, lam_init)
    for h in range(MOBA_HEADS):
        am_ref[0:DEC_POS, h * LANES:(h + 1) * LANES] = omoba_ref[h]


def _dec_finalize(o_cmp, o_slc, o_win, proj3, o_diff, lam, subln, o_moba, *, lam_init):
    Bd = o_cmp.shape[0]

    def b4(a):
        return pl.BlockSpec((None,) + a.shape[1:], lambda b: (b, 0, 0, 0))

    outs = [(NSA_HEADS * LANES), (DIFF_HEADS * LANES), (MOBA_HEADS * LANES)]
    return pl.pallas_call(
        functools.partial(_dec_finalize_body, lam_init=lam_init),
        grid=(Bd,),
        in_specs=[b4(o_cmp), b4(o_slc), b4(o_win),
                  pl.BlockSpec((None, DEC_ROWS, LANES), lambda b: (b, 0, BGATE)),
                  b4(o_diff), pl.BlockSpec(lam.shape, lambda b: (0, 0)), pl.BlockSpec((1, LANES), lambda b: (0, 0)),
                  b4(o_moba)],
        out_specs=[pl.BlockSpec((None, DEC_ROWS, c), lambda b: (b, 0, 0)) for c in outs],
        out_shape=[jax.ShapeDtypeStruct((Bd, DEC_ROWS, c), F32) for c in outs],
        compiler_params=_cparams(("parallel",)),
        name="dec_finalize",
    )(o_cmp, o_slc, o_win, proj3, o_diff, lam, subln.reshape(1, LANES), o_moba)


def _head_major(cols, heads):
    Bd = cols.shape[0]
    return cols[:, :DEC_POS].reshape(Bd, DEC_POS, heads, HEAD_DIM).transpose(0, 2, 1, 3)


def _new_rows(proj3, blk, heads):
    rows = _cols(proj3, blk, heads)
    return jnp.pad(rows, ((0, 0), (0, PAGE_SIZE - rows.shape[1]), (0, 0)))


def _sample_attn(proj3, caches, page_table, P, l, n_new, *, scan_pages=8):
    cache_cmp, cache_slc, cache_win, cache_diff, cache_moba = caches
    Bd = proj3.shape[0]
    n_pages = page_table.shape[1]
    past = n_pages * PAGE_SIZE
    wkeep = cache_win.shape[2]
    assert n_new <= DEC_POS and wkeep == NSA_WINDOW and past >= NSA_WINDOW
    win_pages = wkeep // PAGE_SIZE
    zeros_pt = jnp.zeros_like(page_table)
    win_major = jnp.broadcast_to(jnp.arange(Bd, dtype=jnp.int32)[:, None], (Bd, win_pages))
    win_minor = jnp.broadcast_to(jnp.arange(win_pages, dtype=jnp.int32)[None, :], (Bd, win_pages))
    half = NSA_CMP_STRIDE * HEAD_DIM
    step_keys = scan_pages * PAGE_SIZE

    def w1ab(w1):
        return jnp.concatenate([w1[:half].reshape(NSA_CMP_STRIDE, LANES, LANES),
                                w1[half:].reshape(NSA_CMP_STRIDE, LANES, LANES)], axis=-1)
    wab = jnp.stack([w1ab(P["nsa_w1_k"][l]), w1ab(P["nsa_w1_v"][l])]).astype(BF16)
    ab = _cmp_scan(cache_cmp, page_table, wab, l, P=scan_pages)
    qn = _head_major(_cols(proj3, BQ, NSA_HEADS), NSA_HEADS).reshape(Bd, NSA_GROUPS, NSA_REP * DEC_POS, LANES)
    cw = (P["nsa_pe_k"][l], P["nsa_w1_k"][l], P["nsa_w2_k"][l],
          P["nsa_pe_v"][l], P["nsa_w1_v"][l], P["nsa_w2_v"][l])
    o_cmp, bias_slc = _nsa_dec_select(ab, qn, cw, past=past, per_step=step_keys // NSA_SLC_LEN)
    sc128 = HEAD_DIM ** -0.5
    o_slc = _paged_attn(qn, cache_slc, page_table, zeros_pt, l, _new_rows(proj3, BKS, NSA_GROUPS),
                        _new_rows(proj3, BVS, NSA_GROUPS), bias_slc, scale=sc128, blocksize=NSA_SLC_LEN,
                        P=scan_pages, n_new=n_new)
    o_win = _paged_attn(qn, cache_win, win_major, win_minor, l, _new_rows(proj3, BKW, NSA_GROUPS),
                        _new_rows(proj3, BVW, NSA_GROUPS), scale=sc128, window=True, P=win_pages, n_new=n_new)
    qd = _head_major(_cols(proj3, BDQ, DIFF_HEADS), DIFF_HEADS)
    lane = jnp.arange(LANES)
    qd = jnp.concatenate([jnp.where(lane < DIFF_QK_DIM, qd, 0.0), jnp.where(lane >= DIFF_QK_DIM, qd, 0.0)], axis=2)
    o_diff = _paged_attn(qd, cache_diff, page_table, zeros_pt, l, _new_rows(proj3, BDK, DIFF_HEADS),
                         _new_rows(proj3, BDV, DIFF_HEADS), scale=DIFF_QK_DIM ** -0.5, P=scan_pages, n_new=n_new)
    qm = _head_major(_cols(proj3, BMQ, MOBA_HEADS), MOBA_HEADS)
    kmean = _moba_kmean(cache_moba, page_table, l, P=scan_pages)
    bias_moba = _moba_dec_select(kmean, qm, per_step=step_keys // MOBA_BLOCK)
    o_moba = _paged_attn(qm, cache_moba, page_table, zeros_pt, l, _new_rows(proj3, BMK, MOBA_HEADS),
                         _new_rows(proj3, BMV, MOBA_HEADS), bias_moba, scale=sc128, blocksize=MOBA_BLOCK,
                         P=scan_pages, n_new=n_new)
    return _dec_finalize(o_cmp, o_slc, o_win, proj3, o_diff, P["diff_lambda"][l], P["diff_subln"][l], o_moba,
                         lam_init=_lam_init(l))


def _sample_trunk(x_sample, caches, page_table, P):
    cache_cmp, cache_slc, cache_win, cache_diff, cache_moba, cache_mem = caches
    Bd, n_new, D = x_sample.shape
    depth = P["w_in"].shape[0]
    past = page_table.shape[1] * PAGE_SIZE
    pos = past + jnp.arange(DEC_ROWS, dtype=jnp.int32)
    tabs = [jnp.tile(t, (Bd, 1)) for t in _rope_tables(pos)]
    x = jnp.pad(x_sample, ((0, 0), (0, DEC_ROWS - n_new), (0, 0))).reshape(Bd * DEC_ROWS, D)
    mem4 = cache_mem.reshape(Bd, depth, MEM_TOKENS, 2 * MEM_HEADS * MEM_DIM)
    rows = [[] for _ in range(5)]
    for l in range(depth):
        proj = _norm_matmul(x, P["norm_mix"][l], _regroup_w_in(P["w_in"][l]), tabs, tm=Bd * DEC_ROWS, tn=256)
        proj3 = proj.reshape(Bd, DEC_ROWS, NPROJ)
        new3 = proj3[:, :n_new]
        rows[0].append(_cache_rows(new3, BKC, BVC, NSA_GROUPS))
        rows[1].append(_cache_rows(new3, BKS, BVS, NSA_GROUPS))
        rows[2].append(jnp.concatenate(
            [cache_win[:, l], _cache_rows(new3, BKW, BVW, NSA_GROUPS)], axis=1)[:, n_new:])
        rows[3].append(_cache_rows(new3, BDK, BDV, DIFF_HEADS))
        rows[4].append(_cache_rows(new3, BMK, BMV, MOBA_HEADS))
        a_nsa, a_diff, a_moba = _sample_attn(proj3, caches[:5], page_table, P, l, n_new)
        x3 = _out_mem(a_nsa, a_diff, a_moba, proj3, x.reshape(Bd, DEC_ROWS, D), P["w_out"][l].astype(BF16),
                      P["norm_mem_x"][l].reshape(1, D), P["w_mem_q"][l].astype(BF16), mem4[:, l],
                      P["w_mem_o"][l].astype(BF16), tm=DEC_ROWS)
        x = x3.reshape(Bd * DEC_ROWS, D)
    y = _rmsnorm(x, P["norm_final"], tm=Bd * DEC_ROWS).reshape(Bd, DEC_ROWS, D)[:, :n_new]
    return y, [jnp.stack(r, axis=1) for r in rows]


def kernel(x_prompt, x_sample, mem_prompt, cache_nsa_cmp, cache_nsa_slc, cache_nsa_win, cache_diff, cache_moba, cache_mem, page_table, norm_mix, w_in, w_out, nsa_pe_k, nsa_pe_v, nsa_w1_k, nsa_w2_k, nsa_w1_v, nsa_w2_v, diff_lambda, diff_subln, norm_mem_x, norm_mem_m, w_mem_q, w_mem_kv, w_mem_o, norm_final):
    P = {"norm_mix": norm_mix, "w_in": w_in, "w_out": w_out, "nsa_pe_k": nsa_pe_k, "nsa_pe_v": nsa_pe_v,
         "nsa_w1_k": nsa_w1_k, "nsa_w2_k": nsa_w2_k, "nsa_w1_v": nsa_w1_v, "nsa_w2_v": nsa_w2_v,
         "diff_lambda": diff_lambda, "diff_subln": diff_subln, "norm_mem_x": norm_mem_x,
         "norm_mem_m": norm_mem_m, "w_mem_q": w_mem_q, "w_mem_kv": w_mem_kv, "w_mem_o": w_mem_o,
         "norm_final": norm_final}
    y_prompt, (p_cmp, p_slc, p_win, p_diff, p_moba), p_mem = _prompt_trunk(x_prompt, mem_prompt, P)
    caches = (cache_nsa_cmp, cache_nsa_slc, cache_nsa_win, cache_diff, cache_moba, cache_mem)
    y_sample, (s_cmp, s_slc, s_win, s_diff, s_moba) = _sample_trunk(x_sample, caches, page_table, P)
    return (y_prompt, y_sample, p_cmp, p_slc, p_win, p_diff, p_moba, p_mem, s_cmp, s_slc, s_win, s_diff, s_moba)
```

```python
import functools
import math

import jax
import jax.numpy as jnp
from jax import lax
from jax.experimental import pallas as pl
from jax.experimental.pallas import tpu as pltpu

F32 = jnp.float32
BF16 = jnp.bfloat16

D_MODEL = 2048
HEAD_DIM = 128
ROPE_THETA = 500000.0
NORM_EPS = 1e-6
PAGE_SIZE = 128

NSA_HEADS = 8
NSA_GROUPS = 2
NSA_REP = NSA_HEADS // NSA_GROUPS
NSA_CMP_STRIDE = 16
NSA_SLC_LEN = 64
NSA_TOPN = 16
NSA_WINDOW = 512
NSA_FORCE_BONUS = 1000.0
NSA_COL_GROUPS = 2
PROJ_TILE = 512
DIFF_HEADS = 4
DIFF_QK_DIM = HEAD_DIM // 2
MOBA_HEADS = 4
MOBA_BLOCK = 256
MOBA_TOPK = 3
MEM_TOKENS = 256
MEM_HEADS = 4
MEM_DIM = 128

NEG = -1e30
LANES = 128
SUBLANES = 8
VMEM_LIMIT_BYTES = 56 * 1024 * 1024

BQ, BKC, BKS, BKW, BMQ, BMK = 0, 8, 10, 12, 14, 18
BDQ, BDK = 22, 26
BVC, BVS, BVW, BNZ, BDV, BDZ, BMV, BMZ = 30, 32, 34, 36, 44, 48, 52, 56
BGATE = 60
NBLK = 64
NPROJ = NBLK * LANES
_W_IN_SEGMENTS = (
    (0, 1024), (1024, 1280), (1536, 1792), (2048, 2304), (5656, 6168), (6168, 6680),
    (3608, 4120), (4120, 4632),
    (1280, 1536), (1792, 2048), (2304, 2560), (2584, 3608), (4632, 5144), (5144, 5656),
    (6680, 7192), (7192, 7704),
    (2560, 2584),
)


def _round_up(n, m):
    return -(-n // m) * m


def _nt(a, b):
    return lax.dot_general(a, b, (((1,), (1,)), ((), ())), preferred_element_type=F32)


def _mm(a, b):
    return jnp.dot(a, b, preferred_element_type=F32)


def _silu(z):
    return z * jax.nn.sigmoid(z)


def _cparams(sem):
    return pltpu.CompilerParams(dimension_semantics=sem, vmem_limit_bytes=VMEM_LIMIT_BYTES)


def _norm_matmul_body(*refs, tn, splits, rope):
    if rope:
        x_ref, g_ref, w_ref, c1, a1, b1, c2, a2, b2, o_ref, h_ref = refs
    else:
        x_ref, g_ref, w_ref, o_ref, h_ref = refs
    j = pl.program_id(1)

    @pl.when(j == 0)
    def _():
        x = x_ref[...]
        ms = jnp.mean(x * x, axis=-1, keepdims=True)
        h_ref[...] = (x * lax.rsqrt(ms + NORM_EPS) * g_ref[...]).astype(BF16)

    acc = _mm(h_ref[...], w_ref[...])
    if not rope:
        o_ref[...] = acc
        return

    def epilogue(kind, blk):
        if kind == "rot128":
            half = HEAD_DIM // 8
            return blk * c1[...] + pltpu.roll(blk, LANES - half, 1) * a1[...] + pltpu.roll(blk, half, 1) * b1[...]
        if kind == "rot64":
            half = DIFF_QK_DIM // 8
            return blk * c2[...] + pltpu.roll(blk, LANES - half, 1) * a2[...] + pltpu.roll(blk, half, 1) * b2[...]
        if kind == "gate":
            return jax.nn.sigmoid(blk)
        return blk

    for lo, hi, kinds in splits:
        @pl.when((j >= lo) & (j <= hi))
        def _(kinds=kinds):
            for k, kind in enumerate(kinds):
                o_ref[:, k * LANES:(k + 1) * LANES] = epilogue(kind, acc[:, k * LANES:(k + 1) * LANES])


def _tile_patterns(tn):
    kinds = (["rot128"] * (BDQ - BQ) + ["rot64"] * (BVC - BDQ) + ["plain"] * (BGATE - BVC) + ["gate"]
             + ["plain"] * (NBLK - BGATE - 1))
    per = tn // LANES
    tiles = [tuple(kinds[t * per:(t + 1) * per]) for t in range(NBLK // per)]
    runs = []
    for t, pat in enumerate(tiles):
        if runs and runs[-1][2] == pat:
            runs[-1] = (runs[-1][0], t, pat)
        else:
            runs.append((t, t, pat))
    return tuple(runs)


def _norm_matmul(x, gain, w, tabs=None, *, tm, tn):
    M, K = x.shape
    Np = w.shape[1]
    rope = tabs is not None
    splits = _tile_patterns(tn) if rope else None
    in_specs = [
        pl.BlockSpec((tm, K), lambda i, j: (i, 0)),
        pl.BlockSpec((1, K), lambda i, j: (0, 0)),
        pl.BlockSpec((K, tn), lambda i, j: (0, j)),
    ]
    args = [x, gain.reshape(1, K), w]
    if rope:
        period = tabs[0].shape[0] // tm
        for t in tabs:
            in_specs.append(pl.BlockSpec((tm, LANES), lambda i, j: (i % period, 0)))
            args.append(t)
    return pl.pallas_call(
        functools.partial(_norm_matmul_body, tn=tn, splits=splits, rope=rope),
        grid=(M // tm, Np // tn),
        in_specs=in_specs,
        out_specs=pl.BlockSpec((tm, tn), lambda i, j: (i, j)),
        out_shape=jax.ShapeDtypeStruct((M, Np), F32),
        scratch_shapes=[pltpu.VMEM((tm, K), BF16)],
        compiler_params=_cparams(("parallel", "arbitrary")),
        name="norm_matmul",
    )(*args)


def _rope_tables(pos):
    posf = pos.astype(F32)[:, None]
    n = pos.shape[0]

    def one(width, reps):
        rd = width // 4
        half = rd // 2
        inv = ROPE_THETA ** (-2.0 * jnp.arange(half, dtype=F32) / rd)
        ang = posf * inv[None, :]
        c, s = jnp.cos(ang), jnp.sin(ang)
        z = jnp.zeros((n, width - rd), F32)
        zh = jnp.zeros((n, half), F32)
        C = jnp.concatenate([c, c, jnp.ones((n, width - rd), F32)], axis=1)
        A = jnp.concatenate([-s, zh, z], axis=1)
        B = jnp.concatenate([zh, s, z], axis=1)
        return [jnp.tile(t, (1, reps)) for t in (C, A, B)]

    return one(HEAD_DIM, 1) + one(DIFF_QK_DIM, 2)


def _online_init(m_ref, l_ref, acc_ref):
    m_ref[...] = jnp.full(m_ref.shape, NEG, F32)
    l_ref[...] = jnp.zeros(l_ref.shape, F32)
    acc_ref[...] = jnp.zeros(acc_ref.shape, F32)


def _online_update(s, v, m_ref, l_ref, acc_ref):
    m_prev = m_ref[...]
    m_new = jnp.maximum(m_prev, jnp.max(s, axis=-1, keepdims=True))
    alpha = jnp.exp(m_prev - m_new)
    p = jnp.exp(s - m_new)
    l_ref[...] = alpha * l_ref[...] + jnp.sum(p, axis=-1, keepdims=True)
    acc_ref[...] = alpha * acc_ref[...] + _mm(p.astype(BF16), v)
    m_ref[...] = m_new


def _online_update_t(s, vt, m_ref, l_ref, acc_ref):
    m_prev = m_ref[...]
    m_new = jnp.maximum(m_prev, jnp.max(s, axis=0, keepdims=True))
    alpha = jnp.exp(m_prev - m_new)
    p = jnp.exp(s - m_new)
    l_ref[...] = alpha * l_ref[...] + jnp.sum(p, axis=0, keepdims=True)
    acc_ref[...] = alpha * acc_ref[...] + _mm(vt, p.astype(BF16))
    m_ref[...] = m_new


def _staged_update_t(scores, vts, m_ref, l_ref, acc_ref, idxs):
    n = range(len(scores))
    m_prev = [m_ref[idxs[j]] for j in n]
    m_new = [jnp.maximum(m_prev[j], jnp.max(scores[j], axis=0, keepdims=True)) for j in n]
    alpha = [jnp.exp(m_prev[j] - m_new[j]) for j in n]
    probs = [jnp.exp(scores[j] - m_new[j]) for j in n]
    l_new = [alpha[j] * l_ref[idxs[j]] + jnp.sum(probs[j], axis=0, keepdims=True) for j in n]
    pv = [_mm(vts[j], probs[j].astype(BF16)) for j in n]
    for j in n:
        acc_ref[idxs[j]] = alpha[j] * acc_ref[idxs[j]] + pv[j]
        m_ref[idxs[j]] = m_new[j]
        l_ref[idxs[j]] = l_new[j]


def _masked_softmax(s, mask, axis):
    m = jnp.max(jnp.where(mask, s, NEG), axis=axis, keepdims=True)
    m = jnp.where(m > 0.5 * NEG, m, 0.0)
    e = jnp.where(mask, jnp.exp(s - m), 0.0)
    return e / jnp.maximum(jnp.sum(e, axis=axis, keepdims=True), 1e-30)


def _rank_desc(score, ncand, axis):
    idx = lax.broadcasted_iota(jnp.int32, score.shape, axis)
    rank = jnp.zeros(score.shape, F32)
    for c in range(ncand):
        cand = score[:, c:c + 1] if axis == 1 else score[c:c + 1, :]
        before = (cand > score) | ((cand == score) & (c < idx))
        rank = rank + before.astype(F32)
    return rank


def _lane_pick(x, idx):
    lane = lax.broadcasted_iota(jnp.int32, x.shape, 1)
    return jnp.sum(jnp.where(lane == idx, x, 0.0), axis=-1, keepdims=True)


def _compress(src_ref, pe_ref, w1_ref, w2_ref, nc):
    half = NSA_CMP_STRIDE * HEAD_DIM
    acc_a = jnp.zeros((nc, LANES), F32)
    acc_b = jnp.zeros((nc, LANES), F32)
    for j in range(NSA_CMP_STRIDE):
        rows = src_ref[pl.ds(j, nc, stride=NSA_CMP_STRIDE), :].astype(BF16)
        acc_a = acc_a + _mm(rows, w1_ref[j * LANES:(j + 1) * LANES, :].astype(BF16))
        acc_b = acc_b + _mm(rows, w1_ref[half + j * LANES:half + (j + 1) * LANES, :].astype(BF16))
    return _compress_finish(acc_a, acc_b, pe_ref, w1_ref, w2_ref, nc)


def _pe_term(pe_ref, w1_ref):
    acc = jnp.zeros((SUBLANES, LANES), F32)
    for j in range(2 * NSA_CMP_STRIDE):
        pj = jnp.broadcast_to(pe_ref[j:j + 1, :], (SUBLANES, LANES)).astype(BF16)
        acc = acc + _mm(pj, w1_ref[j * LANES:(j + 1) * LANES, :].astype(BF16))
    return acc[0:1, :]


def _compress_finish(acc_a, acc_b, pe_ref, w1_ref, w2_ref, nc):
    row = lax.broadcasted_iota(jnp.int32, (nc, LANES), 0)
    hb = jnp.where(row < nc - 1, pltpu.roll(acc_b, nc - 1, 0), 0.0)
    hid = _silu(acc_a + hb + _pe_term(pe_ref, w1_ref))
    return _mm(hid.astype(BF16), w2_ref[...].astype(BF16))


def _slc_weights(shape, cmp_axis):
    r = lax.broadcasted_iota(jnp.int32, shape, cmp_axis)
    c = lax.broadcasted_iota(jnp.int32, shape, 1 - cmp_axis)
    d = r - 4 * c
    return jnp.where((d == -1) | (d == 3), 1.0, jnp.where((d >= 0) & (d <= 2), 2.0, 0.0)).astype(F32)


def _transpose_into(dst_ref, src_ref, rows, chunk):
    for c in range(rows // chunk):
        dst_ref[:, c * chunk:(c + 1) * chunk] = src_ref[c * chunk:(c + 1) * chunk, :].T.astype(BF16)


def _nsa_prompt_body(q_ref, kc_ref, ks_ref, kw_ref, vc_ref, vs_ref, vw_ref, gate_ref,
                     pek_ref, w1k_ref, w2k_ref, pev_ref, w1v_ref, w2v_ref, o_ref,
                     kcs, vct, kaug, vst, kwb, vwt, gt_ref, m_ref, l_ref, acc_ref, *, T, tq):
    g = pl.program_id(1)
    i = pl.program_id(2)
    nc = T // NSA_CMP_STRIDE
    nsb = T // NSA_SLC_LEN
    nsbr = _round_up(nsb, SUBLANES)
    nsel = min(NSA_TOPN, nsb)
    R = NSA_REP * tq
    scale = HEAD_DIM ** -0.5

    @pl.when(i == 0)
    def _():
        kcs[...] = _compress(kc_ref, pek_ref, w1k_ref, w2k_ref, nc).astype(BF16)
        vct[...] = _compress(vc_ref, pev_ref, w1v_ref, w2v_ref, nc).T.astype(BF16)
        kaug[:, :LANES] = ks_ref[...].astype(BF16)
        rblk = lax.broadcasted_iota(jnp.int32, (T, LANES), 0) // NSA_SLC_LEN
        lane = lax.broadcasted_iota(jnp.int32, (T, LANES), 1)
        kaug[:, LANES:] = (rblk == lane).astype(BF16)
        kwb[...] = kw_ref[...].astype(BF16)
        _transpose_into(vst, vs_ref, T, tq)
        _transpose_into(vwt, vw_ref, T, tq)

    q = q_ref[...]
    qt = jnp.concatenate([q[:, h * LANES:(h + 1) * LANES].T for h in range(NSA_REP)], axis=1).astype(BF16)

    s = _mm(kcs[...], qt) * scale
    crow = lax.broadcasted_iota(jnp.int32, (nc, R), 0)
    t_col = i * tq + lax.broadcasted_iota(jnp.int32, (nc, R), 1) % tq
    pc = _masked_softmax(s, (NSA_CMP_STRIDE * crow + 2 * NSA_CMP_STRIDE - 1) <= t_col, 0)
    o_cmp = _mm(vct[...], pc.astype(BF16))
    pg = pc[:, 0:tq]
    for h in range(1, NSA_REP):
        pg = pg + pc[:, h * tq:(h + 1) * tq]

    slc = jnp.dot(_slc_weights((LANES, nc), 1), pg, precision=lax.Precision.HIGHEST,
                  preferred_element_type=F32)[:nsbr]
    blk = lax.broadcasted_iota(jnp.int32, (nsbr, tq), 0)
    tb = (i * tq + lax.broadcasted_iota(jnp.int32, (nsbr, tq), 1)) // NSA_SLC_LEN
    valid = blk <= tb
    forced = (blk == 0) | (blk == tb) | (blk == tb - 1)
    score = jnp.where(valid, slc + jnp.where(forced, NSA_FORCE_BONUS, 0.0), -jnp.inf)
    sel = (_rank_desc(score, nsb, 0) < nsel) & valid
    bias = jnp.where(sel, 0.0, NEG)
    if nsbr < LANES:
        bias = jnp.concatenate([bias, jnp.full((LANES - nsbr, tq), NEG, F32)], axis=0)
    bias = bias.astype(BF16)
    qaug = jnp.concatenate([qt, jnp.concatenate([bias] * NSA_REP, axis=1)], axis=0)

    gw = R // NSA_COL_GROUPS
    cols = [(slice(None), slice(j * gw, (j + 1) * gw)) for j in range(NSA_COL_GROUPS)]
    krow = lax.broadcasted_iota(jnp.int32, (tq, gw), 0)
    tloc = lax.broadcasted_iota(jnp.int32, (tq, gw), 1) % tq

    def attend(k_tile, vt_tile, q_all, mask):
        scores = []
        for c in cols:
            s2 = _mm(k_tile, q_all[c]) * scale
            scores.append(s2 if mask is None else jnp.where(mask, s2, NEG))
        _staged_update_t(scores, [vt_tile] * NSA_COL_GROUPS, m_ref, l_ref, acc_ref, cols)

    _online_init(m_ref, l_ref, acc_ref)

    def slc_step(kt, carry):
        off = pl.multiple_of(kt * tq, tq)
        attend(kaug[pl.ds(off, tq), :], vst[:, pl.ds(off, tq)], qaug, None)
        return carry

    lax.fori_loop(0, i, slc_step, 0)
    off_d = pl.multiple_of(i * tq, tq)
    attend(kaug[pl.ds(off_d, tq), :], vst[:, pl.ds(off_d, tq)], qaug, krow <= tloc)
    o_slc = acc_ref[...] / l_ref[...]

    _online_init(m_ref, l_ref, acc_ref)
    nw = NSA_WINDOW // tq
    for d in range(nw, -1, -1):
        @pl.when(i >= d)
        def _(d=d):
            off = pl.multiple_of((i - d) * tq, tq)
            mask = (krow > tloc) if d == nw else ((krow <= tloc) if d == 0 else None)
            attend(kwb[pl.ds(off, tq), :], vwt[:, pl.ds(off, tq)], qt, mask)
    o_win = acc_ref[...] / l_ref[...]

    gt_ref[...] = gate_ref[...].T

    def gate_row(branch):
        return jnp.concatenate(
            [gt_ref[pl.ds(branch * NSA_HEADS + NSA_REP * g + h, 1), :] for h in range(NSA_REP)], axis=1)

    out = gate_row(0) * o_cmp + gate_row(1) * o_slc + gate_row(2) * o_win
    for h in range(NSA_REP):
        o_ref[:, h * LANES:(h + 1) * LANES] = out[:, h * tq:(h + 1) * tq].T


def _nsa_prompt(proj, cw, *, B, T, tq):
    nqt = T // tq
    nc = T // NSA_CMP_STRIDE
    R = NSA_REP * tq
    assert T % NSA_SLC_LEN == 0 and NSA_WINDOW % tq == 0 and T // NSA_SLC_LEN <= LANES and nc <= LANES

    def kv_spec(blk):
        return pl.BlockSpec((T, LANES), lambda b, g, i: (b, blk + g))

    def full(a):
        return pl.BlockSpec(a.shape, lambda b, g, i: (0,) * a.ndim)

    in_specs = [
        pl.BlockSpec((tq, NSA_REP * LANES), lambda b, g, i: (b * nqt + i, g)),
        kv_spec(BKC), kv_spec(BKS), kv_spec(BKW), kv_spec(BVC), kv_spec(BVS), kv_spec(BVW),
        pl.BlockSpec((tq, LANES), lambda b, g, i: (b * nqt + i, BGATE)),
    ] + [full(a) for a in cw]
    return pl.pallas_call(
        functools.partial(_nsa_prompt_body, T=T, tq=tq),
        grid=(B, NSA_GROUPS, nqt),
        in_specs=in_specs,
        out_specs=pl.BlockSpec((tq, NSA_REP * LANES), lambda b, g, i: (b * nqt + i, g)),
        out_shape=jax.ShapeDtypeStruct((B * T, NSA_HEADS * LANES), F32),
        scratch_shapes=[
            pltpu.VMEM((nc, LANES), BF16), pltpu.VMEM((LANES, nc), BF16),
            pltpu.VMEM((T, 2 * LANES), BF16), pltpu.VMEM((LANES, T), BF16),
            pltpu.VMEM((T, LANES), BF16), pltpu.VMEM((LANES, T), BF16),
            pltpu.VMEM((LANES, tq), F32),
            pltpu.VMEM((1, R), F32), pltpu.VMEM((1, R), F32), pltpu.VMEM((LANES, R), F32),
        ],
        compiler_params=_cparams(("parallel", "parallel", "arbitrary")),
        name="nsa_prompt",
    )(proj, proj, proj, proj, proj, proj, proj, proj, *cw)


def _diff_lambda(lam_ref, lam_init):
    lp = lam_ref[...]
    return (jnp.exp(jnp.sum(lp[0:1] * lp[1:2], axis=-1, keepdims=True))
            - jnp.exp(jnp.sum(lp[2:3] * lp[3:4], axis=-1, keepdims=True)) + lam_init)


def _diff_finish(o1, o2, lam_ref, sub_ref, lam_init):
    a = o1 - _diff_lambda(lam_ref, lam_init) * o2
    ms = jnp.mean(a * a, axis=-1, keepdims=True)
    return a * lax.rsqrt(ms + NORM_EPS) * sub_ref[...] * (1.0 - lam_init)


def _head_specs(nqt, T, tq, qblk, kblk, vblk, heads):
    specs = [pl.BlockSpec((tq, LANES), functools.partial(lambda b, i, c: (b * nqt + i, c), c=qblk + h))
             for h in range(heads)]
    for blk in (kblk, vblk):
        specs += [pl.BlockSpec((T, LANES), functools.partial(lambda b, i, c: (b, c), c=blk + h))
                  for h in range(heads)]
    return specs


def _diff_prompt_body(*refs, T, tq, lam_init):
    nh = DIFF_HEADS
    q_refs, k_refs, v_refs = refs[:nh], refs[nh:2 * nh], refs[2 * nh:3 * nh]
    lam_ref, sub_ref, o_ref, kb, vt, m_ref, l_ref, acc_ref = refs[3 * nh:]
    i = pl.program_id(1)
    scale = DIFF_QK_DIM ** -0.5
    heads = range(nh)

    @pl.when(i == 0)
    def _():
        for h in heads:
            kb[h] = k_refs[h][...].astype(BF16)
            _transpose_into(vt.at[h], v_refs[h], T, tq)

    qts = []
    for h in heads:
        q = q_refs[h][...]
        lane = lax.broadcasted_iota(jnp.int32, q.shape, 1)
        qts.append(jnp.concatenate([jnp.where(lane < DIFF_QK_DIM, q, 0.0).T,
                                    jnp.where(lane >= DIFF_QK_DIM, q, 0.0).T], axis=1).astype(BF16))
    R = 2 * tq
    krow = lax.broadcasted_iota(jnp.int32, (tq, R), 0)
    tloc = lax.broadcasted_iota(jnp.int32, (tq, R), 1) % tq
    idxs = [(h,) for h in heads]
    _online_init(m_ref, l_ref, acc_ref)

    def attend(off, mask):
        scores = []
        for h in heads:
            s = _mm(kb[h, pl.ds(off, tq), :], qts[h]) * scale
            scores.append(s if mask is None else jnp.where(mask, s, NEG))
        _staged_update_t(scores, [vt[h, :, pl.ds(off, tq)] for h in heads], m_ref, l_ref, acc_ref, idxs)

    def step(kt, carry):
        attend(pl.multiple_of(kt * tq, tq), None)
        return carry

    lax.fori_loop(0, i, step, 0)
    attend(pl.multiple_of(i * tq, tq), krow <= tloc)
    for h in heads:
        o = acc_ref[h] / l_ref[h]
        o_ref[:, h * LANES:(h + 1) * LANES] = _diff_finish(o[:, :tq].T, o[:, tq:].T, lam_ref, sub_ref, lam_init)


def _diff_prompt(proj, lam, subln, *, B, T, tq, lam_init):
    nqt = T // tq
    nh = DIFF_HEADS
    return pl.pallas_call(
        functools.partial(_diff_prompt_body, T=T, tq=tq, lam_init=lam_init),
        grid=(B, nqt),
        in_specs=_head_specs(nqt, T, tq, BDQ, BDK, BDV, nh) + [
            pl.BlockSpec(lam.shape, lambda b, i: (0, 0)),
            pl.BlockSpec((1, LANES), lambda b, i: (0, 0)),
        ],
        out_specs=pl.BlockSpec((tq, nh * LANES), lambda b, i: (b * nqt + i, 0)),
        out_shape=jax.ShapeDtypeStruct((B * T, nh * LANES), F32),
        scratch_shapes=[
            pltpu.VMEM((nh, T, LANES), BF16), pltpu.VMEM((nh, LANES, T), BF16),
            pltpu.VMEM((nh, 1, 2 * tq), F32), pltpu.VMEM((nh, 1, 2 * tq), F32),
            pltpu.VMEM((nh, LANES, 2 * tq), F32),
        ],
        compiler_params=_cparams(("parallel", "arbitrary")),
        name="diff_prompt",
    )(*([proj] * (3 * nh)), lam, subln.reshape(1, LANES))


def _moba_prompt_body(*refs, T):
    nh = MOBA_HEADS
    q_refs, k_refs, v_refs = refs[:nh], refs[nh:2 * nh], refs[2 * nh:3 * nh]
    o_ref, kb, vt, km, bias_ref, m_ref, l_ref, acc_ref = refs[3 * nh:]
    i = pl.program_id(1)
    tq = MOBA_BLOCK
    nb = T // MOBA_BLOCK
    nbr = _round_up(nb, SUBLANES)
    scale = HEAD_DIM ** -0.5
    heads = range(nh)

    @pl.when(i == 0)
    def _():
        km[...] = jnp.zeros(km.shape, BF16)
        for h in heads:
            kb[h] = k_refs[h][...].astype(BF16)
            _transpose_into(vt.at[h], v_refs[h], T, tq)
            for j in range(nb):
                km[h, j:j + 1, :] = jnp.mean(k_refs[h][j * tq:(j + 1) * tq, :], axis=0, keepdims=True).astype(BF16)

    qts = [q_refs[h][...].T.astype(BF16) for h in heads]
    blk = lax.broadcasted_iota(jnp.int32, (nbr, tq), 0)
    past = blk < i
    for h in heads:
        score = jnp.where(past, _mm(km[h], qts[h])[:nbr], -jnp.inf)
        sel = (_rank_desc(score, nb, 0) < min(MOBA_TOPK, nb)) & past
        bias_ref[h] = jnp.where(sel, 0.0, NEG)
    krow = lax.broadcasted_iota(jnp.int32, (tq, tq), 0)
    tloc = lax.broadcasted_iota(jnp.int32, (tq, tq), 1)
    idxs = [(h,) for h in heads]
    _online_init(m_ref, l_ref, acc_ref)

    def step(kt, carry):
        off = pl.multiple_of(kt * tq, tq)
        scores = [_mm(kb[h, pl.ds(off, tq), :], qts[h]) * scale + bias_ref[h, pl.ds(kt, 1), :] for h in heads]
        _staged_update_t(scores, [vt[h, :, pl.ds(off, tq)] for h in heads], m_ref, l_ref, acc_ref, idxs)
        return carry

    lax.fori_loop(0, i, step, 0)
    off = pl.multiple_of(i * tq, tq)
    scores = [jnp.where(krow <= tloc, _mm(kb[h, pl.ds(off, tq), :], qts[h]) * scale, NEG) for h in heads]
    _staged_update_t(scores, [vt[h, :, pl.ds(off, tq)] for h in heads], m_ref, l_ref, acc_ref, idxs)
    for h in heads:
        o_ref[:, h * LANES:(h + 1) * LANES] = (acc_ref[h] / l_ref[h]).T


def _moba_prompt(proj, *, B, T):
    tq = MOBA_BLOCK
    assert T % tq == 0 and T // tq <= LANES
    nqt = T // tq
    nh = MOBA_HEADS
    nbr = _round_up(T // MOBA_BLOCK, SUBLANES)
    return pl.pallas_call(
        functools.partial(_moba_prompt_body, T=T),
        grid=(B, nqt),
        in_specs=_head_specs(nqt, T, tq, BMQ, BMK, BMV, nh),
        out_specs=pl.BlockSpec((tq, nh * LANES), lambda b, i: (b * nqt + i, 0)),
        out_shape=jax.ShapeDtypeStruct((B * T, nh * LANES), F32),
        scratch_shapes=[
            pltpu.VMEM((nh, T, LANES), BF16), pltpu.VMEM((nh, LANES, T), BF16), pltpu.VMEM((nh, LANES, LANES), BF16),
            pltpu.VMEM((nh, nbr, tq), F32),
            pltpu.VMEM((nh, 1, tq), F32), pltpu.VMEM((nh, 1, tq), F32), pltpu.VMEM((nh, LANES, tq), F32),
        ],
        compiler_params=_cparams(("parallel", "arbitrary")),
        name="moba_prompt",
    )(*([proj] * (3 * nh)))


def _out_mem_body(an_ref, ad_ref, am_ref, nz0_ref, nz1_ref, dz_ref, mz_ref, x_ref, wout_ref, gmem_ref,
                  wq_ref, mkv_ref, wo_ref, o_ref):
    half = NSA_HEADS * LANES // 2
    an = an_ref[...]
    mixed = jnp.concatenate([
        an[:, :half] * _silu(nz0_ref[...]), an[:, half:] * _silu(nz1_ref[...]),
        ad_ref[...] * _silu(dz_ref[...]), am_ref[...] * _silu(mz_ref[...])], axis=1).astype(BF16)
    x1 = x_ref[...] + _mm(mixed, wout_ref[...])
    ms = jnp.mean(x1 * x1, axis=-1, keepdims=True)
    h2 = (x1 * lax.rsqrt(ms + NORM_EPS) * gmem_ref[...]).astype(BF16)
    q = _mm(h2, wq_ref[...])
    mkv = mkv_ref[...]
    scale = MEM_DIM ** -0.5
    outs = []
    for hh in range(MEM_HEADS):
        qh = q[:, hh * LANES:(hh + 1) * LANES].astype(BF16)
        kh = mkv[:, hh * LANES:(hh + 1) * LANES].astype(BF16)
        vh = mkv[:, (MEM_HEADS + hh) * LANES:(MEM_HEADS + hh + 1) * LANES].astype(BF16)
        s = _nt(qh, kh) * scale
        e = jnp.exp(s - jnp.max(s, axis=-1, keepdims=True))
        p = e / jnp.sum(e, axis=-1, keepdims=True)
        outs.append(_mm(p.astype(BF16), vh))
    oc = jnp.concatenate(outs, axis=1).astype(BF16)
    o_ref[...] = x1 + _mm(oc, wo_ref[...])


def _out_mem(a_nsa, a_diff, a_moba, proj3, x3, wout, gmem, wq, mkv, wo, *, tm):
    nb, rows, D = x3.shape
    grid = (nb, rows // tm)

    def row(cols, cb):
        return pl.BlockSpec((None, tm, cols), lambda b, i: (b, i, cb))

    def const(a):
        return pl.BlockSpec(a.shape, lambda b, i: (0,) * a.ndim, pipeline_mode=pl.Buffered(1))

    zc = 4 * LANES
    return pl.pallas_call(
        _out_mem_body,
        grid=grid,
        in_specs=[
            row(NSA_HEADS * LANES, 0), row(DIFF_HEADS * LANES, 0), row(MOBA_HEADS * LANES, 0),
            row(zc, BNZ * LANES // zc), row(zc, BNZ * LANES // zc + 1), row(zc, BDZ * LANES // zc),
            row(zc, BMZ * LANES // zc),
            row(D, 0), const(wout), const(gmem), const(wq),
            pl.BlockSpec((None, MEM_TOKENS, 2 * MEM_HEADS * MEM_DIM), lambda b, i: (b, 0, 0)),
            const(wo),
        ],
        out_specs=row(D, 0),
        out_shape=jax.ShapeDtypeStruct(x3.shape, F32),
        compiler_params=_cparams(("parallel", "parallel")),
        name="out_mem",
    )(a_nsa, a_diff, a_moba, proj3, proj3, proj3, proj3, x3, wout, gmem, wq, mkv, wo)


def _rmsnorm_body(x_ref, g_ref, o_ref):
    x = x_ref[...]
    ms = jnp.mean(x * x, axis=-1, keepdims=True)
    o_ref[...] = x * lax.rsqrt(ms + NORM_EPS) * g_ref[...]


def _rmsnorm(x, gain, *, tm):
    M, D = x.shape
    return pl.pallas_call(
        _rmsnorm_body,
        grid=(M // tm,),
        in_specs=[pl.BlockSpec((tm, D), lambda i: (i, 0)), pl.BlockSpec((1, D), lambda i: (0, 0))],
        out_specs=pl.BlockSpec((tm, D), lambda i: (i, 0)),
        out_shape=jax.ShapeDtypeStruct((M, D), F32),
        compiler_params=_cparams(("parallel",)),
        name="final_norm",
    )(x, gain.reshape(1, D))


def _regroup_w_in(w_in_l):
    K = w_in_l.shape[0]
    used = sum(b - a for a, b in _W_IN_SEGMENTS)
    cols = [w_in_l[:, a:b] for a, b in _W_IN_SEGMENTS] + [jnp.zeros((K, NPROJ - used), w_in_l.dtype)]
    return jnp.concatenate(cols, axis=1).astype(BF16)


def _cols(proj, blk, n):
    return proj[..., blk * LANES:(blk + n) * LANES]


def _cache_rows(proj3, kblk, vblk, heads):
    nb, rows, _ = proj3.shape
    kv = jnp.concatenate([_cols(proj3, kblk, heads), _cols(proj3, vblk, heads)], axis=-1)
    return kv.reshape(nb, rows, 2, heads, HEAD_DIM)


def _lam_init(l):
    return 0.8 - 0.6 * math.exp(-0.3 * l)


def _prompt_trunk(x_prompt, mem_prompt, P, *, tq=256):
    B, T, D = x_prompt.shape
    depth = P["w_in"].shape[0]
    N = B * T
    tabs = _rope_tables(jnp.arange(T, dtype=jnp.int32))
    tm = min(1024, T)
    memx = mem_prompt.reshape(B * MEM_TOKENS, D)
    x = x_prompt.reshape(N, D)
    rows = [[] for _ in range(5)]
    mem_rows = []
    for l in range(depth):
        mkv = _norm_matmul(memx, P["norm_mem_m"][l], P["w_mem_kv"][l].astype(BF16), tm=min(512, B * MEM_TOKENS),
                           tn=256)
        mkv3 = mkv.reshape(B, MEM_TOKENS, 2 * MEM_HEADS * MEM_DIM)
        mem_rows.append(mkv3.reshape(B, MEM_TOKENS, 2, MEM_HEADS, MEM_DIM))
        proj = _norm_matmul(x, P["norm_mix"][l], _regroup_w_in(P["w_in"][l]), tabs, tm=tm, tn=PROJ_TILE)
        proj3 = proj.reshape(B, T, NPROJ)
        rows[0].append(_cache_rows(proj3, BKC, BVC, NSA_GROUPS))
        rows[1].append(_cache_rows(proj3, BKS, BVS, NSA_GROUPS))
        rows[2].append(_cache_rows(proj3, BKW, BVW, NSA_GROUPS)[:, T - min(NSA_WINDOW, T):])
        rows[3].append(_cache_rows(proj3, BDK, BDV, DIFF_HEADS))
        rows[4].append(_cache_rows(proj3, BMK, BMV, MOBA_HEADS))
        cw = (P["nsa_pe_k"][l], P["nsa_w1_k"][l], P["nsa_w2_k"][l],
              P["nsa_pe_v"][l], P["nsa_w1_v"][l], P["nsa_w2_v"][l])
        a_nsa = _nsa_prompt(proj, cw, B=B, T=T, tq=tq)
        a_diff = _diff_prompt(proj, P["diff_lambda"][l], P["diff_subln"][l], B=B, T=T, tq=tq, lam_init=_lam_init(l))
        a_moba = _moba_prompt(proj, B=B, T=T)
        x3 = _out_mem(a_nsa.reshape(B, T, -1), a_diff.reshape(B, T, -1), a_moba.reshape(B, T, -1), proj3,
                      x.reshape(B, T, D), P["w_out"][l].astype(BF16), P["norm_mem_x"][l].reshape(1, D),
                      P["w_mem_q"][l].astype(BF16), mkv3, P["w_mem_o"][l].astype(BF16), tm=min(256, T))
        x = x3.reshape(N, D)
    y = _rmsnorm(x, P["norm_final"], tm=min(512, N)).reshape(B, T, D)
    return y, [jnp.stack(r, axis=1) for r in rows], jnp.stack(mem_rows, axis=1)


DEC_ROWS = 16
DEC_POS = 8


def _plane(page_ref, plane, n_planes):
    return page_ref[pl.ds(plane, PAGE_SIZE, stride=n_planes), :]


def _flat_pages(cache):
    p0, depth, rows, two, heads, dh = cache.shape
    return cache.reshape(p0 * depth * rows * two * heads, dh)


def _cmp_scan_body(pt_ref, *refs, P):
    pages = refs[:P]
    w_ref, o_ref = refs[P], refs[P + 1]
    chunks = PAGE_SIZE // NSA_CMP_STRIDE
    n_planes = 2 * NSA_GROUPS
    r = lax.broadcasted_iota(jnp.int32, (PAGE_SIZE, PAGE_SIZE), 0)
    c = lax.broadcasted_iota(jnp.int32, (PAGE_SIZE, PAGE_SIZE), 1)
    regroup = ((r % chunks) * NSA_CMP_STRIDE + r // chunks == c).astype(BF16)
    rows = chunks * P
    for kv in range(2):
        by_row = [_mm(regroup, jnp.concatenate(
            [_plane(pg, kv * NSA_GROUPS + g, n_planes) for g in range(NSA_GROUPS)], axis=1).astype(BF16))
            for pg in pages]
        acc = jnp.zeros((NSA_GROUPS * rows, 2 * LANES), F32)
        for jp in range(NSA_CMP_STRIDE // 2):
            lhs = jnp.concatenate([
                jnp.concatenate([x[j * chunks:(j + 1) * chunks, g * LANES:(g + 1) * LANES]
                                 for j in (2 * jp, 2 * jp + 1)], axis=1)
                for g in range(NSA_GROUPS) for x in by_row], axis=0).astype(BF16)
            acc = acc + _mm(lhs, w_ref[kv, jp])
        for g in range(NSA_GROUPS):
            lo = (kv * NSA_GROUPS + g) * 2 * LANES
            o_ref[:, lo:lo + 2 * LANES] = acc[g * rows:(g + 1) * rows]


def _cmp_scan(cache, page_table, w1ab, l, *, P):
    Bd, n_pages = page_table.shape
    chunks = PAGE_SIZE // NSA_CMP_STRIDE
    depth = cache.shape[1]
    view = _flat_pages(cache)
    prow = PAGE_SIZE * 2 * NSA_GROUPS
    assert n_pages % P == 0 and cache.shape[2] == PAGE_SIZE

    def page_map(b, s, pt, r):
        return (pt[b, s * P + r] * depth + l, 0)

    in_specs = [pl.BlockSpec((prow, LANES), functools.partial(page_map, r=r)) for r in range(P)]
    in_specs.append(pl.BlockSpec(w1ab.shape, lambda b, s, pt: (0, 0, 0, 0)))
    ncol = 2 * NSA_GROUPS * 2 * LANES
    return pl.pallas_call(
        functools.partial(_cmp_scan_body, P=P),
        grid_spec=pltpu.PrefetchScalarGridSpec(
            num_scalar_prefetch=1, grid=(Bd, n_pages // P), in_specs=in_specs,
            out_specs=pl.BlockSpec((None, chunks * P, ncol), lambda b, s, pt: (b, s, 0))),
        out_shape=jax.ShapeDtypeStruct((Bd, chunks * n_pages, ncol), F32),
        compiler_params=_cparams(("parallel", "arbitrary")),
        name="cmp_scan",
    )(page_table, *([view] * P), w1ab)


def _step_bias(bias, step, per_step):
    lo = step * per_step
    chunk = bias[:, (lo // LANES) * LANES:(lo // LANES + 1) * LANES]
    off = lo % LANES
    return chunk if off == 0 else pltpu.roll(chunk, LANES - off, 1)


def _nsa_dec_select_body(ab_ref, q_ref, pek_ref, w1k_ref, w2k_ref, pev_ref, w1v_ref, w2v_ref,
                         ocmp_ref, bias_ref, *, past, per_step):
    nch = past // NSA_CMP_STRIDE
    nsbp = past // NSA_SLC_LEN
    ncol = _round_up(nsbp, LANES)
    R = NSA_REP * DEC_POS
    scale = HEAD_DIM ** -0.5
    for g in range(NSA_GROUPS):
        def comp(kv, pe_ref, w1_ref, w2_ref):
            lo = (kv * NSA_GROUPS + g) * 2 * LANES
            return _compress_finish(ab_ref[:, lo:lo + LANES], ab_ref[:, lo + LANES:lo + 2 * LANES],
                                    pe_ref, w1_ref, w2_ref, nch).astype(BF16)
        kc = comp(0, pek_ref, w1k_ref, w2k_ref)
        vc = comp(1, pev_ref, w1v_ref, w2v_ref)
        s = _nt(q_ref[g].astype(BF16), kc) * scale
        col = lax.broadcasted_iota(jnp.int32, (R, nch), 1)
        qpos = past + lax.broadcasted_iota(jnp.int32, (R, nch), 0) % DEC_POS
        pc = _masked_softmax(s, (NSA_CMP_STRIDE * col + 2 * NSA_CMP_STRIDE - 1) <= qpos, 1)
        ocmp_ref[g] = _mm(pc.astype(BF16), vc)
        pg = pc[0:DEC_POS]
        for h in range(1, NSA_REP):
            pg = pg + pc[h * DEC_POS:(h + 1) * DEC_POS]
        slc = jnp.dot(pg, _slc_weights((nch, ncol), 0), precision=lax.Precision.HIGHEST,
                      preferred_element_type=F32)
        blk = lax.broadcasted_iota(jnp.int32, (DEC_POS, ncol), 1)
        forced = (blk == 0) | (blk == nsbp - 1)
        score = jnp.where(blk < nsbp, slc + jnp.where(forced, NSA_FORCE_BONUS, 0.0), -jnp.inf)
        rank = _rank_desc(score, nsbp, 1) + jnp.where(forced, 0.0, 1.0)
        b8 = jnp.where(rank < min(NSA_TOPN, nsbp + 1), 0.0, NEG)
        for st in range(nsbp // per_step):
            bias_ref[g, st] = jnp.concatenate([_step_bias(b8, st, per_step)] * NSA_REP, axis=0)


def _nsa_dec_select(ab, qn, cw, *, past, per_step):
    Bd = ab.shape[0]
    R = NSA_REP * DEC_POS
    n_steps = past // NSA_SLC_LEN // per_step

    def full(a):
        return pl.BlockSpec(a.shape, lambda b: (0,) * a.ndim)

    return pl.pallas_call(
        functools.partial(_nsa_dec_select_body, past=past, per_step=per_step),
        grid=(Bd,),
        in_specs=[pl.BlockSpec((None,) + ab.shape[1:], lambda b: (b, 0, 0)),
                  pl.BlockSpec((None, NSA_GROUPS, R, LANES), lambda b: (b, 0, 0, 0))] + [full(a) for a in cw],
        out_specs=[pl.BlockSpec((None, NSA_GROUPS, R, LANES), lambda b: (b, 0, 0, 0)),
                   pl.BlockSpec((None, NSA_GROUPS, n_steps, R, LANES), lambda b: (b, 0, 0, 0, 0))],
        out_shape=[jax.ShapeDtypeStruct((Bd, NSA_GROUPS, R, LANES), F32),
                   jax.ShapeDtypeStruct((Bd, NSA_GROUPS, n_steps, R, LANES), F32)],
        compiler_params=_cparams(("parallel",)),
        name="nsa_dec_select",
    )(ab, qn, *cw)


def _moba_kmean_body(pt_ref, *refs, P):
    pages, o_ref = refs[:P], refs[P]
    ppb = MOBA_BLOCK // PAGE_SIZE
    n_planes = 2 * MOBA_HEADS
    for r in range(P // ppb):
        tot = jnp.zeros((n_planes, LANES), F32)
        for e in range(ppb):
            tot = tot + jnp.sum(pages[ppb * r + e][...].reshape(PAGE_SIZE, n_planes, LANES), axis=0)
        o_ref[r] = tot / MOBA_BLOCK


def _moba_kmean(cache, page_table, l, *, P):
    Bd, n_pages = page_table.shape
    depth = cache.shape[1]
    view = _flat_pages(cache)
    n_planes = 2 * MOBA_HEADS
    prow = PAGE_SIZE * n_planes
    ppb = MOBA_BLOCK // PAGE_SIZE
    assert n_pages % P == 0 and P % ppb == 0 and n_planes == SUBLANES and cache.shape[2] == PAGE_SIZE

    def page_map(b, s, pt, r):
        return (pt[b, s * P + r] * depth + l, 0)

    return pl.pallas_call(
        functools.partial(_moba_kmean_body, P=P),
        grid_spec=pltpu.PrefetchScalarGridSpec(
            num_scalar_prefetch=1, grid=(Bd, n_pages // P),
            in_specs=[pl.BlockSpec((prow, LANES), functools.partial(page_map, r=r)) for r in range(P)],
            out_specs=pl.BlockSpec((None, P // ppb, n_planes, LANES), lambda b, s, pt: (b, s, 0, 0))),
        out_shape=jax.ShapeDtypeStruct((Bd, n_pages // ppb, n_planes, LANES), F32),
        compiler_params=_cparams(("parallel", "arbitrary")),
        name="moba_kmean",
    )(page_table, *([view] * P))


def _moba_dec_select_body(km_ref, q_ref, bias_ref, *, per_step):
    n_planes = 2 * MOBA_HEADS
    nblk = km_ref.shape[0] // n_planes
    for h in range(MOBA_HEADS):
        km = km_ref[pl.ds(h, nblk, stride=n_planes), :]
        if nblk < LANES:
            km = jnp.concatenate([km, jnp.zeros((LANES - nblk, LANES), F32)], axis=0)
        lane = lax.broadcasted_iota(jnp.int32, (DEC_POS, LANES), 1)
        score = jnp.where(lane < nblk, _nt(q_ref[h].astype(BF16), km.astype(BF16)), -jnp.inf)
        b8 = jnp.where(_rank_desc(score, nblk, 1) < min(MOBA_TOPK, nblk), 0.0, NEG)
        for st in range(nblk // per_step):
            bias_ref[h, st] = _step_bias(b8, st, per_step)


def _moba_dec_select(kmean, qm, *, per_step):
    Bd, nblk, n_planes, _ = kmean.shape
    assert nblk <= LANES
    n_steps = nblk // per_step
    return pl.pallas_call(
        functools.partial(_moba_dec_select_body, per_step=per_step),
        grid=(Bd,),
        in_specs=[pl.BlockSpec((None, nblk * n_planes, LANES), lambda b: (b, 0, 0)),
                  pl.BlockSpec((None, MOBA_HEADS, DEC_POS, LANES), lambda b: (b, 0, 0, 0))],
        out_specs=pl.BlockSpec((None, MOBA_HEADS, n_steps, DEC_POS, LANES), lambda b: (b, 0, 0, 0, 0)),
        out_shape=jax.ShapeDtypeStruct((Bd, MOBA_HEADS, n_steps, DEC_POS, LANES), F32),
        compiler_params=_cparams(("parallel",)),
        name="moba_dec_select",
    )(kmean.reshape(Bd, nblk * n_planes, LANES), qm)


def _paged_attn_body(ptM_ref, ptm_ref, *refs, P, H, R, scale, blocksize, window, n_new):
    q_ref = refs[0]
    pages = refs[1:1 + P]
    rest = refs[1 + P:]
    if blocksize is not None:
        bias_ref, rest = rest[0], rest[1:]
    kn_ref, vn_ref, o_ref, m_ref, l_ref, acc_ref = rest
    s_id = pl.program_id(1)
    nk = P * PAGE_SIZE
    n_planes = 2 * H

    @pl.when(s_id == 0)
    def _():
        _online_init(m_ref, l_ref, acc_ref)

    if blocksize is not None:
        kblk = lax.broadcasted_iota(jnp.int32, (nk, LANES), 0) // blocksize
        onehot = (kblk == lax.broadcasted_iota(jnp.int32, (nk, LANES), 1)).astype(BF16)
    if window:
        kidx = s_id * nk + lax.broadcasted_iota(jnp.int32, (R, nk), 1)
        in_window = kidx > lax.broadcasted_iota(jnp.int32, (R, nk), 0) % DEC_POS
    heads = range(H)
    scores = []
    for h in heads:
        q = q_ref[h].astype(BF16)
        k = jnp.concatenate([_plane(pg, h, n_planes) for pg in pages], axis=0).astype(BF16)
        if blocksize is not None:
            q = jnp.concatenate([q, bias_ref[h].astype(BF16)], axis=1)
            k = jnp.concatenate([k, onehot], axis=1)
        s = _nt(q, k) * scale
        scores.append(jnp.where(in_window, s, NEG) if window else s)
    m_prev = [m_ref[h] for h in heads]
    m_new = [jnp.maximum(m_prev[h], jnp.max(scores[h], axis=-1, keepdims=True)) for h in heads]
    alpha = [jnp.exp(m_prev[h] - m_new[h]) for h in heads]
    probs = [jnp.exp(scores[h] - m_new[h]) for h in heads]
    l_new = [alpha[h] * l_ref[h] + jnp.sum(probs[h], axis=-1, keepdims=True) for h in heads]
    pv = []
    for h in heads:
        v = jnp.concatenate([_plane(pg, H + h, n_planes) for pg in pages], axis=0).astype(BF16)
        pv.append(_mm(probs[h].astype(BF16), v))
    for h in heads:
        acc_ref[h] = alpha[h] * acc_ref[h] + pv[h]
        m_ref[h] = m_new[h]
        l_ref[h] = l_new[h]

    @pl.when(s_id == pl.num_programs(1) - 1)
    def _():
        lane = lax.broadcasted_iota(jnp.int32, (R, PAGE_SIZE), 1)
        rpos = lax.broadcasted_iota(jnp.int32, (R, PAGE_SIZE), 0) % DEC_POS
        for h in range(H):
            kn = kn_ref[:, h * LANES:(h + 1) * LANES].astype(BF16)
            vn = vn_ref[:, h * LANES:(h + 1) * LANES].astype(BF16)
            s = _nt(q_ref[h].astype(BF16), kn) * scale
            s = jnp.where((lane <= rpos) & (lane < n_new), s, NEG)
            _online_update(s, vn, m_ref.at[h], l_ref.at[h], acc_ref.at[h])
            o_ref[h] = acc_ref[h] / l_ref[h]


def _paged_attn(q, cache, ptM, ptm, l, knew, vnew, bias=None, *, scale, blocksize=None, window=False, P, n_new):
    Bd, H, R, _ = q.shape
    n_pages = ptM.shape[1]
    depth = cache.shape[1]
    ppe = cache.shape[2] // PAGE_SIZE
    prow = PAGE_SIZE * 2 * H
    view = _flat_pages(cache)
    assert n_pages % P == 0 and cache.shape[4] == H

    def page_map(b, s, pM, pm, r):
        return ((pM[b, s * P + r] * depth + l) * ppe + pm[b, s * P + r], 0)

    in_specs = [pl.BlockSpec((None, H, R, LANES), lambda b, s, pM, pm: (b, 0, 0, 0))]
    in_specs += [pl.BlockSpec((prow, LANES), functools.partial(page_map, r=r)) for r in range(P)]
    args = [q] + [view] * P
    if blocksize is not None:
        in_specs.append(pl.BlockSpec((None, H, None, R, LANES), lambda b, s, pM, pm: (b, 0, s, 0, 0)))
        args.append(bias)
    for a in (knew, vnew):
        in_specs.append(pl.BlockSpec((None,) + a.shape[1:], lambda b, s, pM, pm: (b, 0, 0)))
        args.append(a)
    return pl.pallas_call(
        functools.partial(_paged_attn_body, P=P, H=H, R=R, scale=scale, blocksize=blocksize, window=window,
                          n_new=n_new),
        grid_spec=pltpu.PrefetchScalarGridSpec(
            num_scalar_prefetch=2, grid=(Bd, n_pages // P), in_specs=in_specs,
            out_specs=pl.BlockSpec((None, H, R, LANES), lambda b, s, pM, pm: (b, 0, 0, 0)),
            scratch_shapes=[pltpu.VMEM((H, R, 1), F32), pltpu.VMEM((H, R, 1), F32), pltpu.VMEM((H, R, LANES), F32)]),
        out_shape=jax.ShapeDtypeStruct((Bd, H, R, LANES), F32),
        compiler_params=_cparams(("parallel", "arbitrary")),
        name="paged_attn",
    )(ptM, ptm, *args)


def _dec_finalize_body(ocmp_ref, oslc_ref, owin_ref, gate_ref, odiff_ref, lam_ref, sub_ref, omoba_ref,
                       an_ref, ad_ref, am_ref, *, lam_init):
    gt = gate_ref[0:DEC_POS, :]
    an_ref[...] = jnp.zeros(an_ref.shape, F32)
    ad_ref[...] = jnp.zeros(ad_ref.shape, F32)
    am_ref[...] = jnp.zeros(am_ref.shape, F32)
    for g in range(NSA_GROUPS):
        for h in range(NSA_REP):
            head = NSA_REP * g + h
            rows = slice(h * DEC_POS, (h + 1) * DEC_POS)
            out = (_lane_pick(gt, head) * ocmp_ref[g, rows, :]
                   + _lane_pick(gt, NSA_HEADS + head) * oslc_ref[g, rows, :]
                   + _lane_pick(gt, 2 * NSA_HEADS + head) * owin_ref[g, rows, :])
            an_ref[0:DEC_POS, head * LANES:(head + 1) * LANES] = out
    for h in range(DIFF_HEADS):
        o = odiff_ref[h]
        ad_ref[0:DEC_POS, h * LANES:(h + 1) * LANES] = _diff_finish(o[:DEC_POS], o[DEC_POS:], lam_ref, sub_ref, lam_init)
    for h in range(MOBA_HEADS):
        am_ref[0:DEC_POS, h * LANES:(h + 1) * LANES] = omoba_ref[h]


def _dec_finalize(o_cmp, o_slc, o_win, proj3, o_diff, lam, subln, o_moba, *, lam_init):
    Bd = o_cmp.shape[0]

    def b4(a):
        return pl.BlockSpec((None,) + a.shape[1:], lambda b: (b, 0, 0, 0))

    outs = [(NSA_HEADS * LANES), (DIFF_HEADS * LANES), (MOBA_HEADS * LANES)]
    return pl.pallas_call(
        functools.partial(_dec_finalize_body, lam_init=lam_init),
        grid=(Bd,),
        in_specs=[b4(o_cmp), b4(o_slc), b4(o_win),
                  pl.BlockSpec((None, DEC_ROWS, LANES), lambda b: (b, 0, BGATE)),
                  b4(o_diff), pl.BlockSpec(lam.shape, lambda b: (0, 0)), pl.BlockSpec((1, LANES), lambda b: (0, 0)),
                  b4(o_moba)],
        out_specs=[pl.BlockSpec((None, DEC_ROWS, c), lambda b: (b, 0, 0)) for c in outs],
        out_shape=[jax.ShapeDtypeStruct((Bd, DEC_ROWS, c), F32) for c in outs],
        compiler_params=_cparams(("parallel",)),
        name="dec_finalize",
    )(o_cmp, o_slc, o_win, proj3, o_diff, lam, subln.reshape(1, LANES), o_moba)


def _head_major(cols, heads):
    Bd = cols.shape[0]
    return cols[:, :DEC_POS].reshape(Bd, DEC_POS, heads, HEAD_DIM).transpose(0, 2, 1, 3)


def _new_rows(proj3, blk, heads):
    rows = _cols(proj3, blk, heads)
    return jnp.pad(rows, ((0, 0), (0, PAGE_SIZE - rows.shape[1]), (0, 0)))


def _sample_attn(proj3, caches, page_table, P, l, n_new, *, scan_pages=16, nsa_scan_pages=16):
    cache_cmp, cache_slc, cache_win, cache_diff, cache_moba = caches
    Bd = proj3.shape[0]
    n_pages = page_table.shape[1]
    past = n_pages * PAGE_SIZE
    wkeep = cache_win.shape[2]
    assert n_new <= DEC_POS and wkeep == NSA_WINDOW and past >= NSA_WINDOW
    win_pages = wkeep // PAGE_SIZE
    zeros_pt = jnp.zeros_like(page_table)
    win_major = jnp.broadcast_to(jnp.arange(Bd, dtype=jnp.int32)[:, None], (Bd, win_pages))
    win_minor = jnp.broadcast_to(jnp.arange(win_pages, dtype=jnp.int32)[None, :], (Bd, win_pages))
    half = NSA_CMP_STRIDE * HEAD_DIM
    step_keys = scan_pages * PAGE_SIZE

    def w1ab(w1):
        return jnp.concatenate([w1[:half].reshape(NSA_CMP_STRIDE, LANES, LANES),
                                w1[half:].reshape(NSA_CMP_STRIDE, LANES, LANES)], axis=-1)
    wab = jnp.stack([w1ab(P["nsa_w1_k"][l]), w1ab(P["nsa_w1_v"][l])]).astype(BF16)
    wab = wab.reshape(2, NSA_CMP_STRIDE // 2, 2 * LANES, 2 * LANES)
    ab = _cmp_scan(cache_cmp, page_table, wab, l, P=nsa_scan_pages)
    qn = _head_major(_cols(proj3, BQ, NSA_HEADS), NSA_HEADS).reshape(Bd, NSA_GROUPS, NSA_REP * DEC_POS, LANES)
    cw = (P["nsa_pe_k"][l], P["nsa_w1_k"][l], P["nsa_w2_k"][l],
          P["nsa_pe_v"][l], P["nsa_w1_v"][l], P["nsa_w2_v"][l])
    o_cmp, bias_slc = _nsa_dec_select(ab, qn, cw, past=past,
                                      per_step=nsa_scan_pages * PAGE_SIZE // NSA_SLC_LEN)
    sc128 = HEAD_DIM ** -0.5
    o_slc = _paged_attn(qn, cache_slc, page_table, zeros_pt, l, _new_rows(proj3, BKS, NSA_GROUPS),
                        _new_rows(proj3, BVS, NSA_GROUPS), bias_slc, scale=sc128, blocksize=NSA_SLC_LEN,
                        P=nsa_scan_pages, n_new=n_new)
    o_win = _paged_attn(qn, cache_win, win_major, win_minor, l, _new_rows(proj3, BKW, NSA_GROUPS),
                        _new_rows(proj3, BVW, NSA_GROUPS), scale=sc128, window=True, P=win_pages, n_new=n_new)
    qd = _head_major(_cols(proj3, BDQ, DIFF_HEADS), DIFF_HEADS)
    lane = jnp.arange(LANES)
    qd = jnp.concatenate([jnp.where(lane < DIFF_QK_DIM, qd, 0.0), jnp.where(lane >= DIFF_QK_DIM, qd, 0.0)], axis=2)
    o_diff = _paged_attn(qd, cache_diff, page_table, zeros_pt, l, _new_rows(proj3, BDK, DIFF_HEADS),
                         _new_rows(proj3, BDV, DIFF_HEADS), scale=DIFF_QK_DIM ** -0.5, P=scan_pages, n_new=n_new)
    qm = _head_major(_cols(proj3, BMQ, MOBA_HEADS), MOBA_HEADS)
    kmean = _moba_kmean(cache_moba, page_table, l, P=scan_pages)
    bias_moba = _moba_dec_select(kmean, qm, per_step=step_keys // MOBA_BLOCK)
    o_moba = _paged_attn(qm, cache_moba, page_table, zeros_pt, l, _new_rows(proj3, BMK, MOBA_HEADS),
                         _new_rows(proj3, BMV, MOBA_HEADS), bias_moba, scale=sc128, blocksize=MOBA_BLOCK,
                         P=scan_pages, n_new=n_new)
    return _dec_finalize(o_cmp, o_slc, o_win, proj3, o_diff, P["diff_lambda"][l], P["diff_subln"][l], o_moba,
                         lam_init=_lam_init(l))


def _sample_trunk(x_sample, caches, page_table, P):
    cache_cmp, cache_slc, cache_win, cache_diff, cache_moba, cache_mem = caches
    Bd, n_new, D = x_sample.shape
    depth = P["w_in"].shape[0]
    past = page_table.shape[1] * PAGE_SIZE
    pos = past + jnp.arange(DEC_ROWS, dtype=jnp.int32)
    tabs = [jnp.tile(t, (Bd, 1)) for t in _rope_tables(pos)]
    x = jnp.pad(x_sample, ((0, 0), (0, DEC_ROWS - n_new), (0, 0))).reshape(Bd * DEC_ROWS, D)
    mem4 = cache_mem.reshape(Bd, depth, MEM_TOKENS, 2 * MEM_HEADS * MEM_DIM)
    rows = [[] for _ in range(5)]
    for l in range(depth):
        proj = _norm_matmul(x, P["norm_mix"][l], _regroup_w_in(P["w_in"][l]), tabs, tm=Bd * DEC_ROWS, tn=PROJ_TILE)
        proj3 = proj.reshape(Bd, DEC_ROWS, NPROJ)
        new3 = proj3[:, :n_new]
        rows[0].append(_cache_rows(new3, BKC, BVC, NSA_GROUPS))
        rows[1].append(_cache_rows(new3, BKS, BVS, NSA_GROUPS))
        rows[2].append(jnp.concatenate(
            [cache_win[:, l], _cache_rows(new3, BKW, BVW, NSA_GROUPS)], axis=1)[:, n_new:])
        rows[3].append(_cache_rows(new3, BDK, BDV, DIFF_HEADS))
        rows[4].append(_cache_rows(new3, BMK, BMV, MOBA_HEADS))
        a_nsa, a_diff, a_moba = _sample_attn(proj3, caches[:5], page_table, P, l, n_new)
        x3 = _out_mem(a_nsa, a_diff, a_moba, proj3, x.reshape(Bd, DEC_ROWS, D), P["w_out"][l].astype(BF16),
                      P["norm_mem_x"][l].reshape(1, D), P["w_mem_q"][l].astype(BF16), mem4[:, l],
                      P["w_mem_o"][l].astype(BF16), tm=DEC_ROWS)
        x = x3.reshape(Bd * DEC_ROWS, D)
    y = _rmsnorm(x, P["norm_final"], tm=Bd * DEC_ROWS).reshape(Bd, DEC_ROWS, D)[:, :n_new]
    return y, [jnp.stack(r, axis=1) for r in rows]


def kernel(x_prompt, x_sample, mem_prompt, cache_nsa_cmp, cache_nsa_slc, cache_nsa_win, cache_diff, cache_moba, cache_mem, page_table, norm_mix, w_in, w_out, nsa_pe_k, nsa_pe_v, nsa_w1_k, nsa_w2_k, nsa_w1_v, nsa_w2_v, diff_lambda, diff_subln, norm_mem_x, norm_mem_m, w_mem_q, w_mem_kv, w_mem_o, norm_final):
    P = {"norm_mix": norm_mix, "w_in": w_in, "w_out": w_out, "nsa_pe_k": nsa_pe_k, "nsa_pe_v": nsa_pe_v,
         "nsa_w1_k": nsa_w1_k, "nsa_w2_k": nsa_w2_k, "nsa_w1_v": nsa_w1_v, "nsa_w2_v": nsa_w2_v,
         "diff_lambda": diff_lambda, "diff_subln": diff_subln, "norm_mem_x": norm_mem_x,
         "norm_mem_m": norm_mem_m, "w_mem_q": w_mem_q, "w_mem_kv": w_mem_kv, "w_mem_o": w_mem_o,
         "norm_final": norm_final}
    y_prompt, (p_cmp, p_slc, p_win, p_diff, p_moba), p_mem = _prompt_trunk(x_prompt, mem_prompt, P)
    caches = (cache_nsa_cmp, cache_nsa_slc, cache_nsa_win, cache_diff, cache_moba, cache_mem)
    y_sample, (s_cmp, s_slc, s_win, s_diff, s_moba) = _sample_trunk(x_sample, caches, page_table, P)
    return (y_prompt, y_sample, p_cmp, p_slc, p_win, p_diff, p_moba, p_mem, s_cmp, s_slc, s_win, s_diff, s_moba)
```

```python
import functools
import math

import jax
import jax.numpy as jnp
from jax import lax
from jax.experimental import pallas as pl
from jax.experimental.pallas import tpu as pltpu

F32 = jnp.float32
BF16 = jnp.bfloat16

D_MODEL = 2048
HEAD_DIM = 128
ROPE_THETA = 500000.0
NORM_EPS = 1e-6
PAGE_SIZE = 128

NSA_HEADS = 8
NSA_GROUPS = 2
NSA_REP = NSA_HEADS // NSA_GROUPS
NSA_CMP_STRIDE = 16
NSA_SLC_LEN = 64
NSA_TOPN = 16
NSA_WINDOW = 512
NSA_FORCE_BONUS = 1000.0
NSA_COL_GROUPS = 2
PROJ_TILE = 512
DIFF_HEADS = 4
DIFF_QK_DIM = HEAD_DIM // 2
MOBA_HEADS = 4
MOBA_BLOCK = 256
MOBA_TOPK = 3
MEM_TOKENS = 256
MEM_HEADS = 4
MEM_DIM = 128

NEG = -1e30
LANES = 128
SUBLANES = 8
VMEM_LIMIT_BYTES = 56 * 1024 * 1024

BQ, BKC, BKS, BKW, BMQ, BMK = 0, 8, 10, 12, 14, 18
BDQ, BDK = 22, 26
BVC, BVS, BVW, BNZ, BDV, BDZ, BMV, BMZ = 30, 32, 34, 36, 44, 48, 52, 56
BGATE = 60
NBLK = 64
NPROJ = NBLK * LANES
_W_IN_SEGMENTS = (
    (0, 1024), (1024, 1280), (1536, 1792), (2048, 2304), (5656, 6168), (6168, 6680),
    (3608, 4120), (4120, 4632),
    (1280, 1536), (1792, 2048), (2304, 2560), (2584, 3608), (4632, 5144), (5144, 5656),
    (6680, 7192), (7192, 7704),
    (2560, 2584),
)


def _round_up(n, m):
    return -(-n // m) * m


def _nt(a, b):
    return lax.dot_general(a, b, (((1,), (1,)), ((), ())), preferred_element_type=F32)


def _mm(a, b):
    return jnp.dot(a, b, preferred_element_type=F32)


def _silu(z):
    return z * jax.nn.sigmoid(z)


def _cparams(sem):
    return pltpu.CompilerParams(dimension_semantics=sem, vmem_limit_bytes=VMEM_LIMIT_BYTES)


def _norm_matmul_body(*refs, tn, splits, rope):
    if rope:
        x_ref, g_ref, w_ref, c1, a1, b1, c2, a2, b2, o_ref, h_ref = refs
    else:
        x_ref, g_ref, w_ref, o_ref, h_ref = refs
    j = pl.program_id(1)

    @pl.when(j == 0)
    def _():
        x = x_ref[...]
        ms = jnp.mean(x * x, axis=-1, keepdims=True)
        h_ref[...] = (x * lax.rsqrt(ms + NORM_EPS) * g_ref[...]).astype(BF16)

    acc = _mm(h_ref[...], w_ref[...])
    if not rope:
        o_ref[...] = acc
        return

    def epilogue(kind, blk):
        if kind == "rot128":
            half = HEAD_DIM // 8
            return blk * c1[...] + pltpu.roll(blk, LANES - half, 1) * a1[...] + pltpu.roll(blk, half, 1) * b1[...]
        if kind == "rot64":
            half = DIFF_QK_DIM // 8
            return blk * c2[...] + pltpu.roll(blk, LANES - half, 1) * a2[...] + pltpu.roll(blk, half, 1) * b2[...]
        if kind == "gate":
            return jax.nn.sigmoid(blk)
        return blk

    for lo, hi, kinds in splits:
        @pl.when((j >= lo) & (j <= hi))
        def _(kinds=kinds):
            for k, kind in enumerate(kinds):
                o_ref[:, k * LANES:(k + 1) * LANES] = epilogue(kind, acc[:, k * LANES:(k + 1) * LANES])


def _tile_patterns(tn):
    kinds = (["rot128"] * (BDQ - BQ) + ["rot64"] * (BVC - BDQ) + ["plain"] * (BGATE - BVC) + ["gate"]
             + ["plain"] * (NBLK - BGATE - 1))
    per = tn // LANES
    tiles = [tuple(kinds[t * per:(t + 1) * per]) for t in range(NBLK // per)]
    runs = []
    for t, pat in enumerate(tiles):
        if runs and runs[-1][2] == pat:
            runs[-1] = (runs[-1][0], t, pat)
        else:
            runs.append((t, t, pat))
    return tuple(runs)


def _norm_matmul(x, gain, w, tabs=None, *, tm, tn):
    M, K = x.shape
    Np = w.shape[1]
    rope = tabs is not None
    splits = _tile_patterns(tn) if rope else None
    in_specs = [
        pl.BlockSpec((tm, K), lambda i, j: (i, 0)),
        pl.BlockSpec((1, K), lambda i, j: (0, 0)),
        pl.BlockSpec((K, tn), lambda i, j: (0, j)),
    ]
    args = [x, gain.reshape(1, K), w]
    if rope:
        period = tabs[0].shape[0] // tm
        for t in tabs:
            in_specs.append(pl.BlockSpec((tm, LANES), lambda i, j: (i % period, 0)))
            args.append(t)
    return pl.pallas_call(
        functools.partial(_norm_matmul_body, tn=tn, splits=splits, rope=rope),
        grid=(M // tm, Np // tn),
        in_specs=in_specs,
        out_specs=pl.BlockSpec((tm, tn), lambda i, j: (i, j)),
        out_shape=jax.ShapeDtypeStruct((M, Np), F32),
        scratch_shapes=[pltpu.VMEM((tm, K), BF16)],
        compiler_params=_cparams(("parallel", "arbitrary")),
        name="norm_matmul",
    )(*args)


def _rope_tables(pos):
    posf = pos.astype(F32)[:, None]
    n = pos.shape[0]

    def one(width, reps):
        rd = width // 4
        half = rd // 2
        inv = ROPE_THETA ** (-2.0 * jnp.arange(half, dtype=F32) / rd)
        ang = posf * inv[None, :]
        c, s = jnp.cos(ang), jnp.sin(ang)
        z = jnp.zeros((n, width - rd), F32)
        zh = jnp.zeros((n, half), F32)
        C = jnp.concatenate([c, c, jnp.ones((n, width - rd), F32)], axis=1)
        A = jnp.concatenate([-s, zh, z], axis=1)
        B = jnp.concatenate([zh, s, z], axis=1)
        return [jnp.tile(t, (1, reps)) for t in (C, A, B)]

    return one(HEAD_DIM, 1) + one(DIFF_QK_DIM, 2)


def _online_init(m_ref, l_ref, acc_ref):
    m_ref[...] = jnp.full(m_ref.shape, NEG, F32)
    l_ref[...] = jnp.zeros(l_ref.shape, F32)
    acc_ref[...] = jnp.zeros(acc_ref.shape, F32)


def _online_update(s, v, m_ref, l_ref, acc_ref):
    m_prev = m_ref[...]
    m_new = jnp.maximum(m_prev, jnp.max(s, axis=-1, keepdims=True))
    alpha = jnp.exp(m_prev - m_new)
    p = jnp.exp(s - m_new)
    l_ref[...] = alpha * l_ref[...] + jnp.sum(p, axis=-1, keepdims=True)
    acc_ref[...] = alpha * acc_ref[...] + _mm(p.astype(BF16), v)
    m_ref[...] = m_new


def _online_update_t(s, vt, m_ref, l_ref, acc_ref):
    m_prev = m_ref[...]
    m_new = jnp.maximum(m_prev, jnp.max(s, axis=0, keepdims=True))
    alpha = jnp.exp(m_prev - m_new)
    p = jnp.exp(s - m_new)
    l_ref[...] = alpha * l_ref[...] + jnp.sum(p, axis=0, keepdims=True)
    acc_ref[...] = alpha * acc_ref[...] + _mm(vt, p.astype(BF16))
    m_ref[...] = m_new


def _staged_update_t(scores, vts, m_ref, l_ref, acc_ref, idxs):
    n = range(len(scores))
    m_prev = [m_ref[idxs[j]] for j in n]
    m_new = [jnp.maximum(m_prev[j], jnp.max(scores[j], axis=0, keepdims=True)) for j in n]
    alpha = [jnp.exp(m_prev[j] - m_new[j]) for j in n]
    probs = [jnp.exp(scores[j] - m_new[j]) for j in n]
    l_new = [alpha[j] * l_ref[idxs[j]] + jnp.sum(probs[j], axis=0, keepdims=True) for j in n]
    pv = [_mm(vts[j], probs[j].astype(BF16)) for j in n]
    for j in n:
        acc_ref[idxs[j]] = alpha[j] * acc_ref[idxs[j]] + pv[j]
        m_ref[idxs[j]] = m_new[j]
        l_ref[idxs[j]] = l_new[j]


def _masked_softmax(s, mask, axis):
    m = jnp.max(jnp.where(mask, s, NEG), axis=axis, keepdims=True)
    m = jnp.where(m > 0.5 * NEG, m, 0.0)
    e = jnp.where(mask, jnp.exp(s - m), 0.0)
    return e / jnp.maximum(jnp.sum(e, axis=axis, keepdims=True), 1e-30)


def _rank_desc(score, ncand, axis):
    idx = lax.broadcasted_iota(jnp.int32, score.shape, axis)
    rank = jnp.zeros(score.shape, F32)
    for c in range(ncand):
        cand = score[:, c:c + 1] if axis == 1 else score[c:c + 1, :]
        before = (cand > score) | ((cand == score) & (c < idx))
        rank = rank + before.astype(F32)
    return rank


def _lane_pick(x, idx):
    lane = lax.broadcasted_iota(jnp.int32, x.shape, 1)
    return jnp.sum(jnp.where(lane == idx, x, 0.0), axis=-1, keepdims=True)


def _compress(src_ref, pe_ref, w1_ref, w2_ref, nc):
    half = NSA_CMP_STRIDE * HEAD_DIM
    acc_a = jnp.zeros((nc, LANES), F32)
    acc_b = jnp.zeros((nc, LANES), F32)
    for j in range(NSA_CMP_STRIDE):
        rows = src_ref[pl.ds(j, nc, stride=NSA_CMP_STRIDE), :].astype(BF16)
        acc_a = acc_a + _mm(rows, w1_ref[j * LANES:(j + 1) * LANES, :].astype(BF16))
        acc_b = acc_b + _mm(rows, w1_ref[half + j * LANES:half + (j + 1) * LANES, :].astype(BF16))
    return _compress_finish(acc_a, acc_b, pe_ref, w1_ref, w2_ref, nc)


def _pe_term(pe_ref, w1_ref):
    acc = jnp.zeros((SUBLANES, LANES), F32)
    for j in range(2 * NSA_CMP_STRIDE):
        pj = jnp.broadcast_to(pe_ref[j:j + 1, :], (SUBLANES, LANES)).astype(BF16)
        acc = acc + _mm(pj, w1_ref[j * LANES:(j + 1) * LANES, :].astype(BF16))
    return acc[0:1, :]


def _compress_finish(acc_a, acc_b, pe_ref, w1_ref, w2_ref, nc):
    row = lax.broadcasted_iota(jnp.int32, (nc, LANES), 0)
    hb = jnp.where(row < nc - 1, pltpu.roll(acc_b, nc - 1, 0), 0.0)
    hid = _silu(acc_a + hb + _pe_term(pe_ref, w1_ref))
    return _mm(hid.astype(BF16), w2_ref[...].astype(BF16))


def _slc_weights(shape, cmp_axis):
    r = lax.broadcasted_iota(jnp.int32, shape, cmp_axis)
    c = lax.broadcasted_iota(jnp.int32, shape, 1 - cmp_axis)
    d = r - 4 * c
    return jnp.where((d == -1) | (d == 3), 1.0, jnp.where((d >= 0) & (d <= 2), 2.0, 0.0)).astype(F32)


def _transpose_into(dst_ref, src_ref, rows, chunk):
    for c in range(rows // chunk):
        dst_ref[:, c * chunk:(c + 1) * chunk] = src_ref[c * chunk:(c + 1) * chunk, :].T.astype(BF16)


def _nsa_prompt_body(q_ref, kc_ref, ks_ref, kw_ref, vc_ref, vs_ref, vw_ref, gate_ref,
                     pek_ref, w1k_ref, w2k_ref, pev_ref, w1v_ref, w2v_ref, o_ref,
                     kcs, vct, kaug, vst, kwb, vwt, gt_ref, m_ref, l_ref, acc_ref, *, T, tq):
    g = pl.program_id(1)
    i = pl.program_id(2)
    nc = T // NSA_CMP_STRIDE
    nsb = T // NSA_SLC_LEN
    nsbr = _round_up(nsb, SUBLANES)
    nsel = min(NSA_TOPN, nsb)
    R = NSA_REP * tq
    scale = HEAD_DIM ** -0.5

    @pl.when(i == 0)
    def _():
        kcs[...] = _compress(kc_ref, pek_ref, w1k_ref, w2k_ref, nc).astype(BF16)
        vct[...] = _compress(vc_ref, pev_ref, w1v_ref, w2v_ref, nc).T.astype(BF16)
        kaug[:, :LANES] = ks_ref[...].astype(BF16)
        rblk = lax.broadcasted_iota(jnp.int32, (T, LANES), 0) // NSA_SLC_LEN
        lane = lax.broadcasted_iota(jnp.int32, (T, LANES), 1)
        kaug[:, LANES:] = (rblk == lane).astype(BF16)
        kwb[...] = kw_ref[...].astype(BF16)
        _transpose_into(vst, vs_ref, T, tq)
        _transpose_into(vwt, vw_ref, T, tq)

    q = q_ref[...]
    qt = jnp.concatenate([q[:, h * LANES:(h + 1) * LANES].T for h in range(NSA_REP)], axis=1).astype(BF16)

    s = _mm(kcs[...], qt) * scale
    crow = lax.broadcasted_iota(jnp.int32, (nc, R), 0)
    t_col = i * tq + lax.broadcasted_iota(jnp.int32, (nc, R), 1) % tq
    pc = _masked_softmax(s, (NSA_CMP_STRIDE * crow + 2 * NSA_CMP_STRIDE - 1) <= t_col, 0)
    o_cmp = _mm(vct[...], pc.astype(BF16))
    pg = pc[:, 0:tq]
    for h in range(1, NSA_REP):
        pg = pg + pc[:, h * tq:(h + 1) * tq]

    slc = jnp.dot(_slc_weights((LANES, nc), 1), pg, precision=lax.Precision.HIGHEST,
                  preferred_element_type=F32)[:nsbr]
    blk = lax.broadcasted_iota(jnp.int32, (nsbr, tq), 0)
    tb = (i * tq + lax.broadcasted_iota(jnp.int32, (nsbr, tq), 1)) // NSA_SLC_LEN
    valid = blk <= tb
    forced = (blk == 0) | (blk == tb) | (blk == tb - 1)
    score = jnp.where(valid, slc + jnp.where(forced, NSA_FORCE_BONUS, 0.0), -jnp.inf)
    sel = (_rank_desc(score, nsb, 0) < nsel) & valid
    bias = jnp.where(sel, 0.0, NEG)
    if nsbr < LANES:
        bias = jnp.concatenate([bias, jnp.full((LANES - nsbr, tq), NEG, F32)], axis=0)
    bias = bias.astype(BF16)
    qaug = jnp.concatenate([qt, jnp.concatenate([bias] * NSA_REP, axis=1)], axis=0)

    gw = R // NSA_COL_GROUPS
    cols = [(slice(None), slice(j * gw, (j + 1) * gw)) for j in range(NSA_COL_GROUPS)]
    krow = lax.broadcasted_iota(jnp.int32, (tq, gw), 0)
    tloc = lax.broadcasted_iota(jnp.int32, (tq, gw), 1) % tq

    def attend(k_tile, vt_tile, q_all, mask):
        scores = []
        for c in cols:
            s2 = _mm(k_tile, q_all[c]) * scale
            scores.append(s2 if mask is None else jnp.where(mask, s2, NEG))
        _staged_update_t(scores, [vt_tile] * NSA_COL_GROUPS, m_ref, l_ref, acc_ref, cols)

    _online_init(m_ref, l_ref, acc_ref)

    def slc_step(kt, carry):
        off = pl.multiple_of(kt * tq, tq)
        attend(kaug[pl.ds(off, tq), :], vst[:, pl.ds(off, tq)], qaug, None)
        return carry

    lax.fori_loop(0, i, slc_step, 0)
    off_d = pl.multiple_of(i * tq, tq)
    attend(kaug[pl.ds(off_d, tq), :], vst[:, pl.ds(off_d, tq)], qaug, krow <= tloc)
    o_slc = acc_ref[...] / l_ref[...]

    _online_init(m_ref, l_ref, acc_ref)
    nw = NSA_WINDOW // tq
    for d in range(nw, -1, -1):
        @pl.when(i >= d)
        def _(d=d):
            off = pl.multiple_of((i - d) * tq, tq)
            mask = (krow > tloc) if d == nw else ((krow <= tloc) if d == 0 else None)
            attend(kwb[pl.ds(off, tq), :], vwt[:, pl.ds(off, tq)], qt, mask)
    o_win = acc_ref[...] / l_ref[...]

    gt_ref[...] = gate_ref[...].T

    def gate_row(branch):
        return jnp.concatenate(
            [gt_ref[pl.ds(branch * NSA_HEADS + NSA_REP * g + h, 1), :] for h in range(NSA_REP)], axis=1)

    out = gate_row(0) * o_cmp + gate_row(1) * o_slc + gate_row(2) * o_win
    for h in range(NSA_REP):
        o_ref[:, h * LANES:(h + 1) * LANES] = out[:, h * tq:(h + 1) * tq].T


def _nsa_prompt(proj, cw, *, B, T, tq):
    nqt = T // tq
    nc = T // NSA_CMP_STRIDE
    R = NSA_REP * tq
    assert T % NSA_SLC_LEN == 0 and NSA_WINDOW % tq == 0 and T // NSA_SLC_LEN <= LANES and nc <= LANES

    def kv_spec(blk):
        return pl.BlockSpec((T, LANES), lambda b, g, i: (b, blk + g))

    def full(a):
        return pl.BlockSpec(a.shape, lambda b, g, i: (0,) * a.ndim)

    in_specs = [
        pl.BlockSpec((tq, NSA_REP * LANES), lambda b, g, i: (b * nqt + i, g)),
        kv_spec(BKC), kv_spec(BKS), kv_spec(BKW), kv_spec(BVC), kv_spec(BVS), kv_spec(BVW),
        pl.BlockSpec((tq, LANES), lambda b, g, i: (b * nqt + i, BGATE)),
    ] + [full(a) for a in cw]
    return pl.pallas_call(
        functools.partial(_nsa_prompt_body, T=T, tq=tq),
        grid=(B, NSA_GROUPS, nqt),
        in_specs=in_specs,
        out_specs=pl.BlockSpec((tq, NSA_REP * LANES), lambda b, g, i: (b * nqt + i, g)),
        out_shape=jax.ShapeDtypeStruct((B * T, NSA_HEADS * LANES), F32),
        scratch_shapes=[
            pltpu.VMEM((nc, LANES), BF16), pltpu.VMEM((LANES, nc), BF16),
            pltpu.VMEM((T, 2 * LANES), BF16), pltpu.VMEM((LANES, T), BF16),
            pltpu.VMEM((T, LANES), BF16), pltpu.VMEM((LANES, T), BF16),
            pltpu.VMEM((LANES, tq), F32),
            pltpu.VMEM((1, R), F32), pltpu.VMEM((1, R), F32), pltpu.VMEM((LANES, R), F32),
        ],
        compiler_params=_cparams(("parallel", "parallel", "arbitrary")),
        name="nsa_prompt",
    )(proj, proj, proj, proj, proj, proj, proj, proj, *cw)


def _diff_lambda(lam_ref, lam_init):
    lp = lam_ref[...]
    return (jnp.exp(jnp.sum(lp[0:1] * lp[1:2], axis=-1, keepdims=True))
            - jnp.exp(jnp.sum(lp[2:3] * lp[3:4], axis=-1, keepdims=True)) + lam_init)


def _diff_finish(o1, o2, lam_ref, sub_ref, lam_init):
    a = o1 - _diff_lambda(lam_ref, lam_init) * o2
    ms = jnp.mean(a * a, axis=-1, keepdims=True)
    return a * lax.rsqrt(ms + NORM_EPS) * sub_ref[...] * (1.0 - lam_init)


def _head_specs(nqt, T, tq, qblk, kblk, vblk, heads):
    specs = [pl.BlockSpec((tq, LANES), functools.partial(lambda b, i, c: (b * nqt + i, c), c=qblk + h))
             for h in range(heads)]
    for blk in (kblk, vblk):
        specs += [pl.BlockSpec((T, LANES), functools.partial(lambda b, i, c: (b, c), c=blk + h))
                  for h in range(heads)]
    return specs


def _diff_prompt_body(*refs, T, tq, lam_init):
    nh = DIFF_HEADS
    q_refs, k_refs, v_refs = refs[:nh], refs[nh:2 * nh], refs[2 * nh:3 * nh]
    lam_ref, sub_ref, o_ref, kb, vt, m_ref, l_ref, acc_ref = refs[3 * nh:]
    i = pl.program_id(1)
    scale = DIFF_QK_DIM ** -0.5
    heads = range(nh)

    @pl.when(i == 0)
    def _():
        for h in heads:
            kb[h] = k_refs[h][...].astype(BF16)
            _transpose_into(vt.at[h], v_refs[h], T, tq)

    qts = []
    for h in heads:
        q = q_refs[h][...]
        lane = lax.broadcasted_iota(jnp.int32, q.shape, 1)
        qts.append(jnp.concatenate([jnp.where(lane < DIFF_QK_DIM, q, 0.0).T,
                                    jnp.where(lane >= DIFF_QK_DIM, q, 0.0).T], axis=1).astype(BF16))
    R = 2 * tq
    krow = lax.broadcasted_iota(jnp.int32, (tq, R), 0)
    tloc = lax.broadcasted_iota(jnp.int32, (tq, R), 1) % tq
    idxs = [(h,) for h in heads]
    _online_init(m_ref, l_ref, acc_ref)

    def attend(off, mask):
        scores = []
        for h in heads:
            s = _mm(kb[h, pl.ds(off, tq), :], qts[h]) * scale
            scores.append(s if mask is None else jnp.where(mask, s, NEG))
        _staged_update_t(scores, [vt[h, :, pl.ds(off, tq)] for h in heads], m_ref, l_ref, acc_ref, idxs)

    def step(kt, carry):
        attend(pl.multiple_of(kt * tq, tq), None)
        return carry

    lax.fori_loop(0, i, step, 0)
    attend(pl.multiple_of(i * tq, tq), krow <= tloc)
    for h in heads:
        o = acc_ref[h] / l_ref[h]
        o_ref[:, h * LANES:(h + 1) * LANES] = _diff_finish(o[:, :tq].T, o[:, tq:].T, lam_ref, sub_ref, lam_init)


def _diff_prompt(proj, lam, subln, *, B, T, tq, lam_init):
    nqt = T // tq
    nh = DIFF_HEADS
    return pl.pallas_call(
        functools.partial(_diff_prompt_body, T=T, tq=tq, lam_init=lam_init),
        grid=(B, nqt),
        in_specs=_head_specs(nqt, T, tq, BDQ, BDK, BDV, nh) + [
            pl.BlockSpec(lam.shape, lambda b, i: (0, 0)),
            pl.BlockSpec((1, LANES), lambda b, i: (0, 0)),
        ],
        out_specs=pl.BlockSpec((tq, nh * LANES), lambda b, i: (b * nqt + i, 0)),
        out_shape=jax.ShapeDtypeStruct((B * T, nh * LANES), F32),
        scratch_shapes=[
            pltpu.VMEM((nh, T, LANES), BF16), pltpu.VMEM((nh, LANES, T), BF16),
            pltpu.VMEM((nh, 1, 2 * tq), F32), pltpu.VMEM((nh, 1, 2 * tq), F32),
            pltpu.VMEM((nh, LANES, 2 * tq), F32),
        ],
        compiler_params=_cparams(("parallel", "arbitrary")),
        name="diff_prompt",
    )(*([proj] * (3 * nh)), lam, subln.reshape(1, LANES))


def _moba_prompt_body(*refs, T):
    nh = MOBA_HEADS
    q_refs, k_refs, v_refs = refs[:nh], refs[nh:2 * nh], refs[2 * nh:3 * nh]
    o_ref, kb, vt, km, bias_ref, m_ref, l_ref, acc_ref = refs[3 * nh:]
    i = pl.program_id(1)
    tq = MOBA_BLOCK
    nb = T // MOBA_BLOCK
    nbr = _round_up(nb, SUBLANES)
    scale = HEAD_DIM ** -0.5
    heads = range(nh)

    @pl.when(i == 0)
    def _():
        km[...] = jnp.zeros(km.shape, BF16)
        for h in heads:
            kb[h] = k_refs[h][...].astype(BF16)
            _transpose_into(vt.at[h], v_refs[h], T, tq)
            for j in range(nb):
                km[h, j:j + 1, :] = jnp.mean(k_refs[h][j * tq:(j + 1) * tq, :], axis=0, keepdims=True).astype(BF16)

    qts = [q_refs[h][...].T.astype(BF16) for h in heads]
    blk = lax.broadcasted_iota(jnp.int32, (nbr, tq), 0)
    past = blk < i
    for h in heads:
        score = jnp.where(past, _mm(km[h], qts[h])[:nbr], -jnp.inf)
        sel = (_rank_desc(score, nb, 0) < min(MOBA_TOPK, nb)) & past
        bias_ref[h] = jnp.where(sel, 0.0, NEG)
    krow = lax.broadcasted_iota(jnp.int32, (tq, tq), 0)
    tloc = lax.broadcasted_iota(jnp.int32, (tq, tq), 1)
    idxs = [(h,) for h in heads]
    _online_init(m_ref, l_ref, acc_ref)

    def step(kt, carry):
        off = pl.multiple_of(kt * tq, tq)
        scores = [_mm(kb[h, pl.ds(off, tq), :], qts[h]) * scale + bias_ref[h, pl.ds(kt, 1), :] for h in heads]
        _staged_update_t(scores, [vt[h, :, pl.ds(off, tq)] for h in heads], m_ref, l_ref, acc_ref, idxs)
        return carry

    lax.fori_loop(0, i, step, 0)
    off = pl.multiple_of(i * tq, tq)
    scores = [jnp.where(krow <= tloc, _mm(kb[h, pl.ds(off, tq), :], qts[h]) * scale, NEG) for h in heads]
    _staged_update_t(scores, [vt[h, :, pl.ds(off, tq)] for h in heads], m_ref, l_ref, acc_ref, idxs)
    for h in heads:
        o_ref[:, h * LANES:(h + 1) * LANES] = (acc_ref[h] / l_ref[h]).T


def _moba_prompt(proj, *, B, T):
    tq = MOBA_BLOCK
    assert T % tq == 0 and T // tq <= LANES
    nqt = T // tq
    nh = MOBA_HEADS
    nbr = _round_up(T // MOBA_BLOCK, SUBLANES)
    return pl.pallas_call(
        functools.partial(_moba_prompt_body, T=T),
        grid=(B, nqt),
        in_specs=_head_specs(nqt, T, tq, BMQ, BMK, BMV, nh),
        out_specs=pl.BlockSpec((tq, nh * LANES), lambda b, i: (b * nqt + i, 0)),
        out_shape=jax.ShapeDtypeStruct((B * T, nh * LANES), F32),
        scratch_shapes=[
            pltpu.VMEM((nh, T, LANES), BF16), pltpu.VMEM((nh, LANES, T), BF16), pltpu.VMEM((nh, LANES, LANES), BF16),
            pltpu.VMEM((nh, nbr, tq), F32),
            pltpu.VMEM((nh, 1, tq), F32), pltpu.VMEM((nh, 1, tq), F32), pltpu.VMEM((nh, LANES, tq), F32),
        ],
        compiler_params=_cparams(("parallel", "arbitrary")),
        name="moba_prompt",
    )(*([proj] * (3 * nh)))


def _out_mem_body(an_ref, ad_ref, am_ref, nz0_ref, nz1_ref, dz_ref, mz_ref, x_ref, wout_ref, gmem_ref,
                  wq_ref, mkv_ref, wo_ref, o_ref):
    half = NSA_HEADS * LANES // 2
    an = an_ref[...]
    mixed = jnp.concatenate([
        an[:, :half] * _silu(nz0_ref[...]), an[:, half:] * _silu(nz1_ref[...]),
        ad_ref[...] * _silu(dz_ref[...]), am_ref[...] * _silu(mz_ref[...])], axis=1).astype(BF16)
    x1 = x_ref[...] + _mm(mixed, wout_ref[...])
    ms = jnp.mean(x1 * x1, axis=-1, keepdims=True)
    h2 = (x1 * lax.rsqrt(ms + NORM_EPS) * gmem_ref[...]).astype(BF16)
    q = _mm(h2, wq_ref[...])
    mkv = mkv_ref[...]
    scale = MEM_DIM ** -0.5
    outs = []
    for hh in range(MEM_HEADS):
        qh = q[:, hh * LANES:(hh + 1) * LANES].astype(BF16)
        kh = mkv[:, hh * LANES:(hh + 1) * LANES].astype(BF16)
        vh = mkv[:, (MEM_HEADS + hh) * LANES:(MEM_HEADS + hh + 1) * LANES].astype(BF16)
        s = _nt(qh, kh) * scale
        e = jnp.exp(s - jnp.max(s, axis=-1, keepdims=True))
        p = e / jnp.sum(e, axis=-1, keepdims=True)
        outs.append(_mm(p.astype(BF16), vh))
    oc = jnp.concatenate(outs, axis=1).astype(BF16)
    o_ref[...] = x1 + _mm(oc, wo_ref[...])


def _out_mem(a_nsa, a_diff, a_moba, proj3, x3, wout, gmem, wq, mkv, wo, *, tm):
    nb, rows, D = x3.shape
    grid = (nb, rows // tm)

    def row(cols, cb):
        return pl.BlockSpec((None, tm, cols), lambda b, i: (b, i, cb))

    def const(a):
        return pl.BlockSpec(a.shape, lambda b, i: (0,) * a.ndim, pipeline_mode=pl.Buffered(1))

    zc = 4 * LANES
    return pl.pallas_call(
        _out_mem_body,
        grid=grid,
        in_specs=[
            row(NSA_HEADS * LANES, 0), row(DIFF_HEADS * LANES, 0), row(MOBA_HEADS * LANES, 0),
            row(zc, BNZ * LANES // zc), row(zc, BNZ * LANES // zc + 1), row(zc, BDZ * LANES // zc),
            row(zc, BMZ * LANES // zc),
            row(D, 0), const(wout), const(gmem), const(wq),
            pl.BlockSpec((None, MEM_TOKENS, 2 * MEM_HEADS * MEM_DIM), lambda b, i: (b, 0, 0)),
            const(wo),
        ],
        out_specs=row(D, 0),
        out_shape=jax.ShapeDtypeStruct(x3.shape, F32),
        compiler_params=_cparams(("parallel", "parallel")),
        name="out_mem",
    )(a_nsa, a_diff, a_moba, proj3, proj3, proj3, proj3, x3, wout, gmem, wq, mkv, wo)


def _rmsnorm_body(x_ref, g_ref, o_ref):
    x = x_ref[...]
    ms = jnp.mean(x * x, axis=-1, keepdims=True)
    o_ref[...] = x * lax.rsqrt(ms + NORM_EPS) * g_ref[...]


def _rmsnorm(x, gain, *, tm):
    M, D = x.shape
    return pl.pallas_call(
        _rmsnorm_body,
        grid=(M // tm,),
        in_specs=[pl.BlockSpec((tm, D), lambda i: (i, 0)), pl.BlockSpec((1, D), lambda i: (0, 0))],
        out_specs=pl.BlockSpec((tm, D), lambda i: (i, 0)),
        out_shape=jax.ShapeDtypeStruct((M, D), F32),
        compiler_params=_cparams(("parallel",)),
        name="final_norm",
    )(x, gain.reshape(1, D))


def _scatter_rows_body(*refs, n_planes):
    planes, o_ref = refs[:n_planes], refs[n_planes + 1]
    rows = planes[0].shape[0]
    for p in range(n_planes):
        o_ref[pl.ds(p, rows, stride=n_planes), :] = planes[p][...]


def _scatter_rows(buf, src, blocks, l, *, depth, seq_rows, t_start, t_len, tm):
    n_planes = len(blocks)
    nb = src.shape[0] // seq_rows
    tiles = t_len // tm
    assert t_len % tm == 0 and t_start % tm == 0 and seq_rows % tm == 0

    def src_map(b, i, c):
        return (b * (seq_rows // tm) + t_start // tm + i, c)

    in_specs = [pl.BlockSpec((tm, LANES), functools.partial(src_map, c=c)) for c in blocks]
    in_specs.append(pl.BlockSpec(memory_space=pl.ANY))
    return pl.pallas_call(
        functools.partial(_scatter_rows_body, n_planes=n_planes),
        grid=(nb, tiles),
        in_specs=in_specs,
        out_specs=pl.BlockSpec((tm * n_planes, LANES), lambda b, i: ((b * depth + l) * tiles + i, 0)),
        out_shape=jax.ShapeDtypeStruct(buf.shape, F32),
        input_output_aliases={n_planes: 0},
        compiler_params=_cparams(("parallel", "parallel")),
        name="scatter_rows",
    )(*([src] * n_planes), buf)


def _kv_blocks(kblk, vblk, heads):
    return [kblk + h for h in range(heads)] + [vblk + h for h in range(heads)]


def _row_buffer(entries, depth, rows, heads):
    return jnp.zeros((entries * depth * rows * 2 * heads, HEAD_DIM), F32)


def _as_rows(buf, entries, depth, rows, heads):
    return buf.reshape(entries, depth, rows, 2, heads, HEAD_DIM)


def _shift_window_body(old_ref, new_ref, buf_ref, o_ref, *, n_shift):
    keep = old_ref.shape[0] - n_shift
    o_ref[0:keep, :] = old_ref[n_shift:, :]
    o_ref[keep:, :] = new_ref[...]


def _shift_window(buf, cache_win, new_flat, l):
    Bd, depth, wkeep = cache_win.shape[:3]
    n_planes = 2 * NSA_GROUPS
    n_shift = new_flat.shape[1]
    old = cache_win.reshape(Bd * depth * wkeep * n_planes, HEAD_DIM)
    wrows = wkeep * n_planes
    return pl.pallas_call(
        functools.partial(_shift_window_body, n_shift=n_shift),
        grid=(Bd,),
        in_specs=[pl.BlockSpec((wrows, LANES), lambda b: (b * depth + l, 0)),
                  pl.BlockSpec((None, n_shift, LANES), lambda b: (b, 0, 0)),
                  pl.BlockSpec(memory_space=pl.ANY)],
        out_specs=pl.BlockSpec((wrows, LANES), lambda b: (b * depth + l, 0)),
        out_shape=jax.ShapeDtypeStruct(buf.shape, F32),
        input_output_aliases={2: 0},
        compiler_params=_cparams(("parallel",)),
        name="shift_window",
    )(old, new_flat, buf)


def _regroup_w_in(w_in_l):
    K = w_in_l.shape[0]
    used = sum(b - a for a, b in _W_IN_SEGMENTS)
    cols = [w_in_l[:, a:b] for a, b in _W_IN_SEGMENTS] + [jnp.zeros((K, NPROJ - used), w_in_l.dtype)]
    return jnp.concatenate(cols, axis=1).astype(BF16)


def _cols(proj, blk, n):
    return proj[..., blk * LANES:(blk + n) * LANES]


def _cache_rows(proj3, kblk, vblk, heads):
    nb, rows, _ = proj3.shape
    kv = jnp.concatenate([_cols(proj3, kblk, heads), _cols(proj3, vblk, heads)], axis=-1)
    return kv.reshape(nb, rows, 2, heads, HEAD_DIM)


def _lam_init(l):
    return 0.8 - 0.6 * math.exp(-0.3 * l)


def _prompt_trunk(x_prompt, mem_prompt, P, *, tq=256):
    B, T, D = x_prompt.shape
    depth = P["w_in"].shape[0]
    N = B * T
    tabs = _rope_tables(jnp.arange(T, dtype=jnp.int32))
    tm = min(1024, T)
    memx = mem_prompt.reshape(B * MEM_TOKENS, D)
    x = x_prompt.reshape(N, D)
    wk = min(NSA_WINDOW, T)
    outs = ((BKC, BVC, NSA_GROUPS, 0, T), (BKS, BVS, NSA_GROUPS, 0, T), (BKW, BVW, NSA_GROUPS, T - wk, wk),
            (BDK, BDV, DIFF_HEADS, 0, T), (BMK, BMV, MOBA_HEADS, 0, T))
    bufs = [_row_buffer(B, depth, n, h) for _, _, h, _, n in outs]
    mem_buf = _row_buffer(B, depth, MEM_TOKENS, MEM_HEADS)
    for l in range(depth):
        mkv = _norm_matmul(memx, P["norm_mem_m"][l], P["w_mem_kv"][l].astype(BF16), tm=min(512, B * MEM_TOKENS),
                           tn=256)
        mkv3 = mkv.reshape(B, MEM_TOKENS, 2 * MEM_HEADS * MEM_DIM)
        mem_buf = _scatter_rows(mem_buf, mkv, list(range(2 * MEM_HEADS)), l, depth=depth, seq_rows=MEM_TOKENS,
                                t_start=0, t_len=MEM_TOKENS, tm=MEM_TOKENS)
        proj = _norm_matmul(x, P["norm_mix"][l], _regroup_w_in(P["w_in"][l]), tabs, tm=tm, tn=PROJ_TILE)
        proj3 = proj.reshape(B, T, NPROJ)
        for n, (kblk, vblk, heads, t0, t_len) in enumerate(outs):
            bufs[n] = _scatter_rows(bufs[n], proj, _kv_blocks(kblk, vblk, heads), l, depth=depth, seq_rows=T,
                                    t_start=t0, t_len=t_len, tm=min(512, t_len))
        cw = (P["nsa_pe_k"][l], P["nsa_w1_k"][l], P["nsa_w2_k"][l],
              P["nsa_pe_v"][l], P["nsa_w1_v"][l], P["nsa_w2_v"][l])
        a_nsa = _nsa_prompt(proj, cw, B=B, T=T, tq=tq)
        a_diff = _diff_prompt(proj, P["diff_lambda"][l], P["diff_subln"][l], B=B, T=T, tq=tq, lam_init=_lam_init(l))
        a_moba = _moba_prompt(proj, B=B, T=T)
        x3 = _out_mem(a_nsa.reshape(B, T, -1), a_diff.reshape(B, T, -1), a_moba.reshape(B, T, -1), proj3,
                      x.reshape(B, T, D), P["w_out"][l].astype(BF16), P["norm_mem_x"][l].reshape(1, D),
                      P["w_mem_q"][l].astype(BF16), mkv3, P["w_mem_o"][l].astype(BF16), tm=min(256, T))
        x = x3.reshape(N, D)
    y = _rmsnorm(x, P["norm_final"], tm=min(512, N)).reshape(B, T, D)
    rows = [_as_rows(bufs[n], B, depth, t_len, heads) for n, (_, _, heads, _, t_len) in enumerate(outs)]
    return y, rows, _as_rows(mem_buf, B, depth, MEM_TOKENS, MEM_HEADS)


DEC_ROWS = 16
DEC_POS = 8


def _plane(page_ref, plane, n_planes):
    return page_ref[pl.ds(plane, PAGE_SIZE, stride=n_planes), :]


def _flat_pages(cache):
    p0, depth, rows, two, heads, dh = cache.shape
    return cache.reshape(p0 * depth * rows * two * heads, dh)


def _cmp_scan_body(pt_ref, *refs, P):
    pages = refs[:P]
    w_ref, o_ref = refs[P], refs[P + 1]
    chunks = PAGE_SIZE // NSA_CMP_STRIDE
    n_planes = 2 * NSA_GROUPS
    r = lax.broadcasted_iota(jnp.int32, (PAGE_SIZE, PAGE_SIZE), 0)
    c = lax.broadcasted_iota(jnp.int32, (PAGE_SIZE, PAGE_SIZE), 1)
    regroup = ((r % chunks) * NSA_CMP_STRIDE + r // chunks == c).astype(BF16)
    rows = chunks * P
    for kv in range(2):
        by_row = [_mm(regroup, jnp.concatenate(
            [_plane(pg, kv * NSA_GROUPS + g, n_planes) for g in range(NSA_GROUPS)], axis=1).astype(BF16))
            for pg in pages]
        acc = jnp.zeros((NSA_GROUPS * rows, 2 * LANES), F32)
        for jp in range(NSA_CMP_STRIDE // 2):
            lhs = jnp.concatenate([
                jnp.concatenate([x[j * chunks:(j + 1) * chunks, g * LANES:(g + 1) * LANES]
                                 for j in (2 * jp, 2 * jp + 1)], axis=1)
                for g in range(NSA_GROUPS) for x in by_row], axis=0).astype(BF16)
            acc = acc + _mm(lhs, w_ref[kv, jp])
        for g in range(NSA_GROUPS):
            lo = (kv * NSA_GROUPS + g) * 2 * LANES
            o_ref[:, lo:lo + 2 * LANES] = acc[g * rows:(g + 1) * rows]


def _cmp_scan(cache, page_table, w1ab, l, *, P):
    Bd, n_pages = page_table.shape
    chunks = PAGE_SIZE // NSA_CMP_STRIDE
    depth = cache.shape[1]
    view = _flat_pages(cache)
    prow = PAGE_SIZE * 2 * NSA_GROUPS
    assert n_pages % P == 0 and cache.shape[2] == PAGE_SIZE

    def page_map(b, s, pt, r):
        return (pt[b, s * P + r] * depth + l, 0)

    in_specs = [pl.BlockSpec((prow, LANES), functools.partial(page_map, r=r)) for r in range(P)]
    in_specs.append(pl.BlockSpec(w1ab.shape, lambda b, s, pt: (0, 0, 0, 0)))
    ncol = 2 * NSA_GROUPS * 2 * LANES
    return pl.pallas_call(
        functools.partial(_cmp_scan_body, P=P),
        grid_spec=pltpu.PrefetchScalarGridSpec(
            num_scalar_prefetch=1, grid=(Bd, n_pages // P), in_specs=in_specs,
            out_specs=pl.BlockSpec((None, chunks * P, ncol), lambda b, s, pt: (b, s, 0))),
        out_shape=jax.ShapeDtypeStruct((Bd, chunks * n_pages, ncol), F32),
        compiler_params=_cparams(("parallel", "arbitrary")),
        name="cmp_scan",
    )(page_table, *([view] * P), w1ab)


def _step_bias(bias, step, per_step):
    lo = step * per_step
    chunk = bias[:, (lo // LANES) * LANES:(lo // LANES + 1) * LANES]
    off = lo % LANES
    return chunk if off == 0 else pltpu.roll(chunk, LANES - off, 1)


def _nsa_dec_select_body(ab_ref, q_ref, pek_ref, w1k_ref, w2k_ref, pev_ref, w1v_ref, w2v_ref,
                         ocmp_ref, bias_ref, *, past, per_step):
    nch = past // NSA_CMP_STRIDE
    nsbp = past // NSA_SLC_LEN
    ncol = _round_up(nsbp, LANES)
    R = NSA_REP * DEC_POS
    scale = HEAD_DIM ** -0.5
    for g in range(NSA_GROUPS):
        def comp(kv, pe_ref, w1_ref, w2_ref):
            lo = (kv * NSA_GROUPS + g) * 2 * LANES
            return _compress_finish(ab_ref[:, lo:lo + LANES], ab_ref[:, lo + LANES:lo + 2 * LANES],
                                    pe_ref, w1_ref, w2_ref, nch).astype(BF16)
        kc = comp(0, pek_ref, w1k_ref, w2k_ref)
        vc = comp(1, pev_ref, w1v_ref, w2v_ref)
        s = _nt(q_ref[g].astype(BF16), kc) * scale
        col = lax.broadcasted_iota(jnp.int32, (R, nch), 1)
        qpos = past + lax.broadcasted_iota(jnp.int32, (R, nch), 0) % DEC_POS
        pc = _masked_softmax(s, (NSA_CMP_STRIDE * col + 2 * NSA_CMP_STRIDE - 1) <= qpos, 1)
        ocmp_ref[g] = _mm(pc.astype(BF16), vc)
        pg = pc[0:DEC_POS]
        for h in range(1, NSA_REP):
            pg = pg + pc[h * DEC_POS:(h + 1) * DEC_POS]
        slc = jnp.dot(pg, _slc_weights((nch, ncol), 0), precision=lax.Precision.HIGHEST,
                      preferred_element_type=F32)
        blk = lax.broadcasted_iota(jnp.int32, (DEC_POS, ncol), 1)
        forced = (blk == 0) | (blk == nsbp - 1)
        score = jnp.where(blk < nsbp, slc + jnp.where(forced, NSA_FORCE_BONUS, 0.0), -jnp.inf)
        rank = _rank_desc(score, nsbp, 1) + jnp.where(forced, 0.0, 1.0)
        b8 = jnp.where(rank < min(NSA_TOPN, nsbp + 1), 0.0, NEG)
        for st in range(nsbp // per_step):
            bias_ref[g, st] = jnp.concatenate([_step_bias(b8, st, per_step)] * NSA_REP, axis=0)


def _nsa_dec_select(ab, qn, cw, *, past, per_step):
    Bd = ab.shape[0]
    R = NSA_REP * DEC_POS
    n_steps = past // NSA_SLC_LEN // per_step

    def full(a):
        return pl.BlockSpec(a.shape, lambda b: (0,) * a.ndim)

    return pl.pallas_call(
        functools.partial(_nsa_dec_select_body, past=past, per_step=per_step),
        grid=(Bd,),
        in_specs=[pl.BlockSpec((None,) + ab.shape[1:], lambda b: (b, 0, 0)),
                  pl.BlockSpec((None, NSA_GROUPS, R, LANES), lambda b: (b, 0, 0, 0))] + [full(a) for a in cw],
        out_specs=[pl.BlockSpec((None, NSA_GROUPS, R, LANES), lambda b: (b, 0, 0, 0)),
                   pl.BlockSpec((None, NSA_GROUPS, n_steps, R, LANES), lambda b: (b, 0, 0, 0, 0))],
        out_shape=[jax.ShapeDtypeStruct((Bd, NSA_GROUPS, R, LANES), F32),
                   jax.ShapeDtypeStruct((Bd, NSA_GROUPS, n_steps, R, LANES), F32)],
        compiler_params=_cparams(("parallel",)),
        name="nsa_dec_select",
    )(ab, qn, *cw)


def _moba_kmean_body(pt_ref, *refs, P):
    pages, o_ref = refs[:P], refs[P]
    ppb = MOBA_BLOCK // PAGE_SIZE
    n_planes = 2 * MOBA_HEADS
    for r in range(P // ppb):
        tot = jnp.zeros((n_planes, LANES), F32)
        for e in range(ppb):
            tot = tot + jnp.sum(pages[ppb * r + e][...].reshape(PAGE_SIZE, n_planes, LANES), axis=0)
        o_ref[r] = tot / MOBA_BLOCK


def _moba_kmean(cache, page_table, l, *, P):
    Bd, n_pages = page_table.shape
    depth = cache.shape[1]
    view = _flat_pages(cache)
    n_planes = 2 * MOBA_HEADS
    prow = PAGE_SIZE * n_planes
    ppb = MOBA_BLOCK // PAGE_SIZE
    assert n_pages % P == 0 and P % ppb == 0 and n_planes == SUBLANES and cache.shape[2] == PAGE_SIZE

    def page_map(b, s, pt, r):
        return (pt[b, s * P + r] * depth + l, 0)

    return pl.pallas_call(
        functools.partial(_moba_kmean_body, P=P),
        grid_spec=pltpu.PrefetchScalarGridSpec(
            num_scalar_prefetch=1, grid=(Bd, n_pages // P),
            in_specs=[pl.BlockSpec((prow, LANES), functools.partial(page_map, r=r)) for r in range(P)],
            out_specs=pl.BlockSpec((None, P // ppb, n_planes, LANES), lambda b, s, pt: (b, s, 0, 0))),
        out_shape=jax.ShapeDtypeStruct((Bd, n_pages // ppb, n_planes, LANES), F32),
        compiler_params=_cparams(("parallel", "arbitrary")),
        name="moba_kmean",
    )(page_table, *([view] * P))


def _moba_dec_select_body(km_ref, q_ref, bias_ref, *, per_step):
    n_planes = 2 * MOBA_HEADS
    nblk = km_ref.shape[0] // n_planes
    for h in range(MOBA_HEADS):
        km = km_ref[pl.ds(h, nblk, stride=n_planes), :]
        if nblk < LANES:
            km = jnp.concatenate([km, jnp.zeros((LANES - nblk, LANES), F32)], axis=0)
        lane = lax.broadcasted_iota(jnp.int32, (DEC_POS, LANES), 1)
        score = jnp.where(lane < nblk, _nt(q_ref[h].astype(BF16), km.astype(BF16)), -jnp.inf)
        b8 = jnp.where(_rank_desc(score, nblk, 1) < min(MOBA_TOPK, nblk), 0.0, NEG)
        for st in range(nblk // per_step):
            bias_ref[h, st] = _step_bias(b8, st, per_step)


def _moba_dec_select(kmean, qm, *, per_step):
    Bd, nblk, n_planes, _ = kmean.shape
    assert nblk <= LANES
    n_steps = nblk // per_step
    return pl.pallas_call(
        functools.partial(_moba_dec_select_body, per_step=per_step),
        grid=(Bd,),
        in_specs=[pl.BlockSpec((None, nblk * n_planes, LANES), lambda b: (b, 0, 0)),
                  pl.BlockSpec((None, MOBA_HEADS, DEC_POS, LANES), lambda b: (b, 0, 0, 0))],
        out_specs=pl.BlockSpec((None, MOBA_HEADS, n_steps, DEC_POS, LANES), lambda b: (b, 0, 0, 0, 0)),
        out_shape=jax.ShapeDtypeStruct((Bd, MOBA_HEADS, n_steps, DEC_POS, LANES), F32),
        compiler_params=_cparams(("parallel",)),
        name="moba_dec_select",
    )(kmean.reshape(Bd, nblk * n_planes, LANES), qm)


def _paged_attn_body(ptM_ref, ptm_ref, *refs, P, H, R, scale, blocksize, window, n_new):
    q_ref = refs[0]
    pages = refs[1:1 + P]
    rest = refs[1 + P:]
    if blocksize is not None:
        bias_ref, rest = rest[0], rest[1:]
    kn_ref, vn_ref, o_ref, m_ref, l_ref, acc_ref = rest
    s_id = pl.program_id(1)
    nk = P * PAGE_SIZE
    n_planes = 2 * H

    @pl.when(s_id == 0)
    def _():
        _online_init(m_ref, l_ref, acc_ref)

    if blocksize is not None:
        kblk = lax.broadcasted_iota(jnp.int32, (nk, LANES), 0) // blocksize
        onehot = (kblk == lax.broadcasted_iota(jnp.int32, (nk, LANES), 1)).astype(BF16)
    if window:
        kidx = s_id * nk + lax.broadcasted_iota(jnp.int32, (R, nk), 1)
        in_window = kidx > lax.broadcasted_iota(jnp.int32, (R, nk), 0) % DEC_POS
    heads = range(H)
    scores = []
    for h in heads:
        q = q_ref[h].astype(BF16)
        k = jnp.concatenate([_plane(pg, h, n_planes) for pg in pages], axis=0).astype(BF16)
        if blocksize is not None:
            q = jnp.concatenate([q, bias_ref[h].astype(BF16)], axis=1)
            k = jnp.concatenate([k, onehot], axis=1)
        s = _nt(q, k) * scale
        scores.append(jnp.where(in_window, s, NEG) if window else s)
    m_prev = [m_ref[h] for h in heads]
    m_new = [jnp.maximum(m_prev[h], jnp.max(scores[h], axis=-1, keepdims=True)) for h in heads]
    alpha = [jnp.exp(m_prev[h] - m_new[h]) for h in heads]
    probs = [jnp.exp(scores[h] - m_new[h]) for h in heads]
    l_new = [alpha[h] * l_ref[h] + jnp.sum(probs[h], axis=-1, keepdims=True) for h in heads]
    pv = []
    for h in heads:
        v = jnp.concatenate([_plane(pg, H + h, n_planes) for pg in pages], axis=0).astype(BF16)
        pv.append(_mm(probs[h].astype(BF16), v))
    for h in heads:
        acc_ref[h] = alpha[h] * acc_ref[h] + pv[h]
        m_ref[h] = m_new[h]
        l_ref[h] = l_new[h]

    @pl.when(s_id == pl.num_programs(1) - 1)
    def _():
        lane = lax.broadcasted_iota(jnp.int32, (R, PAGE_SIZE), 1)
        rpos = lax.broadcasted_iota(jnp.int32, (R, PAGE_SIZE), 0) % DEC_POS
        for h in range(H):
            kn = kn_ref[:, h * LANES:(h + 1) * LANES].astype(BF16)
            vn = vn_ref[:, h * LANES:(h + 1) * LANES].astype(BF16)
            s = _nt(q_ref[h].astype(BF16), kn) * scale
            s = jnp.where((lane <= rpos) & (lane < n_new), s, NEG)
            _online_update(s, vn, m_ref.at[h], l_ref.at[h], acc_ref.at[h])
            o_ref[h] = acc_ref[h] / l_ref[h]


def _paged_attn(q, cache, ptM, ptm, l, knew, vnew, bias=None, *, scale, blocksize=None, window=False, P, n_new):
    Bd, H, R, _ = q.shape
    n_pages = ptM.shape[1]
    depth = cache.shape[1]
    ppe = cache.shape[2] // PAGE_SIZE
    prow = PAGE_SIZE * 2 * H
    view = _flat_pages(cache)
    assert n_pages % P == 0 and cache.shape[4] == H

    def page_map(b, s, pM, pm, r):
        return ((pM[b, s * P + r] * depth + l) * ppe + pm[b, s * P + r], 0)

    in_specs = [pl.BlockSpec((None, H, R, LANES), lambda b, s, pM, pm: (b, 0, 0, 0))]
    in_specs += [pl.BlockSpec((prow, LANES), functools.partial(page_map, r=r)) for r in range(P)]
    args = [q] + [view] * P
    if blocksize is not None:
        in_specs.append(pl.BlockSpec((None, H, None, R, LANES), lambda b, s, pM, pm: (b, 0, s, 0, 0)))
        args.append(bias)
    for a in (knew, vnew):
        in_specs.append(pl.BlockSpec((None,) + a.shape[1:], lambda b, s, pM, pm: (b, 0, 0)))
        args.append(a)
    return pl.pallas_call(
        functools.partial(_paged_attn_body, P=P, H=H, R=R, scale=scale, blocksize=blocksize, window=window,
                          n_new=n_new),
        grid_spec=pltpu.PrefetchScalarGridSpec(
            num_scalar_prefetch=2, grid=(Bd, n_pages // P), in_specs=in_specs,
            out_specs=pl.BlockSpec((None, H, R, LANES), lambda b, s, pM, pm: (b, 0, 0, 0)),
            scratch_shapes=[pltpu.VMEM((H, R, 1), F32), pltpu.VMEM((H, R, 1), F32), pltpu.VMEM((H, R, LANES), F32)]),
        out_shape=jax.ShapeDtypeStruct((Bd, H, R, LANES), F32),
        compiler_params=_cparams(("parallel", "arbitrary")),
        name="paged_attn",
    )(ptM, ptm, *args)


def _dec_finalize_body(ocmp_ref, oslc_ref, owin_ref, gate_ref, odiff_ref, lam_ref, sub_ref, omoba_ref,
                       an_ref, ad_ref, am_ref, *, lam_init):
    gt = gate_ref[0:DEC_POS, :]
    an_ref[...] = jnp.zeros(an_ref.shape, F32)
    ad_ref[...] = jnp.zeros(ad_ref.shape, F32)
    am_ref[...] = jnp.zeros(am_ref.shape, F32)
    for g in range(NSA_GROUPS):
        for h in range(NSA_REP):
            head = NSA_REP * g + h
            rows = slice(h * DEC_POS, (h + 1) * DEC_POS)
            out = (_lane_pick(gt, head) * ocmp_ref[g, rows, :]
                   + _lane_pick(gt, NSA_HEADS + head) * oslc_ref[g, rows, :]
                   + _lane_pick(gt, 2 * NSA_HEADS + head) * owin_ref[g, rows, :])
            an_ref[0:DEC_POS, head * LANES:(head + 1) * LANES] = out
    for h in range(DIFF_HEADS):
        o = odiff_ref[h]
        ad_ref[0:DEC_POS, h * LANES:(h + 1) * LANES] = _diff_finish(o[:DEC_POS], o[DEC_POS:], lam_ref, sub_ref, lam_init)
    for h in range(MOBA_HEADS):
        am_ref[0:DEC_POS, h * LANES:(h + 1) * LANES] = omoba_ref[h]


def _dec_finalize(o_cmp, o_slc, o_win, proj3, o_diff, lam, subln, o_moba, *, lam_init):
    Bd = o_cmp.shape[0]

    def b4(a):
        return pl.BlockSpec((None,) + a.shape[1:], lambda b: (b, 0, 0, 0))

    outs = [(NSA_HEADS * LANES), (DIFF_HEADS * LANES), (MOBA_HEADS * LANES)]
    return pl.pallas_call(
        functools.partial(_dec_finalize_body, lam_init=lam_init),
        grid=(Bd,),
        in_specs=[b4(o_cmp), b4(o_slc), b4(o_win),
                  pl.BlockSpec((None, DEC_ROWS, LANES), lambda b: (b, 0, BGATE)),
                  b4(o_diff), pl.BlockSpec(lam.shape, lambda b: (0, 0)), pl.BlockSpec((1, LANES), lambda b: (0, 0)),
                  b4(o_moba)],
        out_specs=[pl.BlockSpec((None, DEC_ROWS, c), lambda b: (b, 0, 0)) for c in outs],
        out_shape=[jax.ShapeDtypeStruct((Bd, DEC_ROWS, c), F32) for c in outs],
        compiler_params=_cparams(("parallel",)),
        name="dec_finalize",
    )(o_cmp, o_slc, o_win, proj3, o_diff, lam, subln.reshape(1, LANES), o_moba)


def _head_major(cols, heads):
    Bd = cols.shape[0]
    return cols[:, :DEC_POS].reshape(Bd, DEC_POS, heads, HEAD_DIM).transpose(0, 2, 1, 3)


def _new_rows(proj3, blk, heads):
    rows = _cols(proj3, blk, heads)
    return jnp.pad(rows, ((0, 0), (0, PAGE_SIZE - rows.shape[1]), (0, 0)))


def _sample_attn(proj3, caches, page_table, P, l, n_new, *, scan_pages=16, nsa_scan_pages=16):
    cache_cmp, cache_slc, cache_win, cache_diff, cache_moba = caches
    Bd = proj3.shape[0]
    n_pages = page_table.shape[1]
    past = n_pages * PAGE_SIZE
    wkeep = cache_win.shape[2]
    assert n_new <= DEC_POS and wkeep == NSA_WINDOW and past >= NSA_WINDOW
    win_pages = wkeep // PAGE_SIZE
    zeros_pt = jnp.zeros_like(page_table)
    win_major = jnp.broadcast_to(jnp.arange(Bd, dtype=jnp.int32)[:, None], (Bd, win_pages))
    win_minor = jnp.broadcast_to(jnp.arange(win_pages, dtype=jnp.int32)[None, :], (Bd, win_pages))
    half = NSA_CMP_STRIDE * HEAD_DIM
    step_keys = scan_pages * PAGE_SIZE

    def w1ab(w1):
        return jnp.concatenate([w1[:half].reshape(NSA_CMP_STRIDE, LANES, LANES),
                                w1[half:].reshape(NSA_CMP_STRIDE, LANES, LANES)], axis=-1)
    wab = jnp.stack([w1ab(P["nsa_w1_k"][l]), w1ab(P["nsa_w1_v"][l])]).astype(BF16)
    wab = wab.reshape(2, NSA_CMP_STRIDE // 2, 2 * LANES, 2 * LANES)
    ab = _cmp_scan(cache_cmp, page_table, wab, l, P=nsa_scan_pages)
    qn = _head_major(_cols(proj3, BQ, NSA_HEADS), NSA_HEADS).reshape(Bd, NSA_GROUPS, NSA_REP * DEC_POS, LANES)
    cw = (P["nsa_pe_k"][l], P["nsa_w1_k"][l], P["nsa_w2_k"][l],
          P["nsa_pe_v"][l], P["nsa_w1_v"][l], P["nsa_w2_v"][l])
    o_cmp, bias_slc = _nsa_dec_select(ab, qn, cw, past=past,
                                      per_step=nsa_scan_pages * PAGE_SIZE // NSA_SLC_LEN)
    sc128 = HEAD_DIM ** -0.5
    o_slc = _paged_attn(qn, cache_slc, page_table, zeros_pt, l, _new_rows(proj3, BKS, NSA_GROUPS),
                        _new_rows(proj3, BVS, NSA_GROUPS), bias_slc, scale=sc128, blocksize=NSA_SLC_LEN,
                        P=nsa_scan_pages, n_new=n_new)
    o_win = _paged_attn(qn, cache_win, win_major, win_minor, l, _new_rows(proj3, BKW, NSA_GROUPS),
                        _new_rows(proj3, BVW, NSA_GROUPS), scale=sc128, window=True, P=win_pages, n_new=n_new)
    qd = _head_major(_cols(proj3, BDQ, DIFF_HEADS), DIFF_HEADS)
    lane = jnp.arange(LANES)
    qd = jnp.concatenate([jnp.where(lane < DIFF_QK_DIM, qd, 0.0), jnp.where(lane >= DIFF_QK_DIM, qd, 0.0)], axis=2)
    o_diff = _paged_attn(qd, cache_diff, page_table, zeros_pt, l, _new_rows(proj3, BDK, DIFF_HEADS),
                         _new_rows(proj3, BDV, DIFF_HEADS), scale=DIFF_QK_DIM ** -0.5, P=scan_pages, n_new=n_new)
    qm = _head_major(_cols(proj3, BMQ, MOBA_HEADS), MOBA_HEADS)
    kmean = _moba_kmean(cache_moba, page_table, l, P=scan_pages)
    bias_moba = _moba_dec_select(kmean, qm, per_step=step_keys // MOBA_BLOCK)
    o_moba = _paged_attn(qm, cache_moba, page_table, zeros_pt, l, _new_rows(proj3, BMK, MOBA_HEADS),
                         _new_rows(proj3, BMV, MOBA_HEADS), bias_moba, scale=sc128, blocksize=MOBA_BLOCK,
                         P=scan_pages, n_new=n_new)
    return _dec_finalize(o_cmp, o_slc, o_win, proj3, o_diff, P["diff_lambda"][l], P["diff_subln"][l], o_moba,
                         lam_init=_lam_init(l))


def _sample_trunk(x_sample, caches, page_table, P):
    cache_cmp, cache_slc, cache_win, cache_diff, cache_moba, cache_mem = caches
    Bd, n_new, D = x_sample.shape
    depth = P["w_in"].shape[0]
    past = page_table.shape[1] * PAGE_SIZE
    pos = past + jnp.arange(DEC_ROWS, dtype=jnp.int32)
    tabs = [jnp.tile(t, (Bd, 1)) for t in _rope_tables(pos)]
    x = jnp.pad(x_sample, ((0, 0), (0, DEC_ROWS - n_new), (0, 0))).reshape(Bd * DEC_ROWS, D)
    mem4 = cache_mem.reshape(Bd, depth, MEM_TOKENS, 2 * MEM_HEADS * MEM_DIM)
    rows = [[] for _ in range(5)]
    wkeep = cache_win.shape[2]
    win_buf = _row_buffer(Bd, depth, wkeep, NSA_GROUPS)
    for l in range(depth):
        proj = _norm_matmul(x, P["norm_mix"][l], _regroup_w_in(P["w_in"][l]), tabs, tm=Bd * DEC_ROWS, tn=PROJ_TILE)
        proj3 = proj.reshape(Bd, DEC_ROWS, NPROJ)
        new3 = proj3[:, :n_new]
        rows[0].append(_cache_rows(new3, BKC, BVC, NSA_GROUPS))
        rows[1].append(_cache_rows(new3, BKS, BVS, NSA_GROUPS))
        win_buf = _shift_window(win_buf, cache_win,
                                _cache_rows(new3, BKW, BVW, NSA_GROUPS).reshape(Bd, -1, HEAD_DIM), l)
        rows[3].append(_cache_rows(new3, BDK, BDV, DIFF_HEADS))
        rows[4].append(_cache_rows(new3, BMK, BMV, MOBA_HEADS))
        a_nsa, a_diff, a_moba = _sample_attn(proj3, caches[:5], page_table, P, l, n_new)
        x3 = _out_mem(a_nsa, a_diff, a_moba, proj3, x.reshape(Bd, DEC_ROWS, D), P["w_out"][l].astype(BF16),
                      P["norm_mem_x"][l].reshape(1, D), P["w_mem_q"][l].astype(BF16), mem4[:, l],
                      P["w_mem_o"][l].astype(BF16), tm=DEC_ROWS)
        x = x3.reshape(Bd * DEC_ROWS, D)
    y = _rmsnorm(x, P["norm_final"], tm=Bd * DEC_ROWS).reshape(Bd, DEC_ROWS, D)[:, :n_new]
    stacked = [jnp.stack(r, axis=1) if r else _as_rows(win_buf, Bd, depth, wkeep, NSA_GROUPS) for r in rows]
    return y, stacked


def kernel(x_prompt, x_sample, mem_prompt, cache_nsa_cmp, cache_nsa_slc, cache_nsa_win, cache_diff, cache_moba, cache_mem, page_table, norm_mix, w_in, w_out, nsa_pe_k, nsa_pe_v, nsa_w1_k, nsa_w2_k, nsa_w1_v, nsa_w2_v, diff_lambda, diff_subln, norm_mem_x, norm_mem_m, w_mem_q, w_mem_kv, w_mem_o, norm_final):
    P = {"norm_mix": norm_mix, "w_in": w_in, "w_out": w_out, "nsa_pe_k": nsa_pe_k, "nsa_pe_v": nsa_pe_v,
         "nsa_w1_k": nsa_w1_k, "nsa_w2_k": nsa_w2_k, "nsa_w1_v": nsa_w1_v, "nsa_w2_v": nsa_w2_v,
         "diff_lambda": diff_lambda, "diff_subln": diff_subln, "norm_mem_x": norm_mem_x,
         "norm_mem_m": norm_mem_m, "w_mem_q": w_mem_q, "w_mem_kv": w_mem_kv, "w_mem_o": w_mem_o,
         "norm_final": norm_final}
    y_prompt, (p_cmp, p_slc, p_win, p_diff, p_moba), p_mem = _prompt_trunk(x_prompt, mem_prompt, P)
    caches = (cache_nsa_cmp, cache_nsa_slc, cache_nsa_win, cache_diff, cache_moba, cache_mem)
    y_sample, (s_cmp, s_slc, s_win, s_diff, s_moba) = _sample_trunk(x_sample, caches, page_table, P)
    return (y_prompt, y_sample, p_cmp, p_slc, p_win, p_diff, p_moba, p_mem, s_cmp, s_slc, s_win, s_diff, s_moba)
```

```python
import functools
import math

import jax
import jax.numpy as jnp
from jax import lax
from jax.experimental import pallas as pl
from jax.experimental.pallas import tpu as pltpu

F32 = jnp.float32
BF16 = jnp.bfloat16

D_MODEL = 2048
HEAD_DIM = 128
ROPE_THETA = 500000.0
NORM_EPS = 1e-6
PAGE_SIZE = 128

NSA_HEADS = 8
NSA_GROUPS = 2
NSA_REP = NSA_HEADS // NSA_GROUPS
NSA_CMP_STRIDE = 16
NSA_SLC_LEN = 64
NSA_TOPN = 16
NSA_WINDOW = 512
NSA_FORCE_BONUS = 1000.0
NSA_COL_GROUPS = 2
NSA_TQ = 512
DIFF_TQ = 512
PROJ_TILE = 512
DIFF_HEADS = 4
DIFF_QK_DIM = HEAD_DIM // 2
MOBA_HEADS = 4
MOBA_BLOCK = 256
MOBA_TOPK = 3
MEM_TOKENS = 256
MEM_HEADS = 4
MEM_DIM = 128

NEG = -1e30
LANES = 128
SUBLANES = 8
VMEM_LIMIT_BYTES = 56 * 1024 * 1024

BQ, BKC, BKS, BKW, BMQ, BMK = 0, 8, 10, 12, 14, 18
BDQ, BDK = 22, 26
BVC, BVS, BVW, BNZ, BDV, BDZ, BMV, BMZ = 30, 32, 34, 36, 44, 48, 52, 56
BGATE = 60
NBLK = 64
NPROJ = NBLK * LANES
_W_IN_SEGMENTS = (
    (0, 1024), (1024, 1280), (1536, 1792), (2048, 2304), (5656, 6168), (6168, 6680),
    (3608, 4120), (4120, 4632),
    (1280, 1536), (1792, 2048), (2304, 2560), (2584, 3608), (4632, 5144), (5144, 5656),
    (6680, 7192), (7192, 7704),
    (2560, 2584),
)


def _round_up(n, m):
    return -(-n // m) * m


def _nt(a, b):
    return lax.dot_general(a, b, (((1,), (1,)), ((), ())), preferred_element_type=F32)


def _mm(a, b):
    return jnp.dot(a, b, preferred_element_type=F32)


def _silu(z):
    return z * jax.nn.sigmoid(z)


def _cparams(sem):
    return pltpu.CompilerParams(dimension_semantics=sem, vmem_limit_bytes=VMEM_LIMIT_BYTES)


def _norm_matmul_body(*refs, tn, splits, rope):
    if rope:
        x_ref, g_ref, w_ref, c1, a1, b1, c2, a2, b2, o_ref, h_ref = refs
    else:
        x_ref, g_ref, w_ref, o_ref, h_ref = refs
    j = pl.program_id(1)

    @pl.when(j == 0)
    def _():
        x = x_ref[...]
        ms = jnp.mean(x * x, axis=-1, keepdims=True)
        h_ref[...] = (x * lax.rsqrt(ms + NORM_EPS) * g_ref[...]).astype(BF16)

    acc = _mm(h_ref[...], w_ref[...])
    if not rope:
        o_ref[...] = acc
        return

    def epilogue(kind, blk):
        if kind == "rot128":
            half = HEAD_DIM // 8
            return blk * c1[...] + pltpu.roll(blk, LANES - half, 1) * a1[...] + pltpu.roll(blk, half, 1) * b1[...]
        if kind == "rot64":
            half = DIFF_QK_DIM // 8
            return blk * c2[...] + pltpu.roll(blk, LANES - half, 1) * a2[...] + pltpu.roll(blk, half, 1) * b2[...]
        if kind == "gate":
            return jax.nn.sigmoid(blk)
        return blk

    for lo, hi, kinds in splits:
        @pl.when((j >= lo) & (j <= hi))
        def _(kinds=kinds):
            for k, kind in enumerate(kinds):
                o_ref[:, k * LANES:(k + 1) * LANES] = epilogue(kind, acc[:, k * LANES:(k + 1) * LANES])


def _tile_patterns(tn):
    kinds = (["rot128"] * (BDQ - BQ) + ["rot64"] * (BVC - BDQ) + ["plain"] * (BGATE - BVC) + ["gate"]
             + ["plain"] * (NBLK - BGATE - 1))
    per = tn // LANES
    tiles = [tuple(kinds[t * per:(t + 1) * per]) for t in range(NBLK // per)]
    runs = []
    for t, pat in enumerate(tiles):
        if runs and runs[-1][2] == pat:
            runs[-1] = (runs[-1][0], t, pat)
        else:
            runs.append((t, t, pat))
    return tuple(runs)


def _norm_matmul(x, gain, w, tabs=None, *, tm, tn):
    M, K = x.shape
    Np = w.shape[1]
    rope = tabs is not None
    splits = _tile_patterns(tn) if rope else None
    in_specs = [
        pl.BlockSpec((tm, K), lambda i, j: (i, 0)),
        pl.BlockSpec((1, K), lambda i, j: (0, 0)),
        pl.BlockSpec((K, tn), lambda i, j: (0, j)),
    ]
    args = [x, gain.reshape(1, K), w]
    if rope:
        period = tabs[0].shape[0] // tm
        for t in tabs:
            in_specs.append(pl.BlockSpec((tm, LANES), lambda i, j: (i % period, 0)))
            args.append(t)
    return pl.pallas_call(
        functools.partial(_norm_matmul_body, tn=tn, splits=splits, rope=rope),
        grid=(M // tm, Np // tn),
        in_specs=in_specs,
        out_specs=pl.BlockSpec((tm, tn), lambda i, j: (i, j)),
        out_shape=jax.ShapeDtypeStruct((M, Np), F32),
        scratch_shapes=[pltpu.VMEM((tm, K), BF16)],
        compiler_params=_cparams(("parallel", "arbitrary")),
        name="norm_matmul",
    )(*args)


def _rope_tables(pos):
    posf = pos.astype(F32)[:, None]
    n = pos.shape[0]

    def one(width, reps):
        rd = width // 4
        half = rd // 2
        inv = ROPE_THETA ** (-2.0 * jnp.arange(half, dtype=F32) / rd)
        ang = posf * inv[None, :]
        c, s = jnp.cos(ang), jnp.sin(ang)
        z = jnp.zeros((n, width - rd), F32)
        zh = jnp.zeros((n, half), F32)
        C = jnp.concatenate([c, c, jnp.ones((n, width - rd), F32)], axis=1)
        A = jnp.concatenate([-s, zh, z], axis=1)
        B = jnp.concatenate([zh, s, z], axis=1)
        return [jnp.tile(t, (1, reps)) for t in (C, A, B)]

    return one(HEAD_DIM, 1) + one(DIFF_QK_DIM, 2)


def _online_init(m_ref, l_ref, acc_ref):
    m_ref[...] = jnp.full(m_ref.shape, NEG, F32)
    l_ref[...] = jnp.zeros(l_ref.shape, F32)
    acc_ref[...] = jnp.zeros(acc_ref.shape, F32)


def _online_update(s, v, m_ref, l_ref, acc_ref):
    m_prev = m_ref[...]
    m_new = jnp.maximum(m_prev, jnp.max(s, axis=-1, keepdims=True))
    alpha = jnp.exp(m_prev - m_new)
    p = jnp.exp(s - m_new)
    l_ref[...] = alpha * l_ref[...] + jnp.sum(p, axis=-1, keepdims=True)
    acc_ref[...] = alpha * acc_ref[...] + _mm(p.astype(BF16), v)
    m_ref[...] = m_new


def _online_update_t(s, vt, m_ref, l_ref, acc_ref):
    m_prev = m_ref[...]
    m_new = jnp.maximum(m_prev, jnp.max(s, axis=0, keepdims=True))
    alpha = jnp.exp(m_prev - m_new)
    p = jnp.exp(s - m_new)
    l_ref[...] = alpha * l_ref[...] + jnp.sum(p, axis=0, keepdims=True)
    acc_ref[...] = alpha * acc_ref[...] + _mm(vt, p.astype(BF16))
    m_ref[...] = m_new


def _staged_update_t(scores, vts, m_ref, l_ref, acc_ref, idxs):
    n = range(len(scores))
    m_prev = [m_ref[idxs[j]] for j in n]
    m_new = [jnp.maximum(m_prev[j], jnp.max(scores[j], axis=0, keepdims=True)) for j in n]
    alpha = [jnp.exp(m_prev[j] - m_new[j]) for j in n]
    probs = [jnp.exp(scores[j] - m_new[j]) for j in n]
    l_new = [alpha[j] * l_ref[idxs[j]] + jnp.sum(probs[j], axis=0, keepdims=True) for j in n]
    pv = [_mm(vts[j], probs[j].astype(BF16)) for j in n]
    for j in n:
        acc_ref[idxs[j]] = alpha[j] * acc_ref[idxs[j]] + pv[j]
        m_ref[idxs[j]] = m_new[j]
        l_ref[idxs[j]] = l_new[j]


def _masked_softmax(s, mask, axis):
    m = jnp.max(jnp.where(mask, s, NEG), axis=axis, keepdims=True)
    m = jnp.where(m > 0.5 * NEG, m, 0.0)
    e = jnp.where(mask, jnp.exp(s - m), 0.0)
    return e / jnp.maximum(jnp.sum(e, axis=axis, keepdims=True), 1e-30)


def _rank_desc(score, ncand, axis):
    idx = lax.broadcasted_iota(jnp.int32, score.shape, axis)
    rank = jnp.zeros(score.shape, F32)
    for c in range(ncand):
        cand = score[:, c:c + 1] if axis == 1 else score[c:c + 1, :]
        before = (cand > score) | ((cand == score) & (c < idx))
        rank = rank + before.astype(F32)
    return rank


def _lane_pick(x, idx):
    lane = lax.broadcasted_iota(jnp.int32, x.shape, 1)
    return jnp.sum(jnp.where(lane == idx, x, 0.0), axis=-1, keepdims=True)


def _compress(src_ref, pe_ref, w1_ref, w2_ref, nc):
    half = NSA_CMP_STRIDE * HEAD_DIM
    acc_a = jnp.zeros((nc, LANES), F32)
    acc_b = jnp.zeros((nc, LANES), F32)
    for j in range(NSA_CMP_STRIDE):
        rows = src_ref[pl.ds(j, nc, stride=NSA_CMP_STRIDE), :].astype(BF16)
        acc_a = acc_a + _mm(rows, w1_ref[j * LANES:(j + 1) * LANES, :].astype(BF16))
        acc_b = acc_b + _mm(rows, w1_ref[half + j * LANES:half + (j + 1) * LANES, :].astype(BF16))
    return _compress_finish(acc_a, acc_b, pe_ref, w1_ref, w2_ref, nc)


def _pe_term(pe_ref, w1_ref):
    acc = jnp.zeros((SUBLANES, LANES), F32)
    for j in range(2 * NSA_CMP_STRIDE):
        pj = jnp.broadcast_to(pe_ref[j:j + 1, :], (SUBLANES, LANES)).astype(BF16)
        acc = acc + _mm(pj, w1_ref[j * LANES:(j + 1) * LANES, :].astype(BF16))
    return acc[0:1, :]


def _compress_finish(acc_a, acc_b, pe_ref, w1_ref, w2_ref, nc):
    row = lax.broadcasted_iota(jnp.int32, (nc, LANES), 0)
    hb = jnp.where(row < nc - 1, pltpu.roll(acc_b, nc - 1, 0), 0.0)
    hid = _silu(acc_a + hb + _pe_term(pe_ref, w1_ref))
    return _mm(hid.astype(BF16), w2_ref[...].astype(BF16))


def _slc_weights(shape, cmp_axis):
    r = lax.broadcasted_iota(jnp.int32, shape, cmp_axis)
    c = lax.broadcasted_iota(jnp.int32, shape, 1 - cmp_axis)
    d = r - 4 * c
    return jnp.where((d == -1) | (d == 3), 1.0, jnp.where((d >= 0) & (d <= 2), 2.0, 0.0)).astype(F32)


def _transpose_into(dst_ref, src_ref, rows, chunk):
    for c in range(rows // chunk):
        dst_ref[:, c * chunk:(c + 1) * chunk] = src_ref[c * chunk:(c + 1) * chunk, :].T.astype(BF16)


def _nsa_prompt_body(q_ref, kc_ref, ks_ref, kw_ref, vc_ref, vs_ref, vw_ref, gate_ref,
                     pek_ref, w1k_ref, w2k_ref, pev_ref, w1v_ref, w2v_ref, o_ref,
                     kcs, vct, kaug, vst, kwb, vwt, gt_ref, m_ref, l_ref, acc_ref, *, T, tq):
    g = pl.program_id(1)
    i = pl.program_id(2)
    nc = T // NSA_CMP_STRIDE
    nsb = T // NSA_SLC_LEN
    nsbr = _round_up(nsb, SUBLANES)
    nsel = min(NSA_TOPN, nsb)
    R = NSA_REP * tq
    scale = HEAD_DIM ** -0.5

    @pl.when(i == 0)
    def _():
        kcs[...] = _compress(kc_ref, pek_ref, w1k_ref, w2k_ref, nc).astype(BF16)
        vct[...] = _compress(vc_ref, pev_ref, w1v_ref, w2v_ref, nc).T.astype(BF16)
        kaug[:, :LANES] = ks_ref[...].astype(BF16)
        rblk = lax.broadcasted_iota(jnp.int32, (T, LANES), 0) // NSA_SLC_LEN
        lane = lax.broadcasted_iota(jnp.int32, (T, LANES), 1)
        kaug[:, LANES:] = (rblk == lane).astype(BF16)
        kwb[...] = kw_ref[...].astype(BF16)
        _transpose_into(vst, vs_ref, T, tq)
        _transpose_into(vwt, vw_ref, T, tq)

    q = q_ref[...]
    qt = jnp.concatenate([q[:, h * LANES:(h + 1) * LANES].T for h in range(NSA_REP)], axis=1).astype(BF16)

    s = _mm(kcs[...], qt) * scale
    crow = lax.broadcasted_iota(jnp.int32, (nc, R), 0)
    t_col = i * tq + lax.broadcasted_iota(jnp.int32, (nc, R), 1) % tq
    pc = _masked_softmax(s, (NSA_CMP_STRIDE * crow + 2 * NSA_CMP_STRIDE - 1) <= t_col, 0)
    o_cmp = _mm(vct[...], pc.astype(BF16))
    pg = pc[:, 0:tq]
    for h in range(1, NSA_REP):
        pg = pg + pc[:, h * tq:(h + 1) * tq]

    slc = jnp.dot(_slc_weights((LANES, nc), 1), pg, precision=lax.Precision.HIGHEST,
                  preferred_element_type=F32)[:nsbr]
    blk = lax.broadcasted_iota(jnp.int32, (nsbr, tq), 0)
    tb = (i * tq + lax.broadcasted_iota(jnp.int32, (nsbr, tq), 1)) // NSA_SLC_LEN
    valid = blk <= tb
    forced = (blk == 0) | (blk == tb) | (blk == tb - 1)
    score = jnp.where(valid, slc + jnp.where(forced, NSA_FORCE_BONUS, 0.0), -jnp.inf)
    sel = (_rank_desc(score, nsb, 0) < nsel) & valid
    bias = jnp.where(sel, 0.0, NEG)
    if nsbr < LANES:
        bias = jnp.concatenate([bias, jnp.full((LANES - nsbr, tq), NEG, F32)], axis=0)
    bias = bias.astype(BF16)
    qaug = jnp.concatenate([qt, jnp.concatenate([bias] * NSA_REP, axis=1)], axis=0)

    gw = R // NSA_COL_GROUPS
    cols = [(slice(None), slice(j * gw, (j + 1) * gw)) for j in range(NSA_COL_GROUPS)]
    krow = lax.broadcasted_iota(jnp.int32, (tq, gw), 0)
    tloc = lax.broadcasted_iota(jnp.int32, (tq, gw), 1) % tq

    def attend(k_tile, vt_tile, q_all, mask):
        scores = []
        for c in cols:
            s2 = _mm(k_tile, q_all[c]) * scale
            scores.append(s2 if mask is None else jnp.where(mask, s2, NEG))
        _staged_update_t(scores, [vt_tile] * NSA_COL_GROUPS, m_ref, l_ref, acc_ref, cols)

    _online_init(m_ref, l_ref, acc_ref)

    def slc_step(kt, carry):
        off = pl.multiple_of(kt * tq, tq)
        attend(kaug[pl.ds(off, tq), :], vst[:, pl.ds(off, tq)], qaug, None)
        return carry

    lax.fori_loop(0, i, slc_step, 0)
    off_d = pl.multiple_of(i * tq, tq)
    attend(kaug[pl.ds(off_d, tq), :], vst[:, pl.ds(off_d, tq)], qaug, krow <= tloc)
    o_slc = acc_ref[...] / l_ref[...]

    _online_init(m_ref, l_ref, acc_ref)
    nw = NSA_WINDOW // tq
    for d in range(nw, -1, -1):
        @pl.when(i >= d)
        def _(d=d):
            off = pl.multiple_of((i - d) * tq, tq)
            mask = (krow > tloc) if d == nw else ((krow <= tloc) if d == 0 else None)
            attend(kwb[pl.ds(off, tq), :], vwt[:, pl.ds(off, tq)], qt, mask)
    o_win = acc_ref[...] / l_ref[...]

    gt_ref[...] = gate_ref[...].T

    def gate_row(branch):
        return jnp.concatenate(
            [gt_ref[pl.ds(branch * NSA_HEADS + NSA_REP * g + h, 1), :] for h in range(NSA_REP)], axis=1)

    out = gate_row(0) * o_cmp + gate_row(1) * o_slc + gate_row(2) * o_win
    for h in range(NSA_REP):
        o_ref[:, h * LANES:(h + 1) * LANES] = out[:, h * tq:(h + 1) * tq].T


def _nsa_prompt(proj, cw, *, B, T, tq):
    nqt = T // tq
    nc = T // NSA_CMP_STRIDE
    R = NSA_REP * tq
    assert T % NSA_SLC_LEN == 0 and NSA_WINDOW % tq == 0 and T // NSA_SLC_LEN <= LANES and nc <= LANES

    def kv_spec(blk):
        return pl.BlockSpec((T, LANES), lambda b, g, i: (b, blk + g))

    def full(a):
        return pl.BlockSpec(a.shape, lambda b, g, i: (0,) * a.ndim)

    in_specs = [
        pl.BlockSpec((tq, NSA_REP * LANES), lambda b, g, i: (b * nqt + i, g)),
        kv_spec(BKC), kv_spec(BKS), kv_spec(BKW), kv_spec(BVC), kv_spec(BVS), kv_spec(BVW),
        pl.BlockSpec((tq, LANES), lambda b, g, i: (b * nqt + i, BGATE)),
    ] + [full(a) for a in cw]
    return pl.pallas_call(
        functools.partial(_nsa_prompt_body, T=T, tq=tq),
        grid=(B, NSA_GROUPS, nqt),
        in_specs=in_specs,
        out_specs=pl.BlockSpec((tq, NSA_REP * LANES), lambda b, g, i: (b * nqt + i, g)),
        out_shape=jax.ShapeDtypeStruct((B * T, NSA_HEADS * LANES), F32),
        scratch_shapes=[
            pltpu.VMEM((nc, LANES), BF16), pltpu.VMEM((LANES, nc), BF16),
            pltpu.VMEM((T, 2 * LANES), BF16), pltpu.VMEM((LANES, T), BF16),
            pltpu.VMEM((T, LANES), BF16), pltpu.VMEM((LANES, T), BF16),
            pltpu.VMEM((LANES, tq), F32),
            pltpu.VMEM((1, R), F32), pltpu.VMEM((1, R), F32), pltpu.VMEM((LANES, R), F32),
        ],
        compiler_params=_cparams(("parallel", "parallel", "arbitrary")),
        name="nsa_prompt",
    )(proj, proj, proj, proj, proj, proj, proj, proj, *cw)


def _diff_lambda(lam_ref, lam_init):
    lp = lam_ref[...]
    return (jnp.exp(jnp.sum(lp[0:1] * lp[1:2], axis=-1, keepdims=True))
            - jnp.exp(jnp.sum(lp[2:3] * lp[3:4], axis=-1, keepdims=True)) + lam_init)


def _diff_finish(o1, o2, lam_ref, sub_ref, lam_init):
    a = o1 - _diff_lambda(lam_ref, lam_init) * o2
    ms = jnp.mean(a * a, axis=-1, keepdims=True)
    return a * lax.rsqrt(ms + NORM_EPS) * sub_ref[...] * (1.0 - lam_init)


def _head_specs(nqt, T, tq, qblk, kblk, vblk, heads):
    specs = [pl.BlockSpec((tq, LANES), functools.partial(lambda b, i, c: (b * nqt + i, c), c=qblk + h))
             for h in range(heads)]
    for blk in (kblk, vblk):
        specs += [pl.BlockSpec((T, LANES), functools.partial(lambda b, i, c: (b, c), c=blk + h))
                  for h in range(heads)]
    return specs


def _diff_prompt_body(*refs, T, tq, lam_init):
    nh = DIFF_HEADS
    q_refs, k_refs, v_refs = refs[:nh], refs[nh:2 * nh], refs[2 * nh:3 * nh]
    lam_ref, sub_ref, o_ref, kb, vt, m_ref, l_ref, acc_ref = refs[3 * nh:]
    i = pl.program_id(1)
    scale = DIFF_QK_DIM ** -0.5
    heads = range(nh)

    @pl.when(i == 0)
    def _():
        for h in heads:
            kb[h] = k_refs[h][...].astype(BF16)
            _transpose_into(vt.at[h], v_refs[h], T, tq)

    qts = []
    for h in heads:
        q = q_refs[h][...]
        lane = lax.broadcasted_iota(jnp.int32, q.shape, 1)
        qts.append(jnp.concatenate([jnp.where(lane < DIFF_QK_DIM, q, 0.0).T,
                                    jnp.where(lane >= DIFF_QK_DIM, q, 0.0).T], axis=1).astype(BF16))
    R = 2 * tq
    krow = lax.broadcasted_iota(jnp.int32, (tq, R), 0)
    tloc = lax.broadcasted_iota(jnp.int32, (tq, R), 1) % tq
    idxs = [(h,) for h in heads]
    _online_init(m_ref, l_ref, acc_ref)

    def attend(off, mask):
        scores = []
        for h in heads:
            s = _mm(kb[h, pl.ds(off, tq), :], qts[h]) * scale
            scores.append(s if mask is None else jnp.where(mask, s, NEG))
        _staged_update_t(scores, [vt[h, :, pl.ds(off, tq)] for h in heads], m_ref, l_ref, acc_ref, idxs)

    def step(kt, carry):
        attend(pl.multiple_of(kt * tq, tq), None)
        return carry

    lax.fori_loop(0, i, step, 0)
    attend(pl.multiple_of(i * tq, tq), krow <= tloc)
    for h in heads:
        o = acc_ref[h] / l_ref[h]
        o_ref[:, h * LANES:(h + 1) * LANES] = _diff_finish(o[:, :tq].T, o[:, tq:].T, lam_ref, sub_ref, lam_init)


def _diff_prompt(proj, lam, subln, *, B, T, tq, lam_init):
    nqt = T // tq
    nh = DIFF_HEADS
    return pl.pallas_call(
        functools.partial(_diff_prompt_body, T=T, tq=tq, lam_init=lam_init),
        grid=(B, nqt),
        in_specs=_head_specs(nqt, T, tq, BDQ, BDK, BDV, nh) + [
            pl.BlockSpec(lam.shape, lambda b, i: (0, 0)),
            pl.BlockSpec((1, LANES), lambda b, i: (0, 0)),
        ],
        out_specs=pl.BlockSpec((tq, nh * LANES), lambda b, i: (b * nqt + i, 0)),
        out_shape=jax.ShapeDtypeStruct((B * T, nh * LANES), F32),
        scratch_shapes=[
            pltpu.VMEM((nh, T, LANES), BF16), pltpu.VMEM((nh, LANES, T), BF16),
            pltpu.VMEM((nh, 1, 2 * tq), F32), pltpu.VMEM((nh, 1, 2 * tq), F32),
            pltpu.VMEM((nh, LANES, 2 * tq), F32),
        ],
        compiler_params=_cparams(("parallel", "arbitrary")),
        name="diff_prompt",
    )(*([proj] * (3 * nh)), lam, subln.reshape(1, LANES))


def _moba_prompt_body(*refs, T):
    nh = MOBA_HEADS
    q_refs, k_refs, v_refs = refs[:nh], refs[nh:2 * nh], refs[2 * nh:3 * nh]
    o_ref, kb, vt, km, bias_ref, m_ref, l_ref, acc_ref = refs[3 * nh:]
    i = pl.program_id(1)
    tq = MOBA_BLOCK
    nb = T // MOBA_BLOCK
    nbr = _round_up(nb, SUBLANES)
    scale = HEAD_DIM ** -0.5
    heads = range(nh)

    @pl.when(i == 0)
    def _():
        km[...] = jnp.zeros(km.shape, BF16)
        for h in heads:
            kb[h] = k_refs[h][...].astype(BF16)
            _transpose_into(vt.at[h], v_refs[h], T, tq)
            for j in range(nb):
                km[h, j:j + 1, :] = jnp.mean(k_refs[h][j * tq:(j + 1) * tq, :], axis=0, keepdims=True).astype(BF16)

    qts = [q_refs[h][...].T.astype(BF16) for h in heads]
    blk = lax.broadcasted_iota(jnp.int32, (nbr, tq), 0)
    past = blk < i
    for h in heads:
        score = jnp.where(past, _mm(km[h], qts[h])[:nbr], -jnp.inf)
        sel = (_rank_desc(score, nb, 0) < min(MOBA_TOPK, nb)) & past
        bias_ref[h] = jnp.where(sel, 0.0, NEG)
    krow = lax.broadcasted_iota(jnp.int32, (tq, tq), 0)
    tloc = lax.broadcasted_iota(jnp.int32, (tq, tq), 1)
    idxs = [(h,) for h in heads]
    _online_init(m_ref, l_ref, acc_ref)

    def step(kt, carry):
        off = pl.multiple_of(kt * tq, tq)
        scores = [_mm(kb[h, pl.ds(off, tq), :], qts[h]) * scale + bias_ref[h, pl.ds(kt, 1), :] for h in heads]
        _staged_update_t(scores, [vt[h, :, pl.ds(off, tq)] for h in heads], m_ref, l_ref, acc_ref, idxs)
        return carry

    lax.fori_loop(0, i, step, 0)
    off = pl.multiple_of(i * tq, tq)
    scores = [jnp.where(krow <= tloc, _mm(kb[h, pl.ds(off, tq), :], qts[h]) * scale, NEG) for h in heads]
    _staged_update_t(scores, [vt[h, :, pl.ds(off, tq)] for h in heads], m_ref, l_ref, acc_ref, idxs)
    for h in heads:
        o_ref[:, h * LANES:(h + 1) * LANES] = (acc_ref[h] / l_ref[h]).T


def _moba_prompt(proj, *, B, T):
    tq = MOBA_BLOCK
    assert T % tq == 0 and T // tq <= LANES
    nqt = T // tq
    nh = MOBA_HEADS
    nbr = _round_up(T // MOBA_BLOCK, SUBLANES)
    return pl.pallas_call(
        functools.partial(_moba_prompt_body, T=T),
        grid=(B, nqt),
        in_specs=_head_specs(nqt, T, tq, BMQ, BMK, BMV, nh),
        out_specs=pl.BlockSpec((tq, nh * LANES), lambda b, i: (b * nqt + i, 0)),
        out_shape=jax.ShapeDtypeStruct((B * T, nh * LANES), F32),
        scratch_shapes=[
            pltpu.VMEM((nh, T, LANES), BF16), pltpu.VMEM((nh, LANES, T), BF16), pltpu.VMEM((nh, LANES, LANES), BF16),
            pltpu.VMEM((nh, nbr, tq), F32),
            pltpu.VMEM((nh, 1, tq), F32), pltpu.VMEM((nh, 1, tq), F32), pltpu.VMEM((nh, LANES, tq), F32),
        ],
        compiler_params=_cparams(("parallel", "arbitrary")),
        name="moba_prompt",
    )(*([proj] * (3 * nh)))


def _out_mem_body(an_ref, ad_ref, am_ref, nz0_ref, nz1_ref, dz_ref, mz_ref, x_ref, wout_ref, gmem_ref,
                  wq_ref, mkv_ref, wo_ref, o_ref):
    half = NSA_HEADS * LANES // 2
    an = an_ref[...]
    mixed = jnp.concatenate([
        an[:, :half] * _silu(nz0_ref[...]), an[:, half:] * _silu(nz1_ref[...]),
        ad_ref[...] * _silu(dz_ref[...]), am_ref[...] * _silu(mz_ref[...])], axis=1).astype(BF16)
    x1 = x_ref[...] + _mm(mixed, wout_ref[...])
    ms = jnp.mean(x1 * x1, axis=-1, keepdims=True)
    h2 = (x1 * lax.rsqrt(ms + NORM_EPS) * gmem_ref[...]).astype(BF16)
    q = _mm(h2, wq_ref[...])
    mkv = mkv_ref[...]
    scale = MEM_DIM ** -0.5
    outs = []
    for hh in range(MEM_HEADS):
        qh = q[:, hh * LANES:(hh + 1) * LANES].astype(BF16)
        kh = mkv[:, hh * LANES:(hh + 1) * LANES].astype(BF16)
        vh = mkv[:, (MEM_HEADS + hh) * LANES:(MEM_HEADS + hh + 1) * LANES].astype(BF16)
        s = _nt(qh, kh) * scale
        e = jnp.exp(s - jnp.max(s, axis=-1, keepdims=True))
        p = e / jnp.sum(e, axis=-1, keepdims=True)
        outs.append(_mm(p.astype(BF16), vh))
    oc = jnp.concatenate(outs, axis=1).astype(BF16)
    o_ref[...] = x1 + _mm(oc, wo_ref[...])


def _out_mem(a_nsa, a_diff, a_moba, proj3, x3, wout, gmem, wq, mkv, wo, *, tm):
    nb, rows, D = x3.shape
    grid = (nb, rows // tm)

    def row(cols, cb):
        return pl.BlockSpec((None, tm, cols), lambda b, i: (b, i, cb))

    def const(a):
        return pl.BlockSpec(a.shape, lambda b, i: (0,) * a.ndim, pipeline_mode=pl.Buffered(1))

    zc = 4 * LANES
    return pl.pallas_call(
        _out_mem_body,
        grid=grid,
        in_specs=[
            row(NSA_HEADS * LANES, 0), row(DIFF_HEADS * LANES, 0), row(MOBA_HEADS * LANES, 0),
            row(zc, BNZ * LANES // zc), row(zc, BNZ * LANES // zc + 1), row(zc, BDZ * LANES // zc),
            row(zc, BMZ * LANES // zc),
            row(D, 0), const(wout), const(gmem), const(wq),
            pl.BlockSpec((None, MEM_TOKENS, 2 * MEM_HEADS * MEM_DIM), lambda b, i: (b, 0, 0)),
            const(wo),
        ],
        out_specs=row(D, 0),
        out_shape=jax.ShapeDtypeStruct(x3.shape, F32),
        compiler_params=_cparams(("parallel", "parallel")),
        name="out_mem",
    )(a_nsa, a_diff, a_moba, proj3, proj3, proj3, proj3, x3, wout, gmem, wq, mkv, wo)


def _rmsnorm_body(x_ref, g_ref, o_ref):
    x = x_ref[...]
    ms = jnp.mean(x * x, axis=-1, keepdims=True)
    o_ref[...] = x * lax.rsqrt(ms + NORM_EPS) * g_ref[...]


def _rmsnorm(x, gain, *, tm):
    M, D = x.shape
    return pl.pallas_call(
        _rmsnorm_body,
        grid=(M // tm,),
        in_specs=[pl.BlockSpec((tm, D), lambda i: (i, 0)), pl.BlockSpec((1, D), lambda i: (0, 0))],
        out_specs=pl.BlockSpec((tm, D), lambda i: (i, 0)),
        out_shape=jax.ShapeDtypeStruct((M, D), F32),
        compiler_params=_cparams(("parallel",)),
        name="final_norm",
    )(x, gain.reshape(1, D))


def _scatter_rows_body(*refs, n_planes):
    planes, o_ref = refs[:n_planes], refs[n_planes + 1]
    rows = planes[0].shape[0]
    for p in range(n_planes):
        o_ref[pl.ds(p, rows, stride=n_planes), :] = planes[p][...]


def _scatter_rows(buf, src, blocks, l, *, depth, seq_rows, t_start, t_len, tm):
    n_planes = len(blocks)
    nb = src.shape[0] // seq_rows
    tiles = t_len // tm
    assert t_len % tm == 0 and t_start % tm == 0 and seq_rows % tm == 0

    def src_map(b, i, c):
        return (b * (seq_rows // tm) + t_start // tm + i, c)

    in_specs = [pl.BlockSpec((tm, LANES), functools.partial(src_map, c=c)) for c in blocks]
    in_specs.append(pl.BlockSpec(memory_space=pl.ANY))
    return pl.pallas_call(
        functools.partial(_scatter_rows_body, n_planes=n_planes),
        grid=(nb, tiles),
        in_specs=in_specs,
        out_specs=pl.BlockSpec((tm * n_planes, LANES), lambda b, i: ((b * depth + l) * tiles + i, 0)),
        out_shape=jax.ShapeDtypeStruct(buf.shape, F32),
        input_output_aliases={n_planes: 0},
        compiler_params=_cparams(("parallel", "parallel")),
        name="scatter_rows",
    )(*([src] * n_planes), buf)


def _kv_blocks(kblk, vblk, heads):
    return [kblk + h for h in range(heads)] + [vblk + h for h in range(heads)]


def _row_buffer(entries, depth, rows, heads):
    return jnp.zeros((entries * depth * rows * 2 * heads, HEAD_DIM), F32)


def _as_rows(buf, entries, depth, rows, heads):
    return buf.reshape(entries, depth, rows, 2, heads, HEAD_DIM)


def _shift_window_body(old_ref, new_ref, buf_ref, o_ref, *, n_shift):
    keep = old_ref.shape[0] - n_shift
    o_ref[0:keep, :] = old_ref[n_shift:, :]
    o_ref[keep:, :] = new_ref[...]


def _shift_window(buf, cache_win, new_flat, l):
    Bd, depth, wkeep = cache_win.shape[:3]
    n_planes = 2 * NSA_GROUPS
    n_shift = new_flat.shape[1]
    old = cache_win.reshape(Bd * depth * wkeep * n_planes, HEAD_DIM)
    wrows = wkeep * n_planes
    return pl.pallas_call(
        functools.partial(_shift_window_body, n_shift=n_shift),
        grid=(Bd,),
        in_specs=[pl.BlockSpec((wrows, LANES), lambda b: (b * depth + l, 0)),
                  pl.BlockSpec((None, n_shift, LANES), lambda b: (b, 0, 0)),
                  pl.BlockSpec(memory_space=pl.ANY)],
        out_specs=pl.BlockSpec((wrows, LANES), lambda b: (b * depth + l, 0)),
        out_shape=jax.ShapeDtypeStruct(buf.shape, F32),
        input_output_aliases={2: 0},
        compiler_params=_cparams(("parallel",)),
        name="shift_window",
    )(old, new_flat, buf)


def _regroup_w_in(w_in_l):
    K = w_in_l.shape[0]
    used = sum(b - a for a, b in _W_IN_SEGMENTS)
    cols = [w_in_l[:, a:b] for a, b in _W_IN_SEGMENTS] + [jnp.zeros((K, NPROJ - used), w_in_l.dtype)]
    return jnp.concatenate(cols, axis=1).astype(BF16)


def _cols(proj, blk, n):
    return proj[..., blk * LANES:(blk + n) * LANES]


def _cache_rows(proj3, kblk, vblk, heads):
    nb, rows, _ = proj3.shape
    kv = jnp.concatenate([_cols(proj3, kblk, heads), _cols(proj3, vblk, heads)], axis=-1)
    return kv.reshape(nb, rows, 2, heads, HEAD_DIM)


def _lam_init(l):
    return 0.8 - 0.6 * math.exp(-0.3 * l)


def _prompt_trunk(x_prompt, mem_prompt, P, *, tq=256):
    B, T, D = x_prompt.shape
    depth = P["w_in"].shape[0]
    N = B * T
    tabs = _rope_tables(jnp.arange(T, dtype=jnp.int32))
    tm = min(1024, T)
    memx = mem_prompt.reshape(B * MEM_TOKENS, D)
    x = x_prompt.reshape(N, D)
    wk = min(NSA_WINDOW, T)
    outs = ((BKC, BVC, NSA_GROUPS, 0, T), (BKS, BVS, NSA_GROUPS, 0, T), (BKW, BVW, NSA_GROUPS, T - wk, wk),
            (BDK, BDV, DIFF_HEADS, 0, T), (BMK, BMV, MOBA_HEADS, 0, T))
    bufs = [_row_buffer(B, depth, n, h) for _, _, h, _, n in outs]
    mem_buf = _row_buffer(B, depth, MEM_TOKENS, MEM_HEADS)
    for l in range(depth):
        mkv = _norm_matmul(memx, P["norm_mem_m"][l], P["w_mem_kv"][l].astype(BF16), tm=min(512, B * MEM_TOKENS),
                           tn=256)
        mkv3 = mkv.reshape(B, MEM_TOKENS, 2 * MEM_HEADS * MEM_DIM)
        mem_buf = _scatter_rows(mem_buf, mkv, list(range(2 * MEM_HEADS)), l, depth=depth, seq_rows=MEM_TOKENS,
                                t_start=0, t_len=MEM_TOKENS, tm=MEM_TOKENS)
        proj = _norm_matmul(x, P["norm_mix"][l], _regroup_w_in(P["w_in"][l]), tabs, tm=tm, tn=PROJ_TILE)
        proj3 = proj.reshape(B, T, NPROJ)
        for n, (kblk, vblk, heads, t0, t_len) in enumerate(outs):
            bufs[n] = _scatter_rows(bufs[n], proj, _kv_blocks(kblk, vblk, heads), l, depth=depth, seq_rows=T,
                                    t_start=t0, t_len=t_len, tm=min(512, t_len))
        cw = (P["nsa_pe_k"][l], P["nsa_w1_k"][l], P["nsa_w2_k"][l],
              P["nsa_pe_v"][l], P["nsa_w1_v"][l], P["nsa_w2_v"][l])
        a_nsa = _nsa_prompt(proj, cw, B=B, T=T, tq=min(NSA_TQ, T))
        a_diff = _diff_prompt(proj, P["diff_lambda"][l], P["diff_subln"][l], B=B, T=T, tq=min(DIFF_TQ, T),
                              lam_init=_lam_init(l))
        a_moba = _moba_prompt(proj, B=B, T=T)
        x3 = _out_mem(a_nsa.reshape(B, T, -1), a_diff.reshape(B, T, -1), a_moba.reshape(B, T, -1), proj3,
                      x.reshape(B, T, D), P["w_out"][l].astype(BF16), P["norm_mem_x"][l].reshape(1, D),
                      P["w_mem_q"][l].astype(BF16), mkv3, P["w_mem_o"][l].astype(BF16), tm=min(256, T))
        x = x3.reshape(N, D)
    y = _rmsnorm(x, P["norm_final"], tm=min(512, N)).reshape(B, T, D)
    rows = [_as_rows(bufs[n], B, depth, t_len, heads) for n, (_, _, heads, _, t_len) in enumerate(outs)]
    return y, rows, _as_rows(mem_buf, B, depth, MEM_TOKENS, MEM_HEADS)


DEC_ROWS = 16
DEC_POS = 8


def _plane(page_ref, plane, n_planes):
    return page_ref[pl.ds(plane, PAGE_SIZE, stride=n_planes), :]


def _flat_pages(cache):
    p0, depth, rows, two, heads, dh = cache.shape
    return cache.reshape(p0 * depth * rows * two * heads, dh)


def _cmp_scan_body(pt_ref, *refs, P):
    pages = refs[:P]
    w_ref, o_ref = refs[P], refs[P + 1]
    chunks = PAGE_SIZE // NSA_CMP_STRIDE
    n_planes = 2 * NSA_GROUPS
    r = lax.broadcasted_iota(jnp.int32, (PAGE_SIZE, PAGE_SIZE), 0)
    c = lax.broadcasted_iota(jnp.int32, (PAGE_SIZE, PAGE_SIZE), 1)
    regroup = ((r % chunks) * NSA_CMP_STRIDE + r // chunks == c).astype(BF16)
    rows = chunks * P
    for kv in range(2):
        by_row = [_mm(regroup, jnp.concatenate(
            [_plane(pg, kv * NSA_GROUPS + g, n_planes) for g in range(NSA_GROUPS)], axis=1).astype(BF16))
            for pg in pages]
        acc = jnp.zeros((NSA_GROUPS * rows, 2 * LANES), F32)
        for jp in range(NSA_CMP_STRIDE // 2):
            lhs = jnp.concatenate([
                jnp.concatenate([x[j * chunks:(j + 1) * chunks, g * LANES:(g + 1) * LANES]
                                 for j in (2 * jp, 2 * jp + 1)], axis=1)
                for g in range(NSA_GROUPS) for x in by_row], axis=0).astype(BF16)
            acc = acc + _mm(lhs, w_ref[kv, jp])
        for g in range(NSA_GROUPS):
            lo = (kv * NSA_GROUPS + g) * 2 * LANES
            o_ref[:, lo:lo + 2 * LANES] = acc[g * rows:(g + 1) * rows]


def _cmp_scan(cache, page_table, w1ab, l, *, P):
    Bd, n_pages = page_table.shape
    chunks = PAGE_SIZE // NSA_CMP_STRIDE
    depth = cache.shape[1]
    view = _flat_pages(cache)
    prow = PAGE_SIZE * 2 * NSA_GROUPS
    assert n_pages % P == 0 and cache.shape[2] == PAGE_SIZE

    def page_map(b, s, pt, r):
        return (pt[b, s * P + r] * depth + l, 0)

    in_specs = [pl.BlockSpec((prow, LANES), functools.partial(page_map, r=r)) for r in range(P)]
    in_specs.append(pl.BlockSpec(w1ab.shape, lambda b, s, pt: (0, 0, 0, 0)))
    ncol = 2 * NSA_GROUPS * 2 * LANES
    return pl.pallas_call(
        functools.partial(_cmp_scan_body, P=P),
        grid_spec=pltpu.PrefetchScalarGridSpec(
            num_scalar_prefetch=1, grid=(Bd, n_pages // P), in_specs=in_specs,
            out_specs=pl.BlockSpec((None, chunks * P, ncol), lambda b, s, pt: (b, s, 0))),
        out_shape=jax.ShapeDtypeStruct((Bd, chunks * n_pages, ncol), F32),
        compiler_params=_cparams(("parallel", "arbitrary")),
        name="cmp_scan",
    )(page_table, *([view] * P), w1ab)


def _step_bias(bias, step, per_step):
    lo = step * per_step
    chunk = bias[:, (lo // LANES) * LANES:(lo // LANES + 1) * LANES]
    off = lo % LANES
    return chunk if off == 0 else pltpu.roll(chunk, LANES - off, 1)


def _nsa_dec_select_body(ab_ref, q_ref, pek_ref, w1k_ref, w2k_ref, pev_ref, w1v_ref, w2v_ref,
                         ocmp_ref, bias_ref, *, past, per_step):
    nch = past // NSA_CMP_STRIDE
    nsbp = past // NSA_SLC_LEN
    ncol = _round_up(nsbp, LANES)
    R = NSA_REP * DEC_POS
    scale = HEAD_DIM ** -0.5
    for g in range(NSA_GROUPS):
        def comp(kv, pe_ref, w1_ref, w2_ref):
            lo = (kv * NSA_GROUPS + g) * 2 * LANES
            return _compress_finish(ab_ref[:, lo:lo + LANES], ab_ref[:, lo + LANES:lo + 2 * LANES],
                                    pe_ref, w1_ref, w2_ref, nch).astype(BF16)
        kc = comp(0, pek_ref, w1k_ref, w2k_ref)
        vc = comp(1, pev_ref, w1v_ref, w2v_ref)
        s = _nt(q_ref[g].astype(BF16), kc) * scale
        col = lax.broadcasted_iota(jnp.int32, (R, nch), 1)
        qpos = past + lax.broadcasted_iota(jnp.int32, (R, nch), 0) % DEC_POS
        pc = _masked_softmax(s, (NSA_CMP_STRIDE * col + 2 * NSA_CMP_STRIDE - 1) <= qpos, 1)
        ocmp_ref[g] = _mm(pc.astype(BF16), vc)
        pg = pc[0:DEC_POS]
        for h in range(1, NSA_REP):
            pg = pg + pc[h * DEC_POS:(h + 1) * DEC_POS]
        slc = jnp.dot(pg, _slc_weights((nch, ncol), 0), precision=lax.Precision.HIGHEST,
                      preferred_element_type=F32)
        blk = lax.broadcasted_iota(jnp.int32, (DEC_POS, ncol), 1)
        forced = (blk == 0) | (blk == nsbp - 1)
        score = jnp.where(blk < nsbp, slc + jnp.where(forced, NSA_FORCE_BONUS, 0.0), -jnp.inf)
        rank = _rank_desc(score, nsbp, 1) + jnp.where(forced, 0.0, 1.0)
        b8 = jnp.where(rank < min(NSA_TOPN, nsbp + 1), 0.0, NEG)
        for st in range(nsbp // per_step):
            bias_ref[g, st] = jnp.concatenate([_step_bias(b8, st, per_step)] * NSA_REP, axis=0)


def _nsa_dec_select(ab, qn, cw, *, past, per_step):
    Bd = ab.shape[0]
    R = NSA_REP * DEC_POS
    n_steps = past // NSA_SLC_LEN // per_step

    def full(a):
        return pl.BlockSpec(a.shape, lambda b: (0,) * a.ndim)

    return pl.pallas_call(
        functools.partial(_nsa_dec_select_body, past=past, per_step=per_step),
        grid=(Bd,),
        in_specs=[pl.BlockSpec((None,) + ab.shape[1:], lambda b: (b, 0, 0)),
                  pl.BlockSpec((None, NSA_GROUPS, R, LANES), lambda b: (b, 0, 0, 0))] + [full(a) for a in cw],
        out_specs=[pl.BlockSpec((None, NSA_GROUPS, R, LANES), lambda b: (b, 0, 0, 0)),
                   pl.BlockSpec((None, NSA_GROUPS, n_steps, R, LANES), lambda b: (b, 0, 0, 0, 0))],
        out_shape=[jax.ShapeDtypeStruct((Bd, NSA_GROUPS, R, LANES), F32),
                   jax.ShapeDtypeStruct((Bd, NSA_GROUPS, n_steps, R, LANES), F32)],
        compiler_params=_cparams(("parallel",)),
        name="nsa_dec_select",
    )(ab, qn, *cw)


def _moba_scan_body(pt_ref, *refs, P):
    q_ref = refs[0]
    pages = refs[1:1 + P]
    acc_ref, m_ref, l_ref, sc_ref = refs[1 + P:]
    H, R = MOBA_HEADS, DEC_POS
    n_planes = 2 * H
    ppb = MOBA_BLOCK // PAGE_SIZE
    nblk = P // ppb
    s_id = pl.program_id(1)
    scale = HEAD_DIM ** -0.5
    lane = lax.broadcasted_iota(jnp.int32, (R, LANES), 1)

    @pl.when(s_id == 0)
    def _():
        m_ref[...] = jnp.zeros(m_ref.shape, F32)
        l_ref[...] = jnp.zeros(l_ref.shape, F32)
        sc_ref[...] = jnp.zeros(sc_ref.shape, F32)

    heads = range(H)
    qs = [q_ref[h].astype(BF16) for h in heads]
    ks = [[jnp.concatenate([_plane(pages[b * ppb + e], h, n_planes) for e in range(ppb)], axis=0)
           for b in range(nblk)] for h in heads]
    km = [jnp.concatenate([jnp.mean(ks[h][b], axis=0, keepdims=True) for b in range(nblk)]
                          + [jnp.zeros((LANES - nblk, LANES), F32)], axis=0).astype(BF16) for h in heads]
    mean_sc = [_nt(qs[h], km[h]) for h in heads]
    raw = [[_nt(qs[h], ks[h][b].astype(BF16)) * scale for b in range(nblk)] for h in heads]
    mx = [[jnp.max(raw[h][b], axis=-1, keepdims=True) for b in range(nblk)] for h in heads]
    pr = [[jnp.exp(raw[h][b] - mx[h][b]) for b in range(nblk)] for h in heads]
    sm = [[jnp.sum(pr[h][b], axis=-1, keepdims=True) for b in range(nblk)] for h in heads]
    for h in heads:
        m_t, l_t, s_t = m_ref[h], l_ref[h], sc_ref[h]
        for b in range(nblk):
            v = jnp.concatenate([_plane(pages[b * ppb + e], H + h, n_planes) for e in range(ppb)],
                                axis=0).astype(BF16)
            acc_ref[h, b] = _mm(pr[h][b].astype(BF16), v)
            here = lane == s_id * nblk + b
            m_t = jnp.where(here, mx[h][b], m_t)
            l_t = jnp.where(here, sm[h][b], l_t)
            s_t = jnp.where(here, _lane_pick(mean_sc[h], b), s_t)
        m_ref[h], l_ref[h], sc_ref[h] = m_t, l_t, s_t


def _moba_scan(q, cache, page_table, l, *, P):
    Bd, H, R, _ = q.shape
    n_pages = page_table.shape[1]
    depth = cache.shape[1]
    ppb = MOBA_BLOCK // PAGE_SIZE
    nblk = n_pages // ppb
    prow = PAGE_SIZE * 2 * H
    view = _flat_pages(cache)
    assert n_pages % P == 0 and P % ppb == 0 and nblk <= LANES and cache.shape[2] == PAGE_SIZE

    def page_map(b, s, pt, r):
        return (pt[b, s * P + r] * depth + l, 0)

    stat = pl.BlockSpec((None, H, R, LANES), lambda b, s, pt: (b, 0, 0, 0))
    stat_shape = jax.ShapeDtypeStruct((Bd, H, R, LANES), F32)
    return pl.pallas_call(
        functools.partial(_moba_scan_body, P=P),
        grid_spec=pltpu.PrefetchScalarGridSpec(
            num_scalar_prefetch=1, grid=(Bd, n_pages // P),
            in_specs=[pl.BlockSpec((None, H, R, LANES), lambda b, s, pt: (b, 0, 0, 0))]
            + [pl.BlockSpec((prow, LANES), functools.partial(page_map, r=r)) for r in range(P)],
            out_specs=[pl.BlockSpec((None, H, P // ppb, R, LANES), lambda b, s, pt: (b, 0, s, 0, 0)),
                       stat, stat, stat]),
        out_shape=[jax.ShapeDtypeStruct((Bd, H, nblk, R, LANES), F32), stat_shape, stat_shape, stat_shape],
        compiler_params=_cparams(("parallel", "arbitrary")),
        name="moba_scan",
    )(page_table, q, *([view] * P))


def _moba_combine_body(q_ref, acc_ref, m_ref, l_ref, sc_ref, kn_ref, vn_ref, o_ref, *, nblk, n_new):
    H, R = MOBA_HEADS, DEC_POS
    scale = HEAD_DIM ** -0.5
    lane = lax.broadcasted_iota(jnp.int32, (R, LANES), 1)
    rpos = lax.broadcasted_iota(jnp.int32, (R, LANES), 0) % DEC_POS
    for h in range(H):
        score = jnp.where(lane < nblk, sc_ref[h], -jnp.inf)
        sel = (_rank_desc(score, nblk, 1) < min(MOBA_TOPK, nblk)) & (lane < nblk)
        m_blk = m_ref[h]
        q = q_ref[h].astype(BF16)
        s_new = _nt(q, kn_ref[:, h * LANES:(h + 1) * LANES].astype(BF16)) * scale
        ok_new = (lane <= rpos) & (lane < n_new)
        m_new = jnp.max(jnp.where(ok_new, s_new, NEG), axis=-1, keepdims=True)
        m_all = jnp.maximum(jnp.max(jnp.where(sel, m_blk, NEG), axis=-1, keepdims=True), m_new)
        p_new = jnp.where(ok_new, jnp.exp(s_new - m_all), 0.0)
        w = jnp.where(sel, jnp.exp(m_blk - m_all), 0.0)
        den = jnp.sum(w * l_ref[h], axis=-1, keepdims=True) + jnp.sum(p_new, axis=-1, keepdims=True)
        num = _mm(p_new.astype(BF16), vn_ref[:, h * LANES:(h + 1) * LANES].astype(BF16))
        for b in range(nblk):
            num = num + w[:, b:b + 1] * acc_ref[h, b]
        o_ref[h] = num / den


def _moba_combine(q, acc, m, lsum, sc, knew, vnew, *, n_new):
    Bd, H, nblk, R, _ = acc.shape

    def b4(a):
        return pl.BlockSpec((None,) + a.shape[1:], lambda b: (b,) + (0,) * (a.ndim - 1))

    return pl.pallas_call(
        functools.partial(_moba_combine_body, nblk=nblk, n_new=n_new),
        grid=(Bd,),
        in_specs=[b4(q), b4(acc), b4(m), b4(lsum), b4(sc), b4(knew), b4(vnew)],
        out_specs=pl.BlockSpec((None, H, R, LANES), lambda b: (b, 0, 0, 0)),
        out_shape=jax.ShapeDtypeStruct((Bd, H, R, LANES), F32),
        compiler_params=_cparams(("parallel",)),
        name="moba_combine",
    )(q, acc, m, lsum, sc, knew, vnew)


def _paged_attn_body(ptM_ref, ptm_ref, *refs, P, H, R, scale, blocksize, window, n_new):
    q_ref = refs[0]
    pages = refs[1:1 + P]
    rest = refs[1 + P:]
    if blocksize is not None:
        bias_ref, rest = rest[0], rest[1:]
    kn_ref, vn_ref, o_ref, m_ref, l_ref, acc_ref = rest
    s_id = pl.program_id(1)
    nk = P * PAGE_SIZE
    n_planes = 2 * H

    @pl.when(s_id == 0)
    def _():
        _online_init(m_ref, l_ref, acc_ref)

    if blocksize is not None:
        kblk = lax.broadcasted_iota(jnp.int32, (nk, LANES), 0) // blocksize
        onehot = (kblk == lax.broadcasted_iota(jnp.int32, (nk, LANES), 1)).astype(BF16)
    if window:
        kidx = s_id * nk + lax.broadcasted_iota(jnp.int32, (R, nk), 1)
        in_window = kidx > lax.broadcasted_iota(jnp.int32, (R, nk), 0) % DEC_POS
    heads = range(H)
    scores = []
    for h in heads:
        q = q_ref[h].astype(BF16)
        k = jnp.concatenate([_plane(pg, h, n_planes) for pg in pages], axis=0).astype(BF16)
        if blocksize is not None:
            q = jnp.concatenate([q, bias_ref[h].astype(BF16)], axis=1)
            k = jnp.concatenate([k, onehot], axis=1)
        s = _nt(q, k) * scale
        scores.append(jnp.where(in_window, s, NEG) if window else s)
    m_prev = [m_ref[h] for h in heads]
    m_new = [jnp.maximum(m_prev[h], jnp.max(scores[h], axis=-1, keepdims=True)) for h in heads]
    alpha = [jnp.exp(m_prev[h] - m_new[h]) for h in heads]
    probs = [jnp.exp(scores[h] - m_new[h]) for h in heads]
    l_new = [alpha[h] * l_ref[h] + jnp.sum(probs[h], axis=-1, keepdims=True) for h in heads]
    pv = []
    for h in heads:
        v = jnp.concatenate([_plane(pg, H + h, n_planes) for pg in pages], axis=0).astype(BF16)
        pv.append(_mm(probs[h].astype(BF16), v))
    for h in heads:
        acc_ref[h] = alpha[h] * acc_ref[h] + pv[h]
        m_ref[h] = m_new[h]
        l_ref[h] = l_new[h]

    @pl.when(s_id == pl.num_programs(1) - 1)
    def _():
        lane = lax.broadcasted_iota(jnp.int32, (R, PAGE_SIZE), 1)
        rpos = lax.broadcasted_iota(jnp.int32, (R, PAGE_SIZE), 0) % DEC_POS
        for h in range(H):
            kn = kn_ref[:, h * LANES:(h + 1) * LANES].astype(BF16)
            vn = vn_ref[:, h * LANES:(h + 1) * LANES].astype(BF16)
            s = _nt(q_ref[h].astype(BF16), kn) * scale
            s = jnp.where((lane <= rpos) & (lane < n_new), s, NEG)
            _online_update(s, vn, m_ref.at[h], l_ref.at[h], acc_ref.at[h])
            o_ref[h] = acc_ref[h] / l_ref[h]


def _paged_attn(q, cache, ptM, ptm, l, knew, vnew, bias=None, *, scale, blocksize=None, window=False, P, n_new):
    Bd, H, R, _ = q.shape
    n_pages = ptM.shape[1]
    depth = cache.shape[1]
    ppe = cache.shape[2] // PAGE_SIZE
    prow = PAGE_SIZE * 2 * H
    view = _flat_pages(cache)
    assert n_pages % P == 0 and cache.shape[4] == H

    def page_map(b, s, pM, pm, r):
        return ((pM[b, s * P + r] * depth + l) * ppe + pm[b, s * P + r], 0)

    in_specs = [pl.BlockSpec((None, H, R, LANES), lambda b, s, pM, pm: (b, 0, 0, 0))]
    in_specs += [pl.BlockSpec((prow, LANES), functools.partial(page_map, r=r)) for r in range(P)]
    args = [q] + [view] * P
    if blocksize is not None:
        in_specs.append(pl.BlockSpec((None, H, None, R, LANES), lambda b, s, pM, pm: (b, 0, s, 0, 0)))
        args.append(bias)
    for a in (knew, vnew):
        in_specs.append(pl.BlockSpec((None,) + a.shape[1:], lambda b, s, pM, pm: (b, 0, 0)))
        args.append(a)
    return pl.pallas_call(
        functools.partial(_paged_attn_body, P=P, H=H, R=R, scale=scale, blocksize=blocksize, window=window,
                          n_new=n_new),
        grid_spec=pltpu.PrefetchScalarGridSpec(
            num_scalar_prefetch=2, grid=(Bd, n_pages // P), in_specs=in_specs,
            out_specs=pl.BlockSpec((None, H, R, LANES), lambda b, s, pM, pm: (b, 0, 0, 0)),
            scratch_shapes=[pltpu.VMEM((H, R, 1), F32), pltpu.VMEM((H, R, 1), F32), pltpu.VMEM((H, R, LANES), F32)]),
        out_shape=jax.ShapeDtypeStruct((Bd, H, R, LANES), F32),
        compiler_params=_cparams(("parallel", "arbitrary")),
        name="paged_attn",
    )(ptM, ptm, *args)


def _dec_finalize_body(ocmp_ref, oslc_ref, owin_ref, gate_ref, odiff_ref, lam_ref, sub_ref, omoba_ref,
                       an_ref, ad_ref, am_ref, *, lam_init):
    gt = gate_ref[0:DEC_POS, :]
    an_ref[...] = jnp.zeros(an_ref.shape, F32)
    ad_ref[...] = jnp.zeros(ad_ref.shape, F32)
    am_ref[...] = jnp.zeros(am_ref.shape, F32)
    for g in range(NSA_GROUPS):
        for h in range(NSA_REP):
            head = NSA_REP * g + h
            rows = slice(h * DEC_POS, (h + 1) * DEC_POS)
            out = (_lane_pick(gt, head) * ocmp_ref[g, rows, :]
                   + _lane_pick(gt, NSA_HEADS + head) * oslc_ref[g, rows, :]
                   + _lane_pick(gt, 2 * NSA_HEADS + head) * owin_ref[g, rows, :])
            an_ref[0:DEC_POS, head * LANES:(head + 1) * LANES] = out
    for h in range(DIFF_HEADS):
        o = odiff_ref[h]
        ad_ref[0:DEC_POS, h * LANES:(h + 1) * LANES] = _diff_finish(o[:DEC_POS], o[DEC_POS:], lam_ref, sub_ref, lam_init)
    for h in range(MOBA_HEADS):
        am_ref[0:DEC_POS, h * LANES:(h + 1) * LANES] = omoba_ref[h]


def _dec_finalize(o_cmp, o_slc, o_win, proj3, o_diff, lam, subln, o_moba, *, lam_init):
    Bd = o_cmp.shape[0]

    def b4(a):
        return pl.BlockSpec((None,) + a.shape[1:], lambda b: (b, 0, 0, 0))

    outs = [(NSA_HEADS * LANES), (DIFF_HEADS * LANES), (MOBA_HEADS * LANES)]
    return pl.pallas_call(
        functools.partial(_dec_finalize_body, lam_init=lam_init),
        grid=(Bd,),
        in_specs=[b4(o_cmp), b4(o_slc), b4(o_win),
                  pl.BlockSpec((None, DEC_ROWS, LANES), lambda b: (b, 0, BGATE)),
                  b4(o_diff), pl.BlockSpec(lam.shape, lambda b: (0, 0)), pl.BlockSpec((1, LANES), lambda b: (0, 0)),
                  b4(o_moba)],
        out_specs=[pl.BlockSpec((None, DEC_ROWS, c), lambda b: (b, 0, 0)) for c in outs],
        out_shape=[jax.ShapeDtypeStruct((Bd, DEC_ROWS, c), F32) for c in outs],
        compiler_params=_cparams(("parallel",)),
        name="dec_finalize",
    )(o_cmp, o_slc, o_win, proj3, o_diff, lam, subln.reshape(1, LANES), o_moba)


def _head_major(cols, heads):
    Bd = cols.shape[0]
    return cols[:, :DEC_POS].reshape(Bd, DEC_POS, heads, HEAD_DIM).transpose(0, 2, 1, 3)


def _new_rows(proj3, blk, heads):
    rows = _cols(proj3, blk, heads)
    return jnp.pad(rows, ((0, 0), (0, PAGE_SIZE - rows.shape[1]), (0, 0)))


def _sample_attn(proj3, caches, page_table, P, l, n_new, *, scan_pages=16, nsa_scan_pages=16):
    cache_cmp, cache_slc, cache_win, cache_diff, cache_moba = caches
    Bd = proj3.shape[0]
    n_pages = page_table.shape[1]
    past = n_pages * PAGE_SIZE
    wkeep = cache_win.shape[2]
    assert n_new <= DEC_POS and wkeep == NSA_WINDOW and past >= NSA_WINDOW
    win_pages = wkeep // PAGE_SIZE
    zeros_pt = jnp.zeros_like(page_table)
    win_major = jnp.broadcast_to(jnp.arange(Bd, dtype=jnp.int32)[:, None], (Bd, win_pages))
    win_minor = jnp.broadcast_to(jnp.arange(win_pages, dtype=jnp.int32)[None, :], (Bd, win_pages))
    half = NSA_CMP_STRIDE * HEAD_DIM

    def w1ab(w1):
        return jnp.concatenate([w1[:half].reshape(NSA_CMP_STRIDE, LANES, LANES),
                                w1[half:].reshape(NSA_CMP_STRIDE, LANES, LANES)], axis=-1)
    wab = jnp.stack([w1ab(P["nsa_w1_k"][l]), w1ab(P["nsa_w1_v"][l])]).astype(BF16)
    wab = wab.reshape(2, NSA_CMP_STRIDE // 2, 2 * LANES, 2 * LANES)
    ab = _cmp_scan(cache_cmp, page_table, wab, l, P=nsa_scan_pages)
    qn = _head_major(_cols(proj3, BQ, NSA_HEADS), NSA_HEADS).reshape(Bd, NSA_GROUPS, NSA_REP * DEC_POS, LANES)
    cw = (P["nsa_pe_k"][l], P["nsa_w1_k"][l], P["nsa_w2_k"][l],
          P["nsa_pe_v"][l], P["nsa_w1_v"][l], P["nsa_w2_v"][l])
    o_cmp, bias_slc = _nsa_dec_select(ab, qn, cw, past=past,
                                      per_step=nsa_scan_pages * PAGE_SIZE // NSA_SLC_LEN)
    sc128 = HEAD_DIM ** -0.5
    o_slc = _paged_attn(qn, cache_slc, page_table, zeros_pt, l, _new_rows(proj3, BKS, NSA_GROUPS),
                        _new_rows(proj3, BVS, NSA_GROUPS), bias_slc, scale=sc128, blocksize=NSA_SLC_LEN,
                        P=nsa_scan_pages, n_new=n_new)
    o_win = _paged_attn(qn, cache_win, win_major, win_minor, l, _new_rows(proj3, BKW, NSA_GROUPS),
                        _new_rows(proj3, BVW, NSA_GROUPS), scale=sc128, window=True, P=win_pages, n_new=n_new)
    qd = _head_major(_cols(proj3, BDQ, DIFF_HEADS), DIFF_HEADS)
    lane = jnp.arange(LANES)
    qd = jnp.concatenate([jnp.where(lane < DIFF_QK_DIM, qd, 0.0), jnp.where(lane >= DIFF_QK_DIM, qd, 0.0)], axis=2)
    o_diff = _paged_attn(qd, cache_diff, page_table, zeros_pt, l, _new_rows(proj3, BDK, DIFF_HEADS),
                         _new_rows(proj3, BDV, DIFF_HEADS), scale=DIFF_QK_DIM ** -0.5, P=scan_pages, n_new=n_new)
    qm = _head_major(_cols(proj3, BMQ, MOBA_HEADS), MOBA_HEADS)
    parts = _moba_scan(qm, cache_moba, page_table, l, P=scan_pages)
    o_moba = _moba_combine(qm, *parts, _new_rows(proj3, BMK, MOBA_HEADS), _new_rows(proj3, BMV, MOBA_HEADS),
                           n_new=n_new)
    return _dec_finalize(o_cmp, o_slc, o_win, proj3, o_diff, P["diff_lambda"][l], P["diff_subln"][l], o_moba,
                         lam_init=_lam_init(l))


def _sample_trunk(x_sample, caches, page_table, P):
    cache_cmp, cache_slc, cache_win, cache_diff, cache_moba, cache_mem = caches
    Bd, n_new, D = x_sample.shape
    depth = P["w_in"].shape[0]
    past = page_table.shape[1] * PAGE_SIZE
    pos = past + jnp.arange(DEC_ROWS, dtype=jnp.int32)
    tabs = [jnp.tile(t, (Bd, 1)) for t in _rope_tables(pos)]
    x = jnp.pad(x_sample, ((0, 0), (0, DEC_ROWS - n_new), (0, 0))).reshape(Bd * DEC_ROWS, D)
    mem4 = cache_mem.reshape(Bd, depth, MEM_TOKENS, 2 * MEM_HEADS * MEM_DIM)
    rows = [[] for _ in range(5)]
    wkeep = cache_win.shape[2]
    win_buf = _row_buffer(Bd, depth, wkeep, NSA_GROUPS)
    for l in range(depth):
        proj = _norm_matmul(x, P["norm_mix"][l], _regroup_w_in(P["w_in"][l]), tabs, tm=Bd * DEC_ROWS, tn=PROJ_TILE)
        proj3 = proj.reshape(Bd, DEC_ROWS, NPROJ)
        new3 = proj3[:, :n_new]
        rows[0].append(_cache_rows(new3, BKC, BVC, NSA_GROUPS))
        rows[1].append(_cache_rows(new3, BKS, BVS, NSA_GROUPS))
        win_buf = _shift_window(win_buf, cache_win,
                                _cache_rows(new3, BKW, BVW, NSA_GROUPS).reshape(Bd, -1, HEAD_DIM), l)
        rows[3].append(_cache_rows(new3, BDK, BDV, DIFF_HEADS))
        rows[4].append(_cache_rows(new3, BMK, BMV, MOBA_HEADS))
        a_nsa, a_diff, a_moba = _sample_attn(proj3, caches[:5], page_table, P, l, n_new)
        x3 = _out_mem(a_nsa, a_diff, a_moba, proj3, x.reshape(Bd, DEC_ROWS, D), P["w_out"][l].astype(BF16),
                      P["norm_mem_x"][l].reshape(1, D), P["w_mem_q"][l].astype(BF16), mem4[:, l],
                      P["w_mem_o"][l].astype(BF16), tm=DEC_ROWS)
        x = x3.reshape(Bd * DEC_ROWS, D)
    y = _rmsnorm(x, P["norm_final"], tm=Bd * DEC_ROWS).reshape(Bd, DEC_ROWS, D)[:, :n_new]
    stacked = [jnp.stack(r, axis=1) if r else _as_rows(win_buf, Bd, depth, wkeep, NSA_GROUPS) for r in rows]
    return y, stacked


def kernel(x_prompt, x_sample, mem_prompt, cache_nsa_cmp, cache_nsa_slc, cache_nsa_win, cache_diff, cache_moba, cache_mem, page_table, norm_mix, w_in, w_out, nsa_pe_k, nsa_pe_v, nsa_w1_k, nsa_w2_k, nsa_w1_v, nsa_w2_v, diff_lambda, diff_subln, norm_mem_x, norm_mem_m, w_mem_q, w_mem_kv, w_mem_o, norm_final):
    P = {"norm_mix": norm_mix, "w_in": w_in, "w_out": w_out, "nsa_pe_k": nsa_pe_k, "nsa_pe_v": nsa_pe_v,
         "nsa_w1_k": nsa_w1_k, "nsa_w2_k": nsa_w2_k, "nsa_w1_v": nsa_w1_v, "nsa_w2_v": nsa_w2_v,
         "diff_lambda": diff_lambda, "diff_subln": diff_subln, "norm_mem_x": norm_mem_x,
         "norm_mem_m": norm_mem_m, "w_mem_q": w_mem_q, "w_mem_kv": w_mem_kv, "w_mem_o": w_mem_o,
         "norm_final": norm_final}
    y_prompt, (p_cmp, p_slc, p_win, p_diff, p_moba), p_mem = _prompt_trunk(x_prompt, mem_prompt, P)
    caches = (cache_nsa_cmp, cache_nsa_slc, cache_nsa_win, cache_diff, cache_moba, cache_mem)
    y_sample, (s_cmp, s_slc, s_win, s_diff, s_moba) = _sample_trunk(x_sample, caches, page_table, P)
    return (y_prompt, y_sample, p_cmp, p_slc, p_win, p_diff, p_moba, p_mem, s_cmp, s_slc, s_win, s_diff, s_moba)
```

```python
import functools
import math

import jax
import jax.numpy as jnp
from jax import lax
from jax.experimental import pallas as pl
from jax.experimental.pallas import tpu as pltpu

F32 = jnp.float32
BF16 = jnp.bfloat16

D_MODEL = 2048
HEAD_DIM = 128
ROPE_THETA = 500000.0
NORM_EPS = 1e-6
PAGE_SIZE = 128

NSA_HEADS = 8
NSA_GROUPS = 2
NSA_REP = NSA_HEADS // NSA_GROUPS
NSA_CMP_STRIDE = 16
NSA_SLC_LEN = 64
NSA_TOPN = 16
NSA_WINDOW = 512
NSA_FORCE_BONUS = 1000.0
NSA_COL_GROUPS = 4
NSA_TQ = 512
DIFF_TQ = 512
PROJ_TILE = 512
DIFF_HEADS = 4
DIFF_QK_DIM = HEAD_DIM // 2
MOBA_HEADS = 4
MOBA_BLOCK = 256
MOBA_TOPK = 3
MEM_TOKENS = 256
MEM_HEADS = 4
MEM_DIM = 128

NEG = -1e30
LANES = 128
SUBLANES = 8
VMEM_LIMIT_BYTES = 56 * 1024 * 1024

BQ, BKC, BKS, BKW, BMQ, BMK = 0, 8, 10, 12, 14, 18
BDQ, BDK = 22, 26
BVC, BVS, BVW, BNZ, BDV, BDZ, BMV, BMZ = 30, 32, 34, 36, 44, 48, 52, 56
BGATE = 60
NBLK = 64
NPROJ = NBLK * LANES
_W_IN_SEGMENTS = (
    (0, 1024), (1024, 1280), (1536, 1792), (2048, 2304), (5656, 6168), (6168, 6680),
    (3608, 4120), (4120, 4632),
    (1280, 1536), (1792, 2048), (2304, 2560), (2584, 3608), (4632, 5144), (5144, 5656),
    (6680, 7192), (7192, 7704),
    (2560, 2584),
)


def _round_up(n, m):
    return -(-n // m) * m


def _nt(a, b):
    return lax.dot_general(a, b, (((1,), (1,)), ((), ())), preferred_element_type=F32)


def _mm(a, b):
    return jnp.dot(a, b, preferred_element_type=F32)


def _silu(z):
    return z * jax.nn.sigmoid(z)


def _cparams(sem):
    return pltpu.CompilerParams(dimension_semantics=sem, vmem_limit_bytes=VMEM_LIMIT_BYTES)


def _norm_matmul_body(*refs, tn, splits, rope):
    if rope:
        x_ref, g_ref, w_ref, c1, a1, b1, c2, a2, b2, o_ref, h_ref = refs
    else:
        x_ref, g_ref, w_ref, o_ref, h_ref = refs
    j = pl.program_id(1)

    @pl.when(j == 0)
    def _():
        x = x_ref[...]
        ms = jnp.mean(x * x, axis=-1, keepdims=True)
        h_ref[...] = (x * lax.rsqrt(ms + NORM_EPS) * g_ref[...]).astype(BF16)

    acc = _mm(h_ref[...], w_ref[...])
    if not rope:
        o_ref[...] = acc
        return

    def epilogue(kind, blk):
        if kind == "rot128":
            half = HEAD_DIM // 8
            return blk * c1[...] + pltpu.roll(blk, LANES - half, 1) * a1[...] + pltpu.roll(blk, half, 1) * b1[...]
        if kind == "rot64":
            half = DIFF_QK_DIM // 8
            return blk * c2[...] + pltpu.roll(blk, LANES - half, 1) * a2[...] + pltpu.roll(blk, half, 1) * b2[...]
        if kind == "gate":
            return jax.nn.sigmoid(blk)
        return blk

    for lo, hi, kinds in splits:
        @pl.when((j >= lo) & (j <= hi))
        def _(kinds=kinds):
            for k, kind in enumerate(kinds):
                o_ref[:, k * LANES:(k + 1) * LANES] = epilogue(kind, acc[:, k * LANES:(k + 1) * LANES])


def _tile_patterns(tn):
    kinds = (["rot128"] * (BDQ - BQ) + ["rot64"] * (BVC - BDQ) + ["plain"] * (BGATE - BVC) + ["gate"]
             + ["plain"] * (NBLK - BGATE - 1))
    per = tn // LANES
    tiles = [tuple(kinds[t * per:(t + 1) * per]) for t in range(NBLK // per)]
    runs = []
    for t, pat in enumerate(tiles):
        if runs and runs[-1][2] == pat:
            runs[-1] = (runs[-1][0], t, pat)
        else:
            runs.append((t, t, pat))
    return tuple(runs)


def _norm_matmul(x, gain, w, tabs=None, *, tm, tn):
    M, K = x.shape
    Np = w.shape[1]
    rope = tabs is not None
    splits = _tile_patterns(tn) if rope else None
    in_specs = [
        pl.BlockSpec((tm, K), lambda i, j: (i, 0)),
        pl.BlockSpec((1, K), lambda i, j: (0, 0)),
        pl.BlockSpec((K, tn), lambda i, j: (0, j)),
    ]
    args = [x, gain.reshape(1, K), w]
    if rope:
        period = tabs[0].shape[0] // tm
        for t in tabs:
            in_specs.append(pl.BlockSpec((tm, LANES), lambda i, j: (i % period, 0)))
            args.append(t)
    return pl.pallas_call(
        functools.partial(_norm_matmul_body, tn=tn, splits=splits, rope=rope),
        grid=(M // tm, Np // tn),
        in_specs=in_specs,
        out_specs=pl.BlockSpec((tm, tn), lambda i, j: (i, j)),
        out_shape=jax.ShapeDtypeStruct((M, Np), F32),
        scratch_shapes=[pltpu.VMEM((tm, K), BF16)],
        compiler_params=_cparams(("parallel", "arbitrary")),
        name="norm_matmul",
    )(*args)


def _rope_tables(pos):
    posf = pos.astype(F32)[:, None]
    n = pos.shape[0]

    def one(width, reps):
        rd = width // 4
        half = rd // 2
        inv = ROPE_THETA ** (-2.0 * jnp.arange(half, dtype=F32) / rd)
        ang = posf * inv[None, :]
        c, s = jnp.cos(ang), jnp.sin(ang)
        z = jnp.zeros((n, width - rd), F32)
        zh = jnp.zeros((n, half), F32)
        C = jnp.concatenate([c, c, jnp.ones((n, width - rd), F32)], axis=1)
        A = jnp.concatenate([-s, zh, z], axis=1)
        B = jnp.concatenate([zh, s, z], axis=1)
        return [jnp.tile(t, (1, reps)) for t in (C, A, B)]

    return one(HEAD_DIM, 1) + one(DIFF_QK_DIM, 2)


def _online_init(m_ref, l_ref, acc_ref):
    m_ref[...] = jnp.full(m_ref.shape, NEG, F32)
    l_ref[...] = jnp.zeros(l_ref.shape, F32)
    acc_ref[...] = jnp.zeros(acc_ref.shape, F32)


def _online_update(s, v, m_ref, l_ref, acc_ref):
    m_prev = m_ref[...]
    m_new = jnp.maximum(m_prev, jnp.max(s, axis=-1, keepdims=True))
    alpha = jnp.exp(m_prev - m_new)
    p = jnp.exp(s - m_new)
    l_ref[...] = alpha * l_ref[...] + jnp.sum(p, axis=-1, keepdims=True)
    acc_ref[...] = alpha * acc_ref[...] + _mm(p.astype(BF16), v)
    m_ref[...] = m_new


def _online_update_t(s, vt, m_ref, l_ref, acc_ref):
    m_prev = m_ref[...]
    m_new = jnp.maximum(m_prev, jnp.max(s, axis=0, keepdims=True))
    alpha = jnp.exp(m_prev - m_new)
    p = jnp.exp(s - m_new)
    l_ref[...] = alpha * l_ref[...] + jnp.sum(p, axis=0, keepdims=True)
    acc_ref[...] = alpha * acc_ref[...] + _mm(vt, p.astype(BF16))
    m_ref[...] = m_new


def _staged_update_t(scores, vts, m_ref, l_ref, acc_ref, idxs):
    n = range(len(scores))
    m_prev = [m_ref[idxs[j]] for j in n]
    m_new = [jnp.maximum(m_prev[j], jnp.max(scores[j], axis=0, keepdims=True)) for j in n]
    alpha = [jnp.exp(m_prev[j] - m_new[j]) for j in n]
    probs = [jnp.exp(scores[j] - m_new[j]) for j in n]
    l_new = [alpha[j] * l_ref[idxs[j]] + jnp.sum(probs[j], axis=0, keepdims=True) for j in n]
    pv = [_mm(vts[j], probs[j].astype(BF16)) for j in n]
    for j in n:
        acc_ref[idxs[j]] = alpha[j] * acc_ref[idxs[j]] + pv[j]
        m_ref[idxs[j]] = m_new[j]
        l_ref[idxs[j]] = l_new[j]


def _masked_softmax(s, mask, axis):
    m = jnp.max(jnp.where(mask, s, NEG), axis=axis, keepdims=True)
    m = jnp.where(m > 0.5 * NEG, m, 0.0)
    e = jnp.where(mask, jnp.exp(s - m), 0.0)
    return e / jnp.maximum(jnp.sum(e, axis=axis, keepdims=True), 1e-30)


def _rank_desc(score, ncand, axis):
    idx = lax.broadcasted_iota(jnp.int32, score.shape, axis)
    rank = jnp.zeros(score.shape, F32)
    for c in range(ncand):
        cand = score[:, c:c + 1] if axis == 1 else score[c:c + 1, :]
        before = (cand > score) | ((cand == score) & (c < idx))
        rank = rank + before.astype(F32)
    return rank


def _lane_pick(x, idx):
    lane = lax.broadcasted_iota(jnp.int32, x.shape, 1)
    return jnp.sum(jnp.where(lane == idx, x, 0.0), axis=-1, keepdims=True)


def _compress(src_ref, pe_ref, w1_ref, w2_ref, nc):
    half = NSA_CMP_STRIDE * HEAD_DIM
    acc_a = jnp.zeros((nc, LANES), F32)
    acc_b = jnp.zeros((nc, LANES), F32)
    for j in range(NSA_CMP_STRIDE):
        rows = src_ref[pl.ds(j, nc, stride=NSA_CMP_STRIDE), :].astype(BF16)
        acc_a = acc_a + _mm(rows, w1_ref[j * LANES:(j + 1) * LANES, :].astype(BF16))
        acc_b = acc_b + _mm(rows, w1_ref[half + j * LANES:half + (j + 1) * LANES, :].astype(BF16))
    return _compress_finish(acc_a, acc_b, pe_ref, w1_ref, w2_ref, nc)


def _pe_term(pe_ref, w1_ref):
    acc = jnp.zeros((SUBLANES, LANES), F32)
    for j in range(2 * NSA_CMP_STRIDE):
        pj = jnp.broadcast_to(pe_ref[j:j + 1, :], (SUBLANES, LANES)).astype(BF16)
        acc = acc + _mm(pj, w1_ref[j * LANES:(j + 1) * LANES, :].astype(BF16))
    return acc[0:1, :]


def _compress_finish(acc_a, acc_b, pe_ref, w1_ref, w2_ref, nc):
    row = lax.broadcasted_iota(jnp.int32, (nc, LANES), 0)
    hb = jnp.where(row < nc - 1, pltpu.roll(acc_b, nc - 1, 0), 0.0)
    hid = _silu(acc_a + hb + _pe_term(pe_ref, w1_ref))
    return _mm(hid.astype(BF16), w2_ref[...].astype(BF16))


def _slc_weights(shape, cmp_axis):
    r = lax.broadcasted_iota(jnp.int32, shape, cmp_axis)
    c = lax.broadcasted_iota(jnp.int32, shape, 1 - cmp_axis)
    d = r - 4 * c
    return jnp.where((d == -1) | (d == 3), 1.0, jnp.where((d >= 0) & (d <= 2), 2.0, 0.0)).astype(F32)


def _transpose_into(dst_ref, src_ref, rows, chunk):
    for c in range(rows // chunk):
        dst_ref[:, c * chunk:(c + 1) * chunk] = src_ref[c * chunk:(c + 1) * chunk, :].T.astype(BF16)


def _nsa_prompt_body(q_ref, kc_ref, ks_ref, kw_ref, vc_ref, vs_ref, vw_ref, gate_ref,
                     pek_ref, w1k_ref, w2k_ref, pev_ref, w1v_ref, w2v_ref, o_ref,
                     kcs, vct, kaug, vst, kwb, vwt, gt_ref, m_ref, l_ref, acc_ref, *, T, tq):
    g = pl.program_id(1)
    i = pl.program_id(2)
    nc = T // NSA_CMP_STRIDE
    nsb = T // NSA_SLC_LEN
    nsbr = _round_up(nsb, SUBLANES)
    nsel = min(NSA_TOPN, nsb)
    R = NSA_REP * tq
    scale = HEAD_DIM ** -0.5

    @pl.when(i == 0)
    def _():
        kcs[...] = _compress(kc_ref, pek_ref, w1k_ref, w2k_ref, nc).astype(BF16)
        vct[...] = _compress(vc_ref, pev_ref, w1v_ref, w2v_ref, nc).T.astype(BF16)
        kaug[:, :LANES] = ks_ref[...].astype(BF16)
        rblk = lax.broadcasted_iota(jnp.int32, (T, LANES), 0) // NSA_SLC_LEN
        lane = lax.broadcasted_iota(jnp.int32, (T, LANES), 1)
        kaug[:, LANES:] = (rblk == lane).astype(BF16)
        kwb[...] = kw_ref[...].astype(BF16)
        _transpose_into(vst, vs_ref, T, tq)
        _transpose_into(vwt, vw_ref, T, tq)

    q = q_ref[...]
    qt = jnp.concatenate([q[:, h * LANES:(h + 1) * LANES].T for h in range(NSA_REP)], axis=1).astype(BF16)

    s = _mm(kcs[...], qt) * scale
    crow = lax.broadcasted_iota(jnp.int32, (nc, R), 0)
    t_col = i * tq + lax.broadcasted_iota(jnp.int32, (nc, R), 1) % tq
    pc = _masked_softmax(s, (NSA_CMP_STRIDE * crow + 2 * NSA_CMP_STRIDE - 1) <= t_col, 0)
    o_cmp = _mm(vct[...], pc.astype(BF16))
    pg = pc[:, 0:tq]
    for h in range(1, NSA_REP):
        pg = pg + pc[:, h * tq:(h + 1) * tq]

    slc = jnp.dot(_slc_weights((LANES, nc), 1), pg, precision=lax.Precision.HIGHEST,
                  preferred_element_type=F32)[:nsbr]
    blk = lax.broadcasted_iota(jnp.int32, (nsbr, tq), 0)
    tb = (i * tq + lax.broadcasted_iota(jnp.int32, (nsbr, tq), 1)) // NSA_SLC_LEN
    valid = blk <= tb
    forced = (blk == 0) | (blk == tb) | (blk == tb - 1)
    score = jnp.where(valid, slc + jnp.where(forced, NSA_FORCE_BONUS, 0.0), -jnp.inf)
    sel = (_rank_desc(score, nsb, 0) < nsel) & valid
    bias = jnp.where(sel, 0.0, NEG)
    if nsbr < LANES:
        bias = jnp.concatenate([bias, jnp.full((LANES - nsbr, tq), NEG, F32)], axis=0)
    bias = bias.astype(BF16)
    qaug = jnp.concatenate([qt, jnp.concatenate([bias] * NSA_REP, axis=1)], axis=0)

    gw = R // NSA_COL_GROUPS
    cols = [(slice(None), slice(j * gw, (j + 1) * gw)) for j in range(NSA_COL_GROUPS)]
    krow = lax.broadcasted_iota(jnp.int32, (tq, gw), 0)
    tloc = lax.broadcasted_iota(jnp.int32, (tq, gw), 1) % tq

    def attend(k_tile, vt_tile, q_all, mask):
        scores = []
        for c in cols:
            s2 = _mm(k_tile, q_all[c]) * scale
            scores.append(s2 if mask is None else jnp.where(mask, s2, NEG))
        _staged_update_t(scores, [vt_tile] * NSA_COL_GROUPS, m_ref, l_ref, acc_ref, cols)

    _online_init(m_ref, l_ref, acc_ref)

    def slc_step(kt, carry):
        off = pl.multiple_of(kt * tq, tq)
        attend(kaug[pl.ds(off, tq), :], vst[:, pl.ds(off, tq)], qaug, None)
        return carry

    lax.fori_loop(0, i, slc_step, 0)
    off_d = pl.multiple_of(i * tq, tq)
    attend(kaug[pl.ds(off_d, tq), :], vst[:, pl.ds(off_d, tq)], qaug, krow <= tloc)
    o_slc = acc_ref[...] / l_ref[...]

    _online_init(m_ref, l_ref, acc_ref)
    nw = NSA_WINDOW // tq
    for d in range(nw, -1, -1):
        @pl.when(i >= d)
        def _(d=d):
            off = pl.multiple_of((i - d) * tq, tq)
            mask = (krow > tloc) if d == nw else ((krow <= tloc) if d == 0 else None)
            attend(kwb[pl.ds(off, tq), :], vwt[:, pl.ds(off, tq)], qt, mask)
    o_win = acc_ref[...] / l_ref[...]

    gt_ref[...] = gate_ref[...].T

    def gate_row(branch):
        return jnp.concatenate(
            [gt_ref[pl.ds(branch * NSA_HEADS + NSA_REP * g + h, 1), :] for h in range(NSA_REP)], axis=1)

    out = gate_row(0) * o_cmp + gate_row(1) * o_slc + gate_row(2) * o_win
    for h in range(NSA_REP):
        o_ref[:, h * LANES:(h + 1) * LANES] = out[:, h * tq:(h + 1) * tq].T


def _nsa_prompt(proj, cw, *, B, T, tq):
    nqt = T // tq
    nc = T // NSA_CMP_STRIDE
    R = NSA_REP * tq
    assert T % NSA_SLC_LEN == 0 and NSA_WINDOW % tq == 0 and T // NSA_SLC_LEN <= LANES and nc <= LANES

    def kv_spec(blk):
        return pl.BlockSpec((T, LANES), lambda b, g, i: (b, blk + g))

    def full(a):
        return pl.BlockSpec(a.shape, lambda b, g, i: (0,) * a.ndim)

    in_specs = [
        pl.BlockSpec((tq, NSA_REP * LANES), lambda b, g, i: (b * nqt + i, g)),
        kv_spec(BKC), kv_spec(BKS), kv_spec(BKW), kv_spec(BVC), kv_spec(BVS), kv_spec(BVW),
        pl.BlockSpec((tq, LANES), lambda b, g, i: (b * nqt + i, BGATE)),
    ] + [full(a) for a in cw]
    return pl.pallas_call(
        functools.partial(_nsa_prompt_body, T=T, tq=tq),
        grid=(B, NSA_GROUPS, nqt),
        in_specs=in_specs,
        out_specs=pl.BlockSpec((tq, NSA_REP * LANES), lambda b, g, i: (b * nqt + i, g)),
        out_shape=jax.ShapeDtypeStruct((B * T, NSA_HEADS * LANES), F32),
        scratch_shapes=[
            pltpu.VMEM((nc, LANES), BF16), pltpu.VMEM((LANES, nc), BF16),
            pltpu.VMEM((T, 2 * LANES), BF16), pltpu.VMEM((LANES, T), BF16),
            pltpu.VMEM((T, LANES), BF16), pltpu.VMEM((LANES, T), BF16),
            pltpu.VMEM((LANES, tq), F32),
            pltpu.VMEM((1, R), F32), pltpu.VMEM((1, R), F32), pltpu.VMEM((LANES, R), F32),
        ],
        compiler_params=_cparams(("parallel", "parallel", "arbitrary")),
        name="nsa_prompt",
    )(proj, proj, proj, proj, proj, proj, proj, proj, *cw)


def _diff_lambda(lam_ref, lam_init):
    lp = lam_ref[...]
    return (jnp.exp(jnp.sum(lp[0:1] * lp[1:2], axis=-1, keepdims=True))
            - jnp.exp(jnp.sum(lp[2:3] * lp[3:4], axis=-1, keepdims=True)) + lam_init)


def _diff_finish(o1, o2, lam_ref, sub_ref, lam_init):
    a = o1 - _diff_lambda(lam_ref, lam_init) * o2
    ms = jnp.mean(a * a, axis=-1, keepdims=True)
    return a * lax.rsqrt(ms + NORM_EPS) * sub_ref[...] * (1.0 - lam_init)


def _head_specs(nqt, T, tq, qblk, kblk, vblk, heads):
    specs = [pl.BlockSpec((tq, LANES), functools.partial(lambda b, i, c: (b * nqt + i, c), c=qblk + h))
             for h in range(heads)]
    for blk in (kblk, vblk):
        specs += [pl.BlockSpec((T, LANES), functools.partial(lambda b, i, c: (b, c), c=blk + h))
                  for h in range(heads)]
    return specs


def _diff_prompt_body(*refs, T, tq, lam_init):
    nh = DIFF_HEADS
    q_refs, k_refs, v_refs = refs[:nh], refs[nh:2 * nh], refs[2 * nh:3 * nh]
    lam_ref, sub_ref, o_ref, kb, vt, m_ref, l_ref, acc_ref = refs[3 * nh:]
    i = pl.program_id(1)
    scale = DIFF_QK_DIM ** -0.5
    heads = range(nh)

    @pl.when(i == 0)
    def _():
        for h in heads:
            kb[h] = k_refs[h][...].astype(BF16)
            _transpose_into(vt.at[h], v_refs[h], T, tq)

    qts = []
    for h in heads:
        q = q_refs[h][...]
        lane = lax.broadcasted_iota(jnp.int32, q.shape, 1)
        qts.append(jnp.concatenate([jnp.where(lane < DIFF_QK_DIM, q, 0.0).T,
                                    jnp.where(lane >= DIFF_QK_DIM, q, 0.0).T], axis=1).astype(BF16))
    R = 2 * tq
    krow = lax.broadcasted_iota(jnp.int32, (tq, R), 0)
    tloc = lax.broadcasted_iota(jnp.int32, (tq, R), 1) % tq
    idxs = [(h,) for h in heads]
    _online_init(m_ref, l_ref, acc_ref)

    def attend(off, mask):
        scores = []
        for h in heads:
            s = _mm(kb[h, pl.ds(off, tq), :], qts[h]) * scale
            scores.append(s if mask is None else jnp.where(mask, s, NEG))
        _staged_update_t(scores, [vt[h, :, pl.ds(off, tq)] for h in heads], m_ref, l_ref, acc_ref, idxs)

    def step(kt, carry):
        attend(pl.multiple_of(kt * tq, tq), None)
        return carry

    lax.fori_loop(0, i, step, 0)
    attend(pl.multiple_of(i * tq, tq), krow <= tloc)
    for h in heads:
        o = acc_ref[h] / l_ref[h]
        o_ref[:, h * LANES:(h + 1) * LANES] = _diff_finish(o[:, :tq].T, o[:, tq:].T, lam_ref, sub_ref, lam_init)


def _diff_prompt(proj, lam, subln, *, B, T, tq, lam_init):
    nqt = T // tq
    nh = DIFF_HEADS
    return pl.pallas_call(
        functools.partial(_diff_prompt_body, T=T, tq=tq, lam_init=lam_init),
        grid=(B, nqt),
        in_specs=_head_specs(nqt, T, tq, BDQ, BDK, BDV, nh) + [
            pl.BlockSpec(lam.shape, lambda b, i: (0, 0)),
            pl.BlockSpec((1, LANES), lambda b, i: (0, 0)),
        ],
        out_specs=pl.BlockSpec((tq, nh * LANES), lambda b, i: (b * nqt + i, 0)),
        out_shape=jax.ShapeDtypeStruct((B * T, nh * LANES), F32),
        scratch_shapes=[
            pltpu.VMEM((nh, T, LANES), BF16), pltpu.VMEM((nh, LANES, T), BF16),
            pltpu.VMEM((nh, 1, 2 * tq), F32), pltpu.VMEM((nh, 1, 2 * tq), F32),
            pltpu.VMEM((nh, LANES, 2 * tq), F32),
        ],
        compiler_params=_cparams(("parallel", "arbitrary")),
        name="diff_prompt",
    )(*([proj] * (3 * nh)), lam, subln.reshape(1, LANES))


def _moba_prompt_body(*refs, T):
    nh = MOBA_HEADS
    q_refs, k_refs, v_refs = refs[:nh], refs[nh:2 * nh], refs[2 * nh:3 * nh]
    o_ref, kb, vt, km, bias_ref, m_ref, l_ref, acc_ref = refs[3 * nh:]
    i = pl.program_id(1)
    tq = MOBA_BLOCK
    nb = T // MOBA_BLOCK
    nbr = _round_up(nb, SUBLANES)
    scale = HEAD_DIM ** -0.5
    heads = range(nh)

    @pl.when(i == 0)
    def _():
        km[...] = jnp.zeros(km.shape, BF16)
        for h in heads:
            kb[h] = k_refs[h][...].astype(BF16)
            _transpose_into(vt.at[h], v_refs[h], T, tq)
            for j in range(nb):
                km[h, j:j + 1, :] = jnp.mean(k_refs[h][j * tq:(j + 1) * tq, :], axis=0, keepdims=True).astype(BF16)

    qts = [q_refs[h][...].T.astype(BF16) for h in heads]
    blk = lax.broadcasted_iota(jnp.int32, (nbr, tq), 0)
    past = blk < i
    for h in heads:
        score = jnp.where(past, _mm(km[h], qts[h])[:nbr], -jnp.inf)
        sel = (_rank_desc(score, nb, 0) < min(MOBA_TOPK, nb)) & past
        bias_ref[h] = jnp.where(sel, 0.0, NEG)
    krow = lax.broadcasted_iota(jnp.int32, (tq, tq), 0)
    tloc = lax.broadcasted_iota(jnp.int32, (tq, tq), 1)
    idxs = [(h,) for h in heads]
    _online_init(m_ref, l_ref, acc_ref)

    def step(kt, carry):
        off = pl.multiple_of(kt * tq, tq)
        scores = [_mm(kb[h, pl.ds(off, tq), :], qts[h]) * scale + bias_ref[h, pl.ds(kt, 1), :] for h in heads]
        _staged_update_t(scores, [vt[h, :, pl.ds(off, tq)] for h in heads], m_ref, l_ref, acc_ref, idxs)
        return carry

    lax.fori_loop(0, i, step, 0)
    off = pl.multiple_of(i * tq, tq)
    scores = [jnp.where(krow <= tloc, _mm(kb[h, pl.ds(off, tq), :], qts[h]) * scale, NEG) for h in heads]
    _staged_update_t(scores, [vt[h, :, pl.ds(off, tq)] for h in heads], m_ref, l_ref, acc_ref, idxs)
    for h in heads:
        o_ref[:, h * LANES:(h + 1) * LANES] = (acc_ref[h] / l_ref[h]).T


def _moba_prompt(proj, *, B, T):
    tq = MOBA_BLOCK
    assert T % tq == 0 and T // tq <= LANES
    nqt = T // tq
    nh = MOBA_HEADS
    nbr = _round_up(T // MOBA_BLOCK, SUBLANES)
    return pl.pallas_call(
        functools.partial(_moba_prompt_body, T=T),
        grid=(B, nqt),
        in_specs=_head_specs(nqt, T, tq, BMQ, BMK, BMV, nh),
        out_specs=pl.BlockSpec((tq, nh * LANES), lambda b, i: (b * nqt + i, 0)),
        out_shape=jax.ShapeDtypeStruct((B * T, nh * LANES), F32),
        scratch_shapes=[
            pltpu.VMEM((nh, T, LANES), BF16), pltpu.VMEM((nh, LANES, T), BF16), pltpu.VMEM((nh, LANES, LANES), BF16),
            pltpu.VMEM((nh, nbr, tq), F32),
            pltpu.VMEM((nh, 1, tq), F32), pltpu.VMEM((nh, 1, tq), F32), pltpu.VMEM((nh, LANES, tq), F32),
        ],
        compiler_params=_cparams(("parallel", "arbitrary")),
        name="moba_prompt",
    )(*([proj] * (3 * nh)))


def _out_mem_body(an_ref, ad_ref, am_ref, nz0_ref, nz1_ref, dz_ref, mz_ref, x_ref, wout_ref, gmem_ref,
                  wq_ref, mkv_ref, wo_ref, o_ref):
    half = NSA_HEADS * LANES // 2
    an = an_ref[...]
    mixed = jnp.concatenate([
        an[:, :half] * _silu(nz0_ref[...]), an[:, half:] * _silu(nz1_ref[...]),
        ad_ref[...] * _silu(dz_ref[...]), am_ref[...] * _silu(mz_ref[...])], axis=1).astype(BF16)
    x1 = x_ref[...] + _mm(mixed, wout_ref[...])
    ms = jnp.mean(x1 * x1, axis=-1, keepdims=True)
    h2 = (x1 * lax.rsqrt(ms + NORM_EPS) * gmem_ref[...]).astype(BF16)
    q = _mm(h2, wq_ref[...])
    mkv = mkv_ref[...]
    scale = MEM_DIM ** -0.5
    outs = []
    for hh in range(MEM_HEADS):
        qh = q[:, hh * LANES:(hh + 1) * LANES].astype(BF16)
        kh = mkv[:, hh * LANES:(hh + 1) * LANES].astype(BF16)
        vh = mkv[:, (MEM_HEADS + hh) * LANES:(MEM_HEADS + hh + 1) * LANES].astype(BF16)
        s = _nt(qh, kh) * scale
        e = jnp.exp(s - jnp.max(s, axis=-1, keepdims=True))
        p = e / jnp.sum(e, axis=-1, keepdims=True)
        outs.append(_mm(p.astype(BF16), vh))
    oc = jnp.concatenate(outs, axis=1).astype(BF16)
    o_ref[...] = x1 + _mm(oc, wo_ref[...])


def _out_mem(a_nsa, a_diff, a_moba, proj3, x3, wout, gmem, wq, mkv, wo, *, tm):
    nb, rows, D = x3.shape
    grid = (nb, rows // tm)

    def row(cols, cb):
        return pl.BlockSpec((None, tm, cols), lambda b, i: (b, i, cb))

    def const(a):
        return pl.BlockSpec(a.shape, lambda b, i: (0,) * a.ndim, pipeline_mode=pl.Buffered(1))

    zc = 4 * LANES
    return pl.pallas_call(
        _out_mem_body,
        grid=grid,
        in_specs=[
            row(NSA_HEADS * LANES, 0), row(DIFF_HEADS * LANES, 0), row(MOBA_HEADS * LANES, 0),
            row(zc, BNZ * LANES // zc), row(zc, BNZ * LANES // zc + 1), row(zc, BDZ * LANES // zc),
            row(zc, BMZ * LANES // zc),
            row(D, 0), const(wout), const(gmem), const(wq),
            pl.BlockSpec((None, MEM_TOKENS, 2 * MEM_HEADS * MEM_DIM), lambda b, i: (b, 0, 0)),
            const(wo),
        ],
        out_specs=row(D, 0),
        out_shape=jax.ShapeDtypeStruct(x3.shape, F32),
        compiler_params=_cparams(("parallel", "parallel")),
        name="out_mem",
    )(a_nsa, a_diff, a_moba, proj3, proj3, proj3, proj3, x3, wout, gmem, wq, mkv, wo)


def _rmsnorm_body(x_ref, g_ref, o_ref):
    x = x_ref[...]
    ms = jnp.mean(x * x, axis=-1, keepdims=True)
    o_ref[...] = x * lax.rsqrt(ms + NORM_EPS) * g_ref[...]


def _rmsnorm(x, gain, *, tm):
    M, D = x.shape
    return pl.pallas_call(
        _rmsnorm_body,
        grid=(M // tm,),
        in_specs=[pl.BlockSpec((tm, D), lambda i: (i, 0)), pl.BlockSpec((1, D), lambda i: (0, 0))],
        out_specs=pl.BlockSpec((tm, D), lambda i: (i, 0)),
        out_shape=jax.ShapeDtypeStruct((M, D), F32),
        compiler_params=_cparams(("parallel",)),
        name="final_norm",
    )(x, gain.reshape(1, D))


def _scatter_rows_body(*refs, n_planes):
    planes, o_ref = refs[:n_planes], refs[n_planes + 1]
    rows = planes[0].shape[0]
    for p in range(n_planes):
        o_ref[pl.ds(p, rows, stride=n_planes), :] = planes[p][...]


def _scatter_rows(buf, src, blocks, l, *, depth, seq_rows, t_start, t_len, tm):
    n_planes = len(blocks)
    nb = src.shape[0] // seq_rows
    tiles = t_len // tm
    assert t_len % tm == 0 and t_start % tm == 0 and seq_rows % tm == 0

    def src_map(b, i, c):
        return (b * (seq_rows // tm) + t_start // tm + i, c)

    in_specs = [pl.BlockSpec((tm, LANES), functools.partial(src_map, c=c)) for c in blocks]
    in_specs.append(pl.BlockSpec(memory_space=pl.ANY))
    return pl.pallas_call(
        functools.partial(_scatter_rows_body, n_planes=n_planes),
        grid=(nb, tiles),
        in_specs=in_specs,
        out_specs=pl.BlockSpec((tm * n_planes, LANES), lambda b, i: ((b * depth + l) * tiles + i, 0)),
        out_shape=jax.ShapeDtypeStruct(buf.shape, F32),
        input_output_aliases={n_planes: 0},
        compiler_params=_cparams(("parallel", "parallel")),
        name="scatter_rows",
    )(*([src] * n_planes), buf)


def _kv_blocks(kblk, vblk, heads):
    return [kblk + h for h in range(heads)] + [vblk + h for h in range(heads)]


def _row_buffer(entries, depth, rows, heads):
    return jnp.zeros((entries * depth * rows * 2 * heads, HEAD_DIM), F32)


def _as_rows(buf, entries, depth, rows, heads):
    return buf.reshape(entries, depth, rows, 2, heads, HEAD_DIM)


def _shift_window_body(old_ref, new_ref, buf_ref, o_ref, *, n_shift):
    keep = old_ref.shape[0] - n_shift
    o_ref[0:keep, :] = old_ref[n_shift:, :]
    o_ref[keep:, :] = new_ref[...]


def _shift_window(buf, cache_win, new_flat, l):
    Bd, depth, wkeep = cache_win.shape[:3]
    n_planes = 2 * NSA_GROUPS
    n_shift = new_flat.shape[1]
    old = cache_win.reshape(Bd * depth * wkeep * n_planes, HEAD_DIM)
    wrows = wkeep * n_planes
    return pl.pallas_call(
        functools.partial(_shift_window_body, n_shift=n_shift),
        grid=(Bd,),
        in_specs=[pl.BlockSpec((wrows, LANES), lambda b: (b * depth + l, 0)),
                  pl.BlockSpec((None, n_shift, LANES), lambda b: (b, 0, 0)),
                  pl.BlockSpec(memory_space=pl.ANY)],
        out_specs=pl.BlockSpec((wrows, LANES), lambda b: (b * depth + l, 0)),
        out_shape=jax.ShapeDtypeStruct(buf.shape, F32),
        input_output_aliases={2: 0},
        compiler_params=_cparams(("parallel",)),
        name="shift_window",
    )(old, new_flat, buf)


def _regroup_w_in(w_in_l):
    K = w_in_l.shape[0]
    used = sum(b - a for a, b in _W_IN_SEGMENTS)
    w = w_in_l.astype(BF16)
    cols = [w[:, a:b] for a, b in _W_IN_SEGMENTS] + [jnp.zeros((K, NPROJ - used), BF16)]
    return jnp.concatenate(cols, axis=1)


def _cols(proj, blk, n):
    return proj[..., blk * LANES:(blk + n) * LANES]


def _cache_rows(proj3, kblk, vblk, heads):
    nb, rows, _ = proj3.shape
    kv = jnp.concatenate([_cols(proj3, kblk, heads), _cols(proj3, vblk, heads)], axis=-1)
    return kv.reshape(nb, rows, 2, heads, HEAD_DIM)


def _lam_init(l):
    return 0.8 - 0.6 * math.exp(-0.3 * l)


def _prompt_trunk(x_prompt, mem_prompt, P, *, tq=256):
    B, T, D = x_prompt.shape
    depth = P["w_in"].shape[0]
    N = B * T
    tabs = _rope_tables(jnp.arange(T, dtype=jnp.int32))
    tm = min(1024, T)
    memx = mem_prompt.reshape(B * MEM_TOKENS, D)
    x = x_prompt.reshape(N, D)
    wk = min(NSA_WINDOW, T)
    outs = ((BKC, BVC, NSA_GROUPS, 0, T), (BKS, BVS, NSA_GROUPS, 0, T), (BKW, BVW, NSA_GROUPS, T - wk, wk),
            (BDK, BDV, DIFF_HEADS, 0, T), (BMK, BMV, MOBA_HEADS, 0, T))
    bufs = [_row_buffer(B, depth, n, h) for _, _, h, _, n in outs]
    mem_buf = _row_buffer(B, depth, MEM_TOKENS, MEM_HEADS)
    for l in range(depth):
        mkv = _norm_matmul(memx, P["norm_mem_m"][l], P["w_mem_kv"][l].astype(BF16), tm=min(512, B * MEM_TOKENS),
                           tn=256)
        mkv3 = mkv.reshape(B, MEM_TOKENS, 2 * MEM_HEADS * MEM_DIM)
        mem_buf = _scatter_rows(mem_buf, mkv, list(range(2 * MEM_HEADS)), l, depth=depth, seq_rows=MEM_TOKENS,
                                t_start=0, t_len=MEM_TOKENS, tm=MEM_TOKENS)
        proj = _norm_matmul(x, P["norm_mix"][l], _regroup_w_in(P["w_in"][l]), tabs, tm=tm, tn=PROJ_TILE)
        proj3 = proj.reshape(B, T, NPROJ)
        for n, (kblk, vblk, heads, t0, t_len) in enumerate(outs):
            bufs[n] = _scatter_rows(bufs[n], proj, _kv_blocks(kblk, vblk, heads), l, depth=depth, seq_rows=T,
                                    t_start=t0, t_len=t_len, tm=min(512, t_len))
        cw = (P["nsa_pe_k"][l], P["nsa_w1_k"][l], P["nsa_w2_k"][l],
              P["nsa_pe_v"][l], P["nsa_w1_v"][l], P["nsa_w2_v"][l])
        a_nsa = _nsa_prompt(proj, cw, B=B, T=T, tq=min(NSA_TQ, T))
        a_diff = _diff_prompt(proj, P["diff_lambda"][l], P["diff_subln"][l], B=B, T=T, tq=min(DIFF_TQ, T),
                              lam_init=_lam_init(l))
        a_moba = _moba_prompt(proj, B=B, T=T)
        x3 = _out_mem(a_nsa.reshape(B, T, -1), a_diff.reshape(B, T, -1), a_moba.reshape(B, T, -1), proj3,
                      x.reshape(B, T, D), P["w_out"][l].astype(BF16), P["norm_mem_x"][l].reshape(1, D),
                      P["w_mem_q"][l].astype(BF16), mkv3, P["w_mem_o"][l].astype(BF16), tm=min(256, T))
        x = x3.reshape(N, D)
    y = _rmsnorm(x, P["norm_final"], tm=min(512, N)).reshape(B, T, D)
    rows = [_as_rows(bufs[n], B, depth, t_len, heads) for n, (_, _, heads, _, t_len) in enumerate(outs)]
    return y, rows, _as_rows(mem_buf, B, depth, MEM_TOKENS, MEM_HEADS)


DEC_ROWS = 16
DEC_POS = 8


def _plane(page_ref, plane, n_planes):
    return page_ref[pl.ds(plane, PAGE_SIZE, stride=n_planes), :]


def _flat_pages(cache):
    p0, depth, rows, two, heads, dh = cache.shape
    return cache.reshape(p0 * depth * rows * two * heads, dh)


def _cmp_scan_body(pt_ref, *refs, P):
    pages = refs[:P]
    w_ref, o_ref = refs[P], refs[P + 1]
    chunks = PAGE_SIZE // NSA_CMP_STRIDE
    n_planes = 2 * NSA_GROUPS
    r = lax.broadcasted_iota(jnp.int32, (PAGE_SIZE, PAGE_SIZE), 0)
    c = lax.broadcasted_iota(jnp.int32, (PAGE_SIZE, PAGE_SIZE), 1)
    regroup = ((r % chunks) * NSA_CMP_STRIDE + r // chunks == c).astype(BF16)
    rows = chunks * P
    for kv in range(2):
        by_row = [_mm(regroup, jnp.concatenate(
            [_plane(pg, kv * NSA_GROUPS + g, n_planes) for g in range(NSA_GROUPS)], axis=1).astype(BF16))
            for pg in pages]
        acc = jnp.zeros((NSA_GROUPS * rows, 2 * LANES), F32)
        for jp in range(NSA_CMP_STRIDE // 2):
            lhs = jnp.concatenate([
                jnp.concatenate([x[j * chunks:(j + 1) * chunks, g * LANES:(g + 1) * LANES]
                                 for j in (2 * jp, 2 * jp + 1)], axis=1)
                for g in range(NSA_GROUPS) for x in by_row], axis=0).astype(BF16)
            acc = acc + _mm(lhs, w_ref[kv, jp])
        for g in range(NSA_GROUPS):
            lo = (kv * NSA_GROUPS + g) * 2 * LANES
            o_ref[:, lo:lo + 2 * LANES] = acc[g * rows:(g + 1) * rows]


def _cmp_scan(cache, page_table, w1ab, l, *, P):
    Bd, n_pages = page_table.shape
    chunks = PAGE_SIZE // NSA_CMP_STRIDE
    depth = cache.shape[1]
    view = _flat_pages(cache)
    prow = PAGE_SIZE * 2 * NSA_GROUPS
    assert n_pages % P == 0 and cache.shape[2] == PAGE_SIZE

    def page_map(b, s, pt, r):
        return (pt[b, s * P + r] * depth + l, 0)

    in_specs = [pl.BlockSpec((prow, LANES), functools.partial(page_map, r=r)) for r in range(P)]
    in_specs.append(pl.BlockSpec(w1ab.shape, lambda b, s, pt: (0, 0, 0, 0)))
    ncol = 2 * NSA_GROUPS * 2 * LANES
    return pl.pallas_call(
        functools.partial(_cmp_scan_body, P=P),
        grid_spec=pltpu.PrefetchScalarGridSpec(
            num_scalar_prefetch=1, grid=(Bd, n_pages // P), in_specs=in_specs,
            out_specs=pl.BlockSpec((None, chunks * P, ncol), lambda b, s, pt: (b, s, 0))),
        out_shape=jax.ShapeDtypeStruct((Bd, chunks * n_pages, ncol), F32),
        compiler_params=_cparams(("parallel", "arbitrary")),
        name="cmp_scan",
    )(page_table, *([view] * P), w1ab)


def _step_bias(bias, step, per_step):
    lo = step * per_step
    chunk = bias[:, (lo // LANES) * LANES:(lo // LANES + 1) * LANES]
    off = lo % LANES
    return chunk if off == 0 else pltpu.roll(chunk, LANES - off, 1)


def _nsa_dec_select_body(ab_ref, q_ref, pek_ref, w1k_ref, w2k_ref, pev_ref, w1v_ref, w2v_ref,
                         ocmp_ref, bias_ref, *, past, per_step):
    nch = past // NSA_CMP_STRIDE
    nsbp = past // NSA_SLC_LEN
    ncol = _round_up(nsbp, LANES)
    R = NSA_REP * DEC_POS
    scale = HEAD_DIM ** -0.5
    for g in range(NSA_GROUPS):
        def comp(kv, pe_ref, w1_ref, w2_ref):
            lo = (kv * NSA_GROUPS + g) * 2 * LANES
            return _compress_finish(ab_ref[:, lo:lo + LANES], ab_ref[:, lo + LANES:lo + 2 * LANES],
                                    pe_ref, w1_ref, w2_ref, nch).astype(BF16)
        kc = comp(0, pek_ref, w1k_ref, w2k_ref)
        vc = comp(1, pev_ref, w1v_ref, w2v_ref)
        s = _nt(q_ref[g].astype(BF16), kc) * scale
        col = lax.broadcasted_iota(jnp.int32, (R, nch), 1)
        qpos = past + lax.broadcasted_iota(jnp.int32, (R, nch), 0) % DEC_POS
        pc = _masked_softmax(s, (NSA_CMP_STRIDE * col + 2 * NSA_CMP_STRIDE - 1) <= qpos, 1)
        ocmp_ref[g] = _mm(pc.astype(BF16), vc)
        pg = pc[0:DEC_POS]
        for h in range(1, NSA_REP):
            pg = pg + pc[h * DEC_POS:(h + 1) * DEC_POS]
        slc = jnp.dot(pg, _slc_weights((nch, ncol), 0), precision=lax.Precision.HIGHEST,
                      preferred_element_type=F32)
        blk = lax.broadcasted_iota(jnp.int32, (DEC_POS, ncol), 1)
        forced = (blk == 0) | (blk == nsbp - 1)
        score = jnp.where(blk < nsbp, slc + jnp.where(forced, NSA_FORCE_BONUS, 0.0), -jnp.inf)
        rank = _rank_desc(score, nsbp, 1) + jnp.where(forced, 0.0, 1.0)
        b8 = jnp.where(rank < min(NSA_TOPN, nsbp + 1), 0.0, NEG)
        for st in range(nsbp // per_step):
            bias_ref[g, st] = jnp.concatenate([_step_bias(b8, st, per_step)] * NSA_REP, axis=0)


def _nsa_dec_select(ab, qn, cw, *, past, per_step):
    Bd = ab.shape[0]
    R = NSA_REP * DEC_POS
    n_steps = past // NSA_SLC_LEN // per_step

    def full(a):
        return pl.BlockSpec(a.shape, lambda b: (0,) * a.ndim)

    return pl.pallas_call(
        functools.partial(_nsa_dec_select_body, past=past, per_step=per_step),
        grid=(Bd,),
        in_specs=[pl.BlockSpec((None,) + ab.shape[1:], lambda b: (b, 0, 0)),
                  pl.BlockSpec((None, NSA_GROUPS, R, LANES), lambda b: (b, 0, 0, 0))] + [full(a) for a in cw],
        out_specs=[pl.BlockSpec((None, NSA_GROUPS, R, LANES), lambda b: (b, 0, 0, 0)),
                   pl.BlockSpec((None, NSA_GROUPS, n_steps, R, LANES), lambda b: (b, 0, 0, 0, 0))],
        out_shape=[jax.ShapeDtypeStruct((Bd, NSA_GROUPS, R, LANES), F32),
                   jax.ShapeDtypeStruct((Bd, NSA_GROUPS, n_steps, R, LANES), F32)],
        compiler_params=_cparams(("parallel",)),
        name="nsa_dec_select",
    )(ab, qn, *cw)


def _moba_scan_body(pt_ref, *refs, P):
    q_ref = refs[0]
    pages = refs[1:1 + P]
    acc_ref, m_ref, l_ref, sc_ref = refs[1 + P:]
    H, R = MOBA_HEADS, DEC_POS
    n_planes = 2 * H
    ppb = MOBA_BLOCK // PAGE_SIZE
    nblk = P // ppb
    s_id = pl.program_id(1)
    scale = HEAD_DIM ** -0.5
    lane = lax.broadcasted_iota(jnp.int32, (R, LANES), 1)

    @pl.when(s_id == 0)
    def _():
        m_ref[...] = jnp.zeros(m_ref.shape, F32)
        l_ref[...] = jnp.zeros(l_ref.shape, F32)
        sc_ref[...] = jnp.zeros(sc_ref.shape, F32)

    heads = range(H)
    qs = [q_ref[h].astype(BF16) for h in heads]
    ks = [[jnp.concatenate([_plane(pages[b * ppb + e], h, n_planes) for e in range(ppb)], axis=0)
           for b in range(nblk)] for h in heads]
    km = [jnp.concatenate([jnp.mean(ks[h][b], axis=0, keepdims=True) for b in range(nblk)]
                          + [jnp.zeros((LANES - nblk, LANES), F32)], axis=0).astype(BF16) for h in heads]
    mean_sc = [_nt(qs[h], km[h]) for h in heads]
    raw = [[_nt(qs[h], ks[h][b].astype(BF16)) * scale for b in range(nblk)] for h in heads]
    mx = [[jnp.max(raw[h][b], axis=-1, keepdims=True) for b in range(nblk)] for h in heads]
    pr = [[jnp.exp(raw[h][b] - mx[h][b]) for b in range(nblk)] for h in heads]
    sm = [[jnp.sum(pr[h][b], axis=-1, keepdims=True) for b in range(nblk)] for h in heads]
    for h in heads:
        m_t, l_t, s_t = m_ref[h], l_ref[h], sc_ref[h]
        for b in range(nblk):
            v = jnp.concatenate([_plane(pages[b * ppb + e], H + h, n_planes) for e in range(ppb)],
                                axis=0).astype(BF16)
            acc_ref[h, b] = _mm(pr[h][b].astype(BF16), v)
            here = lane == s_id * nblk + b
            m_t = jnp.where(here, mx[h][b], m_t)
            l_t = jnp.where(here, sm[h][b], l_t)
            s_t = jnp.where(here, _lane_pick(mean_sc[h], b), s_t)
        m_ref[h], l_ref[h], sc_ref[h] = m_t, l_t, s_t


def _moba_scan(q, cache, page_table, l, *, P):
    Bd, H, R, _ = q.shape
    n_pages = page_table.shape[1]
    depth = cache.shape[1]
    ppb = MOBA_BLOCK // PAGE_SIZE
    nblk = n_pages // ppb
    prow = PAGE_SIZE * 2 * H
    view = _flat_pages(cache)
    assert n_pages % P == 0 and P % ppb == 0 and nblk <= LANES and cache.shape[2] == PAGE_SIZE

    def page_map(b, s, pt, r):
        return (pt[b, s * P + r] * depth + l, 0)

    stat = pl.BlockSpec((None, H, R, LANES), lambda b, s, pt: (b, 0, 0, 0))
    stat_shape = jax.ShapeDtypeStruct((Bd, H, R, LANES), F32)
    return pl.pallas_call(
        functools.partial(_moba_scan_body, P=P),
        grid_spec=pltpu.PrefetchScalarGridSpec(
            num_scalar_prefetch=1, grid=(Bd, n_pages // P),
            in_specs=[pl.BlockSpec((None, H, R, LANES), lambda b, s, pt: (b, 0, 0, 0))]
            + [pl.BlockSpec((prow, LANES), functools.partial(page_map, r=r)) for r in range(P)],
            out_specs=[pl.BlockSpec((None, H, P // ppb, R, LANES), lambda b, s, pt: (b, 0, s, 0, 0)),
                       stat, stat, stat]),
        out_shape=[jax.ShapeDtypeStruct((Bd, H, nblk, R, LANES), F32), stat_shape, stat_shape, stat_shape],
        compiler_params=_cparams(("parallel", "arbitrary")),
        name="moba_scan",
    )(page_table, q, *([view] * P))


def _moba_combine_body(q_ref, acc_ref, m_ref, l_ref, sc_ref, kn_ref, vn_ref, o_ref, *, nblk, n_new):
    H, R = MOBA_HEADS, DEC_POS
    scale = HEAD_DIM ** -0.5
    lane = lax.broadcasted_iota(jnp.int32, (R, LANES), 1)
    rpos = lax.broadcasted_iota(jnp.int32, (R, LANES), 0) % DEC_POS
    for h in range(H):
        score = jnp.where(lane < nblk, sc_ref[h], -jnp.inf)
        sel = (_rank_desc(score, nblk, 1) < min(MOBA_TOPK, nblk)) & (lane < nblk)
        m_blk = m_ref[h]
        q = q_ref[h].astype(BF16)
        s_new = _nt(q, kn_ref[:, h * LANES:(h + 1) * LANES].astype(BF16)) * scale
        ok_new = (lane <= rpos) & (lane < n_new)
        m_new = jnp.max(jnp.where(ok_new, s_new, NEG), axis=-1, keepdims=True)
        m_all = jnp.maximum(jnp.max(jnp.where(sel, m_blk, NEG), axis=-1, keepdims=True), m_new)
        p_new = jnp.where(ok_new, jnp.exp(s_new - m_all), 0.0)
        w = jnp.where(sel, jnp.exp(m_blk - m_all), 0.0)
        den = jnp.sum(w * l_ref[h], axis=-1, keepdims=True) + jnp.sum(p_new, axis=-1, keepdims=True)
        num = _mm(p_new.astype(BF16), vn_ref[:, h * LANES:(h + 1) * LANES].astype(BF16))
        for b in range(nblk):
            num = num + w[:, b:b + 1] * acc_ref[h, b]
        o_ref[h] = num / den


def _moba_combine(q, acc, m, lsum, sc, knew, vnew, *, n_new):
    Bd, H, nblk, R, _ = acc.shape

    def b4(a):
        return pl.BlockSpec((None,) + a.shape[1:], lambda b: (b,) + (0,) * (a.ndim - 1))

    return pl.pallas_call(
        functools.partial(_moba_combine_body, nblk=nblk, n_new=n_new),
        grid=(Bd,),
        in_specs=[b4(q), b4(acc), b4(m), b4(lsum), b4(sc), b4(knew), b4(vnew)],
        out_specs=pl.BlockSpec((None, H, R, LANES), lambda b: (b, 0, 0, 0)),
        out_shape=jax.ShapeDtypeStruct((Bd, H, R, LANES), F32),
        compiler_params=_cparams(("parallel",)),
        name="moba_combine",
    )(q, acc, m, lsum, sc, knew, vnew)


def _paged_attn_body(ptM_ref, ptm_ref, *refs, P, H, R, scale, blocksize, window, n_new):
    q_ref = refs[0]
    pages = refs[1:1 + P]
    rest = refs[1 + P:]
    if blocksize is not None:
        bias_ref, rest = rest[0], rest[1:]
    kn_ref, vn_ref, o_ref, m_ref, l_ref, acc_ref = rest
    s_id = pl.program_id(1)
    nk = P * PAGE_SIZE
    n_planes = 2 * H

    @pl.when(s_id == 0)
    def _():
        _online_init(m_ref, l_ref, acc_ref)

    if blocksize is not None:
        kblk = lax.broadcasted_iota(jnp.int32, (nk, LANES), 0) // blocksize
        onehot = (kblk == lax.broadcasted_iota(jnp.int32, (nk, LANES), 1)).astype(BF16)
    if window:
        kidx = s_id * nk + lax.broadcasted_iota(jnp.int32, (R, nk), 1)
        in_window = kidx > lax.broadcasted_iota(jnp.int32, (R, nk), 0) % DEC_POS
    heads = range(H)
    scores = []
    for h in heads:
        q = q_ref[h].astype(BF16)
        k = jnp.concatenate([_plane(pg, h, n_planes) for pg in pages], axis=0).astype(BF16)
        if blocksize is not None:
            q = jnp.concatenate([q, bias_ref[h].astype(BF16)], axis=1)
            k = jnp.concatenate([k, onehot], axis=1)
        s = _nt(q, k) * scale
        scores.append(jnp.where(in_window, s, NEG) if window else s)
    m_prev = [m_ref[h] for h in heads]
    m_new = [jnp.maximum(m_prev[h], jnp.max(scores[h], axis=-1, keepdims=True)) for h in heads]
    alpha = [jnp.exp(m_prev[h] - m_new[h]) for h in heads]
    probs = [jnp.exp(scores[h] - m_new[h]) for h in heads]
    l_new = [alpha[h] * l_ref[h] + jnp.sum(probs[h], axis=-1, keepdims=True) for h in heads]
    pv = []
    for h in heads:
        v = jnp.concatenate([_plane(pg, H + h, n_planes) for pg in pages], axis=0).astype(BF16)
        pv.append(_mm(probs[h].astype(BF16), v))
    for h in heads:
        acc_ref[h] = alpha[h] * acc_ref[h] + pv[h]
        m_ref[h] = m_new[h]
        l_ref[h] = l_new[h]

    @pl.when(s_id == pl.num_programs(1) - 1)
    def _():
        lane = lax.broadcasted_iota(jnp.int32, (R, PAGE_SIZE), 1)
        rpos = lax.broadcasted_iota(jnp.int32, (R, PAGE_SIZE), 0) % DEC_POS
        for h in range(H):
            kn = kn_ref[:, h * LANES:(h + 1) * LANES].astype(BF16)
            vn = vn_ref[:, h * LANES:(h + 1) * LANES].astype(BF16)
            s = _nt(q_ref[h].astype(BF16), kn) * scale
            s = jnp.where((lane <= rpos) & (lane < n_new), s, NEG)
            _online_update(s, vn, m_ref.at[h], l_ref.at[h], acc_ref.at[h])
            o_ref[h] = acc_ref[h] / l_ref[h]


def _paged_attn(q, cache, ptM, ptm, l, knew, vnew, bias=None, *, scale, blocksize=None, window=False, P, n_new):
    Bd, H, R, _ = q.shape
    n_pages = ptM.shape[1]
    depth = cache.shape[1]
    ppe = cache.shape[2] // PAGE_SIZE
    prow = PAGE_SIZE * 2 * H
    view = _flat_pages(cache)
    assert n_pages % P == 0 and cache.shape[4] == H

    def page_map(b, s, pM, pm, r):
        return ((pM[b, s * P + r] * depth + l) * ppe + pm[b, s * P + r], 0)

    in_specs = [pl.BlockSpec((None, H, R, LANES), lambda b, s, pM, pm: (b, 0, 0, 0))]
    in_specs += [pl.BlockSpec((prow, LANES), functools.partial(page_map, r=r)) for r in range(P)]
    args = [q] + [view] * P
    if blocksize is not None:
        in_specs.append(pl.BlockSpec((None, H, None, R, LANES), lambda b, s, pM, pm: (b, 0, s, 0, 0)))
        args.append(bias)
    for a in (knew, vnew):
        in_specs.append(pl.BlockSpec((None,) + a.shape[1:], lambda b, s, pM, pm: (b, 0, 0)))
        args.append(a)
    return pl.pallas_call(
        functools.partial(_paged_attn_body, P=P, H=H, R=R, scale=scale, blocksize=blocksize, window=window,
                          n_new=n_new),
        grid_spec=pltpu.PrefetchScalarGridSpec(
            num_scalar_prefetch=2, grid=(Bd, n_pages // P), in_specs=in_specs,
            out_specs=pl.BlockSpec((None, H, R, LANES), lambda b, s, pM, pm: (b, 0, 0, 0)),
            scratch_shapes=[pltpu.VMEM((H, R, 1), F32), pltpu.VMEM((H, R, 1), F32), pltpu.VMEM((H, R, LANES), F32)]),
        out_shape=jax.ShapeDtypeStruct((Bd, H, R, LANES), F32),
        compiler_params=_cparams(("parallel", "arbitrary")),
        name="paged_attn",
    )(ptM, ptm, *args)


def _dec_finalize_body(ocmp_ref, oslc_ref, owin_ref, gate_ref, odiff_ref, lam_ref, sub_ref, omoba_ref,
                       an_ref, ad_ref, am_ref, *, lam_init):
    gt = gate_ref[0:DEC_POS, :]
    an_ref[...] = jnp.zeros(an_ref.shape, F32)
    ad_ref[...] = jnp.zeros(ad_ref.shape, F32)
    am_ref[...] = jnp.zeros(am_ref.shape, F32)
    for g in range(NSA_GROUPS):
        for h in range(NSA_REP):
            head = NSA_REP * g + h
            rows = slice(h * DEC_POS, (h + 1) * DEC_POS)
            out = (_lane_pick(gt, head) * ocmp_ref[g, rows, :]
                   + _lane_pick(gt, NSA_HEADS + head) * oslc_ref[g, rows, :]
                   + _lane_pick(gt, 2 * NSA_HEADS + head) * owin_ref[g, rows, :])
            an_ref[0:DEC_POS, head * LANES:(head + 1) * LANES] = out
    for h in range(DIFF_HEADS):
        o = odiff_ref[h]
        ad_ref[0:DEC_POS, h * LANES:(h + 1) * LANES] = _diff_finish(o[:DEC_POS], o[DEC_POS:], lam_ref, sub_ref, lam_init)
    for h in range(MOBA_HEADS):
        am_ref[0:DEC_POS, h * LANES:(h + 1) * LANES] = omoba_ref[h]


def _dec_finalize(o_cmp, o_slc, o_win, proj3, o_diff, lam, subln, o_moba, *, lam_init):
    Bd = o_cmp.shape[0]

    def b4(a):
        return pl.BlockSpec((None,) + a.shape[1:], lambda b: (b, 0, 0, 0))

    outs = [(NSA_HEADS * LANES), (DIFF_HEADS * LANES), (MOBA_HEADS * LANES)]
    return pl.pallas_call(
        functools.partial(_dec_finalize_body, lam_init=lam_init),
        grid=(Bd,),
        in_specs=[b4(o_cmp), b4(o_slc), b4(o_win),
                  pl.BlockSpec((None, DEC_ROWS, LANES), lambda b: (b, 0, BGATE)),
                  b4(o_diff), pl.BlockSpec(lam.shape, lambda b: (0, 0)), pl.BlockSpec((1, LANES), lambda b: (0, 0)),
                  b4(o_moba)],
        out_specs=[pl.BlockSpec((None, DEC_ROWS, c), lambda b: (b, 0, 0)) for c in outs],
        out_shape=[jax.ShapeDtypeStruct((Bd, DEC_ROWS, c), F32) for c in outs],
        compiler_params=_cparams(("parallel",)),
        name="dec_finalize",
    )(o_cmp, o_slc, o_win, proj3, o_diff, lam, subln.reshape(1, LANES), o_moba)


def _head_major(cols, heads):
    Bd = cols.shape[0]
    return cols[:, :DEC_POS].reshape(Bd, DEC_POS, heads, HEAD_DIM).transpose(0, 2, 1, 3)


def _new_rows(proj3, blk, heads):
    rows = _cols(proj3, blk, heads)
    return jnp.pad(rows, ((0, 0), (0, PAGE_SIZE - rows.shape[1]), (0, 0)))


def _sample_attn(proj3, caches, page_table, P, l, n_new, *, scan_pages=16, nsa_scan_pages=32):
    cache_cmp, cache_slc, cache_win, cache_diff, cache_moba = caches
    Bd = proj3.shape[0]
    n_pages = page_table.shape[1]
    past = n_pages * PAGE_SIZE
    wkeep = cache_win.shape[2]
    assert n_new <= DEC_POS and wkeep == NSA_WINDOW and past >= NSA_WINDOW
    win_pages = wkeep // PAGE_SIZE
    zeros_pt = jnp.zeros_like(page_table)
    win_major = jnp.broadcast_to(jnp.arange(Bd, dtype=jnp.int32)[:, None], (Bd, win_pages))
    win_minor = jnp.broadcast_to(jnp.arange(win_pages, dtype=jnp.int32)[None, :], (Bd, win_pages))
    half = NSA_CMP_STRIDE * HEAD_DIM

    def w1ab(w1):
        return jnp.concatenate([w1[:half].reshape(NSA_CMP_STRIDE, LANES, LANES),
                                w1[half:].reshape(NSA_CMP_STRIDE, LANES, LANES)], axis=-1)
    wab = jnp.stack([w1ab(P["nsa_w1_k"][l]), w1ab(P["nsa_w1_v"][l])]).astype(BF16)
    wab = wab.reshape(2, NSA_CMP_STRIDE // 2, 2 * LANES, 2 * LANES)
    ab = _cmp_scan(cache_cmp, page_table, wab, l, P=nsa_scan_pages)
    qn = _head_major(_cols(proj3, BQ, NSA_HEADS), NSA_HEADS).reshape(Bd, NSA_GROUPS, NSA_REP * DEC_POS, LANES)
    cw = (P["nsa_pe_k"][l], P["nsa_w1_k"][l], P["nsa_w2_k"][l],
          P["nsa_pe_v"][l], P["nsa_w1_v"][l], P["nsa_w2_v"][l])
    o_cmp, bias_slc = _nsa_dec_select(ab, qn, cw, past=past,
                                      per_step=nsa_scan_pages * PAGE_SIZE // NSA_SLC_LEN)
    sc128 = HEAD_DIM ** -0.5
    o_slc = _paged_attn(qn, cache_slc, page_table, zeros_pt, l, _new_rows(proj3, BKS, NSA_GROUPS),
                        _new_rows(proj3, BVS, NSA_GROUPS), bias_slc, scale=sc128, blocksize=NSA_SLC_LEN,
                        P=nsa_scan_pages, n_new=n_new)
    o_win = _paged_attn(qn, cache_win, win_major, win_minor, l, _new_rows(proj3, BKW, NSA_GROUPS),
                        _new_rows(proj3, BVW, NSA_GROUPS), scale=sc128, window=True, P=win_pages, n_new=n_new)
    qd = _head_major(_cols(proj3, BDQ, DIFF_HEADS), DIFF_HEADS)
    lane = jnp.arange(LANES)
    qd = jnp.concatenate([jnp.where(lane < DIFF_QK_DIM, qd, 0.0), jnp.where(lane >= DIFF_QK_DIM, qd, 0.0)], axis=2)
    o_diff = _paged_attn(qd, cache_diff, page_table, zeros_pt, l, _new_rows(proj3, BDK, DIFF_HEADS),
                         _new_rows(proj3, BDV, DIFF_HEADS), scale=DIFF_QK_DIM ** -0.5, P=scan_pages, n_new=n_new)
    qm = _head_major(_cols(proj3, BMQ, MOBA_HEADS), MOBA_HEADS)
    parts = _moba_scan(qm, cache_moba, page_table, l, P=scan_pages)
    o_moba = _moba_combine(qm, *parts, _new_rows(proj3, BMK, MOBA_HEADS), _new_rows(proj3, BMV, MOBA_HEADS),
                           n_new=n_new)
    return _dec_finalize(o_cmp, o_slc, o_win, proj3, o_diff, P["diff_lambda"][l], P["diff_subln"][l], o_moba,
                         lam_init=_lam_init(l))


def _sample_trunk(x_sample, caches, page_table, P):
    cache_cmp, cache_slc, cache_win, cache_diff, cache_moba, cache_mem = caches
    Bd, n_new, D = x_sample.shape
    depth = P["w_in"].shape[0]
    past = page_table.shape[1] * PAGE_SIZE
    pos = past + jnp.arange(DEC_ROWS, dtype=jnp.int32)
    tabs = [jnp.tile(t, (Bd, 1)) for t in _rope_tables(pos)]
    x = jnp.pad(x_sample, ((0, 0), (0, DEC_ROWS - n_new), (0, 0))).reshape(Bd * DEC_ROWS, D)
    mem4 = cache_mem.reshape(Bd, depth, MEM_TOKENS, 2 * MEM_HEADS * MEM_DIM)
    rows = [[] for _ in range(5)]
    wkeep = cache_win.shape[2]
    win_buf = _row_buffer(Bd, depth, wkeep, NSA_GROUPS)
    for l in range(depth):
        proj = _norm_matmul(x, P["norm_mix"][l], _regroup_w_in(P["w_in"][l]), tabs, tm=Bd * DEC_ROWS, tn=PROJ_TILE)
        proj3 = proj.reshape(Bd, DEC_ROWS, NPROJ)
        new3 = proj3[:, :n_new]
        rows[0].append(_cache_rows(new3, BKC, BVC, NSA_GROUPS))
        rows[1].append(_cache_rows(new3, BKS, BVS, NSA_GROUPS))
        win_buf = _shift_window(win_buf, cache_win,
                                _cache_rows(new3, BKW, BVW, NSA_GROUPS).reshape(Bd, -1, HEAD_DIM), l)
        rows[3].append(_cache_rows(new3, BDK, BDV, DIFF_HEADS))
        rows[4].append(_cache_rows(new3, BMK, BMV, MOBA_HEADS))
        a_nsa, a_diff, a_moba = _sample_attn(proj3, caches[:5], page_table, P, l, n_new)
        x3 = _out_mem(a_nsa, a_diff, a_moba, proj3, x.reshape(Bd, DEC_ROWS, D), P["w_out"][l].astype(BF16),
                      P["norm_mem_x"][l].reshape(1, D), P["w_mem_q"][l].astype(BF16), mem4[:, l],
                      P["w_mem_o"][l].astype(BF16), tm=DEC_ROWS)
        x = x3.reshape(Bd * DEC_ROWS, D)
    y = _rmsnorm(x, P["norm_final"], tm=Bd * DEC_ROWS).reshape(Bd, DEC_ROWS, D)[:, :n_new]
    stacked = [jnp.stack(r, axis=1) if r else _as_rows(win_buf, Bd, depth, wkeep, NSA_GROUPS) for r in rows]
    return y, stacked


def kernel(x_prompt, x_sample, mem_prompt, cache_nsa_cmp, cache_nsa_slc, cache_nsa_win, cache_diff, cache_moba, cache_mem, page_table, norm_mix, w_in, w_out, nsa_pe_k, nsa_pe_v, nsa_w1_k, nsa_w2_k, nsa_w1_v, nsa_w2_v, diff_lambda, diff_subln, norm_mem_x, norm_mem_m, w_mem_q, w_mem_kv, w_mem_o, norm_final):
    P = {"norm_mix": norm_mix, "w_in": w_in, "w_out": w_out, "nsa_pe_k": nsa_pe_k, "nsa_pe_v": nsa_pe_v,
         "nsa_w1_k": nsa_w1_k, "nsa_w2_k": nsa_w2_k, "nsa_w1_v": nsa_w1_v, "nsa_w2_v": nsa_w2_v,
         "diff_lambda": diff_lambda, "diff_subln": diff_subln, "norm_mem_x": norm_mem_x,
         "norm_mem_m": norm_mem_m, "w_mem_q": w_mem_q, "w_mem_kv": w_mem_kv, "w_mem_o": w_mem_o,
         "norm_final": norm_final}
    y_prompt, (p_cmp, p_slc, p_win, p_diff, p_moba), p_mem = _prompt_trunk(x_prompt, mem_prompt, P)
    caches = (cache_nsa_cmp, cache_nsa_slc, cache_nsa_win, cache_diff, cache_moba, cache_mem)
    y_sample, (s_cmp, s_slc, s_win, s_diff, s_moba) = _sample_trunk(x_sample, caches, page_table, P)
    return (y_prompt, y_sample, p_cmp, p_slc, p_win, p_diff, p_moba, p_mem, s_cmp, s_slc, s_win, s_diff, s_moba)
```

```python
import functools
import math

import jax
import jax.numpy as jnp
from jax import lax
from jax.experimental import pallas as pl
from jax.experimental.pallas import tpu as pltpu

F32 = jnp.float32
BF16 = jnp.bfloat16

D_MODEL = 2048
HEAD_DIM = 128
ROPE_THETA = 500000.0
NORM_EPS = 1e-6
PAGE_SIZE = 128

NSA_HEADS = 8
NSA_GROUPS = 2
NSA_REP = NSA_HEADS // NSA_GROUPS
NSA_CMP_STRIDE = 16
NSA_SLC_LEN = 64
NSA_TOPN = 16
NSA_WINDOW = 512
NSA_FORCE_BONUS = 1000.0
NSA_COL_GROUPS = 4
NSA_TQ = 512
DIFF_TQ = 512
PROJ_TILE = 512
DIFF_HEADS = 4
DIFF_QK_DIM = HEAD_DIM // 2
MOBA_HEADS = 4
MOBA_BLOCK = 256
MOBA_TOPK = 3
MEM_TOKENS = 256
MEM_HEADS = 4
MEM_DIM = 128

NEG = -1e30
LANES = 128
SUBLANES = 8
VMEM_LIMIT_BYTES = 56 * 1024 * 1024

BQ, BKC, BKS, BKW, BMQ, BMK = 0, 8, 10, 12, 14, 18
BDQ, BDK = 22, 26
BVC, BVS, BVW, BNZ, BDV, BDZ, BMV, BMZ = 30, 32, 34, 36, 44, 48, 52, 56
BGATE = 60
NBLK = 64
NPROJ = NBLK * LANES
_W_IN_SEGMENTS = (
    (0, 1024), (1024, 1280), (1536, 1792), (2048, 2304), (5656, 6168), (6168, 6680),
    (3608, 4120), (4120, 4632),
    (1280, 1536), (1792, 2048), (2304, 2560), (2584, 3608), (4632, 5144), (5144, 5656),
    (6680, 7192), (7192, 7704),
    (2560, 2584),
)


def _round_up(n, m):
    return -(-n // m) * m


def _nt(a, b):
    return lax.dot_general(a, b, (((1,), (1,)), ((), ())), preferred_element_type=F32)


def _mm(a, b):
    return jnp.dot(a, b, preferred_element_type=F32)


def _silu(z):
    return z * jax.nn.sigmoid(z)


def _cparams(sem):
    return pltpu.CompilerParams(dimension_semantics=sem, vmem_limit_bytes=VMEM_LIMIT_BYTES)


def _norm_matmul_body(*refs, tn, splits, rope):
    if rope:
        x_ref, g_ref, w_ref, c1, a1, b1, c2, a2, b2, o_ref, h_ref = refs
    else:
        x_ref, g_ref, w_ref, o_ref, h_ref = refs
    j = pl.program_id(1)

    @pl.when(j == 0)
    def _():
        x = x_ref[...]
        ms = jnp.mean(x * x, axis=-1, keepdims=True)
        h_ref[...] = (x * lax.rsqrt(ms + NORM_EPS) * g_ref[...]).astype(BF16)

    if not rope:
        o_ref[...] = _mm(h_ref[...], w_ref[...])
        return

    def epilogue(kind, blk):
        if kind == "rot128":
            half = HEAD_DIM // 8
            return blk * c1[...] + pltpu.roll(blk, LANES - half, 1) * a1[...] + pltpu.roll(blk, half, 1) * b1[...]
        if kind == "rot64":
            half = DIFF_QK_DIM // 8
            return blk * c2[...] + pltpu.roll(blk, LANES - half, 1) * a2[...] + pltpu.roll(blk, half, 1) * b2[...]
        if kind == "gate":
            return jax.nn.sigmoid(blk)
        return blk

    parts = 2
    pw = tn // parts
    for lo, hi, kinds in splits:
        @pl.when((j >= lo) & (j <= hi))
        def _(kinds=kinds):
            h = h_ref[...]
            accs = [_mm(h, w_ref[:, s * pw:(s + 1) * pw]) for s in range(parts)]
            for k, kind in enumerate(kinds):
                s, o = divmod(k * LANES, pw)
                o_ref[:, k * LANES:(k + 1) * LANES] = epilogue(kind, accs[s][:, o:o + LANES])


def _tile_patterns(tn):
    kinds = (["rot128"] * (BDQ - BQ) + ["rot64"] * (BVC - BDQ) + ["plain"] * (BGATE - BVC) + ["gate"]
             + ["plain"] * (NBLK - BGATE - 1))
    per = tn // LANES
    tiles = [tuple(kinds[t * per:(t + 1) * per]) for t in range(NBLK // per)]
    runs = []
    for t, pat in enumerate(tiles):
        if runs and runs[-1][2] == pat:
            runs[-1] = (runs[-1][0], t, pat)
        else:
            runs.append((t, t, pat))
    return tuple(runs)


def _norm_matmul(x, gain, w, tabs=None, *, tm, tn):
    M, K = x.shape
    Np = w.shape[1]
    rope = tabs is not None
    splits = _tile_patterns(tn) if rope else None
    in_specs = [
        pl.BlockSpec((tm, K), lambda i, j: (i, 0)),
        pl.BlockSpec((1, K), lambda i, j: (0, 0)),
        pl.BlockSpec((K, tn), lambda i, j: (0, j)),
    ]
    args = [x, gain.reshape(1, K), w]
    if rope:
        period = tabs[0].shape[0] // tm
        for t in tabs:
            in_specs.append(pl.BlockSpec((tm, LANES), lambda i, j: (i % period, 0)))
            args.append(t)
    return pl.pallas_call(
        functools.partial(_norm_matmul_body, tn=tn, splits=splits, rope=rope),
        grid=(M // tm, Np // tn),
        in_specs=in_specs,
        out_specs=pl.BlockSpec((tm, tn), lambda i, j: (i, j)),
        out_shape=jax.ShapeDtypeStruct((M, Np), F32),
        scratch_shapes=[pltpu.VMEM((tm, K), BF16)],
        compiler_params=_cparams(("parallel", "arbitrary")),
        name="norm_matmul",
    )(*args)


def _rope_tables(pos):
    posf = pos.astype(F32)[:, None]
    n = pos.shape[0]

    def one(width, reps):
        rd = width // 4
        half = rd // 2
        inv = ROPE_THETA ** (-2.0 * jnp.arange(half, dtype=F32) / rd)
        ang = posf * inv[None, :]
        c, s = jnp.cos(ang), jnp.sin(ang)
        z = jnp.zeros((n, width - rd), F32)
        zh = jnp.zeros((n, half), F32)
        C = jnp.concatenate([c, c, jnp.ones((n, width - rd), F32)], axis=1)
        A = jnp.concatenate([-s, zh, z], axis=1)
        B = jnp.concatenate([zh, s, z], axis=1)
        return [jnp.tile(t, (1, reps)) for t in (C, A, B)]

    return one(HEAD_DIM, 1) + one(DIFF_QK_DIM, 2)


def _online_init(m_ref, l_ref, acc_ref):
    m_ref[...] = jnp.full(m_ref.shape, NEG, F32)
    l_ref[...] = jnp.zeros(l_ref.shape, F32)
    acc_ref[...] = jnp.zeros(acc_ref.shape, F32)


def _online_update(s, v, m_ref, l_ref, acc_ref):
    m_prev = m_ref[...]
    m_new = jnp.maximum(m_prev, jnp.max(s, axis=-1, keepdims=True))
    alpha = jnp.exp(m_prev - m_new)
    p = jnp.exp(s - m_new)
    l_ref[...] = alpha * l_ref[...] + jnp.sum(p, axis=-1, keepdims=True)
    acc_ref[...] = alpha * acc_ref[...] + _mm(p.astype(BF16), v)
    m_ref[...] = m_new


def _online_update_t(s, vt, m_ref, l_ref, acc_ref):
    m_prev = m_ref[...]
    m_new = jnp.maximum(m_prev, jnp.max(s, axis=0, keepdims=True))
    alpha = jnp.exp(m_prev - m_new)
    p = jnp.exp(s - m_new)
    l_ref[...] = alpha * l_ref[...] + jnp.sum(p, axis=0, keepdims=True)
    acc_ref[...] = alpha * acc_ref[...] + _mm(vt, p.astype(BF16))
    m_ref[...] = m_new


def _staged_update_t(scores, vts, m_ref, l_ref, acc_ref, idxs):
    n = range(len(scores))
    m_prev = [m_ref[idxs[j]] for j in n]
    m_new = [jnp.maximum(m_prev[j], jnp.max(scores[j], axis=0, keepdims=True)) for j in n]
    alpha = [jnp.exp(m_prev[j] - m_new[j]) for j in n]
    probs = [jnp.exp(scores[j] - m_new[j]) for j in n]
    l_new = [alpha[j] * l_ref[idxs[j]] + jnp.sum(probs[j], axis=0, keepdims=True) for j in n]
    pv = [_mm(vts[j], probs[j].astype(BF16)) for j in n]
    for j in n:
        acc_ref[idxs[j]] = alpha[j] * acc_ref[idxs[j]] + pv[j]
        m_ref[idxs[j]] = m_new[j]
        l_ref[idxs[j]] = l_new[j]


def _masked_softmax(s, mask, axis):
    m = jnp.max(jnp.where(mask, s, NEG), axis=axis, keepdims=True)
    m = jnp.where(m > 0.5 * NEG, m, 0.0)
    e = jnp.where(mask, jnp.exp(s - m), 0.0)
    return e / jnp.maximum(jnp.sum(e, axis=axis, keepdims=True), 1e-30)


def _rank_desc(score, ncand, axis):
    idx = lax.broadcasted_iota(jnp.int32, score.shape, axis)
    rank = jnp.zeros(score.shape, F32)
    for c in range(ncand):
        cand = score[:, c:c + 1] if axis == 1 else score[c:c + 1, :]
        before = (cand > score) | ((cand == score) & (c < idx))
        rank = rank + before.astype(F32)
    return rank


def _lane_pick(x, idx):
    lane = lax.broadcasted_iota(jnp.int32, x.shape, 1)
    return jnp.sum(jnp.where(lane == idx, x, 0.0), axis=-1, keepdims=True)


def _compress(src_ref, pe_ref, w1_ref, w2_ref, nc):
    half = NSA_CMP_STRIDE * HEAD_DIM
    acc_a = jnp.zeros((nc, LANES), F32)
    acc_b = jnp.zeros((nc, LANES), F32)
    for j in range(NSA_CMP_STRIDE):
        rows = src_ref[pl.ds(j, nc, stride=NSA_CMP_STRIDE), :].astype(BF16)
        acc_a = acc_a + _mm(rows, w1_ref[j * LANES:(j + 1) * LANES, :].astype(BF16))
        acc_b = acc_b + _mm(rows, w1_ref[half + j * LANES:half + (j + 1) * LANES, :].astype(BF16))
    return _compress_finish(acc_a, acc_b, pe_ref, w1_ref, w2_ref, nc)


def _pe_term(pe_ref, w1_ref):
    acc = jnp.zeros((SUBLANES, LANES), F32)
    for j in range(2 * NSA_CMP_STRIDE):
        pj = jnp.broadcast_to(pe_ref[j:j + 1, :], (SUBLANES, LANES)).astype(BF16)
        acc = acc + _mm(pj, w1_ref[j * LANES:(j + 1) * LANES, :].astype(BF16))
    return acc[0:1, :]


def _compress_finish(acc_a, acc_b, pe_ref, w1_ref, w2_ref, nc):
    row = lax.broadcasted_iota(jnp.int32, (nc, LANES), 0)
    hb = jnp.where(row < nc - 1, pltpu.roll(acc_b, nc - 1, 0), 0.0)
    hid = _silu(acc_a + hb + _pe_term(pe_ref, w1_ref))
    return _mm(hid.astype(BF16), w2_ref[...].astype(BF16))


def _slc_weights(shape, cmp_axis):
    r = lax.broadcasted_iota(jnp.int32, shape, cmp_axis)
    c = lax.broadcasted_iota(jnp.int32, shape, 1 - cmp_axis)
    d = r - 4 * c
    return jnp.where((d == -1) | (d == 3), 1.0, jnp.where((d >= 0) & (d <= 2), 2.0, 0.0)).astype(F32)


def _transpose_into(dst_ref, src_ref, rows, chunk):
    for c in range(rows // chunk):
        dst_ref[:, c * chunk:(c + 1) * chunk] = src_ref[c * chunk:(c + 1) * chunk, :].T.astype(BF16)


def _nsa_prompt_body(q_ref, kc_ref, ks_ref, kw_ref, vc_ref, vs_ref, vw_ref, gate_ref,
                     pek_ref, w1k_ref, w2k_ref, pev_ref, w1v_ref, w2v_ref, o_ref,
                     kcs, vct, kaug, vst, kwb, vwt, gt_ref, m_ref, l_ref, acc_ref, *, T, tq):
    g = pl.program_id(1)
    i = pl.program_id(2)
    nc = T // NSA_CMP_STRIDE
    nsb = T // NSA_SLC_LEN
    nsbr = _round_up(nsb, SUBLANES)
    nsel = min(NSA_TOPN, nsb)
    R = NSA_REP * tq
    scale = HEAD_DIM ** -0.5

    @pl.when(i == 0)
    def _():
        kcs[...] = _compress(kc_ref, pek_ref, w1k_ref, w2k_ref, nc).astype(BF16)
        vct[...] = _compress(vc_ref, pev_ref, w1v_ref, w2v_ref, nc).T.astype(BF16)
        kaug[:, :LANES] = ks_ref[...].astype(BF16)
        rblk = lax.broadcasted_iota(jnp.int32, (T, LANES), 0) // NSA_SLC_LEN
        lane = lax.broadcasted_iota(jnp.int32, (T, LANES), 1)
        kaug[:, LANES:] = (rblk == lane).astype(BF16)
        kwb[...] = kw_ref[...].astype(BF16)
        _transpose_into(vst, vs_ref, T, tq)
        _transpose_into(vwt, vw_ref, T, tq)

    q = q_ref[...]
    qt = jnp.concatenate([q[:, h * LANES:(h + 1) * LANES].T for h in range(NSA_REP)], axis=1).astype(BF16)

    s = _mm(kcs[...], qt) * scale
    crow = lax.broadcasted_iota(jnp.int32, (nc, R), 0)
    t_col = i * tq + lax.broadcasted_iota(jnp.int32, (nc, R), 1) % tq
    pc = _masked_softmax(s, (NSA_CMP_STRIDE * crow + 2 * NSA_CMP_STRIDE - 1) <= t_col, 0)
    o_cmp = _mm(vct[...], pc.astype(BF16))
    pg = pc[:, 0:tq]
    for h in range(1, NSA_REP):
        pg = pg + pc[:, h * tq:(h + 1) * tq]

    slc = jnp.dot(_slc_weights((LANES, nc), 1), pg, precision=lax.Precision.HIGHEST,
                  preferred_element_type=F32)[:nsbr]
    blk = lax.broadcasted_iota(jnp.int32, (nsbr, tq), 0)
    tb = (i * tq + lax.broadcasted_iota(jnp.int32, (nsbr, tq), 1)) // NSA_SLC_LEN
    valid = blk <= tb
    forced = (blk == 0) | (blk == tb) | (blk == tb - 1)
    score = jnp.where(valid, slc + jnp.where(forced, NSA_FORCE_BONUS, 0.0), -jnp.inf)
    sel = (_rank_desc(score, nsb, 0) < nsel) & valid
    bias = jnp.where(sel, 0.0, NEG)
    if nsbr < LANES:
        bias = jnp.concatenate([bias, jnp.full((LANES - nsbr, tq), NEG, F32)], axis=0)
    bias = bias.astype(BF16)
    qaug = jnp.concatenate([qt, jnp.concatenate([bias] * NSA_REP, axis=1)], axis=0)

    gw = R // NSA_COL_GROUPS
    cols = [(slice(None), slice(j * gw, (j + 1) * gw)) for j in range(NSA_COL_GROUPS)]
    krow = lax.broadcasted_iota(jnp.int32, (tq, gw), 0)
    tloc = lax.broadcasted_iota(jnp.int32, (tq, gw), 1) % tq

    def attend(k_tile, vt_tile, q_all, mask):
        scores = []
        for c in cols:
            s2 = _mm(k_tile, q_all[c]) * scale
            scores.append(s2 if mask is None else jnp.where(mask, s2, NEG))
        _staged_update_t(scores, [vt_tile] * NSA_COL_GROUPS, m_ref, l_ref, acc_ref, cols)

    _online_init(m_ref, l_ref, acc_ref)

    def slc_step(kt, carry):
        off = pl.multiple_of(kt * tq, tq)
        attend(kaug[pl.ds(off, tq), :], vst[:, pl.ds(off, tq)], qaug, None)
        return carry

    lax.fori_loop(0, i, slc_step, 0)
    off_d = pl.multiple_of(i * tq, tq)
    attend(kaug[pl.ds(off_d, tq), :], vst[:, pl.ds(off_d, tq)], qaug, krow <= tloc)
    o_slc = acc_ref[...] / l_ref[...]

    _online_init(m_ref, l_ref, acc_ref)
    nw = NSA_WINDOW // tq
    for d in range(nw, -1, -1):
        @pl.when(i >= d)
        def _(d=d):
            off = pl.multiple_of((i - d) * tq, tq)
            mask = (krow > tloc) if d == nw else ((krow <= tloc) if d == 0 else None)
            attend(kwb[pl.ds(off, tq), :], vwt[:, pl.ds(off, tq)], qt, mask)
    o_win = acc_ref[...] / l_ref[...]

    gt_ref[...] = gate_ref[...].T

    def gate_row(branch):
        return jnp.concatenate(
            [gt_ref[pl.ds(branch * NSA_HEADS + NSA_REP * g + h, 1), :] for h in range(NSA_REP)], axis=1)

    out = gate_row(0) * o_cmp + gate_row(1) * o_slc + gate_row(2) * o_win
    for h in range(NSA_REP):
        o_ref[:, h * LANES:(h + 1) * LANES] = out[:, h * tq:(h + 1) * tq].T


def _nsa_prompt(proj, cw, *, B, T, tq):
    nqt = T // tq
    nc = T // NSA_CMP_STRIDE
    R = NSA_REP * tq
    assert T % NSA_SLC_LEN == 0 and NSA_WINDOW % tq == 0 and T // NSA_SLC_LEN <= LANES and nc <= LANES

    def kv_spec(blk):
        return pl.BlockSpec((T, LANES), lambda b, g, i: (b, blk + g))

    def full(a):
        return pl.BlockSpec(a.shape, lambda b, g, i: (0,) * a.ndim)

    in_specs = [
        pl.BlockSpec((tq, NSA_REP * LANES), lambda b, g, i: (b * nqt + i, g)),
        kv_spec(BKC), kv_spec(BKS), kv_spec(BKW), kv_spec(BVC), kv_spec(BVS), kv_spec(BVW),
        pl.BlockSpec((tq, LANES), lambda b, g, i: (b * nqt + i, BGATE)),
    ] + [full(a) for a in cw]
    return pl.pallas_call(
        functools.partial(_nsa_prompt_body, T=T, tq=tq),
        grid=(B, NSA_GROUPS, nqt),
        in_specs=in_specs,
        out_specs=pl.BlockSpec((tq, NSA_REP * LANES), lambda b, g, i: (b * nqt + i, g)),
        out_shape=jax.ShapeDtypeStruct((B * T, NSA_HEADS * LANES), F32),
        scratch_shapes=[
            pltpu.VMEM((nc, LANES), BF16), pltpu.VMEM((LANES, nc), BF16),
            pltpu.VMEM((T, 2 * LANES), BF16), pltpu.VMEM((LANES, T), BF16),
            pltpu.VMEM((T, LANES), BF16), pltpu.VMEM((LANES, T), BF16),
            pltpu.VMEM((LANES, tq), F32),
            pltpu.VMEM((1, R), F32), pltpu.VMEM((1, R), F32), pltpu.VMEM((LANES, R), F32),
        ],
        compiler_params=_cparams(("parallel", "parallel", "arbitrary")),
        name="nsa_prompt",
    )(proj, proj, proj, proj, proj, proj, proj, proj, *cw)


def _diff_lambda(lam_ref, lam_init):
    lp = lam_ref[...]
    return (jnp.exp(jnp.sum(lp[0:1] * lp[1:2], axis=-1, keepdims=True))
            - jnp.exp(jnp.sum(lp[2:3] * lp[3:4], axis=-1, keepdims=True)) + lam_init)


def _diff_finish(o1, o2, lam_ref, sub_ref, lam_init):
    a = o1 - _diff_lambda(lam_ref, lam_init) * o2
    ms = jnp.mean(a * a, axis=-1, keepdims=True)
    return a * lax.rsqrt(ms + NORM_EPS) * sub_ref[...] * (1.0 - lam_init)


def _head_specs(nqt, T, tq, qblk, kblk, vblk, heads):
    specs = [pl.BlockSpec((tq, LANES), functools.partial(lambda b, i, c: (b * nqt + i, c), c=qblk + h))
             for h in range(heads)]
    for blk in (kblk, vblk):
        specs += [pl.BlockSpec((T, LANES), functools.partial(lambda b, i, c: (b, c), c=blk + h))
                  for h in range(heads)]
    return specs


def _diff_prompt_body(*refs, T, tq, lam_init):
    nh = DIFF_HEADS
    q_refs, k_refs, v_refs = refs[:nh], refs[nh:2 * nh], refs[2 * nh:3 * nh]
    lam_ref, sub_ref, o_ref, kb, vt, m_ref, l_ref, acc_ref = refs[3 * nh:]
    i = pl.program_id(1)
    scale = DIFF_QK_DIM ** -0.5
    heads = range(nh)

    @pl.when(i == 0)
    def _():
        for h in heads:
            kb[h] = k_refs[h][...].astype(BF16)
            _transpose_into(vt.at[h], v_refs[h], T, tq)

    qts = []
    for h in heads:
        q = q_refs[h][...]
        lane = lax.broadcasted_iota(jnp.int32, q.shape, 1)
        qts.append(jnp.concatenate([jnp.where(lane < DIFF_QK_DIM, q, 0.0).T,
                                    jnp.where(lane >= DIFF_QK_DIM, q, 0.0).T], axis=1).astype(BF16))
    R = 2 * tq
    krow = lax.broadcasted_iota(jnp.int32, (tq, R), 0)
    tloc = lax.broadcasted_iota(jnp.int32, (tq, R), 1) % tq
    idxs = [(h,) for h in heads]
    _online_init(m_ref, l_ref, acc_ref)

    def attend(off, mask):
        scores = []
        for h in heads:
            s = _mm(kb[h, pl.ds(off, tq), :], qts[h]) * scale
            scores.append(s if mask is None else jnp.where(mask, s, NEG))
        _staged_update_t(scores, [vt[h, :, pl.ds(off, tq)] for h in heads], m_ref, l_ref, acc_ref, idxs)

    def step(kt, carry):
        attend(pl.multiple_of(kt * tq, tq), None)
        return carry

    lax.fori_loop(0, i, step, 0)
    attend(pl.multiple_of(i * tq, tq), krow <= tloc)
    for h in heads:
        o = acc_ref[h] / l_ref[h]
        o_ref[:, h * LANES:(h + 1) * LANES] = _diff_finish(o[:, :tq].T, o[:, tq:].T, lam_ref, sub_ref, lam_init)


def _diff_prompt(proj, lam, subln, *, B, T, tq, lam_init):
    nqt = T // tq
    nh = DIFF_HEADS
    return pl.pallas_call(
        functools.partial(_diff_prompt_body, T=T, tq=tq, lam_init=lam_init),
        grid=(B, nqt),
        in_specs=_head_specs(nqt, T, tq, BDQ, BDK, BDV, nh) + [
            pl.BlockSpec(lam.shape, lambda b, i: (0, 0)),
            pl.BlockSpec((1, LANES), lambda b, i: (0, 0)),
        ],
        out_specs=pl.BlockSpec((tq, nh * LANES), lambda b, i: (b * nqt + i, 0)),
        out_shape=jax.ShapeDtypeStruct((B * T, nh * LANES), F32),
        scratch_shapes=[
            pltpu.VMEM((nh, T, LANES), BF16), pltpu.VMEM((nh, LANES, T), BF16),
            pltpu.VMEM((nh, 1, 2 * tq), F32), pltpu.VMEM((nh, 1, 2 * tq), F32),
            pltpu.VMEM((nh, LANES, 2 * tq), F32),
        ],
        compiler_params=_cparams(("parallel", "arbitrary")),
        name="diff_prompt",
    )(*([proj] * (3 * nh)), lam, subln.reshape(1, LANES))


def _moba_prompt_body(*refs, T):
    nh = MOBA_HEADS
    q_refs, k_refs, v_refs = refs[:nh], refs[nh:2 * nh], refs[2 * nh:3 * nh]
    o_ref, kb, vt, km, bias_ref, m_ref, l_ref, acc_ref = refs[3 * nh:]
    i = pl.program_id(1)
    tq = MOBA_BLOCK
    nb = T // MOBA_BLOCK
    nbr = _round_up(nb, SUBLANES)
    scale = HEAD_DIM ** -0.5
    heads = range(nh)

    @pl.when(i == 0)
    def _():
        km[...] = jnp.zeros(km.shape, BF16)
        for h in heads:
            kb[h] = k_refs[h][...].astype(BF16)
            _transpose_into(vt.at[h], v_refs[h], T, tq)
            for j in range(nb):
                km[h, j:j + 1, :] = jnp.mean(k_refs[h][j * tq:(j + 1) * tq, :], axis=0, keepdims=True).astype(BF16)

    qts = [q_refs[h][...].T.astype(BF16) for h in heads]
    blk = lax.broadcasted_iota(jnp.int32, (nbr, tq), 0)
    past = blk < i
    for h in heads:
        score = jnp.where(past, _mm(km[h], qts[h])[:nbr], -jnp.inf)
        sel = (_rank_desc(score, nb, 0) < min(MOBA_TOPK, nb)) & past
        bias_ref[h] = jnp.where(sel, 0.0, NEG)
    krow = lax.broadcasted_iota(jnp.int32, (tq, tq), 0)
    tloc = lax.broadcasted_iota(jnp.int32, (tq, tq), 1)
    idxs = [(h,) for h in heads]
    _online_init(m_ref, l_ref, acc_ref)

    def step(kt, carry):
        off = pl.multiple_of(kt * tq, tq)
        scores = [_mm(kb[h, pl.ds(off, tq), :], qts[h]) * scale + bias_ref[h, pl.ds(kt, 1), :] for h in heads]
        _staged_update_t(scores, [vt[h, :, pl.ds(off, tq)] for h in heads], m_ref, l_ref, acc_ref, idxs)
        return carry

    lax.fori_loop(0, i, step, 0)
    off = pl.multiple_of(i * tq, tq)
    scores = [jnp.where(krow <= tloc, _mm(kb[h, pl.ds(off, tq), :], qts[h]) * scale, NEG) for h in heads]
    _staged_update_t(scores, [vt[h, :, pl.ds(off, tq)] for h in heads], m_ref, l_ref, acc_ref, idxs)
    for h in heads:
        o_ref[:, h * LANES:(h + 1) * LANES] = (acc_ref[h] / l_ref[h]).T


def _moba_prompt(proj, *, B, T):
    tq = MOBA_BLOCK
    assert T % tq == 0 and T // tq <= LANES
    nqt = T // tq
    nh = MOBA_HEADS
    nbr = _round_up(T // MOBA_BLOCK, SUBLANES)
    return pl.pallas_call(
        functools.partial(_moba_prompt_body, T=T),
        grid=(B, nqt),
        in_specs=_head_specs(nqt, T, tq, BMQ, BMK, BMV, nh),
        out_specs=pl.BlockSpec((tq, nh * LANES), lambda b, i: (b * nqt + i, 0)),
        out_shape=jax.ShapeDtypeStruct((B * T, nh * LANES), F32),
        scratch_shapes=[
            pltpu.VMEM((nh, T, LANES), BF16), pltpu.VMEM((nh, LANES, T), BF16), pltpu.VMEM((nh, LANES, LANES), BF16),
            pltpu.VMEM((nh, nbr, tq), F32),
            pltpu.VMEM((nh, 1, tq), F32), pltpu.VMEM((nh, 1, tq), F32), pltpu.VMEM((nh, LANES, tq), F32),
        ],
        compiler_params=_cparams(("parallel", "arbitrary")),
        name="moba_prompt",
    )(*([proj] * (3 * nh)))


def _out_mem_body(an_ref, ad_ref, am_ref, nz0_ref, nz1_ref, dz_ref, mz_ref, x_ref, wout_ref, gmem_ref,
                  wq_ref, mkv_ref, wo_ref, o_ref):
    half = NSA_HEADS * LANES // 2
    an = an_ref[...]
    mixed = jnp.concatenate([
        an[:, :half] * _silu(nz0_ref[...]), an[:, half:] * _silu(nz1_ref[...]),
        ad_ref[...] * _silu(dz_ref[...]), am_ref[...] * _silu(mz_ref[...])], axis=1).astype(BF16)
    x1 = x_ref[...] + _mm(mixed, wout_ref[...])
    ms = jnp.mean(x1 * x1, axis=-1, keepdims=True)
    h2 = (x1 * lax.rsqrt(ms + NORM_EPS) * gmem_ref[...]).astype(BF16)
    q = _mm(h2, wq_ref[...])
    mkv = mkv_ref[...]
    scale = MEM_DIM ** -0.5
    outs = []
    for hh in range(MEM_HEADS):
        qh = q[:, hh * LANES:(hh + 1) * LANES].astype(BF16)
        kh = mkv[:, hh * LANES:(hh + 1) * LANES].astype(BF16)
        vh = mkv[:, (MEM_HEADS + hh) * LANES:(MEM_HEADS + hh + 1) * LANES].astype(BF16)
        s = _nt(qh, kh) * scale
        e = jnp.exp(s - jnp.max(s, axis=-1, keepdims=True))
        p = e / jnp.sum(e, axis=-1, keepdims=True)
        outs.append(_mm(p.astype(BF16), vh))
    oc = jnp.concatenate(outs, axis=1).astype(BF16)
    o_ref[...] = x1 + _mm(oc, wo_ref[...])


def _out_mem(a_nsa, a_diff, a_moba, proj3, x3, wout, gmem, wq, mkv, wo, *, tm):
    nb, rows, D = x3.shape
    grid = (nb, rows // tm)

    def row(cols, cb):
        return pl.BlockSpec((None, tm, cols), lambda b, i: (b, i, cb))

    def const(a):
        return pl.BlockSpec(a.shape, lambda b, i: (0,) * a.ndim, pipeline_mode=pl.Buffered(1))

    zc = 4 * LANES
    return pl.pallas_call(
        _out_mem_body,
        grid=grid,
        in_specs=[
            row(NSA_HEADS * LANES, 0), row(DIFF_HEADS * LANES, 0), row(MOBA_HEADS * LANES, 0),
            row(zc, BNZ * LANES // zc), row(zc, BNZ * LANES // zc + 1), row(zc, BDZ * LANES // zc),
            row(zc, BMZ * LANES // zc),
            row(D, 0), const(wout), const(gmem), const(wq),
            pl.BlockSpec((None, MEM_TOKENS, 2 * MEM_HEADS * MEM_DIM), lambda b, i: (b, 0, 0)),
            const(wo),
        ],
        out_specs=row(D, 0),
        out_shape=jax.ShapeDtypeStruct(x3.shape, F32),
        compiler_params=_cparams(("parallel", "parallel")),
        name="out_mem",
    )(a_nsa, a_diff, a_moba, proj3, proj3, proj3, proj3, x3, wout, gmem, wq, mkv, wo)


def _rmsnorm_body(x_ref, g_ref, o_ref):
    x = x_ref[...]
    ms = jnp.mean(x * x, axis=-1, keepdims=True)
    o_ref[...] = x * lax.rsqrt(ms + NORM_EPS) * g_ref[...]


def _rmsnorm(x, gain, *, tm):
    M, D = x.shape
    return pl.pallas_call(
        _rmsnorm_body,
        grid=(M // tm,),
        in_specs=[pl.BlockSpec((tm, D), lambda i: (i, 0)), pl.BlockSpec((1, D), lambda i: (0, 0))],
        out_specs=pl.BlockSpec((tm, D), lambda i: (i, 0)),
        out_shape=jax.ShapeDtypeStruct((M, D), F32),
        compiler_params=_cparams(("parallel",)),
        name="final_norm",
    )(x, gain.reshape(1, D))


def _scatter_rows_body(*refs, widths):
    slabs, o_ref = refs[:len(widths)], refs[len(widths) + 1]
    n_planes = sum(widths)
    rows = slabs[0].shape[0]
    p = 0
    for slab, w in zip(slabs, widths):
        for j in range(w):
            o_ref[pl.ds(p, rows, stride=n_planes), :] = slab[:, j * LANES:(j + 1) * LANES]
            p += 1


def _column_slabs(blocks):
    slabs = []
    i = 0
    while i < len(blocks):
        run = 1
        while i + run < len(blocks) and blocks[i + run] == blocks[i] + run:
            run += 1
        w = 1
        while 2 * w <= run and blocks[i] % (2 * w) == 0:
            w *= 2
        slabs.append((blocks[i], w))
        i += w
    return slabs


def _scatter_rows(buf, src, blocks, l, *, depth, seq_rows, t_start, t_len, tm):
    n_planes = len(blocks)
    slabs = _column_slabs(blocks)
    nb = src.shape[0] // seq_rows
    tiles = t_len // tm
    assert t_len % tm == 0 and t_start % tm == 0 and seq_rows % tm == 0

    def src_map(b, i, c):
        return (b * (seq_rows // tm) + t_start // tm + i, c)

    in_specs = [pl.BlockSpec((tm, w * LANES), functools.partial(src_map, c=first // w)) for first, w in slabs]
    in_specs.append(pl.BlockSpec(memory_space=pl.ANY))
    return pl.pallas_call(
        functools.partial(_scatter_rows_body, widths=tuple(w for _, w in slabs)),
        grid=(nb, tiles),
        in_specs=in_specs,
        out_specs=pl.BlockSpec((tm * n_planes, LANES), lambda b, i: ((b * depth + l) * tiles + i, 0)),
        out_shape=jax.ShapeDtypeStruct(buf.shape, F32),
        input_output_aliases={len(slabs): 0},
        compiler_params=_cparams(("parallel", "parallel")),
        name="scatter_rows",
    )(*([src] * len(slabs)), buf)


def _kv_blocks(kblk, vblk, heads):
    return [kblk + h for h in range(heads)] + [vblk + h for h in range(heads)]


def _row_buffer(entries, depth, rows, heads):
    return jnp.zeros((entries * depth * rows * 2 * heads, HEAD_DIM), F32)


def _as_rows(buf, entries, depth, rows, heads):
    return buf.reshape(entries, depth, rows, 2, heads, HEAD_DIM)


def _shift_window_body(old_ref, new_ref, buf_ref, o_ref, *, n_shift):
    keep = old_ref.shape[0] - n_shift
    o_ref[0:keep, :] = old_ref[n_shift:, :]
    o_ref[keep:, :] = new_ref[...]


def _shift_window(buf, cache_win, new_flat, l):
    Bd, depth, wkeep = cache_win.shape[:3]
    n_planes = 2 * NSA_GROUPS
    n_shift = new_flat.shape[1]
    old = cache_win.reshape(Bd * depth * wkeep * n_planes, HEAD_DIM)
    wrows = wkeep * n_planes
    return pl.pallas_call(
        functools.partial(_shift_window_body, n_shift=n_shift),
        grid=(Bd,),
        in_specs=[pl.BlockSpec((wrows, LANES), lambda b: (b * depth + l, 0)),
                  pl.BlockSpec((None, n_shift, LANES), lambda b: (b, 0, 0)),
                  pl.BlockSpec(memory_space=pl.ANY)],
        out_specs=pl.BlockSpec((wrows, LANES), lambda b: (b * depth + l, 0)),
        out_shape=jax.ShapeDtypeStruct(buf.shape, F32),
        input_output_aliases={2: 0},
        compiler_params=_cparams(("parallel",)),
        name="shift_window",
    )(old, new_flat, buf)


def _regroup_w_in(w_in_l):
    K = w_in_l.shape[0]
    used = sum(b - a for a, b in _W_IN_SEGMENTS)
    w = w_in_l.astype(BF16)
    cols = [w[:, a:b] for a, b in _W_IN_SEGMENTS] + [jnp.zeros((K, NPROJ - used), BF16)]
    return jnp.concatenate(cols, axis=1)


def _cols(proj, blk, n):
    return proj[..., blk * LANES:(blk + n) * LANES]


def _cache_rows(proj3, kblk, vblk, heads):
    nb, rows, _ = proj3.shape
    kv = jnp.concatenate([_cols(proj3, kblk, heads), _cols(proj3, vblk, heads)], axis=-1)
    return kv.reshape(nb, rows, 2, heads, HEAD_DIM)


def _lam_init(l):
    return 0.8 - 0.6 * math.exp(-0.3 * l)


def _prompt_trunk(x_prompt, mem_prompt, P, *, tq=256):
    B, T, D = x_prompt.shape
    depth = P["w_in"].shape[0]
    N = B * T
    tabs = _rope_tables(jnp.arange(T, dtype=jnp.int32))
    tm = min(1024, T)
    memx = mem_prompt.reshape(B * MEM_TOKENS, D)
    x = x_prompt.reshape(N, D)
    wk = min(NSA_WINDOW, T)
    outs = ((BKC, BVC, NSA_GROUPS, 0, T), (BKS, BVS, NSA_GROUPS, 0, T), (BKW, BVW, NSA_GROUPS, T - wk, wk),
            (BDK, BDV, DIFF_HEADS, 0, T), (BMK, BMV, MOBA_HEADS, 0, T))
    bufs = [_row_buffer(B, depth, n, h) for _, _, h, _, n in outs]
    mem_buf = _row_buffer(B, depth, MEM_TOKENS, MEM_HEADS)
    for l in range(depth):
        mkv = _norm_matmul(memx, P["norm_mem_m"][l], P["w_mem_kv"][l].astype(BF16), tm=min(512, B * MEM_TOKENS),
                           tn=256)
        mkv3 = mkv.reshape(B, MEM_TOKENS, 2 * MEM_HEADS * MEM_DIM)
        mem_buf = _scatter_rows(mem_buf, mkv, list(range(2 * MEM_HEADS)), l, depth=depth, seq_rows=MEM_TOKENS,
                                t_start=0, t_len=MEM_TOKENS, tm=MEM_TOKENS)
        proj = _norm_matmul(x, P["norm_mix"][l], _regroup_w_in(P["w_in"][l]), tabs, tm=tm, tn=PROJ_TILE)
        proj3 = proj.reshape(B, T, NPROJ)
        for n, (kblk, vblk, heads, t0, t_len) in enumerate(outs):
            bufs[n] = _scatter_rows(bufs[n], proj, _kv_blocks(kblk, vblk, heads), l, depth=depth, seq_rows=T,
                                    t_start=t0, t_len=t_len, tm=min(512, t_len))
        cw = (P["nsa_pe_k"][l], P["nsa_w1_k"][l], P["nsa_w2_k"][l],
              P["nsa_pe_v"][l], P["nsa_w1_v"][l], P["nsa_w2_v"][l])
        a_nsa = _nsa_prompt(proj, cw, B=B, T=T, tq=min(NSA_TQ, T))
        a_diff = _diff_prompt(proj, P["diff_lambda"][l], P["diff_subln"][l], B=B, T=T, tq=min(DIFF_TQ, T),
                              lam_init=_lam_init(l))
        a_moba = _moba_prompt(proj, B=B, T=T)
        x3 = _out_mem(a_nsa.reshape(B, T, -1), a_diff.reshape(B, T, -1), a_moba.reshape(B, T, -1), proj3,
                      x.reshape(B, T, D), P["w_out"][l].astype(BF16), P["norm_mem_x"][l].reshape(1, D),
                      P["w_mem_q"][l].astype(BF16), mkv3, P["w_mem_o"][l].astype(BF16), tm=min(256, T))
        x = x3.reshape(N, D)
    y = _rmsnorm(x, P["norm_final"], tm=min(512, N)).reshape(B, T, D)
    rows = [_as_rows(bufs[n], B, depth, t_len, heads) for n, (_, _, heads, _, t_len) in enumerate(outs)]
    return y, rows, _as_rows(mem_buf, B, depth, MEM_TOKENS, MEM_HEADS)


DEC_ROWS = 16
DEC_POS = 8


def _plane(page_ref, plane, n_planes):
    return page_ref[pl.ds(plane, PAGE_SIZE, stride=n_planes), :]


def _flat_pages(cache):
    p0, depth, rows, two, heads, dh = cache.shape
    return cache.reshape(p0 * depth * rows * two * heads, dh)


def _cmp_scan_body(pt_ref, *refs, P):
    pages = refs[:P]
    w_ref, o_ref = refs[P], refs[P + 1]
    chunks = PAGE_SIZE // NSA_CMP_STRIDE
    n_planes = 2 * NSA_GROUPS
    r = lax.broadcasted_iota(jnp.int32, (PAGE_SIZE, PAGE_SIZE), 0)
    c = lax.broadcasted_iota(jnp.int32, (PAGE_SIZE, PAGE_SIZE), 1)
    regroup = ((r % chunks) * NSA_CMP_STRIDE + r // chunks == c).astype(BF16)
    rows = chunks * P
    for kv in range(2):
        by_row = [_mm(regroup, jnp.concatenate(
            [_plane(pg, kv * NSA_GROUPS + g, n_planes) for g in range(NSA_GROUPS)], axis=1).astype(BF16))
            for pg in pages]
        acc = jnp.zeros((NSA_GROUPS * rows, 2 * LANES), F32)
        for jp in range(NSA_CMP_STRIDE // 2):
            lhs = jnp.concatenate([
                jnp.concatenate([x[j * chunks:(j + 1) * chunks, g * LANES:(g + 1) * LANES]
                                 for j in (2 * jp, 2 * jp + 1)], axis=1)
                for g in range(NSA_GROUPS) for x in by_row], axis=0).astype(BF16)
            acc = acc + _mm(lhs, w_ref[kv, jp])
        for g in range(NSA_GROUPS):
            lo = (kv * NSA_GROUPS + g) * 2 * LANES
            o_ref[:, lo:lo + 2 * LANES] = acc[g * rows:(g + 1) * rows]


def _cmp_scan(cache, page_table, w1ab, l, *, P):
    Bd, n_pages = page_table.shape
    chunks = PAGE_SIZE // NSA_CMP_STRIDE
    depth = cache.shape[1]
    view = _flat_pages(cache)
    prow = PAGE_SIZE * 2 * NSA_GROUPS
    assert n_pages % P == 0 and cache.shape[2] == PAGE_SIZE

    def page_map(b, s, pt, r):
        return (pt[b, s * P + r] * depth + l, 0)

    in_specs = [pl.BlockSpec((prow, LANES), functools.partial(page_map, r=r)) for r in range(P)]
    in_specs.append(pl.BlockSpec(w1ab.shape, lambda b, s, pt: (0, 0, 0, 0)))
    ncol = 2 * NSA_GROUPS * 2 * LANES
    return pl.pallas_call(
        functools.partial(_cmp_scan_body, P=P),
        grid_spec=pltpu.PrefetchScalarGridSpec(
            num_scalar_prefetch=1, grid=(Bd, n_pages // P), in_specs=in_specs,
            out_specs=pl.BlockSpec((None, chunks * P, ncol), lambda b, s, pt: (b, s, 0))),
        out_shape=jax.ShapeDtypeStruct((Bd, chunks * n_pages, ncol), F32),
        compiler_params=_cparams(("parallel", "arbitrary")),
        name="cmp_scan",
    )(page_table, *([view] * P), w1ab)


def _step_bias(bias, step, per_step):
    lo = step * per_step
    chunk = bias[:, (lo // LANES) * LANES:(lo // LANES + 1) * LANES]
    off = lo % LANES
    return chunk if off == 0 else pltpu.roll(chunk, LANES - off, 1)


def _nsa_dec_select_body(ab_ref, q_ref, pek_ref, w1k_ref, w2k_ref, pev_ref, w1v_ref, w2v_ref,
                         ocmp_ref, bias_ref, *, past, per_step):
    nch = past // NSA_CMP_STRIDE
    nsbp = past // NSA_SLC_LEN
    ncol = _round_up(nsbp, LANES)
    R = NSA_REP * DEC_POS
    scale = HEAD_DIM ** -0.5
    for g in range(NSA_GROUPS):
        def comp(kv, pe_ref, w1_ref, w2_ref):
            lo = (kv * NSA_GROUPS + g) * 2 * LANES
            return _compress_finish(ab_ref[:, lo:lo + LANES], ab_ref[:, lo + LANES:lo + 2 * LANES],
                                    pe_ref, w1_ref, w2_ref, nch).astype(BF16)
        kc = comp(0, pek_ref, w1k_ref, w2k_ref)
        vc = comp(1, pev_ref, w1v_ref, w2v_ref)
        s = _nt(q_ref[g].astype(BF16), kc) * scale
        col = lax.broadcasted_iota(jnp.int32, (R, nch), 1)
        qpos = past + lax.broadcasted_iota(jnp.int32, (R, nch), 0) % DEC_POS
        pc = _masked_softmax(s, (NSA_CMP_STRIDE * col + 2 * NSA_CMP_STRIDE - 1) <= qpos, 1)
        ocmp_ref[g] = _mm(pc.astype(BF16), vc)
        pg = pc[0:DEC_POS]
        for h in range(1, NSA_REP):
            pg = pg + pc[h * DEC_POS:(h + 1) * DEC_POS]
        slc = jnp.dot(pg, _slc_weights((nch, ncol), 0), precision=lax.Precision.HIGHEST,
                      preferred_element_type=F32)
        blk = lax.broadcasted_iota(jnp.int32, (DEC_POS, ncol), 1)
        forced = (blk == 0) | (blk == nsbp - 1)
        score = jnp.where(blk < nsbp, slc + jnp.where(forced, NSA_FORCE_BONUS, 0.0), -jnp.inf)
        rank = _rank_desc(score, nsbp, 1) + jnp.where(forced, 0.0, 1.0)
        b8 = jnp.where(rank < min(NSA_TOPN, nsbp + 1), 0.0, NEG)
        for st in range(nsbp // per_step):
            bias_ref[g, st] = jnp.concatenate([_step_bias(b8, st, per_step)] * NSA_REP, axis=0)


def _nsa_dec_select(ab, qn, cw, *, past, per_step):
    Bd = ab.shape[0]
    R = NSA_REP * DEC_POS
    n_steps = past // NSA_SLC_LEN // per_step

    def full(a):
        return pl.BlockSpec(a.shape, lambda b: (0,) * a.ndim)

    return pl.pallas_call(
        functools.partial(_nsa_dec_select_body, past=past, per_step=per_step),
        grid=(Bd,),
        in_specs=[pl.BlockSpec((None,) + ab.shape[1:], lambda b: (b, 0, 0)),
                  pl.BlockSpec((None, NSA_GROUPS, R, LANES), lambda b: (b, 0, 0, 0))] + [full(a) for a in cw],
        out_specs=[pl.BlockSpec((None, NSA_GROUPS, R, LANES), lambda b: (b, 0, 0, 0)),
                   pl.BlockSpec((None, NSA_GROUPS, n_steps, R, LANES), lambda b: (b, 0, 0, 0, 0))],
        out_shape=[jax.ShapeDtypeStruct((Bd, NSA_GROUPS, R, LANES), F32),
                   jax.ShapeDtypeStruct((Bd, NSA_GROUPS, n_steps, R, LANES), F32)],
        compiler_params=_cparams(("parallel",)),
        name="nsa_dec_select",
    )(ab, qn, *cw)


def _moba_scan_body(pt_ref, *refs, P):
    q_ref = refs[0]
    pages = refs[1:1 + P]
    acc_ref, m_ref, l_ref, sc_ref = refs[1 + P:]
    H, R = MOBA_HEADS, DEC_POS
    n_planes = 2 * H
    ppb = MOBA_BLOCK // PAGE_SIZE
    nblk = P // ppb
    s_id = pl.program_id(1)
    scale = HEAD_DIM ** -0.5
    lane = lax.broadcasted_iota(jnp.int32, (R, LANES), 1)

    @pl.when(s_id == 0)
    def _():
        m_ref[...] = jnp.zeros(m_ref.shape, F32)
        l_ref[...] = jnp.zeros(l_ref.shape, F32)
        sc_ref[...] = jnp.zeros(sc_ref.shape, F32)

    heads = range(H)
    qs = [q_ref[h].astype(BF16) for h in heads]
    ks = [[jnp.concatenate([_plane(pages[b * ppb + e], h, n_planes) for e in range(ppb)], axis=0)
           for b in range(nblk)] for h in heads]
    km = [jnp.concatenate([jnp.mean(ks[h][b], axis=0, keepdims=True) for b in range(nblk)]
                          + [jnp.zeros((LANES - nblk, LANES), F32)], axis=0).astype(BF16) for h in heads]
    mean_sc = [_nt(qs[h], km[h]) for h in heads]
    raw = [[_nt(qs[h], ks[h][b].astype(BF16)) * scale for b in range(nblk)] for h in heads]
    mx = [[jnp.max(raw[h][b], axis=-1, keepdims=True) for b in range(nblk)] for h in heads]
    pr = [[jnp.exp(raw[h][b] - mx[h][b]) for b in range(nblk)] for h in heads]
    sm = [[jnp.sum(pr[h][b], axis=-1, keepdims=True) for b in range(nblk)] for h in heads]
    for h in heads:
        m_t, l_t, s_t = m_ref[h], l_ref[h], sc_ref[h]
        for b in range(nblk):
            v = jnp.concatenate([_plane(pages[b * ppb + e], H + h, n_planes) for e in range(ppb)],
                                axis=0).astype(BF16)
            acc_ref[h, b] = _mm(pr[h][b].astype(BF16), v)
            here = lane == s_id * nblk + b
            m_t = jnp.where(here, mx[h][b], m_t)
            l_t = jnp.where(here, sm[h][b], l_t)
            s_t = jnp.where(here, _lane_pick(mean_sc[h], b), s_t)
        m_ref[h], l_ref[h], sc_ref[h] = m_t, l_t, s_t


def _moba_scan(q, cache, page_table, l, *, P):
    Bd, H, R, _ = q.shape
    n_pages = page_table.shape[1]
    depth = cache.shape[1]
    ppb = MOBA_BLOCK // PAGE_SIZE
    nblk = n_pages // ppb
    prow = PAGE_SIZE * 2 * H
    view = _flat_pages(cache)
    assert n_pages % P == 0 and P % ppb == 0 and nblk <= LANES and cache.shape[2] == PAGE_SIZE

    def page_map(b, s, pt, r):
        return (pt[b, s * P + r] * depth + l, 0)

    stat = pl.BlockSpec((None, H, R, LANES), lambda b, s, pt: (b, 0, 0, 0))
    stat_shape = jax.ShapeDtypeStruct((Bd, H, R, LANES), F32)
    return pl.pallas_call(
        functools.partial(_moba_scan_body, P=P),
        grid_spec=pltpu.PrefetchScalarGridSpec(
            num_scalar_prefetch=1, grid=(Bd, n_pages // P),
            in_specs=[pl.BlockSpec((None, H, R, LANES), lambda b, s, pt: (b, 0, 0, 0))]
            + [pl.BlockSpec((prow, LANES), functools.partial(page_map, r=r)) for r in range(P)],
            out_specs=[pl.BlockSpec((None, H, P // ppb, R, LANES), lambda b, s, pt: (b, 0, s, 0, 0)),
                       stat, stat, stat]),
        out_shape=[jax.ShapeDtypeStruct((Bd, H, nblk, R, LANES), F32), stat_shape, stat_shape, stat_shape],
        compiler_params=_cparams(("parallel", "arbitrary")),
        name="moba_scan",
    )(page_table, q, *([view] * P))


def _moba_combine_body(q_ref, acc_ref, m_ref, l_ref, sc_ref, kn_ref, vn_ref, o_ref, *, nblk, n_new):
    H, R = MOBA_HEADS, DEC_POS
    scale = HEAD_DIM ** -0.5
    lane = lax.broadcasted_iota(jnp.int32, (R, LANES), 1)
    rpos = lax.broadcasted_iota(jnp.int32, (R, LANES), 0) % DEC_POS
    for h in range(H):
        score = jnp.where(lane < nblk, sc_ref[h], -jnp.inf)
        sel = (_rank_desc(score, nblk, 1) < min(MOBA_TOPK, nblk)) & (lane < nblk)
        m_blk = m_ref[h]
        q = q_ref[h].astype(BF16)
        s_new = _nt(q, kn_ref[:, h * LANES:(h + 1) * LANES].astype(BF16)) * scale
        ok_new = (lane <= rpos) & (lane < n_new)
        m_new = jnp.max(jnp.where(ok_new, s_new, NEG), axis=-1, keepdims=True)
        m_all = jnp.maximum(jnp.max(jnp.where(sel, m_blk, NEG), axis=-1, keepdims=True), m_new)
        p_new = jnp.where(ok_new, jnp.exp(s_new - m_all), 0.0)
        w = jnp.where(sel, jnp.exp(m_blk - m_all), 0.0)
        den = jnp.sum(w * l_ref[h], axis=-1, keepdims=True) + jnp.sum(p_new, axis=-1, keepdims=True)
        num = _mm(p_new.astype(BF16), vn_ref[:, h * LANES:(h + 1) * LANES].astype(BF16))
        for b in range(nblk):
            num = num + w[:, b:b + 1] * acc_ref[h, b]
        o_ref[h] = num / den


def _moba_combine(q, acc, m, lsum, sc, knew, vnew, *, n_new):
    Bd, H, nblk, R, _ = acc.shape

    def b4(a):
        return pl.BlockSpec((None,) + a.shape[1:], lambda b: (b,) + (0,) * (a.ndim - 1))

    return pl.pallas_call(
        functools.partial(_moba_combine_body, nblk=nblk, n_new=n_new),
        grid=(Bd,),
        in_specs=[b4(q), b4(acc), b4(m), b4(lsum), b4(sc), b4(knew), b4(vnew)],
        out_specs=pl.BlockSpec((None, H, R, LANES), lambda b: (b, 0, 0, 0)),
        out_shape=jax.ShapeDtypeStruct((Bd, H, R, LANES), F32),
        compiler_params=_cparams(("parallel",)),
        name="moba_combine",
    )(q, acc, m, lsum, sc, knew, vnew)


def _paged_attn_body(ptM_ref, ptm_ref, *refs, P, H, R, scale, blocksize, window, n_new):
    q_ref = refs[0]
    pages = refs[1:1 + P]
    rest = refs[1 + P:]
    if blocksize is not None:
        bias_ref, rest = rest[0], rest[1:]
    kn_ref, vn_ref, o_ref, m_ref, l_ref, acc_ref = rest
    s_id = pl.program_id(1)
    nk = P * PAGE_SIZE
    n_planes = 2 * H

    @pl.when(s_id == 0)
    def _():
        _online_init(m_ref, l_ref, acc_ref)

    if blocksize is not None:
        kblk = lax.broadcasted_iota(jnp.int32, (nk, LANES), 0) // blocksize
        onehot = (kblk == lax.broadcasted_iota(jnp.int32, (nk, LANES), 1)).astype(BF16)
    if window:
        kidx = s_id * nk + lax.broadcasted_iota(jnp.int32, (R, nk), 1)
        in_window = kidx > lax.broadcasted_iota(jnp.int32, (R, nk), 0) % DEC_POS
    heads = range(H)
    scores = []
    for h in heads:
        q = q_ref[h].astype(BF16)
        k = jnp.concatenate([_plane(pg, h, n_planes) for pg in pages], axis=0).astype(BF16)
        if blocksize is not None:
            q = jnp.concatenate([q, bias_ref[h].astype(BF16)], axis=1)
            k = jnp.concatenate([k, onehot], axis=1)
        s = _nt(q, k) * scale
        scores.append(jnp.where(in_window, s, NEG) if window else s)
    m_prev = [m_ref[h] for h in heads]
    m_new = [jnp.maximum(m_prev[h], jnp.max(scores[h], axis=-1, keepdims=True)) for h in heads]
    alpha = [jnp.exp(m_prev[h] - m_new[h]) for h in heads]
    probs = [jnp.exp(scores[h] - m_new[h]) for h in heads]
    l_new = [alpha[h] * l_ref[h] + jnp.sum(probs[h], axis=-1, keepdims=True) for h in heads]
    pv = []
    for h in heads:
        v = jnp.concatenate([_plane(pg, H + h, n_planes) for pg in pages], axis=0).astype(BF16)
        pv.append(_mm(probs[h].astype(BF16), v))
    for h in heads:
        acc_ref[h] = alpha[h] * acc_ref[h] + pv[h]
        m_ref[h] = m_new[h]
        l_ref[h] = l_new[h]

    @pl.when(s_id == pl.num_programs(1) - 1)
    def _():
        lane = lax.broadcasted_iota(jnp.int32, (R, PAGE_SIZE), 1)
        rpos = lax.broadcasted_iota(jnp.int32, (R, PAGE_SIZE), 0) % DEC_POS
        for h in range(H):
            kn = kn_ref[:, h * LANES:(h + 1) * LANES].astype(BF16)
            vn = vn_ref[:, h * LANES:(h + 1) * LANES].astype(BF16)
            s = _nt(q_ref[h].astype(BF16), kn) * scale
            s = jnp.where((lane <= rpos) & (lane < n_new), s, NEG)
            _online_update(s, vn, m_ref.at[h], l_ref.at[h], acc_ref.at[h])
            o_ref[h] = acc_ref[h] / l_ref[h]


def _paged_attn(q, cache, ptM, ptm, l, knew, vnew, bias=None, *, scale, blocksize=None, window=False, P, n_new):
    Bd, H, R, _ = q.shape
    n_pages = ptM.shape[1]
    depth = cache.shape[1]
    ppe = cache.shape[2] // PAGE_SIZE
    prow = PAGE_SIZE * 2 * H
    view = _flat_pages(cache)
    assert n_pages % P == 0 and cache.shape[4] == H

    def page_map(b, s, pM, pm, r):
        return ((pM[b, s * P + r] * depth + l) * ppe + pm[b, s * P + r], 0)

    in_specs = [pl.BlockSpec((None, H, R, LANES), lambda b, s, pM, pm: (b, 0, 0, 0))]
    in_specs += [pl.BlockSpec((prow, LANES), functools.partial(page_map, r=r)) for r in range(P)]
    args = [q] + [view] * P
    if blocksize is not None:
        in_specs.append(pl.BlockSpec((None, H, None, R, LANES), lambda b, s, pM, pm: (b, 0, s, 0, 0)))
        args.append(bias)
    for a in (knew, vnew):
        in_specs.append(pl.BlockSpec((None,) + a.shape[1:], lambda b, s, pM, pm: (b, 0, 0)))
        args.append(a)
    return pl.pallas_call(
        functools.partial(_paged_attn_body, P=P, H=H, R=R, scale=scale, blocksize=blocksize, window=window,
                          n_new=n_new),
        grid_spec=pltpu.PrefetchScalarGridSpec(
            num_scalar_prefetch=2, grid=(Bd, n_pages // P), in_specs=in_specs,
            out_specs=pl.BlockSpec((None, H, R, LANES), lambda b, s, pM, pm: (b, 0, 0, 0)),
            scratch_shapes=[pltpu.VMEM((H, R, 1), F32), pltpu.VMEM((H, R, 1), F32), pltpu.VMEM((H, R, LANES), F32)]),
        out_shape=jax.ShapeDtypeStruct((Bd, H, R, LANES), F32),
        compiler_params=_cparams(("parallel", "arbitrary")),
        name="paged_attn",
    )(ptM, ptm, *args)


def _dec_finalize_body(ocmp_ref, oslc_ref, owin_ref, gate_ref, odiff_ref, lam_ref, sub_ref, omoba_ref,
                       an_ref, ad_ref, am_ref, *, lam_init):
    gt = gate_ref[0:DEC_POS, :]
    an_ref[...] = jnp.zeros(an_ref.shape, F32)
    ad_ref[...] = jnp.zeros(ad_ref.shape, F32)
    am_ref[...] = jnp.zeros(am_ref.shape, F32)
    for g in range(NSA_GROUPS):
        for h in range(NSA_REP):
            head = NSA_REP * g + h
            rows = slice(h * DEC_POS, (h + 1) * DEC_POS)
            out = (_lane_pick(gt, head) * ocmp_ref[g, rows, :]
                   + _lane_pick(gt, NSA_HEADS + head) * oslc_ref[g, rows, :]
                   + _lane_pick(gt, 2 * NSA_HEADS + head) * owin_ref[g, rows, :])
            an_ref[0:DEC_POS, head * LANES:(head + 1) * LANES] = out
    for h in range(DIFF_HEADS):
        o = odiff_ref[h]
        ad_ref[0:DEC_POS, h * LANES:(h + 1) * LANES] = _diff_finish(o[:DEC_POS], o[DEC_POS:], lam_ref, sub_ref, lam_init)
    for h in range(MOBA_HEADS):
        am_ref[0:DEC_POS, h * LANES:(h + 1) * LANES] = omoba_ref[h]


def _dec_finalize(o_cmp, o_slc, o_win, proj3, o_diff, lam, subln, o_moba, *, lam_init):
    Bd = o_cmp.shape[0]

    def b4(a):
        return pl.BlockSpec((None,) + a.shape[1:], lambda b: (b, 0, 0, 0))

    outs = [(NSA_HEADS * LANES), (DIFF_HEADS * LANES), (MOBA_HEADS * LANES)]
    return pl.pallas_call(
        functools.partial(_dec_finalize_body, lam_init=lam_init),
        grid=(Bd,),
        in_specs=[b4(o_cmp), b4(o_slc), b4(o_win),
                  pl.BlockSpec((None, DEC_ROWS, LANES), lambda b: (b, 0, BGATE)),
                  b4(o_diff), pl.BlockSpec(lam.shape, lambda b: (0, 0)), pl.BlockSpec((1, LANES), lambda b: (0, 0)),
                  b4(o_moba)],
        out_specs=[pl.BlockSpec((None, DEC_ROWS, c), lambda b: (b, 0, 0)) for c in outs],
        out_shape=[jax.ShapeDtypeStruct((Bd, DEC_ROWS, c), F32) for c in outs],
        compiler_params=_cparams(("parallel",)),
        name="dec_finalize",
    )(o_cmp, o_slc, o_win, proj3, o_diff, lam, subln.reshape(1, LANES), o_moba)


def _head_major(cols, heads):
    Bd = cols.shape[0]
    return cols[:, :DEC_POS].reshape(Bd, DEC_POS, heads, HEAD_DIM).transpose(0, 2, 1, 3)


def _new_rows(proj3, blk, heads):
    rows = _cols(proj3, blk, heads)
    return jnp.pad(rows, ((0, 0), (0, PAGE_SIZE - rows.shape[1]), (0, 0)))


def _sample_attn(proj3, caches, page_table, P, l, n_new, *, scan_pages=16, nsa_scan_pages=32):
    cache_cmp, cache_slc, cache_win, cache_diff, cache_moba = caches
    Bd = proj3.shape[0]
    n_pages = page_table.shape[1]
    past = n_pages * PAGE_SIZE
    wkeep = cache_win.shape[2]
    assert n_new <= DEC_POS and wkeep == NSA_WINDOW and past >= NSA_WINDOW
    win_pages = wkeep // PAGE_SIZE
    zeros_pt = jnp.zeros_like(page_table)
    win_major = jnp.broadcast_to(jnp.arange(Bd, dtype=jnp.int32)[:, None], (Bd, win_pages))
    win_minor = jnp.broadcast_to(jnp.arange(win_pages, dtype=jnp.int32)[None, :], (Bd, win_pages))
    half = NSA_CMP_STRIDE * HEAD_DIM

    def w1ab(w1):
        return jnp.concatenate([w1[:half].reshape(NSA_CMP_STRIDE, LANES, LANES),
                                w1[half:].reshape(NSA_CMP_STRIDE, LANES, LANES)], axis=-1)
    wab = jnp.stack([w1ab(P["nsa_w1_k"][l]), w1ab(P["nsa_w1_v"][l])]).astype(BF16)
    wab = wab.reshape(2, NSA_CMP_STRIDE // 2, 2 * LANES, 2 * LANES)
    ab = _cmp_scan(cache_cmp, page_table, wab, l, P=nsa_scan_pages)
    qn = _head_major(_cols(proj3, BQ, NSA_HEADS), NSA_HEADS).reshape(Bd, NSA_GROUPS, NSA_REP * DEC_POS, LANES)
    cw = (P["nsa_pe_k"][l], P["nsa_w1_k"][l], P["nsa_w2_k"][l],
          P["nsa_pe_v"][l], P["nsa_w1_v"][l], P["nsa_w2_v"][l])
    o_cmp, bias_slc = _nsa_dec_select(ab, qn, cw, past=past,
                                      per_step=nsa_scan_pages * PAGE_SIZE // NSA_SLC_LEN)
    sc128 = HEAD_DIM ** -0.5
    o_slc = _paged_attn(qn, cache_slc, page_table, zeros_pt, l, _new_rows(proj3, BKS, NSA_GROUPS),
                        _new_rows(proj3, BVS, NSA_GROUPS), bias_slc, scale=sc128, blocksize=NSA_SLC_LEN,
                        P=nsa_scan_pages, n_new=n_new)
    o_win = _paged_attn(qn, cache_win, win_major, win_minor, l, _new_rows(proj3, BKW, NSA_GROUPS),
                        _new_rows(proj3, BVW, NSA_GROUPS), scale=sc128, window=True, P=win_pages, n_new=n_new)
    qd = _head_major(_cols(proj3, BDQ, DIFF_HEADS), DIFF_HEADS)
    lane = jnp.arange(LANES)
    qd = jnp.concatenate([jnp.where(lane < DIFF_QK_DIM, qd, 0.0), jnp.where(lane >= DIFF_QK_DIM, qd, 0.0)], axis=2)
    o_diff = _paged_attn(qd, cache_diff, page_table, zeros_pt, l, _new_rows(proj3, BDK, DIFF_HEADS),
                         _new_rows(proj3, BDV, DIFF_HEADS), scale=DIFF_QK_DIM ** -0.5, P=2 * scan_pages,
                         n_new=n_new)
    qm = _head_major(_cols(proj3, BMQ, MOBA_HEADS), MOBA_HEADS)
    parts = _moba_scan(qm, cache_moba, page_table, l, P=scan_pages)
    o_moba = _moba_combine(qm, *parts, _new_rows(proj3, BMK, MOBA_HEADS), _new_rows(proj3, BMV, MOBA_HEADS),
                           n_new=n_new)
    return _dec_finalize(o_cmp, o_slc, o_win, proj3, o_diff, P["diff_lambda"][l], P["diff_subln"][l], o_moba,
                         lam_init=_lam_init(l))


def _sample_trunk(x_sample, caches, page_table, P):
    cache_cmp, cache_slc, cache_win, cache_diff, cache_moba, cache_mem = caches
    Bd, n_new, D = x_sample.shape
    depth = P["w_in"].shape[0]
    past = page_table.shape[1] * PAGE_SIZE
    pos = past + jnp.arange(DEC_ROWS, dtype=jnp.int32)
    tabs = [jnp.tile(t, (Bd, 1)) for t in _rope_tables(pos)]
    x = jnp.pad(x_sample, ((0, 0), (0, DEC_ROWS - n_new), (0, 0))).reshape(Bd * DEC_ROWS, D)
    mem4 = cache_mem.reshape(Bd, depth, MEM_TOKENS, 2 * MEM_HEADS * MEM_DIM)
    rows = [[] for _ in range(5)]
    wkeep = cache_win.shape[2]
    win_buf = _row_buffer(Bd, depth, wkeep, NSA_GROUPS)
    for l in range(depth):
        proj = _norm_matmul(x, P["norm_mix"][l], _regroup_w_in(P["w_in"][l]), tabs, tm=Bd * DEC_ROWS, tn=PROJ_TILE)
        proj3 = proj.reshape(Bd, DEC_ROWS, NPROJ)
        new3 = proj3[:, :n_new]
        rows[0].append(_cache_rows(new3, BKC, BVC, NSA_GROUPS))
        rows[1].append(_cache_rows(new3, BKS, BVS, NSA_GROUPS))
        win_buf = _shift_window(win_buf, cache_win,
                                _cache_rows(new3, BKW, BVW, NSA_GROUPS).reshape(Bd, -1, HEAD_DIM), l)
        rows[3].append(_cache_rows(new3, BDK, BDV, DIFF_HEADS))
        rows[4].append(_cache_rows(new3, BMK, BMV, MOBA_HEADS))
        a_nsa, a_diff, a_moba = _sample_attn(proj3, caches[:5], page_table, P, l, n_new)
        x3 = _out_mem(a_nsa, a_diff, a_moba, proj3, x.reshape(Bd, DEC_ROWS, D), P["w_out"][l].astype(BF16),
                      P["norm_mem_x"][l].reshape(1, D), P["w_mem_q"][l].astype(BF16), mem4[:, l],
                      P["w_mem_o"][l].astype(BF16), tm=DEC_ROWS)
        x = x3.reshape(Bd * DEC_ROWS, D)
    y = _rmsnorm(x, P["norm_final"], tm=Bd * DEC_ROWS).reshape(Bd, DEC_ROWS, D)[:, :n_new]
    stacked = [jnp.stack(r, axis=1) if r else _as_rows(win_buf, Bd, depth, wkeep, NSA_GROUPS) for r in rows]
    return y, stacked


def kernel(x_prompt, x_sample, mem_prompt, cache_nsa_cmp, cache_nsa_slc, cache_nsa_win, cache_diff, cache_moba, cache_mem, page_table, norm_mix, w_in, w_out, nsa_pe_k, nsa_pe_v, nsa_w1_k, nsa_w2_k, nsa_w1_v, nsa_w2_v, diff_lambda, diff_subln, norm_mem_x, norm_mem_m, w_mem_q, w_mem_kv, w_mem_o, norm_final):
    P = {"norm_mix": norm_mix, "w_in": w_in, "w_out": w_out, "nsa_pe_k": nsa_pe_k, "nsa_pe_v": nsa_pe_v,
         "nsa_w1_k": nsa_w1_k, "nsa_w2_k": nsa_w2_k, "nsa_w1_v": nsa_w1_v, "nsa_w2_v": nsa_w2_v,
         "diff_lambda": diff_lambda, "diff_subln": diff_subln, "norm_mem_x": norm_mem_x,
         "norm_mem_m": norm_mem_m, "w_mem_q": w_mem_q, "w_mem_kv": w_mem_kv, "w_mem_o": w_mem_o,
         "norm_final": norm_final}
    y_prompt, (p_cmp, p_slc, p_win, p_diff, p_moba), p_mem = _prompt_trunk(x_prompt, mem_prompt, P)
    caches = (cache_nsa_cmp, cache_nsa_slc, cache_nsa_win, cache_diff, cache_moba, cache_mem)
    y_sample, (s_cmp, s_slc, s_win, s_diff, s_moba) = _sample_trunk(x_sample, caches, page_table, P)
    return (y_prompt, y_sample, p_cmp, p_slc, p_win, p_diff, p_moba, p_mem, s_cmp, s_slc, s_win, s_diff, s_moba)
```

```python
import functools
import math

import jax
import jax.numpy as jnp
from jax import lax
from jax.experimental import pallas as pl
from jax.experimental.pallas import tpu as pltpu

F32 = jnp.float32
BF16 = jnp.bfloat16

D_MODEL = 2048
HEAD_DIM = 128
ROPE_THETA = 500000.0
NORM_EPS = 1e-6
PAGE_SIZE = 128

NSA_HEADS = 8
NSA_GROUPS = 2
NSA_REP = NSA_HEADS // NSA_GROUPS
NSA_CMP_STRIDE = 16
NSA_SLC_LEN = 64
NSA_TOPN = 16
NSA_WINDOW = 512
NSA_FORCE_BONUS = 1000.0
NSA_COL_GROUPS = 4
NSA_TQ = 512
DIFF_TQ = 512
PROJ_TILE = 512
DIFF_HEADS = 4
DIFF_QK_DIM = HEAD_DIM // 2
MOBA_HEADS = 4
MOBA_BLOCK = 256
MOBA_TOPK = 3
MEM_TOKENS = 256
MEM_HEADS = 4
MEM_DIM = 128

NEG = -1e30
LANES = 128
SUBLANES = 8
VMEM_LIMIT_BYTES = 56 * 1024 * 1024

BQ, BKC, BKS, BKW, BMQ, BMK = 0, 8, 10, 12, 14, 18
BDQ, BDK = 22, 26
BVC, BVS, BVW, BNZ, BDV, BDZ, BMV, BMZ = 30, 32, 34, 36, 44, 48, 52, 56
BGATE = 60
NBLK = 64
NPROJ = NBLK * LANES
_W_IN_SEGMENTS = (
    (0, 1024), (1024, 1280), (1536, 1792), (2048, 2304), (5656, 6168), (6168, 6680),
    (3608, 4120), (4120, 4632),
    (1280, 1536), (1792, 2048), (2304, 2560), (2584, 3608), (4632, 5144), (5144, 5656),
    (6680, 7192), (7192, 7704),
    (2560, 2584),
)


def _round_up(n, m):
    return -(-n // m) * m


def _nt(a, b):
    return lax.dot_general(a, b, (((1,), (1,)), ((), ())), preferred_element_type=F32)


def _mm(a, b):
    return jnp.dot(a, b, preferred_element_type=F32)


def _silu(z):
    return z * jax.nn.sigmoid(z)


def _cparams(sem):
    return pltpu.CompilerParams(dimension_semantics=sem, vmem_limit_bytes=VMEM_LIMIT_BYTES)


def _norm_matmul_body(*refs, tn, splits, rope):
    if rope:
        x_ref, g_ref, w_ref, c1, a1, b1, c2, a2, b2, o_ref, h_ref = refs
    else:
        x_ref, g_ref, w_ref, o_ref, h_ref = refs
    j = pl.program_id(1)

    @pl.when(j == 0)
    def _():
        x = x_ref[...]
        ms = jnp.mean(x * x, axis=-1, keepdims=True)
        h_ref[...] = (x * lax.rsqrt(ms + NORM_EPS) * g_ref[...]).astype(BF16)

    if not rope:
        o_ref[...] = _mm(h_ref[...], w_ref[...])
        return

    def epilogue(kind, blk):
        if kind == "rot128":
            half = HEAD_DIM // 8
            return blk * c1[...] + pltpu.roll(blk, LANES - half, 1) * a1[...] + pltpu.roll(blk, half, 1) * b1[...]
        if kind == "rot64":
            half = DIFF_QK_DIM // 8
            return blk * c2[...] + pltpu.roll(blk, LANES - half, 1) * a2[...] + pltpu.roll(blk, half, 1) * b2[...]
        if kind == "gate":
            return jax.nn.sigmoid(blk)
        return blk

    parts = 2
    pw = tn // parts
    for lo, hi, kinds in splits:
        @pl.when((j >= lo) & (j <= hi))
        def _(kinds=kinds):
            h = h_ref[...]
            accs = [_mm(h, w_ref[:, s * pw:(s + 1) * pw]) for s in range(parts)]
            for k, kind in enumerate(kinds):
                s, o = divmod(k * LANES, pw)
                o_ref[:, k * LANES:(k + 1) * LANES] = epilogue(kind, accs[s][:, o:o + LANES])


def _tile_patterns(tn):
    kinds = (["rot128"] * (BDQ - BQ) + ["rot64"] * (BVC - BDQ) + ["plain"] * (BGATE - BVC) + ["gate"]
             + ["plain"] * (NBLK - BGATE - 1))
    per = tn // LANES
    tiles = [tuple(kinds[t * per:(t + 1) * per]) for t in range(NBLK // per)]
    runs = []
    for t, pat in enumerate(tiles):
        if runs and runs[-1][2] == pat:
            runs[-1] = (runs[-1][0], t, pat)
        else:
            runs.append((t, t, pat))
    return tuple(runs)


def _norm_matmul(x, gain, w, tabs=None, *, tm, tn):
    M, K = x.shape
    Np = w.shape[1]
    rope = tabs is not None
    splits = _tile_patterns(tn) if rope else None
    in_specs = [
        pl.BlockSpec((tm, K), lambda i, j: (i, 0)),
        pl.BlockSpec((1, K), lambda i, j: (0, 0)),
        pl.BlockSpec((K, tn), lambda i, j: (0, j)),
    ]
    args = [x, gain.reshape(1, K), w]
    if rope:
        period = tabs[0].shape[0] // tm
        for t in tabs:
            in_specs.append(pl.BlockSpec((tm, LANES), lambda i, j: (i % period, 0)))
            args.append(t)
    return pl.pallas_call(
        functools.partial(_norm_matmul_body, tn=tn, splits=splits, rope=rope),
        grid=(M // tm, Np // tn),
        in_specs=in_specs,
        out_specs=pl.BlockSpec((tm, tn), lambda i, j: (i, j)),
        out_shape=jax.ShapeDtypeStruct((M, Np), F32),
        scratch_shapes=[pltpu.VMEM((tm, K), BF16)],
        compiler_params=_cparams(("parallel", "arbitrary")),
        name="norm_matmul",
    )(*args)


def _rope_tables(pos):
    posf = pos.astype(F32)[:, None]
    n = pos.shape[0]

    def one(width, reps):
        rd = width // 4
        half = rd // 2
        inv = ROPE_THETA ** (-2.0 * jnp.arange(half, dtype=F32) / rd)
        ang = posf * inv[None, :]
        c, s = jnp.cos(ang), jnp.sin(ang)
        z = jnp.zeros((n, width - rd), F32)
        zh = jnp.zeros((n, half), F32)
        C = jnp.concatenate([c, c, jnp.ones((n, width - rd), F32)], axis=1)
        A = jnp.concatenate([-s, zh, z], axis=1)
        B = jnp.concatenate([zh, s, z], axis=1)
        return [jnp.tile(t, (1, reps)) for t in (C, A, B)]

    return one(HEAD_DIM, 1) + one(DIFF_QK_DIM, 2)


def _online_init(m_ref, l_ref, acc_ref):
    m_ref[...] = jnp.full(m_ref.shape, NEG, F32)
    l_ref[...] = jnp.zeros(l_ref.shape, F32)
    acc_ref[...] = jnp.zeros(acc_ref.shape, F32)


def _online_update(s, v, m_ref, l_ref, acc_ref):
    m_prev = m_ref[...]
    m_new = jnp.maximum(m_prev, jnp.max(s, axis=-1, keepdims=True))
    alpha = jnp.exp(m_prev - m_new)
    p = jnp.exp(s - m_new)
    l_ref[...] = alpha * l_ref[...] + jnp.sum(p, axis=-1, keepdims=True)
    acc_ref[...] = alpha * acc_ref[...] + _mm(p.astype(BF16), v)
    m_ref[...] = m_new


def _online_update_t(s, vt, m_ref, l_ref, acc_ref):
    m_prev = m_ref[...]
    m_new = jnp.maximum(m_prev, jnp.max(s, axis=0, keepdims=True))
    alpha = jnp.exp(m_prev - m_new)
    p = jnp.exp(s - m_new)
    l_ref[...] = alpha * l_ref[...] + jnp.sum(p, axis=0, keepdims=True)
    acc_ref[...] = alpha * acc_ref[...] + _mm(vt, p.astype(BF16))
    m_ref[...] = m_new


def _staged_update_t(scores, vts, m_ref, l_ref, acc_ref, idxs):
    n = range(len(scores))
    per_group = lambda r: r if isinstance(r, (list, tuple)) else [r] * len(scores)
    m_refs, l_refs, acc_refs = per_group(m_ref), per_group(l_ref), per_group(acc_ref)
    m_prev = [m_refs[j][idxs[j]] for j in n]
    m_new = [jnp.maximum(m_prev[j], jnp.max(scores[j], axis=0, keepdims=True)) for j in n]
    alpha = [jnp.exp(m_prev[j] - m_new[j]) for j in n]
    probs = [jnp.exp(scores[j] - m_new[j]) for j in n]
    l_new = [alpha[j] * l_refs[j][idxs[j]] + jnp.sum(probs[j], axis=0, keepdims=True) for j in n]
    pv = [_mm(vts[j], probs[j].astype(BF16)) for j in n]
    for j in n:
        acc_refs[j][idxs[j]] = alpha[j] * acc_refs[j][idxs[j]] + pv[j]
        m_refs[j][idxs[j]] = m_new[j]
        l_refs[j][idxs[j]] = l_new[j]


def _masked_softmax(s, mask, axis):
    m = jnp.max(jnp.where(mask, s, NEG), axis=axis, keepdims=True)
    m = jnp.where(m > 0.5 * NEG, m, 0.0)
    e = jnp.where(mask, jnp.exp(s - m), 0.0)
    return e / jnp.maximum(jnp.sum(e, axis=axis, keepdims=True), 1e-30)


def _rank_desc(score, ncand, axis):
    idx = lax.broadcasted_iota(jnp.int32, score.shape, axis)
    rank = jnp.zeros(score.shape, F32)
    for c in range(ncand):
        cand = score[:, c:c + 1] if axis == 1 else score[c:c + 1, :]
        before = (cand > score) | ((cand == score) & (c < idx))
        rank = rank + before.astype(F32)
    return rank


def _lane_pick(x, idx):
    lane = lax.broadcasted_iota(jnp.int32, x.shape, 1)
    return jnp.sum(jnp.where(lane == idx, x, 0.0), axis=-1, keepdims=True)


def _compress(src_ref, pe_ref, w1_ref, w2_ref, nc):
    half = NSA_CMP_STRIDE * HEAD_DIM
    acc_a = jnp.zeros((nc, LANES), F32)
    acc_b = jnp.zeros((nc, LANES), F32)
    for j in range(NSA_CMP_STRIDE):
        rows = src_ref[pl.ds(j, nc, stride=NSA_CMP_STRIDE), :].astype(BF16)
        acc_a = acc_a + _mm(rows, w1_ref[j * LANES:(j + 1) * LANES, :].astype(BF16))
        acc_b = acc_b + _mm(rows, w1_ref[half + j * LANES:half + (j + 1) * LANES, :].astype(BF16))
    return _compress_finish(acc_a, acc_b, pe_ref, w1_ref, w2_ref, nc)


def _pe_term(pe_ref, w1_ref):
    acc = jnp.zeros((SUBLANES, LANES), F32)
    for j in range(2 * NSA_CMP_STRIDE):
        pj = jnp.broadcast_to(pe_ref[j:j + 1, :], (SUBLANES, LANES)).astype(BF16)
        acc = acc + _mm(pj, w1_ref[j * LANES:(j + 1) * LANES, :].astype(BF16))
    return acc[0:1, :]


def _compress_finish(acc_a, acc_b, pe_ref, w1_ref, w2_ref, nc):
    row = lax.broadcasted_iota(jnp.int32, (nc, LANES), 0)
    hb = jnp.where(row < nc - 1, pltpu.roll(acc_b, nc - 1, 0), 0.0)
    hid = _silu(acc_a + hb + _pe_term(pe_ref, w1_ref))
    return _mm(hid.astype(BF16), w2_ref[...].astype(BF16))


def _slc_weights(shape, cmp_axis):
    r = lax.broadcasted_iota(jnp.int32, shape, cmp_axis)
    c = lax.broadcasted_iota(jnp.int32, shape, 1 - cmp_axis)
    d = r - 4 * c
    return jnp.where((d == -1) | (d == 3), 1.0, jnp.where((d >= 0) & (d <= 2), 2.0, 0.0)).astype(F32)


def _transpose_into(dst_ref, src_ref, rows, chunk):
    for c in range(rows // chunk):
        dst_ref[:, c * chunk:(c + 1) * chunk] = src_ref[c * chunk:(c + 1) * chunk, :].T.astype(BF16)


def _nsa_prompt_body(q_ref, kc_ref, ks_ref, kw_ref, vc_ref, vs_ref, vw_ref, gate_ref,
                     pek_ref, w1k_ref, w2k_ref, pev_ref, w1v_ref, w2v_ref, o_ref,
                     kcs, vct, kaug, vst, kwb, vwt, gt_ref, m_ref, l_ref, acc_ref, m2_ref, l2_ref, acc2_ref,
                     *, T, tq):
    g = pl.program_id(1)
    i = pl.program_id(2)
    nc = T // NSA_CMP_STRIDE
    nsb = T // NSA_SLC_LEN
    nsbr = _round_up(nsb, SUBLANES)
    nsel = min(NSA_TOPN, nsb)
    R = NSA_REP * tq
    scale = HEAD_DIM ** -0.5

    @pl.when(i == 0)
    def _():
        kcs[...] = _compress(kc_ref, pek_ref, w1k_ref, w2k_ref, nc).astype(BF16)
        vct[...] = _compress(vc_ref, pev_ref, w1v_ref, w2v_ref, nc).T.astype(BF16)
        kaug[:, :LANES] = ks_ref[...].astype(BF16)
        rblk = lax.broadcasted_iota(jnp.int32, (T, LANES), 0) // NSA_SLC_LEN
        lane = lax.broadcasted_iota(jnp.int32, (T, LANES), 1)
        kaug[:, LANES:] = (rblk == lane).astype(BF16)
        kwb[...] = kw_ref[...].astype(BF16)
        _transpose_into(vst, vs_ref, T, tq)
        _transpose_into(vwt, vw_ref, T, tq)

    q = q_ref[...]
    qt = jnp.concatenate([q[:, h * LANES:(h + 1) * LANES].T for h in range(NSA_REP)], axis=1).astype(BF16)

    s = _mm(kcs[...], qt) * scale
    crow = lax.broadcasted_iota(jnp.int32, (nc, R), 0)
    t_col = i * tq + lax.broadcasted_iota(jnp.int32, (nc, R), 1) % tq
    pc = _masked_softmax(s, (NSA_CMP_STRIDE * crow + 2 * NSA_CMP_STRIDE - 1) <= t_col, 0)
    o_cmp = _mm(vct[...], pc.astype(BF16))
    pg = pc[:, 0:tq]
    for h in range(1, NSA_REP):
        pg = pg + pc[:, h * tq:(h + 1) * tq]

    slc = jnp.dot(_slc_weights((LANES, nc), 1), pg, precision=lax.Precision.HIGHEST,
                  preferred_element_type=F32)[:nsbr]
    blk = lax.broadcasted_iota(jnp.int32, (nsbr, tq), 0)
    tb = (i * tq + lax.broadcasted_iota(jnp.int32, (nsbr, tq), 1)) // NSA_SLC_LEN
    valid = blk <= tb
    forced = (blk == 0) | (blk == tb) | (blk == tb - 1)
    score = jnp.where(valid, slc + jnp.where(forced, NSA_FORCE_BONUS, 0.0), -jnp.inf)
    sel = (_rank_desc(score, nsb, 0) < nsel) & valid
    bias = jnp.where(sel, 0.0, NEG)
    if nsbr < LANES:
        bias = jnp.concatenate([bias, jnp.full((LANES - nsbr, tq), NEG, F32)], axis=0)
    bias = bias.astype(BF16)
    qaug = jnp.concatenate([qt, jnp.concatenate([bias] * NSA_REP, axis=1)], axis=0)

    gw = R // NSA_COL_GROUPS
    cols = [(slice(None), slice(j * gw, (j + 1) * gw)) for j in range(NSA_COL_GROUPS)]
    krow = lax.broadcasted_iota(jnp.int32, (tq, gw), 0)
    tloc = lax.broadcasted_iota(jnp.int32, (tq, gw), 1) % tq

    slc_state = (m_ref, l_ref, acc_ref)
    win_state = (m2_ref, l2_ref, acc2_ref)

    def attend(jobs):
        scores, vts, states = [], [], []
        for k_tile, vt_tile, q_all, mask, state in jobs:
            for c in cols:
                s2 = _mm(k_tile, q_all[c]) * scale
                scores.append(s2 if mask is None else jnp.where(mask, s2, NEG))
                vts.append(vt_tile)
                states.append(state)
        _staged_update_t(scores, vts, [s[0] for s in states], [s[1] for s in states], [s[2] for s in states],
                         cols * len(jobs))

    def slc_job(kt, mask):
        off = pl.multiple_of(kt * tq, tq)
        return (kaug[pl.ds(off, tq), :], vst[:, pl.ds(off, tq)], qaug, mask, slc_state)

    nw = NSA_WINDOW // tq

    def win_job(d):
        off = pl.multiple_of((i - d) * tq, tq)
        mask = (krow > tloc) if d == nw else ((krow <= tloc) if d == 0 else None)
        return (kwb[pl.ds(off, tq), :], vwt[:, pl.ds(off, tq)], qt, mask, win_state)

    _online_init(*slc_state)
    _online_init(*win_state)

    def slc_step(kt, carry):
        attend([slc_job(kt, None)])
        return carry

    lax.fori_loop(0, jnp.maximum(i - 1, 0), slc_step, 0)
    for d in range(nw, 1, -1):
        @pl.when(i >= d)
        def _(d=d):
            attend([win_job(d)])

    @pl.when(i >= 1)
    def _():
        attend([slc_job(i - 1, None), win_job(1)])

    attend([slc_job(i, krow <= tloc), win_job(0)])
    o_slc = acc_ref[...] / l_ref[...]
    o_win = acc2_ref[...] / l2_ref[...]

    gt_ref[...] = gate_ref[...].T

    def gate_row(branch):
        return jnp.concatenate(
            [gt_ref[pl.ds(branch * NSA_HEADS + NSA_REP * g + h, 1), :] for h in range(NSA_REP)], axis=1)

    out = gate_row(0) * o_cmp + gate_row(1) * o_slc + gate_row(2) * o_win
    for h in range(NSA_REP):
        o_ref[:, h * LANES:(h + 1) * LANES] = out[:, h * tq:(h + 1) * tq].T


def _nsa_prompt(proj, cw, *, B, T, tq):
    nqt = T // tq
    nc = T // NSA_CMP_STRIDE
    R = NSA_REP * tq
    assert T % NSA_SLC_LEN == 0 and NSA_WINDOW % tq == 0 and T // NSA_SLC_LEN <= LANES and nc <= LANES

    def kv_spec(blk):
        return pl.BlockSpec((T, LANES), lambda b, g, i: (b, blk + g))

    def full(a):
        return pl.BlockSpec(a.shape, lambda b, g, i: (0,) * a.ndim)

    in_specs = [
        pl.BlockSpec((tq, NSA_REP * LANES), lambda b, g, i: (b * nqt + i, g)),
        kv_spec(BKC), kv_spec(BKS), kv_spec(BKW), kv_spec(BVC), kv_spec(BVS), kv_spec(BVW),
        pl.BlockSpec((tq, LANES), lambda b, g, i: (b * nqt + i, BGATE)),
    ] + [full(a) for a in cw]
    return pl.pallas_call(
        functools.partial(_nsa_prompt_body, T=T, tq=tq),
        grid=(B, NSA_GROUPS, nqt),
        in_specs=in_specs,
        out_specs=pl.BlockSpec((tq, NSA_REP * LANES), lambda b, g, i: (b * nqt + i, g)),
        out_shape=jax.ShapeDtypeStruct((B * T, NSA_HEADS * LANES), F32),
        scratch_shapes=[
            pltpu.VMEM((nc, LANES), BF16), pltpu.VMEM((LANES, nc), BF16),
            pltpu.VMEM((T, 2 * LANES), BF16), pltpu.VMEM((LANES, T), BF16),
            pltpu.VMEM((T, LANES), BF16), pltpu.VMEM((LANES, T), BF16),
            pltpu.VMEM((LANES, tq), F32),
            pltpu.VMEM((1, R), F32), pltpu.VMEM((1, R), F32), pltpu.VMEM((LANES, R), F32),
            pltpu.VMEM((1, R), F32), pltpu.VMEM((1, R), F32), pltpu.VMEM((LANES, R), F32),
        ],
        compiler_params=_cparams(("parallel", "parallel", "arbitrary")),
        name="nsa_prompt",
    )(proj, proj, proj, proj, proj, proj, proj, proj, *cw)


def _diff_lambda(lam_ref, lam_init):
    lp = lam_ref[...]
    return (jnp.exp(jnp.sum(lp[0:1] * lp[1:2], axis=-1, keepdims=True))
            - jnp.exp(jnp.sum(lp[2:3] * lp[3:4], axis=-1, keepdims=True)) + lam_init)


def _diff_finish(o1, o2, lam_ref, sub_ref, lam_init):
    a = o1 - _diff_lambda(lam_ref, lam_init) * o2
    ms = jnp.mean(a * a, axis=-1, keepdims=True)
    return a * lax.rsqrt(ms + NORM_EPS) * sub_ref[...] * (1.0 - lam_init)


def _head_specs(nqt, T, tq, qblk, kblk, vblk, heads):
    specs = [pl.BlockSpec((tq, LANES), functools.partial(lambda b, i, c: (b * nqt + i, c), c=qblk + h))
             for h in range(heads)]
    for blk in (kblk, vblk):
        specs += [pl.BlockSpec((T, LANES), functools.partial(lambda b, i, c: (b, c), c=blk + h))
                  for h in range(heads)]
    return specs


def _diff_prompt_body(*refs, T, tq, lam_init):
    nh = DIFF_HEADS
    q_refs, k_refs, v_refs = refs[:nh], refs[nh:2 * nh], refs[2 * nh:3 * nh]
    lam_ref, sub_ref, o_ref, kb, vt, m_ref, l_ref, acc_ref = refs[3 * nh:]
    i = pl.program_id(1)
    scale = DIFF_QK_DIM ** -0.5
    heads = range(nh)

    @pl.when(i == 0)
    def _():
        for h in heads:
            kb[h] = k_refs[h][...].astype(BF16)
            _transpose_into(vt.at[h], v_refs[h], T, tq)

    qts = []
    for h in heads:
        q = q_refs[h][...]
        lane = lax.broadcasted_iota(jnp.int32, q.shape, 1)
        qts.append(jnp.concatenate([jnp.where(lane < DIFF_QK_DIM, q, 0.0).T,
                                    jnp.where(lane >= DIFF_QK_DIM, q, 0.0).T], axis=1).astype(BF16))
    R = 2 * tq
    krow = lax.broadcasted_iota(jnp.int32, (tq, R), 0)
    tloc = lax.broadcasted_iota(jnp.int32, (tq, R), 1) % tq
    idxs = [(h,) for h in heads]
    _online_init(m_ref, l_ref, acc_ref)

    def attend(off, mask):
        scores = []
        for h in heads:
            s = _mm(kb[h, pl.ds(off, tq), :], qts[h]) * scale
            scores.append(s if mask is None else jnp.where(mask, s, NEG))
        _staged_update_t(scores, [vt[h, :, pl.ds(off, tq)] for h in heads], m_ref, l_ref, acc_ref, idxs)

    def step(kt, carry):
        attend(pl.multiple_of(kt * tq, tq), None)
        return carry

    lax.fori_loop(0, i, step, 0)
    attend(pl.multiple_of(i * tq, tq), krow <= tloc)
    for h in heads:
        o = acc_ref[h] / l_ref[h]
        o_ref[:, h * LANES:(h + 1) * LANES] = _diff_finish(o[:, :tq].T, o[:, tq:].T, lam_ref, sub_ref, lam_init)


def _diff_prompt(proj, lam, subln, *, B, T, tq, lam_init):
    nqt = T // tq
    nh = DIFF_HEADS
    return pl.pallas_call(
        functools.partial(_diff_prompt_body, T=T, tq=tq, lam_init=lam_init),
        grid=(B, nqt),
        in_specs=_head_specs(nqt, T, tq, BDQ, BDK, BDV, nh) + [
            pl.BlockSpec(lam.shape, lambda b, i: (0, 0)),
            pl.BlockSpec((1, LANES), lambda b, i: (0, 0)),
        ],
        out_specs=pl.BlockSpec((tq, nh * LANES), lambda b, i: (b * nqt + i, 0)),
        out_shape=jax.ShapeDtypeStruct((B * T, nh * LANES), F32),
        scratch_shapes=[
            pltpu.VMEM((nh, T, LANES), BF16), pltpu.VMEM((nh, LANES, T), BF16),
            pltpu.VMEM((nh, 1, 2 * tq), F32), pltpu.VMEM((nh, 1, 2 * tq), F32),
            pltpu.VMEM((nh, LANES, 2 * tq), F32),
        ],
        compiler_params=_cparams(("parallel", "arbitrary")),
        name="diff_prompt",
    )(*([proj] * (3 * nh)), lam, subln.reshape(1, LANES))


def _moba_prompt_body(*refs, T):
    nh = MOBA_HEADS
    q_refs, k_refs, v_refs = refs[:nh], refs[nh:2 * nh], refs[2 * nh:3 * nh]
    o_ref, kb, vt, km, bias_ref, m_ref, l_ref, acc_ref = refs[3 * nh:]
    i = pl.program_id(1)
    tq = MOBA_BLOCK
    nb = T // MOBA_BLOCK
    nbr = _round_up(nb, SUBLANES)
    scale = HEAD_DIM ** -0.5
    heads = range(nh)

    @pl.when(i == 0)
    def _():
        km[...] = jnp.zeros(km.shape, BF16)
        for h in heads:
            kb[h] = k_refs[h][...].astype(BF16)
            _transpose_into(vt.at[h], v_refs[h], T, tq)
            for j in range(nb):
                km[h, j:j + 1, :] = jnp.mean(k_refs[h][j * tq:(j + 1) * tq, :], axis=0, keepdims=True).astype(BF16)

    qts = [q_refs[h][...].T.astype(BF16) for h in heads]
    blk = lax.broadcasted_iota(jnp.int32, (nbr, tq), 0)
    past = blk < i
    for h in heads:
        score = jnp.where(past, _mm(km[h], qts[h])[:nbr], -jnp.inf)
        sel = (_rank_desc(score, nb, 0) < min(MOBA_TOPK, nb)) & past
        bias_ref[h] = jnp.where(sel, 0.0, NEG)
    krow = lax.broadcasted_iota(jnp.int32, (tq, tq), 0)
    tloc = lax.broadcasted_iota(jnp.int32, (tq, tq), 1)
    idxs = [(h,) for h in heads]
    _online_init(m_ref, l_ref, acc_ref)

    def step(kt, carry):
        off = pl.multiple_of(kt * tq, tq)
        scores = [_mm(kb[h, pl.ds(off, tq), :], qts[h]) * scale + bias_ref[h, pl.ds(kt, 1), :] for h in heads]
        _staged_update_t(scores, [vt[h, :, pl.ds(off, tq)] for h in heads], m_ref, l_ref, acc_ref, idxs)
        return carry

    lax.fori_loop(0, i, step, 0)
    off = pl.multiple_of(i * tq, tq)
    scores = [jnp.where(krow <= tloc, _mm(kb[h, pl.ds(off, tq), :], qts[h]) * scale, NEG) for h in heads]
    _staged_update_t(scores, [vt[h, :, pl.ds(off, tq)] for h in heads], m_ref, l_ref, acc_ref, idxs)
    for h in heads:
        o_ref[:, h * LANES:(h + 1) * LANES] = (acc_ref[h] / l_ref[h]).T


def _moba_prompt(proj, *, B, T):
    tq = MOBA_BLOCK
    assert T % tq == 0 and T // tq <= LANES
    nqt = T // tq
    nh = MOBA_HEADS
    nbr = _round_up(T // MOBA_BLOCK, SUBLANES)
    return pl.pallas_call(
        functools.partial(_moba_prompt_body, T=T),
        grid=(B, nqt),
        in_specs=_head_specs(nqt, T, tq, BMQ, BMK, BMV, nh),
        out_specs=pl.BlockSpec((tq, nh * LANES), lambda b, i: (b * nqt + i, 0)),
        out_shape=jax.ShapeDtypeStruct((B * T, nh * LANES), F32),
        scratch_shapes=[
            pltpu.VMEM((nh, T, LANES), BF16), pltpu.VMEM((nh, LANES, T), BF16), pltpu.VMEM((nh, LANES, LANES), BF16),
            pltpu.VMEM((nh, nbr, tq), F32),
            pltpu.VMEM((nh, 1, tq), F32), pltpu.VMEM((nh, 1, tq), F32), pltpu.VMEM((nh, LANES, tq), F32),
        ],
        compiler_params=_cparams(("parallel", "arbitrary")),
        name="moba_prompt",
    )(*([proj] * (3 * nh)))


def _out_mem_body(an_ref, ad_ref, am_ref, nz0_ref, nz1_ref, dz_ref, mz_ref, x_ref, wout_ref, gmem_ref,
                  wq_ref, mkv_ref, wo_ref, o_ref):
    half = NSA_HEADS * LANES // 2
    an = an_ref[...]
    mixed = jnp.concatenate([
        an[:, :half] * _silu(nz0_ref[...]), an[:, half:] * _silu(nz1_ref[...]),
        ad_ref[...] * _silu(dz_ref[...]), am_ref[...] * _silu(mz_ref[...])], axis=1).astype(BF16)
    x1 = x_ref[...] + _mm(mixed, wout_ref[...])
    ms = jnp.mean(x1 * x1, axis=-1, keepdims=True)
    h2 = (x1 * lax.rsqrt(ms + NORM_EPS) * gmem_ref[...]).astype(BF16)
    q = _mm(h2, wq_ref[...])
    mkv = mkv_ref[...]
    scale = MEM_DIM ** -0.5
    outs = []
    for hh in range(MEM_HEADS):
        qh = q[:, hh * LANES:(hh + 1) * LANES].astype(BF16)
        kh = mkv[:, hh * LANES:(hh + 1) * LANES].astype(BF16)
        vh = mkv[:, (MEM_HEADS + hh) * LANES:(MEM_HEADS + hh + 1) * LANES].astype(BF16)
        s = _nt(qh, kh) * scale
        e = jnp.exp(s - jnp.max(s, axis=-1, keepdims=True))
        p = e / jnp.sum(e, axis=-1, keepdims=True)
        outs.append(_mm(p.astype(BF16), vh))
    oc = jnp.concatenate(outs, axis=1).astype(BF16)
    o_ref[...] = x1 + _mm(oc, wo_ref[...])


def _out_mem(a_nsa, a_diff, a_moba, proj3, x3, wout, gmem, wq, mkv, wo, *, tm):
    nb, rows, D = x3.shape
    grid = (nb, rows // tm)

    def row(cols, cb):
        return pl.BlockSpec((None, tm, cols), lambda b, i: (b, i, cb))

    def const(a):
        return pl.BlockSpec(a.shape, lambda b, i: (0,) * a.ndim, pipeline_mode=pl.Buffered(1))

    zc = 4 * LANES
    return pl.pallas_call(
        _out_mem_body,
        grid=grid,
        in_specs=[
            row(NSA_HEADS * LANES, 0), row(DIFF_HEADS * LANES, 0), row(MOBA_HEADS * LANES, 0),
            row(zc, BNZ * LANES // zc), row(zc, BNZ * LANES // zc + 1), row(zc, BDZ * LANES // zc),
            row(zc, BMZ * LANES // zc),
            row(D, 0), const(wout), const(gmem), const(wq),
            pl.BlockSpec((None, MEM_TOKENS, 2 * MEM_HEADS * MEM_DIM), lambda b, i: (b, 0, 0)),
            const(wo),
        ],
        out_specs=row(D, 0),
        out_shape=jax.ShapeDtypeStruct(x3.shape, F32),
        compiler_params=_cparams(("parallel", "parallel")),
        name="out_mem",
    )(a_nsa, a_diff, a_moba, proj3, proj3, proj3, proj3, x3, wout, gmem, wq, mkv, wo)


def _rmsnorm_body(x_ref, g_ref, o_ref):
    x = x_ref[...]
    ms = jnp.mean(x * x, axis=-1, keepdims=True)
    o_ref[...] = x * lax.rsqrt(ms + NORM_EPS) * g_ref[...]


def _rmsnorm(x, gain, *, tm):
    M, D = x.shape
    return pl.pallas_call(
        _rmsnorm_body,
        grid=(M // tm,),
        in_specs=[pl.BlockSpec((tm, D), lambda i: (i, 0)), pl.BlockSpec((1, D), lambda i: (0, 0))],
        out_specs=pl.BlockSpec((tm, D), lambda i: (i, 0)),
        out_shape=jax.ShapeDtypeStruct((M, D), F32),
        compiler_params=_cparams(("parallel",)),
        name="final_norm",
    )(x, gain.reshape(1, D))


def _scatter_rows_body(*refs, widths):
    slabs, o_ref = refs[:len(widths)], refs[len(widths) + 1]
    n_planes = sum(widths)
    rows = slabs[0].shape[0]
    p = 0
    for slab, w in zip(slabs, widths):
        for j in range(w):
            o_ref[pl.ds(p, rows, stride=n_planes), :] = slab[:, j * LANES:(j + 1) * LANES]
            p += 1


def _column_slabs(blocks):
    slabs = []
    i = 0
    while i < len(blocks):
        run = 1
        while i + run < len(blocks) and blocks[i + run] == blocks[i] + run:
            run += 1
        w = 1
        while 2 * w <= run and blocks[i] % (2 * w) == 0:
            w *= 2
        slabs.append((blocks[i], w))
        i += w
    return slabs


def _scatter_rows(buf, src, blocks, l, *, depth, seq_rows, t_start, t_len, tm):
    n_planes = len(blocks)
    slabs = _column_slabs(blocks)
    nb = src.shape[0] // seq_rows
    tiles = t_len // tm
    assert t_len % tm == 0 and t_start % tm == 0 and seq_rows % tm == 0

    def src_map(b, i, c):
        return (b * (seq_rows // tm) + t_start // tm + i, c)

    in_specs = [pl.BlockSpec((tm, w * LANES), functools.partial(src_map, c=first // w)) for first, w in slabs]
    in_specs.append(pl.BlockSpec(memory_space=pl.ANY))
    return pl.pallas_call(
        functools.partial(_scatter_rows_body, widths=tuple(w for _, w in slabs)),
        grid=(nb, tiles),
        in_specs=in_specs,
        out_specs=pl.BlockSpec((tm * n_planes, LANES), lambda b, i: ((b * depth + l) * tiles + i, 0)),
        out_shape=jax.ShapeDtypeStruct(buf.shape, F32),
        input_output_aliases={len(slabs): 0},
        compiler_params=_cparams(("parallel", "parallel")),
        name="scatter_rows",
    )(*([src] * len(slabs)), buf)


def _kv_blocks(kblk, vblk, heads):
    return [kblk + h for h in range(heads)] + [vblk + h for h in range(heads)]


def _row_buffer(entries, depth, rows, heads):
    return jnp.zeros((entries * depth * rows * 2 * heads, HEAD_DIM), F32)


def _as_rows(buf, entries, depth, rows, heads):
    return buf.reshape(entries, depth, rows, 2, heads, HEAD_DIM)


def _shift_window_body(old_ref, new_ref, buf_ref, o_ref, *, n_shift):
    keep = old_ref.shape[0] - n_shift
    o_ref[0:keep, :] = old_ref[n_shift:, :]
    o_ref[keep:, :] = new_ref[...]


def _shift_window(buf, cache_win, new_flat, l):
    Bd, depth, wkeep = cache_win.shape[:3]
    n_planes = 2 * NSA_GROUPS
    n_shift = new_flat.shape[1]
    old = cache_win.reshape(Bd * depth * wkeep * n_planes, HEAD_DIM)
    wrows = wkeep * n_planes
    return pl.pallas_call(
        functools.partial(_shift_window_body, n_shift=n_shift),
        grid=(Bd,),
        in_specs=[pl.BlockSpec((wrows, LANES), lambda b: (b * depth + l, 0)),
                  pl.BlockSpec((None, n_shift, LANES), lambda b: (b, 0, 0)),
                  pl.BlockSpec(memory_space=pl.ANY)],
        out_specs=pl.BlockSpec((wrows, LANES), lambda b: (b * depth + l, 0)),
        out_shape=jax.ShapeDtypeStruct(buf.shape, F32),
        input_output_aliases={2: 0},
        compiler_params=_cparams(("parallel",)),
        name="shift_window",
    )(old, new_flat, buf)


def _regroup_w_in(w_in_l):
    K = w_in_l.shape[0]
    used = sum(b - a for a, b in _W_IN_SEGMENTS)
    w = w_in_l.astype(BF16)
    cols = [w[:, a:b] for a, b in _W_IN_SEGMENTS] + [jnp.zeros((K, NPROJ - used), BF16)]
    return jnp.concatenate(cols, axis=1)


def _cols(proj, blk, n):
    return proj[..., blk * LANES:(blk + n) * LANES]


def _cache_rows(proj3, kblk, vblk, heads):
    nb, rows, _ = proj3.shape
    kv = jnp.concatenate([_cols(proj3, kblk, heads), _cols(proj3, vblk, heads)], axis=-1)
    return kv.reshape(nb, rows, 2, heads, HEAD_DIM)


def _lam_init(l):
    return 0.8 - 0.6 * math.exp(-0.3 * l)


def _prompt_trunk(x_prompt, mem_prompt, P, *, tq=256):
    B, T, D = x_prompt.shape
    depth = P["w_in"].shape[0]
    N = B * T
    tabs = _rope_tables(jnp.arange(T, dtype=jnp.int32))
    tm = min(1024, T)
    memx = mem_prompt.reshape(B * MEM_TOKENS, D)
    x = x_prompt.reshape(N, D)
    wk = min(NSA_WINDOW, T)
    outs = ((BKC, BVC, NSA_GROUPS, 0, T), (BKS, BVS, NSA_GROUPS, 0, T), (BKW, BVW, NSA_GROUPS, T - wk, wk),
            (BDK, BDV, DIFF_HEADS, 0, T), (BMK, BMV, MOBA_HEADS, 0, T))
    bufs = [_row_buffer(B, depth, n, h) for _, _, h, _, n in outs]
    mem_buf = _row_buffer(B, depth, MEM_TOKENS, MEM_HEADS)
    for l in range(depth):
        mkv = _norm_matmul(memx, P["norm_mem_m"][l], P["w_mem_kv"][l].astype(BF16), tm=min(512, B * MEM_TOKENS),
                           tn=256)
        mkv3 = mkv.reshape(B, MEM_TOKENS, 2 * MEM_HEADS * MEM_DIM)
        mem_buf = _scatter_rows(mem_buf, mkv, list(range(2 * MEM_HEADS)), l, depth=depth, seq_rows=MEM_TOKENS,
                                t_start=0, t_len=MEM_TOKENS, tm=MEM_TOKENS)
        proj = _norm_matmul(x, P["norm_mix"][l], _regroup_w_in(P["w_in"][l]), tabs, tm=tm, tn=PROJ_TILE)
        proj3 = proj.reshape(B, T, NPROJ)
        for n, (kblk, vblk, heads, t0, t_len) in enumerate(outs):
            bufs[n] = _scatter_rows(bufs[n], proj, _kv_blocks(kblk, vblk, heads), l, depth=depth, seq_rows=T,
                                    t_start=t0, t_len=t_len, tm=min(512, t_len))
        cw = (P["nsa_pe_k"][l], P["nsa_w1_k"][l], P["nsa_w2_k"][l],
              P["nsa_pe_v"][l], P["nsa_w1_v"][l], P["nsa_w2_v"][l])
        a_nsa = _nsa_prompt(proj, cw, B=B, T=T, tq=min(NSA_TQ, T))
        a_diff = _diff_prompt(proj, P["diff_lambda"][l], P["diff_subln"][l], B=B, T=T, tq=min(DIFF_TQ, T),
                              lam_init=_lam_init(l))
        a_moba = _moba_prompt(proj, B=B, T=T)
        x3 = _out_mem(a_nsa.reshape(B, T, -1), a_diff.reshape(B, T, -1), a_moba.reshape(B, T, -1), proj3,
                      x.reshape(B, T, D), P["w_out"][l].astype(BF16), P["norm_mem_x"][l].reshape(1, D),
                      P["w_mem_q"][l].astype(BF16), mkv3, P["w_mem_o"][l].astype(BF16), tm=min(256, T))
        x = x3.reshape(N, D)
    y = _rmsnorm(x, P["norm_final"], tm=min(512, N)).reshape(B, T, D)
    rows = [_as_rows(bufs[n], B, depth, t_len, heads) for n, (_, _, heads, _, t_len) in enumerate(outs)]
    return y, rows, _as_rows(mem_buf, B, depth, MEM_TOKENS, MEM_HEADS)


DEC_ROWS = 16
DEC_POS = 8


def _plane(page_ref, plane, n_planes):
    return page_ref[pl.ds(plane, PAGE_SIZE, stride=n_planes), :]


def _flat_pages(cache):
    p0, depth, rows, two, heads, dh = cache.shape
    return cache.reshape(p0 * depth * rows * two * heads, dh)


def _cmp_scan_body(pt_ref, *refs, P):
    pages = refs[:P]
    w_ref, o_ref = refs[P], refs[P + 1]
    chunks = PAGE_SIZE // NSA_CMP_STRIDE
    n_planes = 2 * NSA_GROUPS
    r = lax.broadcasted_iota(jnp.int32, (PAGE_SIZE, PAGE_SIZE), 0)
    c = lax.broadcasted_iota(jnp.int32, (PAGE_SIZE, PAGE_SIZE), 1)
    regroup = ((r % chunks) * NSA_CMP_STRIDE + r // chunks == c).astype(BF16)
    rows = chunks * P
    for kv in range(2):
        by_row = [_mm(regroup, jnp.concatenate(
            [_plane(pg, kv * NSA_GROUPS + g, n_planes) for g in range(NSA_GROUPS)], axis=1).astype(BF16))
            for pg in pages]
        acc = jnp.zeros((NSA_GROUPS * rows, 2 * LANES), F32)
        for jp in range(NSA_CMP_STRIDE // 2):
            lhs = jnp.concatenate([
                jnp.concatenate([x[j * chunks:(j + 1) * chunks, g * LANES:(g + 1) * LANES]
                                 for j in (2 * jp, 2 * jp + 1)], axis=1)
                for g in range(NSA_GROUPS) for x in by_row], axis=0).astype(BF16)
            acc = acc + _mm(lhs, w_ref[kv, jp])
        for g in range(NSA_GROUPS):
            lo = (kv * NSA_GROUPS + g) * 2 * LANES
            o_ref[:, lo:lo + 2 * LANES] = acc[g * rows:(g + 1) * rows]


def _cmp_scan(cache, page_table, w1ab, l, *, P):
    Bd, n_pages = page_table.shape
    chunks = PAGE_SIZE // NSA_CMP_STRIDE
    depth = cache.shape[1]
    view = _flat_pages(cache)
    prow = PAGE_SIZE * 2 * NSA_GROUPS
    assert n_pages % P == 0 and cache.shape[2] == PAGE_SIZE

    def page_map(b, s, pt, r):
        return (pt[b, s * P + r] * depth + l, 0)

    in_specs = [pl.BlockSpec((prow, LANES), functools.partial(page_map, r=r)) for r in range(P)]
    in_specs.append(pl.BlockSpec(w1ab.shape, lambda b, s, pt: (0, 0, 0, 0)))
    ncol = 2 * NSA_GROUPS * 2 * LANES
    return pl.pallas_call(
        functools.partial(_cmp_scan_body, P=P),
        grid_spec=pltpu.PrefetchScalarGridSpec(
            num_scalar_prefetch=1, grid=(Bd, n_pages // P), in_specs=in_specs,
            out_specs=pl.BlockSpec((None, chunks * P, ncol), lambda b, s, pt: (b, s, 0))),
        out_shape=jax.ShapeDtypeStruct((Bd, chunks * n_pages, ncol), F32),
        compiler_params=_cparams(("parallel", "arbitrary")),
        name="cmp_scan",
    )(page_table, *([view] * P), w1ab)


def _step_bias(bias, step, per_step):
    lo = step * per_step
    chunk = bias[:, (lo // LANES) * LANES:(lo // LANES + 1) * LANES]
    off = lo % LANES
    return chunk if off == 0 else pltpu.roll(chunk, LANES - off, 1)


def _nsa_dec_select_body(ab_ref, q_ref, pek_ref, w1k_ref, w2k_ref, pev_ref, w1v_ref, w2v_ref,
                         ocmp_ref, bias_ref, *, past, per_step):
    nch = past // NSA_CMP_STRIDE
    nsbp = past // NSA_SLC_LEN
    ncol = _round_up(nsbp, LANES)
    R = NSA_REP * DEC_POS
    scale = HEAD_DIM ** -0.5
    for g in range(NSA_GROUPS):
        def comp(kv, pe_ref, w1_ref, w2_ref):
            lo = (kv * NSA_GROUPS + g) * 2 * LANES
            return _compress_finish(ab_ref[:, lo:lo + LANES], ab_ref[:, lo + LANES:lo + 2 * LANES],
                                    pe_ref, w1_ref, w2_ref, nch).astype(BF16)
        kc = comp(0, pek_ref, w1k_ref, w2k_ref)
        vc = comp(1, pev_ref, w1v_ref, w2v_ref)
        s = _nt(q_ref[g].astype(BF16), kc) * scale
        col = lax.broadcasted_iota(jnp.int32, (R, nch), 1)
        qpos = past + lax.broadcasted_iota(jnp.int32, (R, nch), 0) % DEC_POS
        pc = _masked_softmax(s, (NSA_CMP_STRIDE * col + 2 * NSA_CMP_STRIDE - 1) <= qpos, 1)
        ocmp_ref[g] = _mm(pc.astype(BF16), vc)
        pg = pc[0:DEC_POS]
        for h in range(1, NSA_REP):
            pg = pg + pc[h * DEC_POS:(h + 1) * DEC_POS]
        slc = jnp.dot(pg, _slc_weights((nch, ncol), 0), precision=lax.Precision.HIGHEST,
                      preferred_element_type=F32)
        blk = lax.broadcasted_iota(jnp.int32, (DEC_POS, ncol), 1)
        forced = (blk == 0) | (blk == nsbp - 1)
        score = jnp.where(blk < nsbp, slc + jnp.where(forced, NSA_FORCE_BONUS, 0.0), -jnp.inf)
        rank = _rank_desc(score, nsbp, 1) + jnp.where(forced, 0.0, 1.0)
        b8 = jnp.where(rank < min(NSA_TOPN, nsbp + 1), 0.0, NEG)
        for st in range(nsbp // per_step):
            bias_ref[g, st] = jnp.concatenate([_step_bias(b8, st, per_step)] * NSA_REP, axis=0)


def _nsa_dec_select(ab, qn, cw, *, past, per_step):
    Bd = ab.shape[0]
    R = NSA_REP * DEC_POS
    n_steps = past // NSA_SLC_LEN // per_step

    def full(a):
        return pl.BlockSpec(a.shape, lambda b: (0,) * a.ndim)

    return pl.pallas_call(
        functools.partial(_nsa_dec_select_body, past=past, per_step=per_step),
        grid=(Bd,),
        in_specs=[pl.BlockSpec((None,) + ab.shape[1:], lambda b: (b, 0, 0)),
                  pl.BlockSpec((None, NSA_GROUPS, R, LANES), lambda b: (b, 0, 0, 0))] + [full(a) for a in cw],
        out_specs=[pl.BlockSpec((None, NSA_GROUPS, R, LANES), lambda b: (b, 0, 0, 0)),
                   pl.BlockSpec((None, NSA_GROUPS, n_steps, R, LANES), lambda b: (b, 0, 0, 0, 0))],
        out_shape=[jax.ShapeDtypeStruct((Bd, NSA_GROUPS, R, LANES), F32),
                   jax.ShapeDtypeStruct((Bd, NSA_GROUPS, n_steps, R, LANES), F32)],
        compiler_params=_cparams(("parallel",)),
        name="nsa_dec_select",
    )(ab, qn, *cw)


def _moba_scan_body(pt_ref, *refs, P):
    q_ref = refs[0]
    pages = refs[1:1 + P]
    acc_ref, m_ref, l_ref, sc_ref = refs[1 + P:]
    H, R = MOBA_HEADS, DEC_POS
    n_planes = 2 * H
    ppb = MOBA_BLOCK // PAGE_SIZE
    nblk = P // ppb
    s_id = pl.program_id(1)
    scale = HEAD_DIM ** -0.5
    lane = lax.broadcasted_iota(jnp.int32, (R, LANES), 1)

    @pl.when(s_id == 0)
    def _():
        m_ref[...] = jnp.zeros(m_ref.shape, F32)
        l_ref[...] = jnp.zeros(l_ref.shape, F32)
        sc_ref[...] = jnp.zeros(sc_ref.shape, F32)

    heads = range(H)
    qs = [q_ref[h].astype(BF16) for h in heads]
    ks = [[jnp.concatenate([_plane(pages[b * ppb + e], h, n_planes) for e in range(ppb)], axis=0)
           for b in range(nblk)] for h in heads]
    km = [jnp.concatenate([jnp.mean(ks[h][b], axis=0, keepdims=True) for b in range(nblk)]
                          + [jnp.zeros((LANES - nblk, LANES), F32)], axis=0).astype(BF16) for h in heads]
    mean_sc = [_nt(qs[h], km[h]) for h in heads]
    raw = [[_nt(qs[h], ks[h][b].astype(BF16)) * scale for b in range(nblk)] for h in heads]
    mx = [[jnp.max(raw[h][b], axis=-1, keepdims=True) for b in range(nblk)] for h in heads]
    pr = [[jnp.exp(raw[h][b] - mx[h][b]) for b in range(nblk)] for h in heads]
    sm = [[jnp.sum(pr[h][b], axis=-1, keepdims=True) for b in range(nblk)] for h in heads]
    for h in heads:
        m_t, l_t, s_t = m_ref[h], l_ref[h], sc_ref[h]
        for b in range(nblk):
            v = jnp.concatenate([_plane(pages[b * ppb + e], H + h, n_planes) for e in range(ppb)],
                                axis=0).astype(BF16)
            acc_ref[h, b] = _mm(pr[h][b].astype(BF16), v)
            here = lane == s_id * nblk + b
            m_t = jnp.where(here, mx[h][b], m_t)
            l_t = jnp.where(here, sm[h][b], l_t)
            s_t = jnp.where(here, _lane_pick(mean_sc[h], b), s_t)
        m_ref[h], l_ref[h], sc_ref[h] = m_t, l_t, s_t


def _moba_scan(q, cache, page_table, l, *, P):
    Bd, H, R, _ = q.shape
    n_pages = page_table.shape[1]
    depth = cache.shape[1]
    ppb = MOBA_BLOCK // PAGE_SIZE
    nblk = n_pages // ppb
    prow = PAGE_SIZE * 2 * H
    view = _flat_pages(cache)
    assert n_pages % P == 0 and P % ppb == 0 and nblk <= LANES and cache.shape[2] == PAGE_SIZE

    def page_map(b, s, pt, r):
        return (pt[b, s * P + r] * depth + l, 0)

    stat = pl.BlockSpec((None, H, R, LANES), lambda b, s, pt: (b, 0, 0, 0))
    stat_shape = jax.ShapeDtypeStruct((Bd, H, R, LANES), F32)
    return pl.pallas_call(
        functools.partial(_moba_scan_body, P=P),
        grid_spec=pltpu.PrefetchScalarGridSpec(
            num_scalar_prefetch=1, grid=(Bd, n_pages // P),
            in_specs=[pl.BlockSpec((None, H, R, LANES), lambda b, s, pt: (b, 0, 0, 0))]
            + [pl.BlockSpec((prow, LANES), functools.partial(page_map, r=r)) for r in range(P)],
            out_specs=[pl.BlockSpec((None, H, P // ppb, R, LANES), lambda b, s, pt: (b, 0, s, 0, 0)),
                       stat, stat, stat]),
        out_shape=[jax.ShapeDtypeStruct((Bd, H, nblk, R, LANES), F32), stat_shape, stat_shape, stat_shape],
        compiler_params=_cparams(("parallel", "arbitrary")),
        name="moba_scan",
    )(page_table, q, *([view] * P))


def _moba_combine_body(q_ref, acc_ref, m_ref, l_ref, sc_ref, kn_ref, vn_ref, o_ref, *, nblk, n_new):
    H, R = MOBA_HEADS, DEC_POS
    scale = HEAD_DIM ** -0.5
    lane = lax.broadcasted_iota(jnp.int32, (R, LANES), 1)
    rpos = lax.broadcasted_iota(jnp.int32, (R, LANES), 0) % DEC_POS
    for h in range(H):
        score = jnp.where(lane < nblk, sc_ref[h], -jnp.inf)
        sel = (_rank_desc(score, nblk, 1) < min(MOBA_TOPK, nblk)) & (lane < nblk)
        m_blk = m_ref[h]
        q = q_ref[h].astype(BF16)
        s_new = _nt(q, kn_ref[:, h * LANES:(h + 1) * LANES].astype(BF16)) * scale
        ok_new = (lane <= rpos) & (lane < n_new)
        m_new = jnp.max(jnp.where(ok_new, s_new, NEG), axis=-1, keepdims=True)
        m_all = jnp.maximum(jnp.max(jnp.where(sel, m_blk, NEG), axis=-1, keepdims=True), m_new)
        p_new = jnp.where(ok_new, jnp.exp(s_new - m_all), 0.0)
        w = jnp.where(sel, jnp.exp(m_blk - m_all), 0.0)
        den = jnp.sum(w * l_ref[h], axis=-1, keepdims=True) + jnp.sum(p_new, axis=-1, keepdims=True)
        num = _mm(p_new.astype(BF16), vn_ref[:, h * LANES:(h + 1) * LANES].astype(BF16))
        for b in range(nblk):
            num = num + w[:, b:b + 1] * acc_ref[h, b]
        o_ref[h] = num / den


def _moba_combine(q, acc, m, lsum, sc, knew, vnew, *, n_new):
    Bd, H, nblk, R, _ = acc.shape

    def b4(a):
        return pl.BlockSpec((None,) + a.shape[1:], lambda b: (b,) + (0,) * (a.ndim - 1))

    return pl.pallas_call(
        functools.partial(_moba_combine_body, nblk=nblk, n_new=n_new),
        grid=(Bd,),
        in_specs=[b4(q), b4(acc), b4(m), b4(lsum), b4(sc), b4(knew), b4(vnew)],
        out_specs=pl.BlockSpec((None, H, R, LANES), lambda b: (b, 0, 0, 0)),
        out_shape=jax.ShapeDtypeStruct((Bd, H, R, LANES), F32),
        compiler_params=_cparams(("parallel",)),
        name="moba_combine",
    )(q, acc, m, lsum, sc, knew, vnew)


def _paged_attn_body(ptM_ref, ptm_ref, *refs, P, H, R, scale, blocksize, window, n_new):
    q_ref = refs[0]
    pages = refs[1:1 + P]
    rest = refs[1 + P:]
    if blocksize is not None:
        bias_ref, rest = rest[0], rest[1:]
    kn_ref, vn_ref, o_ref, m_ref, l_ref, acc_ref = rest
    s_id = pl.program_id(1)
    nk = P * PAGE_SIZE
    n_planes = 2 * H

    @pl.when(s_id == 0)
    def _():
        _online_init(m_ref, l_ref, acc_ref)

    if blocksize is not None:
        kblk = lax.broadcasted_iota(jnp.int32, (nk, LANES), 0) // blocksize
        onehot = (kblk == lax.broadcasted_iota(jnp.int32, (nk, LANES), 1)).astype(BF16)
    if window:
        kidx = s_id * nk + lax.broadcasted_iota(jnp.int32, (R, nk), 1)
        in_window = kidx > lax.broadcasted_iota(jnp.int32, (R, nk), 0) % DEC_POS
    heads = range(H)
    scores = []
    for h in heads:
        q = q_ref[h].astype(BF16)
        k = jnp.concatenate([_plane(pg, h, n_planes) for pg in pages], axis=0).astype(BF16)
        if blocksize is not None:
            q = jnp.concatenate([q, bias_ref[h].astype(BF16)], axis=1)
            k = jnp.concatenate([k, onehot], axis=1)
        s = _nt(q, k) * scale
        scores.append(jnp.where(in_window, s, NEG) if window else s)
    m_prev = [m_ref[h] for h in heads]
    m_new = [jnp.maximum(m_prev[h], jnp.max(scores[h], axis=-1, keepdims=True)) for h in heads]
    alpha = [jnp.exp(m_prev[h] - m_new[h]) for h in heads]
    probs = [jnp.exp(scores[h] - m_new[h]) for h in heads]
    l_new = [alpha[h] * l_ref[h] + jnp.sum(probs[h], axis=-1, keepdims=True) for h in heads]
    pv = []
    for h in heads:
        v = jnp.concatenate([_plane(pg, H + h, n_planes) for pg in pages], axis=0).astype(BF16)
        pv.append(_mm(probs[h].astype(BF16), v))
    for h in heads:
        acc_ref[h] = alpha[h] * acc_ref[h] + pv[h]
        m_ref[h] = m_new[h]
        l_ref[h] = l_new[h]

    @pl.when(s_id == pl.num_programs(1) - 1)
    def _():
        lane = lax.broadcasted_iota(jnp.int32, (R, PAGE_SIZE), 1)
        rpos = lax.broadcasted_iota(jnp.int32, (R, PAGE_SIZE), 0) % DEC_POS
        for h in range(H):
            kn = kn_ref[:, h * LANES:(h + 1) * LANES].astype(BF16)
            vn = vn_ref[:, h * LANES:(h + 1) * LANES].astype(BF16)
            s = _nt(q_ref[h].astype(BF16), kn) * scale
            s = jnp.where((lane <= rpos) & (lane < n_new), s, NEG)
            _online_update(s, vn, m_ref.at[h], l_ref.at[h], acc_ref.at[h])
            o_ref[h] = acc_ref[h] / l_ref[h]


def _paged_attn(q, cache, ptM, ptm, l, knew, vnew, bias=None, *, scale, blocksize=None, window=False, P, n_new):
    Bd, H, R, _ = q.shape
    n_pages = ptM.shape[1]
    depth = cache.shape[1]
    ppe = cache.shape[2] // PAGE_SIZE
    prow = PAGE_SIZE * 2 * H
    view = _flat_pages(cache)
    assert n_pages % P == 0 and cache.shape[4] == H

    def page_map(b, s, pM, pm, r):
        return ((pM[b, s * P + r] * depth + l) * ppe + pm[b, s * P + r], 0)

    in_specs = [pl.BlockSpec((None, H, R, LANES), lambda b, s, pM, pm: (b, 0, 0, 0))]
    in_specs += [pl.BlockSpec((prow, LANES), functools.partial(page_map, r=r)) for r in range(P)]
    args = [q] + [view] * P
    if blocksize is not None:
        in_specs.append(pl.BlockSpec((None, H, None, R, LANES), lambda b, s, pM, pm: (b, 0, s, 0, 0)))
        args.append(bias)
    for a in (knew, vnew):
        in_specs.append(pl.BlockSpec((None,) + a.shape[1:], lambda b, s, pM, pm: (b, 0, 0)))
        args.append(a)
    return pl.pallas_call(
        functools.partial(_paged_attn_body, P=P, H=H, R=R, scale=scale, blocksize=blocksize, window=window,
                          n_new=n_new),
        grid_spec=pltpu.PrefetchScalarGridSpec(
            num_scalar_prefetch=2, grid=(Bd, n_pages // P), in_specs=in_specs,
            out_specs=pl.BlockSpec((None, H, R, LANES), lambda b, s, pM, pm: (b, 0, 0, 0)),
            scratch_shapes=[pltpu.VMEM((H, R, 1), F32), pltpu.VMEM((H, R, 1), F32), pltpu.VMEM((H, R, LANES), F32)]),
        out_shape=jax.ShapeDtypeStruct((Bd, H, R, LANES), F32),
        compiler_params=_cparams(("parallel", "arbitrary")),
        name="paged_attn",
    )(ptM, ptm, *args)


def _dec_finalize_body(ocmp_ref, oslc_ref, owin_ref, gate_ref, odiff_ref, lam_ref, sub_ref, omoba_ref,
                       an_ref, ad_ref, am_ref, *, lam_init):
    gt = gate_ref[0:DEC_POS, :]
    an_ref[...] = jnp.zeros(an_ref.shape, F32)
    ad_ref[...] = jnp.zeros(ad_ref.shape, F32)
    am_ref[...] = jnp.zeros(am_ref.shape, F32)
    for g in range(NSA_GROUPS):
        for h in range(NSA_REP):
            head = NSA_REP * g + h
            rows = slice(h * DEC_POS, (h + 1) * DEC_POS)
            out = (_lane_pick(gt, head) * ocmp_ref[g, rows, :]
                   + _lane_pick(gt, NSA_HEADS + head) * oslc_ref[g, rows, :]
                   + _lane_pick(gt, 2 * NSA_HEADS + head) * owin_ref[g, rows, :])
            an_ref[0:DEC_POS, head * LANES:(head + 1) * LANES] = out
    for h in range(DIFF_HEADS):
        o = odiff_ref[h]
        ad_ref[0:DEC_POS, h * LANES:(h + 1) * LANES] = _diff_finish(o[:DEC_POS], o[DEC_POS:], lam_ref, sub_ref, lam_init)
    for h in range(MOBA_HEADS):
        am_ref[0:DEC_POS, h * LANES:(h + 1) * LANES] = omoba_ref[h]


def _dec_finalize(o_cmp, o_slc, o_win, proj3, o_diff, lam, subln, o_moba, *, lam_init):
    Bd = o_cmp.shape[0]

    def b4(a):
        return pl.BlockSpec((None,) + a.shape[1:], lambda b: (b, 0, 0, 0))

    outs = [(NSA_HEADS * LANES), (DIFF_HEADS * LANES), (MOBA_HEADS * LANES)]
    return pl.pallas_call(
        functools.partial(_dec_finalize_body, lam_init=lam_init),
        grid=(Bd,),
        in_specs=[b4(o_cmp), b4(o_slc), b4(o_win),
                  pl.BlockSpec((None, DEC_ROWS, LANES), lambda b: (b, 0, BGATE)),
                  b4(o_diff), pl.BlockSpec(lam.shape, lambda b: (0, 0)), pl.BlockSpec((1, LANES), lambda b: (0, 0)),
                  b4(o_moba)],
        out_specs=[pl.BlockSpec((None, DEC_ROWS, c), lambda b: (b, 0, 0)) for c in outs],
        out_shape=[jax.ShapeDtypeStruct((Bd, DEC_ROWS, c), F32) for c in outs],
        compiler_params=_cparams(("parallel",)),
        name="dec_finalize",
    )(o_cmp, o_slc, o_win, proj3, o_diff, lam, subln.reshape(1, LANES), o_moba)


def _head_major(cols, heads):
    Bd = cols.shape[0]
    return cols[:, :DEC_POS].reshape(Bd, DEC_POS, heads, HEAD_DIM).transpose(0, 2, 1, 3)


def _new_rows(proj3, blk, heads):
    rows = _cols(proj3, blk, heads)
    return jnp.pad(rows, ((0, 0), (0, PAGE_SIZE - rows.shape[1]), (0, 0)))


def _sample_attn(proj3, caches, page_table, P, l, n_new, *, scan_pages=16, nsa_scan_pages=32):
    cache_cmp, cache_slc, cache_win, cache_diff, cache_moba = caches
    Bd = proj3.shape[0]
    n_pages = page_table.shape[1]
    past = n_pages * PAGE_SIZE
    wkeep = cache_win.shape[2]
    assert n_new <= DEC_POS and wkeep == NSA_WINDOW and past >= NSA_WINDOW
    win_pages = wkeep // PAGE_SIZE
    zeros_pt = jnp.zeros_like(page_table)
    win_major = jnp.broadcast_to(jnp.arange(Bd, dtype=jnp.int32)[:, None], (Bd, win_pages))
    win_minor = jnp.broadcast_to(jnp.arange(win_pages, dtype=jnp.int32)[None, :], (Bd, win_pages))
    half = NSA_CMP_STRIDE * HEAD_DIM

    def w1ab(w1):
        return jnp.concatenate([w1[:half].reshape(NSA_CMP_STRIDE, LANES, LANES),
                                w1[half:].reshape(NSA_CMP_STRIDE, LANES, LANES)], axis=-1)
    wab = jnp.stack([w1ab(P["nsa_w1_k"][l]), w1ab(P["nsa_w1_v"][l])]).astype(BF16)
    wab = wab.reshape(2, NSA_CMP_STRIDE // 2, 2 * LANES, 2 * LANES)
    ab = _cmp_scan(cache_cmp, page_table, wab, l, P=nsa_scan_pages)
    qn = _head_major(_cols(proj3, BQ, NSA_HEADS), NSA_HEADS).reshape(Bd, NSA_GROUPS, NSA_REP * DEC_POS, LANES)
    cw = (P["nsa_pe_k"][l], P["nsa_w1_k"][l], P["nsa_w2_k"][l],
          P["nsa_pe_v"][l], P["nsa_w1_v"][l], P["nsa_w2_v"][l])
    o_cmp, bias_slc = _nsa_dec_select(ab, qn, cw, past=past,
                                      per_step=nsa_scan_pages * PAGE_SIZE // NSA_SLC_LEN)
    sc128 = HEAD_DIM ** -0.5
    o_slc = _paged_attn(qn, cache_slc, page_table, zeros_pt, l, _new_rows(proj3, BKS, NSA_GROUPS),
                        _new_rows(proj3, BVS, NSA_GROUPS), bias_slc, scale=sc128, blocksize=NSA_SLC_LEN,
                        P=nsa_scan_pages, n_new=n_new)
    o_win = _paged_attn(qn, cache_win, win_major, win_minor, l, _new_rows(proj3, BKW, NSA_GROUPS),
                        _new_rows(proj3, BVW, NSA_GROUPS), scale=sc128, window=True, P=win_pages, n_new=n_new)
    qd = _head_major(_cols(proj3, BDQ, DIFF_HEADS), DIFF_HEADS)
    lane = jnp.arange(LANES)
    qd = jnp.concatenate([jnp.where(lane < DIFF_QK_DIM, qd, 0.0), jnp.where(lane >= DIFF_QK_DIM, qd, 0.0)], axis=2)
    o_diff = _paged_attn(qd, cache_diff, page_table, zeros_pt, l, _new_rows(proj3, BDK, DIFF_HEADS),
                         _new_rows(proj3, BDV, DIFF_HEADS), scale=DIFF_QK_DIM ** -0.5, P=2 * scan_pages,
                         n_new=n_new)
    qm = _head_major(_cols(proj3, BMQ, MOBA_HEADS), MOBA_HEADS)
    parts = _moba_scan(qm, cache_moba, page_table, l, P=scan_pages)
    o_moba = _moba_combine(qm, *parts, _new_rows(proj3, BMK, MOBA_HEADS), _new_rows(proj3, BMV, MOBA_HEADS),
                           n_new=n_new)
    return _dec_finalize(o_cmp, o_slc, o_win, proj3, o_diff, P["diff_lambda"][l], P["diff_subln"][l], o_moba,
                         lam_init=_lam_init(l))


def _sample_trunk(x_sample, caches, page_table, P):
    cache_cmp, cache_slc, cache_win, cache_diff, cache_moba, cache_mem = caches
    Bd, n_new, D = x_sample.shape
    depth = P["w_in"].shape[0]
    past = page_table.shape[1] * PAGE_SIZE
    pos = past + jnp.arange(DEC_ROWS, dtype=jnp.int32)
    tabs = [jnp.tile(t, (Bd, 1)) for t in _rope_tables(pos)]
    x = jnp.pad(x_sample, ((0, 0), (0, DEC_ROWS - n_new), (0, 0))).reshape(Bd * DEC_ROWS, D)
    mem4 = cache_mem.reshape(Bd, depth, MEM_TOKENS, 2 * MEM_HEADS * MEM_DIM)
    rows = [[] for _ in range(5)]
    wkeep = cache_win.shape[2]
    win_buf = _row_buffer(Bd, depth, wkeep, NSA_GROUPS)
    for l in range(depth):
        proj = _norm_matmul(x, P["norm_mix"][l], _regroup_w_in(P["w_in"][l]), tabs, tm=Bd * DEC_ROWS, tn=PROJ_TILE)
        proj3 = proj.reshape(Bd, DEC_ROWS, NPROJ)
        new3 = proj3[:, :n_new]
        rows[0].append(_cache_rows(new3, BKC, BVC, NSA_GROUPS))
        rows[1].append(_cache_rows(new3, BKS, BVS, NSA_GROUPS))
        win_buf = _shift_window(win_buf, cache_win,
                                _cache_rows(new3, BKW, BVW, NSA_GROUPS).reshape(Bd, -1, HEAD_DIM), l)
        rows[3].append(_cache_rows(new3, BDK, BDV, DIFF_HEADS))
        rows[4].append(_cache_rows(new3, BMK, BMV, MOBA_HEADS))
        a_nsa, a_diff, a_moba = _sample_attn(proj3, caches[:5], page_table, P, l, n_new)
        x3 = _out_mem(a_nsa, a_diff, a_moba, proj3, x.reshape(Bd, DEC_ROWS, D), P["w_out"][l].astype(BF16),
                      P["norm_mem_x"][l].reshape(1, D), P["w_mem_q"][l].astype(BF16), mem4[:, l],
                      P["w_mem_o"][l].astype(BF16), tm=DEC_ROWS)
        x = x3.reshape(Bd * DEC_ROWS, D)
    y = _rmsnorm(x, P["norm_final"], tm=Bd * DEC_ROWS).reshape(Bd, DEC_ROWS, D)[:, :n_new]
    stacked = [jnp.stack(r, axis=1) if r else _as_rows(win_buf, Bd, depth, wkeep, NSA_GROUPS) for r in rows]
    return y, stacked


def kernel(x_prompt, x_sample, mem_prompt, cache_nsa_cmp, cache_nsa_slc, cache_nsa_win, cache_diff, cache_moba, cache_mem, page_table, norm_mix, w_in, w_out, nsa_pe_k, nsa_pe_v, nsa_w1_k, nsa_w2_k, nsa_w1_v, nsa_w2_v, diff_lambda, diff_subln, norm_mem_x, norm_mem_m, w_mem_q, w_mem_kv, w_mem_o, norm_final):
    P = {"norm_mix": norm_mix, "w_in": w_in, "w_out": w_out, "nsa_pe_k": nsa_pe_k, "nsa_pe_v": nsa_pe_v,
         "nsa_w1_k": nsa_w1_k, "nsa_w2_k": nsa_w2_k, "nsa_w1_v": nsa_w1_v, "nsa_w2_v": nsa_w2_v,
         "diff_lambda": diff_lambda, "diff_subln": diff_subln, "norm_mem_x": norm_mem_x,
         "norm_mem_m": norm_mem_m, "w_mem_q": w_mem_q, "w_mem_kv": w_mem_kv, "w_mem_o": w_mem_o,
         "norm_final": norm_final}
    y_prompt, (p_cmp, p_slc, p_win, p_diff, p_moba), p_mem = _prompt_trunk(x_prompt, mem_prompt, P)
    caches = (cache_nsa_cmp, cache_nsa_slc, cache_nsa_win, cache_diff, cache_moba, cache_mem)
    y_sample, (s_cmp, s_slc, s_win, s_diff, s_moba) = _sample_trunk(x_sample, caches, page_table, P)
    return (y_prompt, y_sample, p_cmp, p_slc, p_win, p_diff, p_moba, p_mem, s_cmp, s_slc, s_win, s_diff, s_moba)
```

```python
import functools
import math

import jax
import jax.numpy as jnp
from jax import lax
from jax.experimental import pallas as pl
from jax.experimental.pallas import tpu as pltpu

F32 = jnp.float32
BF16 = jnp.bfloat16

D_MODEL = 2048
HEAD_DIM = 128
ROPE_THETA = 500000.0
NORM_EPS = 1e-6
PAGE_SIZE = 128

NSA_HEADS = 8
NSA_GROUPS = 2
NSA_REP = NSA_HEADS // NSA_GROUPS
NSA_CMP_STRIDE = 16
NSA_SLC_LEN = 64
NSA_TOPN = 16
NSA_WINDOW = 512
NSA_FORCE_BONUS = 1000.0
NSA_COL_GROUPS = 4
NSA_TQ = 512
DIFF_TQ = 512
PROJ_TILE = 1024
DIFF_HEADS = 4
DIFF_QK_DIM = HEAD_DIM // 2
MOBA_HEADS = 4
MOBA_BLOCK = 256
MOBA_TOPK = 3
MEM_TOKENS = 256
MEM_HEADS = 4
MEM_DIM = 128

NEG = -1e30
LANES = 128
SUBLANES = 8
VMEM_LIMIT_BYTES = 56 * 1024 * 1024

BQ, BKC, BKS, BKW, BMQ, BMK = 0, 8, 10, 12, 14, 18
BDQ, BDK = 22, 26
BVC, BVS, BVW, BNZ, BDV, BDZ, BMV, BMZ = 30, 32, 34, 36, 44, 48, 52, 56
BGATE = 60
NBLK = 64
NPROJ = NBLK * LANES
_W_IN_SEGMENTS = (
    (0, 1024), (1024, 1280), (1536, 1792), (2048, 2304), (5656, 6168), (6168, 6680),
    (3608, 4120), (4120, 4632),
    (1280, 1536), (1792, 2048), (2304, 2560), (2584, 3608), (4632, 5144), (5144, 5656),
    (6680, 7192), (7192, 7704),
    (2560, 2584),
)


def _round_up(n, m):
    return -(-n // m) * m


def _nt(a, b):
    return lax.dot_general(a, b, (((1,), (1,)), ((), ())), preferred_element_type=F32)


def _mm(a, b):
    return jnp.dot(a, b, preferred_element_type=F32)


def _silu(z):
    return z * jax.nn.sigmoid(z)


def _cparams(sem):
    return pltpu.CompilerParams(dimension_semantics=sem, vmem_limit_bytes=VMEM_LIMIT_BYTES)


def _norm_matmul_body(*refs, tn, splits, rope):
    if rope:
        x_ref, g_ref, w_ref, c1, a1, b1, c2, a2, b2, o_ref, h_ref = refs
    else:
        x_ref, g_ref, w_ref, o_ref, h_ref = refs
    j = pl.program_id(1)

    @pl.when(j == 0)
    def _():
        x = x_ref[...]
        ms = jnp.mean(x * x, axis=-1, keepdims=True)
        h_ref[...] = (x * lax.rsqrt(ms + NORM_EPS) * g_ref[...]).astype(BF16)

    if not rope:
        o_ref[...] = _mm(h_ref[...], w_ref[...])
        return

    def epilogue(kind, blk):
        if kind == "rot128":
            half = HEAD_DIM // 8
            return blk * c1[...] + pltpu.roll(blk, LANES - half, 1) * a1[...] + pltpu.roll(blk, half, 1) * b1[...]
        if kind == "rot64":
            half = DIFF_QK_DIM // 8
            return blk * c2[...] + pltpu.roll(blk, LANES - half, 1) * a2[...] + pltpu.roll(blk, half, 1) * b2[...]
        if kind == "gate":
            return jax.nn.sigmoid(blk)
        return blk

    parts = 2
    pw = tn // parts
    for lo, hi, kinds in splits:
        @pl.when((j >= lo) & (j <= hi))
        def _(kinds=kinds):
            h = h_ref[...]
            accs = [_mm(h, w_ref[:, s * pw:(s + 1) * pw]) for s in range(parts)]
            for k, kind in enumerate(kinds):
                s, o = divmod(k * LANES, pw)
                o_ref[:, k * LANES:(k + 1) * LANES] = epilogue(kind, accs[s][:, o:o + LANES])


def _tile_patterns(tn):
    kinds = (["rot128"] * (BDQ - BQ) + ["rot64"] * (BVC - BDQ) + ["plain"] * (BGATE - BVC) + ["gate"]
             + ["plain"] * (NBLK - BGATE - 1))
    per = tn // LANES
    tiles = [tuple(kinds[t * per:(t + 1) * per]) for t in range(NBLK // per)]
    runs = []
    for t, pat in enumerate(tiles):
        if runs and runs[-1][2] == pat:
            runs[-1] = (runs[-1][0], t, pat)
        else:
            runs.append((t, t, pat))
    return tuple(runs)


def _norm_matmul(x, gain, w, tabs=None, *, tm, tn):
    M, K = x.shape
    Np = w.shape[1]
    rope = tabs is not None
    splits = _tile_patterns(tn) if rope else None
    in_specs = [
        pl.BlockSpec((tm, K), lambda i, j: (i, 0)),
        pl.BlockSpec((1, K), lambda i, j: (0, 0)),
        pl.BlockSpec((K, tn), lambda i, j: (0, j)),
    ]
    args = [x, gain.reshape(1, K), w]
    if rope:
        period = tabs[0].shape[0] // tm
        for t in tabs:
            in_specs.append(pl.BlockSpec((tm, LANES), lambda i, j: (i % period, 0)))
            args.append(t)
    return pl.pallas_call(
        functools.partial(_norm_matmul_body, tn=tn, splits=splits, rope=rope),
        grid=(M // tm, Np // tn),
        in_specs=in_specs,
        out_specs=pl.BlockSpec((tm, tn), lambda i, j: (i, j)),
        out_shape=jax.ShapeDtypeStruct((M, Np), F32),
        scratch_shapes=[pltpu.VMEM((tm, K), BF16)],
        compiler_params=_cparams(("parallel", "arbitrary")),
        name="norm_matmul",
    )(*args)


def _rope_tables(pos):
    posf = pos.astype(F32)[:, None]
    n = pos.shape[0]

    def one(width, reps):
        rd = width // 4
        half = rd // 2
        inv = ROPE_THETA ** (-2.0 * jnp.arange(half, dtype=F32) / rd)
        ang = posf * inv[None, :]
        c, s = jnp.cos(ang), jnp.sin(ang)
        z = jnp.zeros((n, width - rd), F32)
        zh = jnp.zeros((n, half), F32)
        C = jnp.concatenate([c, c, jnp.ones((n, width - rd), F32)], axis=1)
        A = jnp.concatenate([-s, zh, z], axis=1)
        B = jnp.concatenate([zh, s, z], axis=1)
        return [jnp.tile(t, (1, reps)) for t in (C, A, B)]

    return one(HEAD_DIM, 1) + one(DIFF_QK_DIM, 2)


def _online_init(m_ref, l_ref, acc_ref):
    m_ref[...] = jnp.full(m_ref.shape, NEG, F32)
    l_ref[...] = jnp.zeros(l_ref.shape, F32)
    acc_ref[...] = jnp.zeros(acc_ref.shape, F32)


def _online_update(s, v, m_ref, l_ref, acc_ref):
    m_prev = m_ref[...]
    m_new = jnp.maximum(m_prev, jnp.max(s, axis=-1, keepdims=True))
    alpha = jnp.exp(m_prev - m_new)
    p = jnp.exp(s - m_new)
    l_ref[...] = alpha * l_ref[...] + jnp.sum(p, axis=-1, keepdims=True)
    acc_ref[...] = alpha * acc_ref[...] + _mm(p.astype(BF16), v)
    m_ref[...] = m_new


def _online_update_t(s, vt, m_ref, l_ref, acc_ref):
    m_prev = m_ref[...]
    m_new = jnp.maximum(m_prev, jnp.max(s, axis=0, keepdims=True))
    alpha = jnp.exp(m_prev - m_new)
    p = jnp.exp(s - m_new)
    l_ref[...] = alpha * l_ref[...] + jnp.sum(p, axis=0, keepdims=True)
    acc_ref[...] = alpha * acc_ref[...] + _mm(vt, p.astype(BF16))
    m_ref[...] = m_new


def _staged_update_t(scores, vts, m_ref, l_ref, acc_ref, idxs):
    n = range(len(scores))
    per_group = lambda r: r if isinstance(r, (list, tuple)) else [r] * len(scores)
    m_refs, l_refs, acc_refs = per_group(m_ref), per_group(l_ref), per_group(acc_ref)
    m_prev = [m_refs[j][idxs[j]] for j in n]
    m_new = [jnp.maximum(m_prev[j], jnp.max(scores[j], axis=0, keepdims=True)) for j in n]
    alpha = [jnp.exp(m_prev[j] - m_new[j]) for j in n]
    probs = [jnp.exp(scores[j] - m_new[j]) for j in n]
    l_new = [alpha[j] * l_refs[j][idxs[j]] + jnp.sum(probs[j], axis=0, keepdims=True) for j in n]
    pv = [_mm(vts[j], probs[j].astype(BF16)) for j in n]
    for j in n:
        acc_refs[j][idxs[j]] = alpha[j] * acc_refs[j][idxs[j]] + pv[j]
        m_refs[j][idxs[j]] = m_new[j]
        l_refs[j][idxs[j]] = l_new[j]


def _masked_softmax(s, mask, axis):
    m = jnp.max(jnp.where(mask, s, NEG), axis=axis, keepdims=True)
    m = jnp.where(m > 0.5 * NEG, m, 0.0)
    e = jnp.where(mask, jnp.exp(s - m), 0.0)
    return e / jnp.maximum(jnp.sum(e, axis=axis, keepdims=True), 1e-30)


def _rank_desc(score, ncand, axis):
    idx = lax.broadcasted_iota(jnp.int32, score.shape, axis)
    rank = jnp.zeros(score.shape, F32)
    for c in range(ncand):
        cand = score[:, c:c + 1] if axis == 1 else score[c:c + 1, :]
        before = (cand > score) | ((cand == score) & (c < idx))
        rank = rank + before.astype(F32)
    return rank


def _lane_pick(x, idx):
    lane = lax.broadcasted_iota(jnp.int32, x.shape, 1)
    return jnp.sum(jnp.where(lane == idx, x, 0.0), axis=-1, keepdims=True)


def _compress(src_ref, pe_ref, w1_ref, w2_ref, nc):
    half = NSA_CMP_STRIDE * HEAD_DIM
    acc_a = jnp.zeros((nc, LANES), F32)
    acc_b = jnp.zeros((nc, LANES), F32)
    for j in range(NSA_CMP_STRIDE):
        rows = src_ref[pl.ds(j, nc, stride=NSA_CMP_STRIDE), :].astype(BF16)
        acc_a = acc_a + _mm(rows, w1_ref[j * LANES:(j + 1) * LANES, :].astype(BF16))
        acc_b = acc_b + _mm(rows, w1_ref[half + j * LANES:half + (j + 1) * LANES, :].astype(BF16))
    return _compress_finish(acc_a, acc_b, pe_ref, w1_ref, w2_ref, nc)


def _pe_term(pe_ref, w1_ref):
    acc = jnp.zeros((SUBLANES, LANES), F32)
    for j in range(2 * NSA_CMP_STRIDE):
        pj = jnp.broadcast_to(pe_ref[j:j + 1, :], (SUBLANES, LANES)).astype(BF16)
        acc = acc + _mm(pj, w1_ref[j * LANES:(j + 1) * LANES, :].astype(BF16))
    return acc[0:1, :]


def _compress_finish(acc_a, acc_b, pe_ref, w1_ref, w2_ref, nc):
    row = lax.broadcasted_iota(jnp.int32, (nc, LANES), 0)
    hb = jnp.where(row < nc - 1, pltpu.roll(acc_b, nc - 1, 0), 0.0)
    hid = _silu(acc_a + hb + _pe_term(pe_ref, w1_ref))
    return _mm(hid.astype(BF16), w2_ref[...].astype(BF16))


def _slc_weights(shape, cmp_axis):
    r = lax.broadcasted_iota(jnp.int32, shape, cmp_axis)
    c = lax.broadcasted_iota(jnp.int32, shape, 1 - cmp_axis)
    d = r - 4 * c
    return jnp.where((d == -1) | (d == 3), 1.0, jnp.where((d >= 0) & (d <= 2), 2.0, 0.0)).astype(F32)


def _transpose_into(dst_ref, src_ref, rows, chunk):
    for c in range(rows // chunk):
        dst_ref[:, c * chunk:(c + 1) * chunk] = src_ref[c * chunk:(c + 1) * chunk, :].T.astype(BF16)


def _nsa_prompt_body(q_ref, kc_ref, ks_ref, kw_ref, vc_ref, vs_ref, vw_ref, gate_ref,
                     pek_ref, w1k_ref, w2k_ref, pev_ref, w1v_ref, w2v_ref, o_ref,
                     kcs, vct, kaug, vst, kwb, vwt, gt_ref, m_ref, l_ref, acc_ref, m2_ref, l2_ref, acc2_ref,
                     *, T, tq):
    g = pl.program_id(1)
    i = pl.program_id(2)
    nc = T // NSA_CMP_STRIDE
    nsb = T // NSA_SLC_LEN
    nsbr = _round_up(nsb, SUBLANES)
    nsel = min(NSA_TOPN, nsb)
    R = NSA_REP * tq
    scale = HEAD_DIM ** -0.5

    @pl.when(i == 0)
    def _():
        kcs[...] = _compress(kc_ref, pek_ref, w1k_ref, w2k_ref, nc).astype(BF16)
        vct[...] = _compress(vc_ref, pev_ref, w1v_ref, w2v_ref, nc).T.astype(BF16)
        kaug[:, :LANES] = ks_ref[...].astype(BF16)
        rblk = lax.broadcasted_iota(jnp.int32, (T, LANES), 0) // NSA_SLC_LEN
        lane = lax.broadcasted_iota(jnp.int32, (T, LANES), 1)
        kaug[:, LANES:] = (rblk == lane).astype(BF16)
        kwb[...] = kw_ref[...].astype(BF16)
        _transpose_into(vst, vs_ref, T, tq)
        _transpose_into(vwt, vw_ref, T, tq)

    q = q_ref[...]
    qt = jnp.concatenate([q[:, h * LANES:(h + 1) * LANES].T for h in range(NSA_REP)], axis=1).astype(BF16)

    s = _mm(kcs[...], qt) * scale
    crow = lax.broadcasted_iota(jnp.int32, (nc, R), 0)
    t_col = i * tq + lax.broadcasted_iota(jnp.int32, (nc, R), 1) % tq
    pc = _masked_softmax(s, (NSA_CMP_STRIDE * crow + 2 * NSA_CMP_STRIDE - 1) <= t_col, 0)
    o_cmp = _mm(vct[...], pc.astype(BF16))
    pg = pc[:, 0:tq]
    for h in range(1, NSA_REP):
        pg = pg + pc[:, h * tq:(h + 1) * tq]

    slc = jnp.dot(_slc_weights((LANES, nc), 1), pg, precision=lax.Precision.HIGHEST,
                  preferred_element_type=F32)[:nsbr]
    blk = lax.broadcasted_iota(jnp.int32, (nsbr, tq), 0)
    tb = (i * tq + lax.broadcasted_iota(jnp.int32, (nsbr, tq), 1)) // NSA_SLC_LEN
    valid = blk <= tb
    forced = (blk == 0) | (blk == tb) | (blk == tb - 1)
    score = jnp.where(valid, slc + jnp.where(forced, NSA_FORCE_BONUS, 0.0), -jnp.inf)
    sel = (_rank_desc(score, nsb, 0) < nsel) & valid
    bias = jnp.where(sel, 0.0, NEG)
    if nsbr < LANES:
        bias = jnp.concatenate([bias, jnp.full((LANES - nsbr, tq), NEG, F32)], axis=0)
    bias = bias.astype(BF16)
    qaug = jnp.concatenate([qt, jnp.concatenate([bias] * NSA_REP, axis=1)], axis=0)

    gw = R // NSA_COL_GROUPS
    cols = [(slice(None), slice(j * gw, (j + 1) * gw)) for j in range(NSA_COL_GROUPS)]
    krow = lax.broadcasted_iota(jnp.int32, (tq, gw), 0)
    tloc = lax.broadcasted_iota(jnp.int32, (tq, gw), 1) % tq

    slc_state = (m_ref, l_ref, acc_ref)
    win_state = (m2_ref, l2_ref, acc2_ref)

    def attend(jobs):
        scores, vts, states = [], [], []
        for k_tile, vt_tile, q_all, mask, state in jobs:
            for c in cols:
                s2 = _mm(k_tile, q_all[c]) * scale
                scores.append(s2 if mask is None else jnp.where(mask, s2, NEG))
                vts.append(vt_tile)
                states.append(state)
        _staged_update_t(scores, vts, [s[0] for s in states], [s[1] for s in states], [s[2] for s in states],
                         cols * len(jobs))

    def slc_job(kt, mask):
        off = pl.multiple_of(kt * tq, tq)
        return (kaug[pl.ds(off, tq), :], vst[:, pl.ds(off, tq)], qaug, mask, slc_state)

    nw = NSA_WINDOW // tq

    def win_job(d):
        off = pl.multiple_of((i - d) * tq, tq)
        mask = (krow > tloc) if d == nw else ((krow <= tloc) if d == 0 else None)
        return (kwb[pl.ds(off, tq), :], vwt[:, pl.ds(off, tq)], qt, mask, win_state)

    _online_init(*slc_state)
    _online_init(*win_state)

    def slc_step(kt, carry):
        attend([slc_job(kt, None)])
        return carry

    lax.fori_loop(0, jnp.maximum(i - 1, 0), slc_step, 0)
    for d in range(nw, 1, -1):
        @pl.when(i >= d)
        def _(d=d):
            attend([win_job(d)])

    @pl.when(i >= 1)
    def _():
        attend([slc_job(i - 1, None), win_job(1)])

    attend([slc_job(i, krow <= tloc), win_job(0)])
    o_slc = acc_ref[...] / l_ref[...]
    o_win = acc2_ref[...] / l2_ref[...]

    gt_ref[...] = gate_ref[...].T

    def gate_row(branch):
        return jnp.concatenate(
            [gt_ref[pl.ds(branch * NSA_HEADS + NSA_REP * g + h, 1), :] for h in range(NSA_REP)], axis=1)

    out = gate_row(0) * o_cmp + gate_row(1) * o_slc + gate_row(2) * o_win
    for h in range(NSA_REP):
        o_ref[:, h * LANES:(h + 1) * LANES] = out[:, h * tq:(h + 1) * tq].T


def _nsa_prompt(proj, cw, *, B, T, tq):
    nqt = T // tq
    nc = T // NSA_CMP_STRIDE
    R = NSA_REP * tq
    assert T % NSA_SLC_LEN == 0 and NSA_WINDOW % tq == 0 and T // NSA_SLC_LEN <= LANES and nc <= LANES

    def kv_spec(blk):
        return pl.BlockSpec((T, LANES), lambda b, g, i: (b, blk + g))

    def full(a):
        return pl.BlockSpec(a.shape, lambda b, g, i: (0,) * a.ndim)

    in_specs = [
        pl.BlockSpec((tq, NSA_REP * LANES), lambda b, g, i: (b * nqt + i, g)),
        kv_spec(BKC), kv_spec(BKS), kv_spec(BKW), kv_spec(BVC), kv_spec(BVS), kv_spec(BVW),
        pl.BlockSpec((tq, LANES), lambda b, g, i: (b * nqt + i, BGATE)),
    ] + [full(a) for a in cw]
    return pl.pallas_call(
        functools.partial(_nsa_prompt_body, T=T, tq=tq),
        grid=(B, NSA_GROUPS, nqt),
        in_specs=in_specs,
        out_specs=pl.BlockSpec((tq, NSA_REP * LANES), lambda b, g, i: (b * nqt + i, g)),
        out_shape=jax.ShapeDtypeStruct((B * T, NSA_HEADS * LANES), F32),
        scratch_shapes=[
            pltpu.VMEM((nc, LANES), BF16), pltpu.VMEM((LANES, nc), BF16),
            pltpu.VMEM((T, 2 * LANES), BF16), pltpu.VMEM((LANES, T), BF16),
            pltpu.VMEM((T, LANES), BF16), pltpu.VMEM((LANES, T), BF16),
            pltpu.VMEM((LANES, tq), F32),
            pltpu.VMEM((1, R), F32), pltpu.VMEM((1, R), F32), pltpu.VMEM((LANES, R), F32),
            pltpu.VMEM((1, R), F32), pltpu.VMEM((1, R), F32), pltpu.VMEM((LANES, R), F32),
        ],
        compiler_params=_cparams(("parallel", "parallel", "arbitrary")),
        name="nsa_prompt",
    )(proj, proj, proj, proj, proj, proj, proj, proj, *cw)


def _diff_lambda(lam_ref, lam_init):
    lp = lam_ref[...]
    return (jnp.exp(jnp.sum(lp[0:1] * lp[1:2], axis=-1, keepdims=True))
            - jnp.exp(jnp.sum(lp[2:3] * lp[3:4], axis=-1, keepdims=True)) + lam_init)


def _diff_finish(o1, o2, lam_ref, sub_ref, lam_init):
    a = o1 - _diff_lambda(lam_ref, lam_init) * o2
    ms = jnp.mean(a * a, axis=-1, keepdims=True)
    return a * lax.rsqrt(ms + NORM_EPS) * sub_ref[...] * (1.0 - lam_init)


def _head_specs(nqt, T, tq, qblk, kblk, vblk, heads):
    specs = [pl.BlockSpec((tq, LANES), functools.partial(lambda b, i, c: (b * nqt + i, c), c=qblk + h))
             for h in range(heads)]
    for blk in (kblk, vblk):
        specs += [pl.BlockSpec((T, LANES), functools.partial(lambda b, i, c: (b, c), c=blk + h))
                  for h in range(heads)]
    return specs


def _diff_prompt_body(*refs, T, tq, lam_init):
    nh = DIFF_HEADS
    q_refs, k_refs, v_refs = refs[:nh], refs[nh:2 * nh], refs[2 * nh:3 * nh]
    lam_ref, sub_ref, o_ref, kb, vt, m_ref, l_ref, acc_ref = refs[3 * nh:]
    i = pl.program_id(1)
    scale = DIFF_QK_DIM ** -0.5
    heads = range(nh)

    @pl.when(i == 0)
    def _():
        for h in heads:
            kb[h] = k_refs[h][...].astype(BF16)
            _transpose_into(vt.at[h], v_refs[h], T, tq)

    qts = []
    for h in heads:
        q = q_refs[h][...]
        lane = lax.broadcasted_iota(jnp.int32, q.shape, 1)
        qts.append(jnp.concatenate([jnp.where(lane < DIFF_QK_DIM, q, 0.0).T,
                                    jnp.where(lane >= DIFF_QK_DIM, q, 0.0).T], axis=1).astype(BF16))
    R = 2 * tq
    krow = lax.broadcasted_iota(jnp.int32, (tq, R), 0)
    tloc = lax.broadcasted_iota(jnp.int32, (tq, R), 1) % tq
    idxs = [(h,) for h in heads]
    _online_init(m_ref, l_ref, acc_ref)

    def attend(off, mask):
        scores = []
        for h in heads:
            s = _mm(kb[h, pl.ds(off, tq), :], qts[h]) * scale
            scores.append(s if mask is None else jnp.where(mask, s, NEG))
        _staged_update_t(scores, [vt[h, :, pl.ds(off, tq)] for h in heads], m_ref, l_ref, acc_ref, idxs)

    def step(kt, carry):
        attend(pl.multiple_of(kt * tq, tq), None)
        return carry

    lax.fori_loop(0, i, step, 0)
    attend(pl.multiple_of(i * tq, tq), krow <= tloc)
    for h in heads:
        o = acc_ref[h] / l_ref[h]
        o_ref[:, h * LANES:(h + 1) * LANES] = _diff_finish(o[:, :tq].T, o[:, tq:].T, lam_ref, sub_ref, lam_init)


def _diff_prompt(proj, lam, subln, *, B, T, tq, lam_init):
    nqt = T // tq
    nh = DIFF_HEADS
    return pl.pallas_call(
        functools.partial(_diff_prompt_body, T=T, tq=tq, lam_init=lam_init),
        grid=(B, nqt),
        in_specs=_head_specs(nqt, T, tq, BDQ, BDK, BDV, nh) + [
            pl.BlockSpec(lam.shape, lambda b, i: (0, 0)),
            pl.BlockSpec((1, LANES), lambda b, i: (0, 0)),
        ],
        out_specs=pl.BlockSpec((tq, nh * LANES), lambda b, i: (b * nqt + i, 0)),
        out_shape=jax.ShapeDtypeStruct((B * T, nh * LANES), F32),
        scratch_shapes=[
            pltpu.VMEM((nh, T, LANES), BF16), pltpu.VMEM((nh, LANES, T), BF16),
            pltpu.VMEM((nh, 1, 2 * tq), F32), pltpu.VMEM((nh, 1, 2 * tq), F32),
            pltpu.VMEM((nh, LANES, 2 * tq), F32),
        ],
        compiler_params=_cparams(("parallel", "arbitrary")),
        name="diff_prompt",
    )(*([proj] * (3 * nh)), lam, subln.reshape(1, LANES))


def _moba_prompt_body(*refs, T):
    nh = MOBA_HEADS
    q_refs, k_refs, v_refs = refs[:nh], refs[nh:2 * nh], refs[2 * nh:3 * nh]
    o_ref, kb, vt, km, bias_ref, m_ref, l_ref, acc_ref = refs[3 * nh:]
    i = pl.program_id(1)
    tq = MOBA_BLOCK
    nb = T // MOBA_BLOCK
    nbr = _round_up(nb, SUBLANES)
    scale = HEAD_DIM ** -0.5
    heads = range(nh)

    @pl.when(i == 0)
    def _():
        km[...] = jnp.zeros(km.shape, BF16)
        for h in heads:
            kb[h] = k_refs[h][...].astype(BF16)
            _transpose_into(vt.at[h], v_refs[h], T, tq)
            for j in range(nb):
                km[h, j:j + 1, :] = jnp.mean(k_refs[h][j * tq:(j + 1) * tq, :], axis=0, keepdims=True).astype(BF16)

    qts = [q_refs[h][...].T.astype(BF16) for h in heads]
    blk = lax.broadcasted_iota(jnp.int32, (nbr, tq), 0)
    past = blk < i
    for h in heads:
        score = jnp.where(past, _mm(km[h], qts[h])[:nbr], -jnp.inf)
        sel = (_rank_desc(score, nb, 0) < min(MOBA_TOPK, nb)) & past
        bias_ref[h] = jnp.where(sel, 0.0, NEG)
    krow = lax.broadcasted_iota(jnp.int32, (tq, tq), 0)
    tloc = lax.broadcasted_iota(jnp.int32, (tq, tq), 1)
    idxs = [(h,) for h in heads]
    _online_init(m_ref, l_ref, acc_ref)

    def step(kt, carry):
        off = pl.multiple_of(kt * tq, tq)
        scores = [_mm(kb[h, pl.ds(off, tq), :], qts[h]) * scale + bias_ref[h, pl.ds(kt, 1), :] for h in heads]
        _staged_update_t(scores, [vt[h, :, pl.ds(off, tq)] for h in heads], m_ref, l_ref, acc_ref, idxs)
        return carry

    lax.fori_loop(0, i, step, 0)
    off = pl.multiple_of(i * tq, tq)
    scores = [jnp.where(krow <= tloc, _mm(kb[h, pl.ds(off, tq), :], qts[h]) * scale, NEG) for h in heads]
    _staged_update_t(scores, [vt[h, :, pl.ds(off, tq)] for h in heads], m_ref, l_ref, acc_ref, idxs)
    for h in heads:
        o_ref[:, h * LANES:(h + 1) * LANES] = (acc_ref[h] / l_ref[h]).T


def _moba_prompt(proj, *, B, T):
    tq = MOBA_BLOCK
    assert T % tq == 0 and T // tq <= LANES
    nqt = T // tq
    nh = MOBA_HEADS
    nbr = _round_up(T // MOBA_BLOCK, SUBLANES)
    return pl.pallas_call(
        functools.partial(_moba_prompt_body, T=T),
        grid=(B, nqt),
        in_specs=_head_specs(nqt, T, tq, BMQ, BMK, BMV, nh),
        out_specs=pl.BlockSpec((tq, nh * LANES), lambda b, i: (b * nqt + i, 0)),
        out_shape=jax.ShapeDtypeStruct((B * T, nh * LANES), F32),
        scratch_shapes=[
            pltpu.VMEM((nh, T, LANES), BF16), pltpu.VMEM((nh, LANES, T), BF16), pltpu.VMEM((nh, LANES, LANES), BF16),
            pltpu.VMEM((nh, nbr, tq), F32),
            pltpu.VMEM((nh, 1, tq), F32), pltpu.VMEM((nh, 1, tq), F32), pltpu.VMEM((nh, LANES, tq), F32),
        ],
        compiler_params=_cparams(("parallel", "arbitrary")),
        name="moba_prompt",
    )(*([proj] * (3 * nh)))


def _out_mem_body(an_ref, ad_ref, am_ref, nz0_ref, nz1_ref, dz_ref, mz_ref, x_ref, wout_ref, gmem_ref,
                  wq_ref, mkv_ref, wo_ref, o_ref):
    half = NSA_HEADS * LANES // 2
    an = an_ref[...]
    mixed = jnp.concatenate([
        an[:, :half] * _silu(nz0_ref[...]), an[:, half:] * _silu(nz1_ref[...]),
        ad_ref[...] * _silu(dz_ref[...]), am_ref[...] * _silu(mz_ref[...])], axis=1).astype(BF16)
    x1 = x_ref[...] + _mm(mixed, wout_ref[...])
    ms = jnp.mean(x1 * x1, axis=-1, keepdims=True)
    h2 = (x1 * lax.rsqrt(ms + NORM_EPS) * gmem_ref[...]).astype(BF16)
    q = _mm(h2, wq_ref[...])
    mkv = mkv_ref[...]
    scale = MEM_DIM ** -0.5
    outs = []
    for hh in range(MEM_HEADS):
        qh = q[:, hh * LANES:(hh + 1) * LANES].astype(BF16)
        kh = mkv[:, hh * LANES:(hh + 1) * LANES].astype(BF16)
        vh = mkv[:, (MEM_HEADS + hh) * LANES:(MEM_HEADS + hh + 1) * LANES].astype(BF16)
        s = _nt(qh, kh) * scale
        e = jnp.exp(s - jnp.max(s, axis=-1, keepdims=True))
        p = e / jnp.sum(e, axis=-1, keepdims=True)
        outs.append(_mm(p.astype(BF16), vh))
    oc = jnp.concatenate(outs, axis=1).astype(BF16)
    o_ref[...] = x1 + _mm(oc, wo_ref[...])


def _out_mem(a_nsa, a_diff, a_moba, proj3, x3, wout, gmem, wq, mkv, wo, *, tm):
    nb, rows, D = x3.shape
    grid = (nb, rows // tm)

    def row(cols, cb):
        return pl.BlockSpec((None, tm, cols), lambda b, i: (b, i, cb))

    def const(a):
        return pl.BlockSpec(a.shape, lambda b, i: (0,) * a.ndim, pipeline_mode=pl.Buffered(1))

    zc = 4 * LANES
    return pl.pallas_call(
        _out_mem_body,
        grid=grid,
        in_specs=[
            row(NSA_HEADS * LANES, 0), row(DIFF_HEADS * LANES, 0), row(MOBA_HEADS * LANES, 0),
            row(zc, BNZ * LANES // zc), row(zc, BNZ * LANES // zc + 1), row(zc, BDZ * LANES // zc),
            row(zc, BMZ * LANES // zc),
            row(D, 0), const(wout), const(gmem), const(wq),
            pl.BlockSpec((None, MEM_TOKENS, 2 * MEM_HEADS * MEM_DIM), lambda b, i: (b, 0, 0)),
            const(wo),
        ],
        out_specs=row(D, 0),
        out_shape=jax.ShapeDtypeStruct(x3.shape, F32),
        compiler_params=_cparams(("parallel", "parallel")),
        name="out_mem",
    )(a_nsa, a_diff, a_moba, proj3, proj3, proj3, proj3, x3, wout, gmem, wq, mkv, wo)


def _rmsnorm_body(x_ref, g_ref, o_ref):
    x = x_ref[...]
    ms = jnp.mean(x * x, axis=-1, keepdims=True)
    o_ref[...] = x * lax.rsqrt(ms + NORM_EPS) * g_ref[...]


def _rmsnorm(x, gain, *, tm):
    M, D = x.shape
    return pl.pallas_call(
        _rmsnorm_body,
        grid=(M // tm,),
        in_specs=[pl.BlockSpec((tm, D), lambda i: (i, 0)), pl.BlockSpec((1, D), lambda i: (0, 0))],
        out_specs=pl.BlockSpec((tm, D), lambda i: (i, 0)),
        out_shape=jax.ShapeDtypeStruct((M, D), F32),
        compiler_params=_cparams(("parallel",)),
        name="final_norm",
    )(x, gain.reshape(1, D))


def _scatter_rows_body(*refs, widths):
    slabs, o_ref = refs[:len(widths)], refs[len(widths) + 1]
    n_planes = sum(widths)
    rows = slabs[0].shape[0]
    p = 0
    for slab, w in zip(slabs, widths):
        for j in range(w):
            o_ref[pl.ds(p, rows, stride=n_planes), :] = slab[:, j * LANES:(j + 1) * LANES]
            p += 1


def _column_slabs(blocks):
    slabs = []
    i = 0
    while i < len(blocks):
        run = 1
        while i + run < len(blocks) and blocks[i + run] == blocks[i] + run:
            run += 1
        w = 1
        while 2 * w <= run and blocks[i] % (2 * w) == 0:
            w *= 2
        slabs.append((blocks[i], w))
        i += w
    return slabs


def _scatter_rows(buf, src, blocks, l, *, depth, seq_rows, t_start, t_len, tm):
    n_planes = len(blocks)
    slabs = _column_slabs(blocks)
    nb = src.shape[0] // seq_rows
    tiles = t_len // tm
    assert t_len % tm == 0 and t_start % tm == 0 and seq_rows % tm == 0

    def src_map(b, i, c):
        return (b * (seq_rows // tm) + t_start // tm + i, c)

    in_specs = [pl.BlockSpec((tm, w * LANES), functools.partial(src_map, c=first // w)) for first, w in slabs]
    in_specs.append(pl.BlockSpec(memory_space=pl.ANY))
    return pl.pallas_call(
        functools.partial(_scatter_rows_body, widths=tuple(w for _, w in slabs)),
        grid=(nb, tiles),
        in_specs=in_specs,
        out_specs=pl.BlockSpec((tm * n_planes, LANES), lambda b, i: ((b * depth + l) * tiles + i, 0)),
        out_shape=jax.ShapeDtypeStruct(buf.shape, F32),
        input_output_aliases={len(slabs): 0},
        compiler_params=_cparams(("parallel", "parallel")),
        name="scatter_rows",
    )(*([src] * len(slabs)), buf)


def _kv_blocks(kblk, vblk, heads):
    return [kblk + h for h in range(heads)] + [vblk + h for h in range(heads)]


def _row_buffer(entries, depth, rows, heads):
    return jnp.zeros((entries * depth * rows * 2 * heads, HEAD_DIM), F32)


def _as_rows(buf, entries, depth, rows, heads):
    return buf.reshape(entries, depth, rows, 2, heads, HEAD_DIM)


def _shift_window_body(old_ref, new_ref, buf_ref, o_ref, *, n_shift):
    keep = old_ref.shape[0] - n_shift
    o_ref[0:keep, :] = old_ref[n_shift:, :]
    o_ref[keep:, :] = new_ref[...]


def _shift_window(buf, cache_win, new_flat, l):
    Bd, depth, wkeep = cache_win.shape[:3]
    n_planes = 2 * NSA_GROUPS
    n_shift = new_flat.shape[1]
    old = cache_win.reshape(Bd * depth * wkeep * n_planes, HEAD_DIM)
    wrows = wkeep * n_planes
    return pl.pallas_call(
        functools.partial(_shift_window_body, n_shift=n_shift),
        grid=(Bd,),
        in_specs=[pl.BlockSpec((wrows, LANES), lambda b: (b * depth + l, 0)),
                  pl.BlockSpec((None, n_shift, LANES), lambda b: (b, 0, 0)),
                  pl.BlockSpec(memory_space=pl.ANY)],
        out_specs=pl.BlockSpec((wrows, LANES), lambda b: (b * depth + l, 0)),
        out_shape=jax.ShapeDtypeStruct(buf.shape, F32),
        input_output_aliases={2: 0},
        compiler_params=_cparams(("parallel",)),
        name="shift_window",
    )(old, new_flat, buf)


def _regroup_w_in(w_in_l):
    K = w_in_l.shape[0]
    used = sum(b - a for a, b in _W_IN_SEGMENTS)
    w = w_in_l.astype(BF16)
    cols = [w[:, a:b] for a, b in _W_IN_SEGMENTS] + [jnp.zeros((K, NPROJ - used), BF16)]
    return jnp.concatenate(cols, axis=1)


def _cols(proj, blk, n):
    return proj[..., blk * LANES:(blk + n) * LANES]


def _cache_rows(proj3, kblk, vblk, heads):
    nb, rows, _ = proj3.shape
    kv = jnp.concatenate([_cols(proj3, kblk, heads), _cols(proj3, vblk, heads)], axis=-1)
    return kv.reshape(nb, rows, 2, heads, HEAD_DIM)


def _lam_init(l):
    return 0.8 - 0.6 * math.exp(-0.3 * l)


def _prompt_trunk(x_prompt, mem_prompt, P, *, tq=256):
    B, T, D = x_prompt.shape
    depth = P["w_in"].shape[0]
    N = B * T
    tabs = _rope_tables(jnp.arange(T, dtype=jnp.int32))
    tm = min(1024, T)
    memx = mem_prompt.reshape(B * MEM_TOKENS, D)
    x = x_prompt.reshape(N, D)
    wk = min(NSA_WINDOW, T)
    outs = ((BKC, BVC, NSA_GROUPS, 0, T), (BKS, BVS, NSA_GROUPS, 0, T), (BKW, BVW, NSA_GROUPS, T - wk, wk),
            (BDK, BDV, DIFF_HEADS, 0, T), (BMK, BMV, MOBA_HEADS, 0, T))
    bufs = [_row_buffer(B, depth, n, h) for _, _, h, _, n in outs]
    mem_buf = _row_buffer(B, depth, MEM_TOKENS, MEM_HEADS)
    for l in range(depth):
        mkv = _norm_matmul(memx, P["norm_mem_m"][l], P["w_mem_kv"][l].astype(BF16), tm=min(512, B * MEM_TOKENS),
                           tn=256)
        mkv3 = mkv.reshape(B, MEM_TOKENS, 2 * MEM_HEADS * MEM_DIM)
        mem_buf = _scatter_rows(mem_buf, mkv, list(range(2 * MEM_HEADS)), l, depth=depth, seq_rows=MEM_TOKENS,
                                t_start=0, t_len=MEM_TOKENS, tm=MEM_TOKENS)
        proj = _norm_matmul(x, P["norm_mix"][l], _regroup_w_in(P["w_in"][l]), tabs, tm=tm, tn=PROJ_TILE)
        proj3 = proj.reshape(B, T, NPROJ)
        for n, (kblk, vblk, heads, t0, t_len) in enumerate(outs):
            bufs[n] = _scatter_rows(bufs[n], proj, _kv_blocks(kblk, vblk, heads), l, depth=depth, seq_rows=T,
                                    t_start=t0, t_len=t_len, tm=min(512, t_len))
        cw = (P["nsa_pe_k"][l], P["nsa_w1_k"][l], P["nsa_w2_k"][l],
              P["nsa_pe_v"][l], P["nsa_w1_v"][l], P["nsa_w2_v"][l])
        a_nsa = _nsa_prompt(proj, cw, B=B, T=T, tq=min(NSA_TQ, T))
        a_diff = _diff_prompt(proj, P["diff_lambda"][l], P["diff_subln"][l], B=B, T=T, tq=min(DIFF_TQ, T),
                              lam_init=_lam_init(l))
        a_moba = _moba_prompt(proj, B=B, T=T)
        x3 = _out_mem(a_nsa.reshape(B, T, -1), a_diff.reshape(B, T, -1), a_moba.reshape(B, T, -1), proj3,
                      x.reshape(B, T, D), P["w_out"][l].astype(BF16), P["norm_mem_x"][l].reshape(1, D),
                      P["w_mem_q"][l].astype(BF16), mkv3, P["w_mem_o"][l].astype(BF16), tm=min(256, T))
        x = x3.reshape(N, D)
    y = _rmsnorm(x, P["norm_final"], tm=min(512, N)).reshape(B, T, D)
    rows = [_as_rows(bufs[n], B, depth, t_len, heads) for n, (_, _, heads, _, t_len) in enumerate(outs)]
    return y, rows, _as_rows(mem_buf, B, depth, MEM_TOKENS, MEM_HEADS)


DEC_ROWS = 16
DEC_POS = 8


def _plane(page_ref, plane, n_planes):
    return page_ref[pl.ds(plane, PAGE_SIZE, stride=n_planes), :]


def _flat_pages(cache):
    p0, depth, rows, two, heads, dh = cache.shape
    return cache.reshape(p0 * depth * rows * two * heads, dh)


def _cmp_scan_body(pt_ref, *refs, P):
    pages = refs[:P]
    w_ref, o_ref = refs[P], refs[P + 1]
    chunks = PAGE_SIZE // NSA_CMP_STRIDE
    n_planes = 2 * NSA_GROUPS
    r = lax.broadcasted_iota(jnp.int32, (PAGE_SIZE, PAGE_SIZE), 0)
    c = lax.broadcasted_iota(jnp.int32, (PAGE_SIZE, PAGE_SIZE), 1)
    regroup = ((r % chunks) * NSA_CMP_STRIDE + r // chunks == c).astype(BF16)
    rows = chunks * P
    for kv in range(2):
        by_row = [_mm(regroup, jnp.concatenate(
            [_plane(pg, kv * NSA_GROUPS + g, n_planes) for g in range(NSA_GROUPS)], axis=1).astype(BF16))
            for pg in pages]
        acc = jnp.zeros((NSA_GROUPS * rows, 2 * LANES), F32)
        for jp in range(NSA_CMP_STRIDE // 2):
            lhs = jnp.concatenate([
                jnp.concatenate([x[j * chunks:(j + 1) * chunks, g * LANES:(g + 1) * LANES]
                                 for j in (2 * jp, 2 * jp + 1)], axis=1)
                for g in range(NSA_GROUPS) for x in by_row], axis=0).astype(BF16)
            acc = acc + _mm(lhs, w_ref[kv, jp])
        for g in range(NSA_GROUPS):
            lo = (kv * NSA_GROUPS + g) * 2 * LANES
            o_ref[:, lo:lo + 2 * LANES] = acc[g * rows:(g + 1) * rows]


def _cmp_scan(cache, page_table, w1ab, l, *, P):
    Bd, n_pages = page_table.shape
    chunks = PAGE_SIZE // NSA_CMP_STRIDE
    depth = cache.shape[1]
    view = _flat_pages(cache)
    prow = PAGE_SIZE * 2 * NSA_GROUPS
    assert n_pages % P == 0 and cache.shape[2] == PAGE_SIZE

    def page_map(b, s, pt, r):
        return (pt[b, s * P + r] * depth + l, 0)

    in_specs = [pl.BlockSpec((prow, LANES), functools.partial(page_map, r=r)) for r in range(P)]
    in_specs.append(pl.BlockSpec(w1ab.shape, lambda b, s, pt: (0, 0, 0, 0)))
    ncol = 2 * NSA_GROUPS * 2 * LANES
    return pl.pallas_call(
        functools.partial(_cmp_scan_body, P=P),
        grid_spec=pltpu.PrefetchScalarGridSpec(
            num_scalar_prefetch=1, grid=(Bd, n_pages // P), in_specs=in_specs,
            out_specs=pl.BlockSpec((None, chunks * P, ncol), lambda b, s, pt: (b, s, 0))),
        out_shape=jax.ShapeDtypeStruct((Bd, chunks * n_pages, ncol), F32),
        compiler_params=_cparams(("parallel", "arbitrary")),
        name="cmp_scan",
    )(page_table, *([view] * P), w1ab)


def _step_bias(bias, step, per_step):
    lo = step * per_step
    chunk = bias[:, (lo // LANES) * LANES:(lo // LANES + 1) * LANES]
    off = lo % LANES
    return chunk if off == 0 else pltpu.roll(chunk, LANES - off, 1)


def _nsa_dec_select_body(ab_ref, q_ref, pek_ref, w1k_ref, w2k_ref, pev_ref, w1v_ref, w2v_ref,
                         ocmp_ref, bias_ref, *, past, per_step):
    nch = past // NSA_CMP_STRIDE
    nsbp = past // NSA_SLC_LEN
    ncol = _round_up(nsbp, LANES)
    R = NSA_REP * DEC_POS
    scale = HEAD_DIM ** -0.5
    for g in range(NSA_GROUPS):
        def comp(kv, pe_ref, w1_ref, w2_ref):
            lo = (kv * NSA_GROUPS + g) * 2 * LANES
            return _compress_finish(ab_ref[:, lo:lo + LANES], ab_ref[:, lo + LANES:lo + 2 * LANES],
                                    pe_ref, w1_ref, w2_ref, nch).astype(BF16)
        kc = comp(0, pek_ref, w1k_ref, w2k_ref)
        vc = comp(1, pev_ref, w1v_ref, w2v_ref)
        s = _nt(q_ref[g].astype(BF16), kc) * scale
        col = lax.broadcasted_iota(jnp.int32, (R, nch), 1)
        qpos = past + lax.broadcasted_iota(jnp.int32, (R, nch), 0) % DEC_POS
        pc = _masked_softmax(s, (NSA_CMP_STRIDE * col + 2 * NSA_CMP_STRIDE - 1) <= qpos, 1)
        ocmp_ref[g] = _mm(pc.astype(BF16), vc)
        pg = pc[0:DEC_POS]
        for h in range(1, NSA_REP):
            pg = pg + pc[h * DEC_POS:(h + 1) * DEC_POS]
        slc = jnp.dot(pg, _slc_weights((nch, ncol), 0), precision=lax.Precision.HIGHEST,
                      preferred_element_type=F32)
        blk = lax.broadcasted_iota(jnp.int32, (DEC_POS, ncol), 1)
        forced = (blk == 0) | (blk == nsbp - 1)
        score = jnp.where(blk < nsbp, slc + jnp.where(forced, NSA_FORCE_BONUS, 0.0), -jnp.inf)
        rank = _rank_desc(score, nsbp, 1) + jnp.where(forced, 0.0, 1.0)
        b8 = jnp.where(rank < min(NSA_TOPN, nsbp + 1), 0.0, NEG)
        for st in range(nsbp // per_step):
            bias_ref[g, st] = jnp.concatenate([_step_bias(b8, st, per_step)] * NSA_REP, axis=0)


def _nsa_dec_select(ab, qn, cw, *, past, per_step):
    Bd = ab.shape[0]
    R = NSA_REP * DEC_POS
    n_steps = past // NSA_SLC_LEN // per_step

    def full(a):
        return pl.BlockSpec(a.shape, lambda b: (0,) * a.ndim)

    return pl.pallas_call(
        functools.partial(_nsa_dec_select_body, past=past, per_step=per_step),
        grid=(Bd,),
        in_specs=[pl.BlockSpec((None,) + ab.shape[1:], lambda b: (b, 0, 0)),
                  pl.BlockSpec((None, NSA_GROUPS, R, LANES), lambda b: (b, 0, 0, 0))] + [full(a) for a in cw],
        out_specs=[pl.BlockSpec((None, NSA_GROUPS, R, LANES), lambda b: (b, 0, 0, 0)),
                   pl.BlockSpec((None, NSA_GROUPS, n_steps, R, LANES), lambda b: (b, 0, 0, 0, 0))],
        out_shape=[jax.ShapeDtypeStruct((Bd, NSA_GROUPS, R, LANES), F32),
                   jax.ShapeDtypeStruct((Bd, NSA_GROUPS, n_steps, R, LANES), F32)],
        compiler_params=_cparams(("parallel",)),
        name="nsa_dec_select",
    )(ab, qn, *cw)


def _moba_scan_body(pt_ref, *refs, P):
    q_ref = refs[0]
    pages = refs[1:1 + P]
    acc_ref, m_ref, l_ref, sc_ref = refs[1 + P:]
    H, R = MOBA_HEADS, DEC_POS
    n_planes = 2 * H
    ppb = MOBA_BLOCK // PAGE_SIZE
    nblk = P // ppb
    s_id = pl.program_id(1)
    scale = HEAD_DIM ** -0.5
    lane = lax.broadcasted_iota(jnp.int32, (R, LANES), 1)

    @pl.when(s_id == 0)
    def _():
        m_ref[...] = jnp.zeros(m_ref.shape, F32)
        l_ref[...] = jnp.zeros(l_ref.shape, F32)
        sc_ref[...] = jnp.zeros(sc_ref.shape, F32)

    heads = range(H)
    qs = [q_ref[h].astype(BF16) for h in heads]
    ks = [[jnp.concatenate([_plane(pages[b * ppb + e], h, n_planes) for e in range(ppb)], axis=0)
           for b in range(nblk)] for h in heads]
    km = [jnp.concatenate([jnp.mean(ks[h][b], axis=0, keepdims=True) for b in range(nblk)]
                          + [jnp.zeros((LANES - nblk, LANES), F32)], axis=0).astype(BF16) for h in heads]
    mean_sc = [_nt(qs[h], km[h]) for h in heads]
    raw = [[_nt(qs[h], ks[h][b].astype(BF16)) * scale for b in range(nblk)] for h in heads]
    mx = [[jnp.max(raw[h][b], axis=-1, keepdims=True) for b in range(nblk)] for h in heads]
    pr = [[jnp.exp(raw[h][b] - mx[h][b]) for b in range(nblk)] for h in heads]
    sm = [[jnp.sum(pr[h][b], axis=-1, keepdims=True) for b in range(nblk)] for h in heads]
    for h in heads:
        m_t, l_t, s_t = m_ref[h], l_ref[h], sc_ref[h]
        for b in range(nblk):
            v = jnp.concatenate([_plane(pages[b * ppb + e], H + h, n_planes) for e in range(ppb)],
                                axis=0).astype(BF16)
            acc_ref[h, b] = _mm(pr[h][b].astype(BF16), v)
            here = lane == s_id * nblk + b
            m_t = jnp.where(here, mx[h][b], m_t)
            l_t = jnp.where(here, sm[h][b], l_t)
            s_t = jnp.where(here, _lane_pick(mean_sc[h], b), s_t)
        m_ref[h], l_ref[h], sc_ref[h] = m_t, l_t, s_t


def _moba_scan(q, cache, page_table, l, *, P):
    Bd, H, R, _ = q.shape
    n_pages = page_table.shape[1]
    depth = cache.shape[1]
    ppb = MOBA_BLOCK // PAGE_SIZE
    nblk = n_pages // ppb
    prow = PAGE_SIZE * 2 * H
    view = _flat_pages(cache)
    assert n_pages % P == 0 and P % ppb == 0 and nblk <= LANES and cache.shape[2] == PAGE_SIZE

    def page_map(b, s, pt, r):
        return (pt[b, s * P + r] * depth + l, 0)

    stat = pl.BlockSpec((None, H, R, LANES), lambda b, s, pt: (b, 0, 0, 0))
    stat_shape = jax.ShapeDtypeStruct((Bd, H, R, LANES), F32)
    return pl.pallas_call(
        functools.partial(_moba_scan_body, P=P),
        grid_spec=pltpu.PrefetchScalarGridSpec(
            num_scalar_prefetch=1, grid=(Bd, n_pages // P),
            in_specs=[pl.BlockSpec((None, H, R, LANES), lambda b, s, pt: (b, 0, 0, 0))]
            + [pl.BlockSpec((prow, LANES), functools.partial(page_map, r=r)) for r in range(P)],
            out_specs=[pl.BlockSpec((None, H, P // ppb, R, LANES), lambda b, s, pt: (b, 0, s, 0, 0)),
                       stat, stat, stat]),
        out_shape=[jax.ShapeDtypeStruct((Bd, H, nblk, R, LANES), F32), stat_shape, stat_shape, stat_shape],
        compiler_params=_cparams(("parallel", "arbitrary")),
        name="moba_scan",
    )(page_table, q, *([view] * P))


def _moba_combine_body(q_ref, acc_ref, m_ref, l_ref, sc_ref, kn_ref, vn_ref, o_ref, *, nblk, n_new):
    H, R = MOBA_HEADS, DEC_POS
    scale = HEAD_DIM ** -0.5
    lane = lax.broadcasted_iota(jnp.int32, (R, LANES), 1)
    rpos = lax.broadcasted_iota(jnp.int32, (R, LANES), 0) % DEC_POS
    for h in range(H):
        score = jnp.where(lane < nblk, sc_ref[h], -jnp.inf)
        sel = (_rank_desc(score, nblk, 1) < min(MOBA_TOPK, nblk)) & (lane < nblk)
        m_blk = m_ref[h]
        q = q_ref[h].astype(BF16)
        s_new = _nt(q, kn_ref[:, h * LANES:(h + 1) * LANES].astype(BF16)) * scale
        ok_new = (lane <= rpos) & (lane < n_new)
        m_new = jnp.max(jnp.where(ok_new, s_new, NEG), axis=-1, keepdims=True)
        m_all = jnp.maximum(jnp.max(jnp.where(sel, m_blk, NEG), axis=-1, keepdims=True), m_new)
        p_new = jnp.where(ok_new, jnp.exp(s_new - m_all), 0.0)
        w = jnp.where(sel, jnp.exp(m_blk - m_all), 0.0)
        den = jnp.sum(w * l_ref[h], axis=-1, keepdims=True) + jnp.sum(p_new, axis=-1, keepdims=True)
        num = _mm(p_new.astype(BF16), vn_ref[:, h * LANES:(h + 1) * LANES].astype(BF16))
        for b in range(nblk):
            num = num + w[:, b:b + 1] * acc_ref[h, b]
        o_ref[h] = num / den


def _moba_combine(q, acc, m, lsum, sc, knew, vnew, *, n_new):
    Bd, H, nblk, R, _ = acc.shape

    def b4(a):
        return pl.BlockSpec((None,) + a.shape[1:], lambda b: (b,) + (0,) * (a.ndim - 1))

    return pl.pallas_call(
        functools.partial(_moba_combine_body, nblk=nblk, n_new=n_new),
        grid=(Bd,),
        in_specs=[b4(q), b4(acc), b4(m), b4(lsum), b4(sc), b4(knew), b4(vnew)],
        out_specs=pl.BlockSpec((None, H, R, LANES), lambda b: (b, 0, 0, 0)),
        out_shape=jax.ShapeDtypeStruct((Bd, H, R, LANES), F32),
        compiler_params=_cparams(("parallel",)),
        name="moba_combine",
    )(q, acc, m, lsum, sc, knew, vnew)


def _paged_attn_body(ptM_ref, ptm_ref, *refs, P, H, R, scale, blocksize, window, n_new):
    q_ref = refs[0]
    pages = refs[1:1 + P]
    rest = refs[1 + P:]
    if blocksize is not None:
        bias_ref, rest = rest[0], rest[1:]
    kn_ref, vn_ref, o_ref, m_ref, l_ref, acc_ref = rest
    s_id = pl.program_id(1)
    nk = P * PAGE_SIZE
    n_planes = 2 * H

    @pl.when(s_id == 0)
    def _():
        _online_init(m_ref, l_ref, acc_ref)

    if blocksize is not None:
        kblk = lax.broadcasted_iota(jnp.int32, (nk, LANES), 0) // blocksize
        onehot = (kblk == lax.broadcasted_iota(jnp.int32, (nk, LANES), 1)).astype(BF16)
    if window:
        kidx = s_id * nk + lax.broadcasted_iota(jnp.int32, (R, nk), 1)
        in_window = kidx > lax.broadcasted_iota(jnp.int32, (R, nk), 0) % DEC_POS
    heads = range(H)
    scores = []
    for h in heads:
        q = q_ref[h].astype(BF16)
        k = jnp.concatenate([_plane(pg, h, n_planes) for pg in pages], axis=0).astype(BF16)
        if blocksize is not None:
            q = jnp.concatenate([q, bias_ref[h].astype(BF16)], axis=1)
            k = jnp.concatenate([k, onehot], axis=1)
        s = _nt(q, k) * scale
        scores.append(jnp.where(in_window, s, NEG) if window else s)
    m_prev = [m_ref[h] for h in heads]
    m_new = [jnp.maximum(m_prev[h], jnp.max(scores[h], axis=-1, keepdims=True)) for h in heads]
    alpha = [jnp.exp(m_prev[h] - m_new[h]) for h in heads]
    probs = [jnp.exp(scores[h] - m_new[h]) for h in heads]
    l_new = [alpha[h] * l_ref[h] + jnp.sum(probs[h], axis=-1, keepdims=True) for h in heads]
    pv = []
    for h in heads:
        v = jnp.concatenate([_plane(pg, H + h, n_planes) for pg in pages], axis=0).astype(BF16)
        pv.append(_mm(probs[h].astype(BF16), v))
    for h in heads:
        acc_ref[h] = alpha[h] * acc_ref[h] + pv[h]
        m_ref[h] = m_new[h]
        l_ref[h] = l_new[h]

    @pl.when(s_id == pl.num_programs(1) - 1)
    def _():
        lane = lax.broadcasted_iota(jnp.int32, (R, PAGE_SIZE), 1)
        rpos = lax.broadcasted_iota(jnp.int32, (R, PAGE_SIZE), 0) % DEC_POS
        for h in range(H):
            kn = kn_ref[:, h * LANES:(h + 1) * LANES].astype(BF16)
            vn = vn_ref[:, h * LANES:(h + 1) * LANES].astype(BF16)
            s = _nt(q_ref[h].astype(BF16), kn) * scale
            s = jnp.where((lane <= rpos) & (lane < n_new), s, NEG)
            _online_update(s, vn, m_ref.at[h], l_ref.at[h], acc_ref.at[h])
            o_ref[h] = acc_ref[h] / l_ref[h]


def _paged_attn(q, cache, ptM, ptm, l, knew, vnew, bias=None, *, scale, blocksize=None, window=False, P, n_new):
    Bd, H, R, _ = q.shape
    n_pages = ptM.shape[1]
    depth = cache.shape[1]
    ppe = cache.shape[2] // PAGE_SIZE
    prow = PAGE_SIZE * 2 * H
    view = _flat_pages(cache)
    assert n_pages % P == 0 and cache.shape[4] == H

    def page_map(b, s, pM, pm, r):
        return ((pM[b, s * P + r] * depth + l) * ppe + pm[b, s * P + r], 0)

    in_specs = [pl.BlockSpec((None, H, R, LANES), lambda b, s, pM, pm: (b, 0, 0, 0))]
    in_specs += [pl.BlockSpec((prow, LANES), functools.partial(page_map, r=r)) for r in range(P)]
    args = [q] + [view] * P
    if blocksize is not None:
        in_specs.append(pl.BlockSpec((None, H, None, R, LANES), lambda b, s, pM, pm: (b, 0, s, 0, 0)))
        args.append(bias)
    for a in (knew, vnew):
        in_specs.append(pl.BlockSpec((None,) + a.shape[1:], lambda b, s, pM, pm: (b, 0, 0)))
        args.append(a)
    return pl.pallas_call(
        functools.partial(_paged_attn_body, P=P, H=H, R=R, scale=scale, blocksize=blocksize, window=window,
                          n_new=n_new),
        grid_spec=pltpu.PrefetchScalarGridSpec(
            num_scalar_prefetch=2, grid=(Bd, n_pages // P), in_specs=in_specs,
            out_specs=pl.BlockSpec((None, H, R, LANES), lambda b, s, pM, pm: (b, 0, 0, 0)),
            scratch_shapes=[pltpu.VMEM((H, R, 1), F32), pltpu.VMEM((H, R, 1), F32), pltpu.VMEM((H, R, LANES), F32)]),
        out_shape=jax.ShapeDtypeStruct((Bd, H, R, LANES), F32),
        compiler_params=_cparams(("parallel", "arbitrary")),
        name="paged_attn",
    )(ptM, ptm, *args)


def _dec_finalize_body(ocmp_ref, oslc_ref, owin_ref, gate_ref, odiff_ref, lam_ref, sub_ref, omoba_ref,
                       an_ref, ad_ref, am_ref, *, lam_init):
    gt = gate_ref[0:DEC_POS, :]
    an_ref[...] = jnp.zeros(an_ref.shape, F32)
    ad_ref[...] = jnp.zeros(ad_ref.shape, F32)
    am_ref[...] = jnp.zeros(am_ref.shape, F32)
    for g in range(NSA_GROUPS):
        for h in range(NSA_REP):
            head = NSA_REP * g + h
            rows = slice(h * DEC_POS, (h + 1) * DEC_POS)
            out = (_lane_pick(gt, head) * ocmp_ref[g, rows, :]
                   + _lane_pick(gt, NSA_HEADS + head) * oslc_ref[g, rows, :]
                   + _lane_pick(gt, 2 * NSA_HEADS + head) * owin_ref[g, rows, :])
            an_ref[0:DEC_POS, head * LANES:(head + 1) * LANES] = out
    for h in range(DIFF_HEADS):
        o = odiff_ref[h]
        ad_ref[0:DEC_POS, h * LANES:(h + 1) * LANES] = _diff_finish(o[:DEC_POS], o[DEC_POS:], lam_ref, sub_ref, lam_init)
    for h in range(MOBA_HEADS):
        am_ref[0:DEC_POS, h * LANES:(h + 1) * LANES] = omoba_ref[h]


def _dec_finalize(o_cmp, o_slc, o_win, proj3, o_diff, lam, subln, o_moba, *, lam_init):
    Bd = o_cmp.shape[0]

    def b4(a):
        return pl.BlockSpec((None,) + a.shape[1:], lambda b: (b, 0, 0, 0))

    outs = [(NSA_HEADS * LANES), (DIFF_HEADS * LANES), (MOBA_HEADS * LANES)]
    return pl.pallas_call(
        functools.partial(_dec_finalize_body, lam_init=lam_init),
        grid=(Bd,),
        in_specs=[b4(o_cmp), b4(o_slc), b4(o_win),
                  pl.BlockSpec((None, DEC_ROWS, LANES), lambda b: (b, 0, BGATE)),
                  b4(o_diff), pl.BlockSpec(lam.shape, lambda b: (0, 0)), pl.BlockSpec((1, LANES), lambda b: (0, 0)),
                  b4(o_moba)],
        out_specs=[pl.BlockSpec((None, DEC_ROWS, c), lambda b: (b, 0, 0)) for c in outs],
        out_shape=[jax.ShapeDtypeStruct((Bd, DEC_ROWS, c), F32) for c in outs],
        compiler_params=_cparams(("parallel",)),
        name="dec_finalize",
    )(o_cmp, o_slc, o_win, proj3, o_diff, lam, subln.reshape(1, LANES), o_moba)


def _head_major(cols, heads):
    Bd = cols.shape[0]
    return cols[:, :DEC_POS].reshape(Bd, DEC_POS, heads, HEAD_DIM).transpose(0, 2, 1, 3)


def _new_rows(proj3, blk, heads):
    rows = _cols(proj3, blk, heads)
    return jnp.pad(rows, ((0, 0), (0, PAGE_SIZE - rows.shape[1]), (0, 0)))


def _sample_attn(proj3, caches, page_table, P, l, n_new, *, scan_pages=16, nsa_scan_pages=32):
    cache_cmp, cache_slc, cache_win, cache_diff, cache_moba = caches
    Bd = proj3.shape[0]
    n_pages = page_table.shape[1]
    past = n_pages * PAGE_SIZE
    wkeep = cache_win.shape[2]
    assert n_new <= DEC_POS and wkeep == NSA_WINDOW and past >= NSA_WINDOW
    win_pages = wkeep // PAGE_SIZE
    zeros_pt = jnp.zeros_like(page_table)
    win_major = jnp.broadcast_to(jnp.arange(Bd, dtype=jnp.int32)[:, None], (Bd, win_pages))
    win_minor = jnp.broadcast_to(jnp.arange(win_pages, dtype=jnp.int32)[None, :], (Bd, win_pages))
    half = NSA_CMP_STRIDE * HEAD_DIM

    def w1ab(w1):
        return jnp.concatenate([w1[:half].reshape(NSA_CMP_STRIDE, LANES, LANES),
                                w1[half:].reshape(NSA_CMP_STRIDE, LANES, LANES)], axis=-1)
    wab = jnp.stack([w1ab(P["nsa_w1_k"][l]), w1ab(P["nsa_w1_v"][l])]).astype(BF16)
    wab = wab.reshape(2, NSA_CMP_STRIDE // 2, 2 * LANES, 2 * LANES)
    ab = _cmp_scan(cache_cmp, page_table, wab, l, P=nsa_scan_pages)
    qn = _head_major(_cols(proj3, BQ, NSA_HEADS), NSA_HEADS).reshape(Bd, NSA_GROUPS, NSA_REP * DEC_POS, LANES)
    cw = (P["nsa_pe_k"][l], P["nsa_w1_k"][l], P["nsa_w2_k"][l],
          P["nsa_pe_v"][l], P["nsa_w1_v"][l], P["nsa_w2_v"][l])
    o_cmp, bias_slc = _nsa_dec_select(ab, qn, cw, past=past,
                                      per_step=nsa_scan_pages * PAGE_SIZE // NSA_SLC_LEN)
    sc128 = HEAD_DIM ** -0.5
    o_slc = _paged_attn(qn, cache_slc, page_table, zeros_pt, l, _new_rows(proj3, BKS, NSA_GROUPS),
                        _new_rows(proj3, BVS, NSA_GROUPS), bias_slc, scale=sc128, blocksize=NSA_SLC_LEN,
                        P=nsa_scan_pages, n_new=n_new)
    o_win = _paged_attn(qn, cache_win, win_major, win_minor, l, _new_rows(proj3, BKW, NSA_GROUPS),
                        _new_rows(proj3, BVW, NSA_GROUPS), scale=sc128, window=True, P=win_pages, n_new=n_new)
    qd = _head_major(_cols(proj3, BDQ, DIFF_HEADS), DIFF_HEADS)
    lane = jnp.arange(LANES)
    qd = jnp.concatenate([jnp.where(lane < DIFF_QK_DIM, qd, 0.0), jnp.where(lane >= DIFF_QK_DIM, qd, 0.0)], axis=2)
    o_diff = _paged_attn(qd, cache_diff, page_table, zeros_pt, l, _new_rows(proj3, BDK, DIFF_HEADS),
                         _new_rows(proj3, BDV, DIFF_HEADS), scale=DIFF_QK_DIM ** -0.5, P=2 * scan_pages,
                         n_new=n_new)
    qm = _head_major(_cols(proj3, BMQ, MOBA_HEADS), MOBA_HEADS)
    parts = _moba_scan(qm, cache_moba, page_table, l, P=scan_pages)
    o_moba = _moba_combine(qm, *parts, _new_rows(proj3, BMK, MOBA_HEADS), _new_rows(proj3, BMV, MOBA_HEADS),
                           n_new=n_new)
    return _dec_finalize(o_cmp, o_slc, o_win, proj3, o_diff, P["diff_lambda"][l], P["diff_subln"][l], o_moba,
                         lam_init=_lam_init(l))


def _sample_trunk(x_sample, caches, page_table, P):
    cache_cmp, cache_slc, cache_win, cache_diff, cache_moba, cache_mem = caches
    Bd, n_new, D = x_sample.shape
    depth = P["w_in"].shape[0]
    past = page_table.shape[1] * PAGE_SIZE
    pos = past + jnp.arange(DEC_ROWS, dtype=jnp.int32)
    tabs = [jnp.tile(t, (Bd, 1)) for t in _rope_tables(pos)]
    x = jnp.pad(x_sample, ((0, 0), (0, DEC_ROWS - n_new), (0, 0))).reshape(Bd * DEC_ROWS, D)
    mem4 = cache_mem.reshape(Bd, depth, MEM_TOKENS, 2 * MEM_HEADS * MEM_DIM)
    rows = [[] for _ in range(5)]
    wkeep = cache_win.shape[2]
    win_buf = _row_buffer(Bd, depth, wkeep, NSA_GROUPS)
    for l in range(depth):
        proj = _norm_matmul(x, P["norm_mix"][l], _regroup_w_in(P["w_in"][l]), tabs, tm=Bd * DEC_ROWS, tn=PROJ_TILE)
        proj3 = proj.reshape(Bd, DEC_ROWS, NPROJ)
        new3 = proj3[:, :n_new]
        rows[0].append(_cache_rows(new3, BKC, BVC, NSA_GROUPS))
        rows[1].append(_cache_rows(new3, BKS, BVS, NSA_GROUPS))
        win_buf = _shift_window(win_buf, cache_win,
                                _cache_rows(new3, BKW, BVW, NSA_GROUPS).reshape(Bd, -1, HEAD_DIM), l)
        rows[3].append(_cache_rows(new3, BDK, BDV, DIFF_HEADS))
        rows[4].append(_cache_rows(new3, BMK, BMV, MOBA_HEADS))
        a_nsa, a_diff, a_moba = _sample_attn(proj3, caches[:5], page_table, P, l, n_new)
        x3 = _out_mem(a_nsa, a_diff, a_moba, proj3, x.reshape(Bd, DEC_ROWS, D), P["w_out"][l].astype(BF16),
                      P["norm_mem_x"][l].reshape(1, D), P["w_mem_q"][l].astype(BF16), mem4[:, l],
                      P["w_mem_o"][l].astype(BF16), tm=DEC_ROWS)
        x = x3.reshape(Bd * DEC_ROWS, D)
    y = _rmsnorm(x, P["norm_final"], tm=Bd * DEC_ROWS).reshape(Bd, DEC_ROWS, D)[:, :n_new]
    stacked = [jnp.stack(r, axis=1) if r else _as_rows(win_buf, Bd, depth, wkeep, NSA_GROUPS) for r in rows]
    return y, stacked


def kernel(x_prompt, x_sample, mem_prompt, cache_nsa_cmp, cache_nsa_slc, cache_nsa_win, cache_diff, cache_moba, cache_mem, page_table, norm_mix, w_in, w_out, nsa_pe_k, nsa_pe_v, nsa_w1_k, nsa_w2_k, nsa_w1_v, nsa_w2_v, diff_lambda, diff_subln, norm_mem_x, norm_mem_m, w_mem_q, w_mem_kv, w_mem_o, norm_final):
    P = {"norm_mix": norm_mix, "w_in": w_in, "w_out": w_out, "nsa_pe_k": nsa_pe_k, "nsa_pe_v": nsa_pe_v,
         "nsa_w1_k": nsa_w1_k, "nsa_w2_k": nsa_w2_k, "nsa_w1_v": nsa_w1_v, "nsa_w2_v": nsa_w2_v,
         "diff_lambda": diff_lambda, "diff_subln": diff_subln, "norm_mem_x": norm_mem_x,
         "norm_mem_m": norm_mem_m, "w_mem_q": w_mem_q, "w_mem_kv": w_mem_kv, "w_mem_o": w_mem_o,
         "norm_final": norm_final}
    y_prompt, (p_cmp, p_slc, p_win, p_diff, p_moba), p_mem = _prompt_trunk(x_prompt, mem_prompt, P)
    caches = (cache_nsa_cmp, cache_nsa_slc, cache_nsa_win, cache_diff, cache_moba, cache_mem)
    y_sample, (s_cmp, s_slc, s_win, s_diff, s_moba) = _sample_trunk(x_sample, caches, page_table, P)
    return (y_prompt, y_sample, p_cmp, p_slc, p_win, p_diff, p_moba, p_mem, s_cmp, s_slc, s_win, s_diff, s_moba)
```
